```python
import math
import jax
import jax.numpy as jnp
from jax import lax
import numpy as np

D_MODEL = 1024
BATCH = 8
SEQ = 2048
DEPTH = 2
DEC_BATCH = 32
DEC_SEQ = 16
PAST_LEN = 2048

CHUNK = 64
N_AB_LAYERS = (DEPTH + 1) // 2
N_POOL_LAYERS = DEPTH // 2
EPS = 1e-6
NEG_INF = -1e30

A_HEADS = 8
A_KV_HEADS = 2
A_HEAD_DIM = 64
A_WIDTH = A_HEADS * A_HEAD_DIM
A_KV_WIDTH = A_KV_HEADS * A_HEAD_DIM
WINDOW = 128
WINDOW_CHUNKS = WINDOW // CHUNK

B_HEADS = 4
B_HEAD_DIM = 128
B_WIDTH = B_HEADS * B_HEAD_DIM
CONV_W = 4

POOL_SIZES = (2, 4, 8, 16)
POOL_GROUPS = len(POOL_SIZES)
C_WIDTH = D_MODEL
C_GROUP = C_WIDTH // POOL_GROUPS
POOL_HIST = max(POOL_SIZES) - 1

AB_WIDTHS = (A_WIDTH, A_KV_WIDTH, A_KV_WIDTH, A_WIDTH, 3 * B_WIDTH, B_WIDTH, B_HEADS, B_HEADS)
AB_IN = sum(AB_WIDTHS)

kernel_name = 'hybrid_swa_deltanet_pool_stream_step'


def rms_norm(x, g):
    xf = x.astype(jnp.float32)
    y = xf * lax.rsqrt(jnp.mean(xf * xf, axis=-1, keepdims=True) + EPS)
    return (y * g.astype(jnp.float32)).astype(x.dtype)


def l2_norm(x):
    return x * lax.rsqrt(jnp.sum(x * x, axis=-1, keepdims=True) + EPS)


def split_cols(z, widths):
    out, start = [], 0
    for w in widths:
        out.append(z[..., start:start + w])
        start += w
    return out


def sink_attention(q, k, v, mask, dist, sinks):
    Bn, N, Lq, H, D = q.shape
    G = k.shape[3]
    R = H // G
    qg = q.reshape(Bn, N, Lq, G, R, D)
    s = jnp.einsum('bnqgrd,bnkgd->bngrqk', qg, k, preferred_element_type=jnp.float32) * (D ** -0.5)
    slopes = 2.0 ** (-8.0 * jnp.arange(1, H + 1, dtype=jnp.float32) / H)
    s = s - slopes.reshape(G, R)[:, :, None, None] * dist[:, None, None]
    s = jnp.where(mask[:, None, None], s, NEG_INF)
    sk = sinks.astype(jnp.float32).reshape(G, R)[:, :, None, None]
    m = jnp.maximum(jnp.max(s, axis=-1, keepdims=True), sk)
    p = jnp.exp(s - m)
    w = p / (jnp.sum(p, axis=-1, keepdims=True) + jnp.exp(sk - m))
    o = jnp.einsum('bngrqk,bnkgd->bnqgrd', w.astype(v.dtype), v)
    return o.reshape(Bn, N, Lq, H * D)


def band_blocks(a):
    Bn, T = a.shape[:2]
    nc = T // CHUNK
    ap = jnp.pad(a, ((0, 0), (WINDOW, 0), (0, 0), (0, 0)))
    ap = ap.reshape((Bn, nc + WINDOW_CHUNKS, CHUNK) + a.shape[2:])
    return jnp.concatenate([ap[:, j:j + nc] for j in range(WINDOW_CHUNKS + 1)], axis=2)


def swa_prompt(q, k, v, sinks):
    Bn, T = q.shape[:2]
    nc = T // CHUNK
    lk = (WINDOW_CHUNKS + 1) * CHUNK
    qb = q.reshape(Bn, nc, CHUNK, A_HEADS, A_HEAD_DIM)
    qpos = jnp.arange(T).reshape(nc, CHUNK)
    kpos = (jnp.arange(nc)[:, None] - WINDOW_CHUNKS) * CHUNK + jnp.arange(lk)[None, :]
    mask = jnp.broadcast_to((kpos >= 0)[:, None, :], (nc, CHUNK, lk))
    dist = jnp.abs(qpos[:, :, None] - kpos[:, None, :]).astype(jnp.float32)
    return sink_attention(qb, band_blocks(k), band_blocks(v), mask, dist, sinks)


def swa_sample(q, k, v, cache_k, cache_v, sinks, pos0):
    T = q.shape[1]
    nbuf = cache_k.shape[1]
    kk = jnp.concatenate([cache_k.astype(k.dtype), k], axis=1)[:, None]
    vv = jnp.concatenate([cache_v.astype(v.dtype), v], axis=1)[:, None]
    qpos = pos0 + jnp.arange(T)
    kpos = pos0 - nbuf + jnp.arange(nbuf + T)
    mask = jnp.ones((1, T, nbuf + T), dtype=bool)
    dist = jnp.abs(qpos[:, None] - kpos[None, :]).astype(jnp.float32)[None]
    return sink_attention(q[:, None], kk, vv, mask, dist, sinks)


def causal_conv(u, hist, w):
    T = u.shape[1]
    up = jnp.concatenate([hist.astype(u.dtype), u], axis=1)
    y = up[:, 0:T] * w[0]
    for j in range(1, CONV_W):
        y = y + up[:, j:j + T] * w[j]
    return jax.nn.silu(y), up[:, -(CONV_W - 1):]


def gated_delta_rule(q, k, v, beta, g, s0, chunk):
    Bn, T, H, Dk = q.shape
    Dv = v.shape[-1]
    n = T // chunk

    def blk(a):
        a = a.reshape((Bn, n, chunk, H) + a.shape[3:])
        return jnp.moveaxis(a, 3, 1)

    q, k, v, beta, g = blk(q) * (Dk ** -0.5), blk(k), blk(v), blk(beta), blk(g)
    G = jnp.cumsum(g, axis=-1)
    lower = jnp.tril(jnp.ones((chunk, chunk), dtype=bool))
    strict = jnp.tril(jnp.ones((chunk, chunk), dtype=bool), k=-1)
    diff = G[..., :, None] - G[..., None, :]
    decay = jnp.where(lower, jnp.exp(jnp.where(lower, diff, 0.0)), 0.0)
    kb = k * beta[..., None]
    a_mat = jnp.where(strict, jnp.einsum('bhnid,bhnjd->bhnij', kb, k) * decay, 0.0) + jnp.eye(chunk, dtype=q.dtype)
    rhs = jnp.concatenate([v * beta[..., None], kb * jnp.exp(G)[..., None]], axis=-1)
    sol = lax.linalg.triangular_solve(a_mat, rhs, left_side=True, lower=True, unit_diagonal=True)
    u, w = sol[..., :Dv], sol[..., Dv:]
    qk = jnp.einsum('bhnid,bhnjd->bhnij', q, k) * decay
    q_dec = q * jnp.exp(G)[..., None]
    k_dec = k * jnp.exp(G[..., -1:] - G)[..., None]
    g_tot = jnp.exp(G[..., -1])

    def step(s, xs):
        qk_n, qd_n, kd_n, u_n, w_n, gt_n = xs
        v_new = u_n - jnp.einsum('bhck,bhkv->bhcv', w_n, s)
        o = jnp.einsum('bhck,bhkv->bhcv', qd_n, s) + jnp.einsum('bhij,bhjv->bhiv', qk_n, v_new)
        s = s * gt_n[..., None, None] + jnp.einsum('bhck,bhcv->bhkv', kd_n, v_new)
        return s, o

    xs = tuple(jnp.moveaxis(a, 2, 0) for a in (qk, q_dec, k_dec, u, w, g_tot))
    s_fin, o = lax.scan(step, s0, xs)
    o = jnp.moveaxis(jnp.moveaxis(o, 0, 2), 1, 3).reshape(Bn, T, H, Dv)
    return o, s_fin


def ab_layer(h, norm_g, w_in, q_norm, k_norm, sinks, conv_w, a_log, dt_bias, o_norm, w_out,
             cache_k, cache_v, s0, conv_hist, pos0):
    Bn, T, _ = h.shape
    f32 = jnp.float32
    z = rms_norm(h, norm_g) @ w_in
    a_q, a_k, a_v, a_g, b_qkv, b_g, b_beta, b_alpha = split_cols(z, AB_WIDTHS)
    q = rms_norm(a_q.reshape(Bn, T, A_HEADS, A_HEAD_DIM), q_norm)
    k = rms_norm(a_k.reshape(Bn, T, A_KV_HEADS, A_HEAD_DIM), k_norm)
    v = a_v.reshape(Bn, T, A_KV_HEADS, A_HEAD_DIM)
    if cache_k is None:
        o_a = swa_prompt(q, k, v, sinks)
        new_k, new_v = k[:, -WINDOW:], v[:, -WINDOW:]
        chunk = CHUNK
    else:
        o_a = swa_sample(q, k, v, cache_k, cache_v, sinks, pos0)
        new_k, new_v = k, v
        chunk = T
    o_a = o_a.reshape(Bn, T, A_WIDTH) * jax.nn.silu(a_g)
    c, new_hist = causal_conv(b_qkv, conv_hist, conv_w)
    bq, bk, bv = jnp.split(c, 3, axis=-1)
    bq = l2_norm(bq.reshape(Bn, T, B_HEADS, B_HEAD_DIM).astype(f32))
    bk = l2_norm(bk.reshape(Bn, T, B_HEADS, B_HEAD_DIM).astype(f32))
    bv = bv.reshape(Bn, T, B_HEADS, B_HEAD_DIM).astype(f32)
    beta = jax.nn.sigmoid(b_beta.astype(f32))
    g = -jnp.exp(a_log.astype(f32)) * jax.nn.softplus(b_alpha.astype(f32) + dt_bias.astype(f32))
    o_b, s_new = gated_delta_rule(bq, bk, bv, beta, g, s0.astype(f32), chunk)
    o_b = rms_norm(o_b, o_norm).astype(h.dtype).reshape(Bn, T, B_WIDTH) * jax.nn.silu(b_g)
    y = h + jnp.concatenate([o_a, o_b], axis=-1) @ w_out
    return y, (new_k, new_v, s_new.astype(h.dtype), new_hist)


def pool_layer(h, norm_g, w_in, w_grp, scale, w_out, hist, pos0):
    Bn, T, _ = h.shape
    f32 = jnp.float32
    z = rms_norm(h, norm_g) @ w_in
    u, gate = z[..., :C_WIDTH], z[..., C_WIDTH:]
    ue = jnp.concatenate([hist.astype(u.dtype), u], axis=1)
    pos = pos0 - POOL_HIST + jnp.arange(POOL_HIST + T)
    uf = ue.astype(f32) * (pos >= 0)[None, :, None]
    cs = jnp.concatenate([jnp.zeros((Bn, 1, C_WIDTH), f32), jnp.cumsum(uf, axis=1)], axis=1)
    hi = cs[:, POOL_HIST + 1:]
    means = []
    for gi, w in enumerate(POOL_SIZES):
        cols = slice(gi * C_GROUP, (gi + 1) * C_GROUP)
        lo = cs[:, POOL_HIST + 1 - w:POOL_HIST + 1 - w + T, cols]
        cnt = jnp.minimum(pos[POOL_HIST:] + 1, w).astype(f32)
        means.append((hi[..., cols] - lo) / cnt[None, :, None])
    pooled = jnp.concatenate(means, axis=-1) - u.astype(f32)
    mixed = jnp.einsum('btgc,gcd->btgd', pooled.reshape(Bn, T, POOL_GROUPS, C_GROUP), w_grp.astype(f32))
    mixed = (mixed.reshape(Bn, T, C_WIDTH) * scale.astype(f32)).astype(h.dtype)
    y = h + (mixed * jax.nn.silu(gate)) @ w_out
    return y, ue[:, -POOL_HIST:]


def setup_inputs(seed: int = 0) -> dict:
    key = jax.random.key(seed)
    ks = jax.random.split(key, 24)
    f32 = jnp.float32
    na, npl = N_AB_LAYERS, N_POOL_LAYERS

    def nrm(k, shape, s):
        return jax.random.normal(k, shape, f32) * s

    dt = jnp.exp(jax.random.uniform(ks[12], (na, B_HEADS), f32, math.log(1e-3), math.log(1e-1)))
    dt_bias = dt + jnp.log(-jnp.expm1(-dt))
    return {
        'x_prompt': nrm(ks[0], (BATCH, SEQ, D_MODEL), 1.0),
        'x_sample': nrm(ks[1], (DEC_BATCH, DEC_SEQ, D_MODEL), 1.0),
        'cache_a_k': nrm(ks[2], (na, DEC_BATCH, WINDOW, A_KV_HEADS, A_HEAD_DIM), 1.0),
        'cache_a_v': nrm(ks[3], (na, DEC_BATCH, WINDOW, A_KV_HEADS, A_HEAD_DIM), 1.0),
        'state_b_s': nrm(ks[4], (na, DEC_BATCH, B_HEADS, B_HEAD_DIM, B_HEAD_DIM), 0.1),
        'state_b_conv': nrm(ks[5], (na, DEC_BATCH, CONV_W - 1, 3 * B_WIDTH), 1.0),
        'state_c_pool': nrm(ks[6], (npl, DEC_BATCH, POOL_HIST, C_WIDTH), 1.0),
        'norm_ab': 1.0 + nrm(ks[7], (na, D_MODEL), 0.02),
        'w_in_ab': nrm(ks[8], (na, D_MODEL, AB_IN), D_MODEL ** -0.5),
        'q_norm_a': 1.0 + nrm(ks[9], (na, A_HEAD_DIM), 0.02),
        'k_norm_a': 1.0 + nrm(ks[10], (na, A_HEAD_DIM), 0.02),
        'sinks_a': nrm(ks[11], (na, A_HEADS), 0.5),
        'conv_b': nrm(ks[13], (na, CONV_W, 3 * B_WIDTH), CONV_W ** -0.5),
        'a_log_b': jnp.log(jax.random.uniform(ks[14], (na, B_HEADS), f32, 1.0, 16.0)),
        'dt_bias_b': dt_bias,
        'o_norm_b': 1.0 + nrm(ks[15], (na, B_HEAD_DIM), 0.02),
        'w_out_ab': nrm(ks[16], (na, A_WIDTH + B_WIDTH, D_MODEL), (A_WIDTH + B_WIDTH) ** -0.5),
        'norm_c': 1.0 + nrm(ks[17], (npl, D_MODEL), 0.02),
        'w_in_c': nrm(ks[18], (npl, D_MODEL, 2 * C_WIDTH), D_MODEL ** -0.5),
        'w_grp_c': nrm(ks[19], (npl, POOL_GROUPS, C_GROUP, C_GROUP), C_GROUP ** -0.5),
        'scale_c': 1.0 + nrm(ks[20], (npl, C_WIDTH), 0.1),
        'w_out_c': nrm(ks[21], (npl, C_WIDTH, D_MODEL), C_WIDTH ** -0.5),
    }


def reference(x_prompt, x_sample, cache_a_k, cache_a_v, state_b_s, state_b_conv, state_c_pool,
              norm_ab, w_in_ab, q_norm_a, k_norm_a, sinks_a, conv_b, a_log_b, dt_bias_b, o_norm_b, w_out_ab,
              norm_c, w_in_c, w_grp_c, scale_c, w_out_c):
    hp, hs = x_prompt, x_sample
    bp = x_prompt.shape[0]
    pa_k, pa_v, pb_s, pb_c, pc = [], [], [], [], []
    sa_k, sa_v, sb_s, sb_c, sc = [], [], [], [], []
    for layer in range(DEPTH):
        i = layer // 2
        if layer % 2 == 0:
            wts = (norm_ab[i], w_in_ab[i], q_norm_a[i], k_norm_a[i], sinks_a[i], conv_b[i],
                   a_log_b[i], dt_bias_b[i], o_norm_b[i], w_out_ab[i])
            s0 = jnp.zeros((bp, B_HEADS, B_HEAD_DIM, B_HEAD_DIM), jnp.float32)
            c0 = jnp.zeros((bp, CONV_W - 1, 3 * B_WIDTH), hp.dtype)
            hp, (k_, v_, s_, c_) = ab_layer(hp, *wts, None, None, s0, c0, 0)
            pa_k.append(k_)
            pa_v.append(v_)
            pb_s.append(s_)
            pb_c.append(c_)
            hs, (k_, v_, s_, c_) = ab_layer(hs, *wts, cache_a_k[i], cache_a_v[i], state_b_s[i],
                                            state_b_conv[i], PAST_LEN)
            sa_k.append(k_)
            sa_v.append(v_)
            sb_s.append(s_)
            sb_c.append(c_)
        else:
            wts = (norm_c[i], w_in_c[i], w_grp_c[i], scale_c[i], w_out_c[i])
            h0 = jnp.zeros((bp, POOL_HIST, C_WIDTH), hp.dtype)
            hp, st = pool_layer(hp, *wts, h0, 0)
            pc.append(st)
            hs, st = pool_layer(hs, *wts, state_c_pool[i], PAST_LEN)
            sc.append(st)
    return (hp, hs,
            jnp.stack(pa_k), jnp.stack(pa_v), jnp.stack(pb_s), jnp.stack(pb_c), jnp.stack(pc),
            jnp.stack(sa_k), jnp.stack(sa_v), jnp.stack(sb_s), jnp.stack(sb_c), jnp.stack(sc))
```

```python
import functools

import jax
import jax.numpy as jnp
from jax import lax
from jax.experimental import pallas as pl
from jax.experimental.pallas import tpu as pltpu

F32 = jnp.float32
BF16 = jnp.bfloat16
HIGHEST = lax.Precision.HIGHEST

D_MODEL = 1024
CHUNK = 64
PAST_LEN = 2048
EPS = 1e-6
NEG_INF = -1e30

A_HEADS = 8
A_KV_HEADS = 2
A_HEAD_DIM = 64
A_WIDTH = A_HEADS * A_HEAD_DIM
A_KV_WIDTH = A_KV_HEADS * A_HEAD_DIM
A_REP = A_HEADS // A_KV_HEADS
WINDOW = 128

B_HEADS = 4
B_HEAD_DIM = 128
B_WIDTH = B_HEADS * B_HEAD_DIM
CONV_W = 4

POOL_SIZES = (2, 4, 8, 16)
C_WIDTH = D_MODEL
C_GROUP = C_WIDTH // len(POOL_SIZES)
POOL_HIST = max(POOL_SIZES) - 1

LANES = 128
SUBLANES = 8

Z_AQ = 0
Z_AG = A_WIDTH
Z_BG = 2 * A_WIDTH
Z_BQKV = 3 * A_WIDTH
Z_AK = Z_BQKV + 3 * B_WIDTH
Z_AV = Z_AK + A_KV_WIDTH
Z_BA = Z_AV + A_KV_WIDTH
Z_WIDTH = Z_BA + LANES
GATE_LANE = B_HEADS

VMEM_LIMIT = 48 * 1024 * 1024


def _sigmoid(x):
    return 1.0 / (1.0 + jnp.exp(-x))


def _silu(x):
    return x * _sigmoid(x)


def _softplus(x):
    return jnp.maximum(x, 0.0) + jnp.log(1.0 + jnp.exp(-jnp.abs(x)))


def _dot(a, b, precision=None):
    return jnp.dot(a, b, preferred_element_type=F32, precision=precision)


def _dot_nt(a, b, precision=None):
    return lax.dot_general(a, b, (((1,), (1,)), ((), ())), preferred_element_type=F32, precision=precision)


def _dot_tn(a, b, precision=None):
    return lax.dot_general(a, b, (((0,), (0,)), ((), ())), preferred_element_type=F32, precision=precision)


def _log2(n):
    assert n & (n - 1) == 0
    return n.bit_length() - 1


def _in_proj_kernel(x_ref, g_ref, w_ref, kn_ref, z_ref):
    x = x_ref[...]
    xn = x * lax.rsqrt(jnp.mean(x * x, axis=-1, keepdims=True) + EPS) * g_ref[...]
    z_ref[...] = _dot(xn.astype(BF16), w_ref[...])
    k = z_ref[:, Z_AK:Z_AK + A_KV_WIDTH]
    sq = k * k
    lane = lax.broadcasted_iota(jnp.int32, k.shape, 1)
    first = lane < A_HEAD_DIM
    s0 = jnp.sum(jnp.where(first, sq, 0.0), axis=-1, keepdims=True)
    s1 = jnp.sum(jnp.where(first, 0.0, sq), axis=-1, keepdims=True)
    ms = jnp.where(first, s0, s1) * (1.0 / A_HEAD_DIM)
    z_ref[:, Z_AK:Z_AK + A_KV_WIDTH] = k * lax.rsqrt(ms + EPS) * kn_ref[...]


def _in_proj(x, g, w, kn, tm):
    n = x.shape[0]
    return pl.pallas_call(
        _in_proj_kernel,
        grid=(n // tm,),
        in_specs=[
            pl.BlockSpec((tm, D_MODEL), lambda i: (i, 0)),
            pl.BlockSpec((1, D_MODEL), lambda i: (0, 0)),
            pl.BlockSpec((D_MODEL, Z_WIDTH), lambda i: (0, 0)),
            pl.BlockSpec((1, A_KV_WIDTH), lambda i: (0, 0)),
        ],
        out_specs=pl.BlockSpec((tm, Z_WIDTH), lambda i: (i, 0)),
        out_shape=jax.ShapeDtypeStruct((n, Z_WIDTH), F32),
        compiler_params=pltpu.CompilerParams(dimension_semantics=("arbitrary",), vmem_limit_bytes=VMEM_LIMIT),
        name="in_proj",
    )(x, g, w, kn)


def _attn_kernel(sinks_ref, q_ref, k_ref, v_ref, ag_ref, qn_ref, o_ref, *, cq, lk, off, tq):
    i = pl.program_id(1)
    n_sub = tq // cq
    rows = A_REP * cq
    qn = qn_ref[...]
    row1 = lax.broadcasted_iota(jnp.int32, (rows, 1), 0)
    rep = row1 >> _log2(cq)
    t = row1 & (cq - 1)
    col = lax.broadcasted_iota(jnp.int32, (rows, lk), 1)
    for cc in range(n_sub):
        cg = i * n_sub + cc
        kstart = pl.multiple_of(jnp.maximum(cg * cq + off - WINDOW, 0), SUBLANES)
        kk = k_ref[pl.ds(kstart, lk), :]
        vv = v_ref[pl.ds(kstart, lk), :]
        qc = q_ref[cc * cq:(cc + 1) * cq, :]
        kpos = kstart + col
        dist = jnp.abs(cg * cq + off + t - kpos).astype(F32)
        valid = kpos < (cg + 1) * cq + off
        outs = []
        for g in range(A_KV_HEADS):
            kg = kk[:, g * A_HEAD_DIM:(g + 1) * A_HEAD_DIM].astype(BF16)
            vg = vv[:, g * A_HEAD_DIM:(g + 1) * A_HEAD_DIM].astype(BF16)
            qs = []
            for r in range(A_REP):
                h = g * A_REP + r
                qh = qc[:, h * A_HEAD_DIM:(h + 1) * A_HEAD_DIM]
                qs.append(qh * lax.rsqrt(jnp.mean(qh * qh, axis=-1, keepdims=True) + EPS) * qn)
            qs = jnp.concatenate(qs, axis=0).astype(BF16)
            s = _dot_nt(qs, kg) * (A_HEAD_DIM ** -0.5)
            slope = jnp.zeros((rows, 1), F32)
            sk = jnp.zeros((rows, 1), F32)
            for r in range(A_REP):
                h = g * A_REP + r
                slope = jnp.where(rep == r, 2.0 ** (-8.0 * (h + 1) / A_HEADS), slope)
                sk = jnp.where(rep == r, sinks_ref[h], sk)
            s = s - slope * dist
            s = jnp.where(valid, s, NEG_INF)
            m = jnp.maximum(jnp.max(s, axis=-1, keepdims=True), sk)
            p = jnp.exp(s - m)
            den = jnp.sum(p, axis=-1, keepdims=True) + jnp.exp(sk - m)
            o = _dot(p.astype(BF16), vg) / den
            outs.extend(o[r * cq:(r + 1) * cq] for r in range(A_REP))
        ag = ag_ref[cc * cq:(cc + 1) * cq, :]
        o_ref[cc * cq:(cc + 1) * cq, :] = jnp.concatenate(outs, axis=1) * _silu(ag)


def _attention(z, kbuf, vbuf, kv_specs, sinks, qn, *, nb, t, tq, cq, lk, off):
    nt = t // tq
    kern = functools.partial(_attn_kernel, cq=cq, lk=lk, off=off, tq=tq)
    return pl.pallas_call(
        kern,
        grid=(nb, nt),
        in_specs=[
            pl.BlockSpec(memory_space=pltpu.SMEM),
            pl.BlockSpec((tq, A_WIDTH), lambda b, i: (b * nt + i, Z_AQ // A_WIDTH)),
            kv_specs[0],
            kv_specs[1],
            pl.BlockSpec((tq, A_WIDTH), lambda b, i: (b * nt + i, Z_AG // A_WIDTH)),
            pl.BlockSpec((1, A_HEAD_DIM), lambda b, i: (0, 0)),
        ],
        out_specs=pl.BlockSpec((tq, A_WIDTH), lambda b, i: (b * nt + i, 0)),
        out_shape=jax.ShapeDtypeStruct((nb * t, A_WIDTH), F32),
        compiler_params=pltpu.CompilerParams(dimension_semantics=("arbitrary", "arbitrary"),
                                             vmem_limit_bytes=VMEM_LIMIT),
        name="swa_attention",
    )(sinks, z, kbuf, vbuf, z, qn)


def _unit_lower_inverse(a, c):
    ii = lax.broadcasted_iota(jnp.int32, (c, c), 0)
    jj = lax.broadcasted_iota(jnp.int32, (c, c), 1)
    base = SUBLANES
    n1 = jnp.where((ii >> _log2(base)) == (jj >> _log2(base)), -a, 0.0)
    x = jnp.where(ii == jj, 1.0, 0.0) + n1
    n2 = _dot(n1, n1, HIGHEST)
    x = x + _dot(x, n2, HIGHEST)
    n4 = _dot(n2, n2, HIGHEST)
    x = x + _dot(x, n4, HIGHEST)
    s = base
    while s < c:
        sel = ((ii >> _log2(2 * s)) == (jj >> _log2(2 * s))) & ((ii >> _log2(s)) != (jj >> _log2(s)))
        x = x - _dot(_dot(x, jnp.where(sel, a, 0.0), HIGHEST), x, HIGHEST)
        s *= 2
    return x


def _delta_kernel(qkv_ref, ba_ref, bg_ref, hist_ref, s0_ref, convw_ref, gate_ref, onorm_ref,
                  o_ref, sout_ref, ext_ref, c_ref, s_ref, *, c, tc):
    i = pl.program_id(1)
    hd = B_HEAD_DIM
    pad = SUBLANES

    @pl.when(i == 0)
    def _():
        ext_ref[0:pad, :] = hist_ref[0]
        s_ref[...] = s0_ref[0]

    @pl.when(i > 0)
    def _():
        ext_ref[0:pad, :] = ext_ref[tc:tc + pad, :]

    ext_ref[pad:pad + tc, :] = qkv_ref[...]

    rc = min(tc, CHUNK)
    for r0 in range(0, tc, rc):
        base = pad - (CONV_W - 1) + r0
        acc = ext_ref[base:base + rc, :] * convw_ref[0:1, :]
        for j in range(1, CONV_W):
            acc = acc + ext_ref[base + j:base + j + rc, :] * convw_ref[j:j + 1, :]
        y = _silu(acc)
        for hh in range(2 * B_HEADS):
            blk = y[:, hh * hd:(hh + 1) * hd]
            c_ref[r0:r0 + rc, hh * hd:(hh + 1) * hd] = blk * lax.rsqrt(jnp.sum(blk * blk, axis=-1, keepdims=True) + EPS)
        c_ref[r0:r0 + rc, 2 * B_WIDTH:] = y[:, 2 * B_WIDTH:]

    ii = lax.broadcasted_iota(jnp.int32, (c, c), 0)
    jj = lax.broadcasted_iota(jnp.int32, (c, c), 1)
    lower = ii >= jj
    strict = ii > jj
    tril = jnp.where(lower, 1.0, 0.0)
    neg_rate = -jnp.exp(gate_ref[0:1, :])
    dt_bias = gate_ref[1:2, :]
    onorm = onorm_ref[...]

    def chunk_body(cc, carry):
        r = pl.multiple_of(cc * c, c)
        ba = ba_ref[pl.ds(r, c), :]
        beta_all = _sigmoid(ba)
        g_all = neg_rate * _softplus(ba + dt_bias)
        gcum = _dot(tril, g_all, HIGHEST)
        gcum_t = gcum.T
        for h in range(B_HEADS):
            q = c_ref[pl.ds(r, c), h * hd:(h + 1) * hd] * (hd ** -0.5)
            k = c_ref[pl.ds(r, c), B_WIDTH + h * hd:B_WIDTH + (h + 1) * hd]
            v = c_ref[pl.ds(r, c), 2 * B_WIDTH + h * hd:2 * B_WIDTH + (h + 1) * hd]
            beta = beta_all[:, h:h + 1]
            gc = gcum[:, GATE_LANE + h:GATE_LANE + h + 1]
            gr = gcum_t[GATE_LANE + h:GATE_LANE + h + 1, :]
            decay = jnp.where(lower, jnp.exp(jnp.where(lower, gc - gr, 0.0)), 0.0)
            kb = k * beta
            a = jnp.where(strict, _dot_nt(kb, k, HIGHEST) * decay, 0.0)
            tinv = _unit_lower_inverse(a, c)
            eg = jnp.exp(gc)
            sol = _dot(tinv, jnp.concatenate([v * beta, kb * eg], axis=1), HIGHEST)
            u = sol[:, :hd]
            w = sol[:, hd:]
            qk = _dot_nt(q, k, HIGHEST) * decay
            g_last = gcum[c - 1:c, GATE_LANE + h:GATE_LANE + h + 1]
            k_dec = k * jnp.exp(g_last - gc)
            s = s_ref[h]
            ws_qs = _dot(jnp.concatenate([w, q * eg], axis=0), s, HIGHEST)
            v_new = u - ws_qs[:c]
            o = ws_qs[c:] + _dot(qk, v_new, HIGHEST)
            s_ref[h] = s * jnp.exp(g_last) + _dot_tn(k_dec, v_new, HIGHEST)
            on = o * lax.rsqrt(jnp.mean(o * o, axis=-1, keepdims=True) + EPS) * onorm
            bg = bg_ref[pl.ds(r, c), h * hd:(h + 1) * hd]
            o_ref[pl.ds(r, c), h * hd:(h + 1) * hd] = on * _silu(bg)
        return carry

    lax.fori_loop(0, tc // c, chunk_body, 0)
    sout_ref[0] = s_ref[...]


def _delta(z, hist, s0, convw, gate, onorm, *, nb, t, tc, c):
    nt = t // tc
    kern = functools.partial(_delta_kernel, c=c, tc=tc)
    return pl.pallas_call(
        kern,
        grid=(nb, nt),
        in_specs=[
            pl.BlockSpec((tc, 3 * B_WIDTH), lambda b, i: (b * nt + i, Z_BQKV // (3 * B_WIDTH))),
            pl.BlockSpec((tc, LANES), lambda b, i: (b * nt + i, Z_BA // LANES)),
            pl.BlockSpec((tc, B_WIDTH), lambda b, i: (b * nt + i, Z_BG // B_WIDTH)),
            pl.BlockSpec((1, SUBLANES, 3 * B_WIDTH), lambda b, i: (b, 0, 0)),
            pl.BlockSpec((1, B_HEADS, B_HEAD_DIM, B_HEAD_DIM), lambda b, i: (b, 0, 0, 0)),
            pl.BlockSpec((CONV_W, 3 * B_WIDTH), lambda b, i: (0, 0)),
            pl.BlockSpec((2, LANES), lambda b, i: (0, 0)),
            pl.BlockSpec((1, B_HEAD_DIM), lambda b, i: (0, 0)),
        ],
        out_specs=[
            pl.BlockSpec((tc, B_WIDTH), lambda b, i: (b * nt + i, 0)),
            pl.BlockSpec((1, B_HEADS, B_HEAD_DIM, B_HEAD_DIM), lambda b, i: (b, 0, 0, 0)),
        ],
        out_shape=[
            jax.ShapeDtypeStruct((nb * t, B_WIDTH), F32),
            jax.ShapeDtypeStruct((nb, B_HEADS, B_HEAD_DIM, B_HEAD_DIM), F32),
        ],
        scratch_shapes=[
            pltpu.VMEM((SUBLANES + tc, 3 * B_WIDTH), F32),
            pltpu.VMEM((tc, 3 * B_WIDTH), F32),
            pltpu.VMEM((B_HEADS, B_HEAD_DIM, B_HEAD_DIM), F32),
        ],
        compiler_params=pltpu.CompilerParams(dimension_semantics=("arbitrary", "arbitrary"),
                                             vmem_limit_bytes=VMEM_LIMIT),
        name="gated_delta",
    )(z, z, z, hist, s0, convw, gate, onorm)


def _out_proj_kernel(h_ref, oa_ref, ob_ref, wa_ref, wb_ref, y_ref):
    y_ref[...] = (h_ref[...] + _dot(oa_ref[...].astype(BF16), wa_ref[...])
                  + _dot(ob_ref[...].astype(BF16), wb_ref[...]))


def _out_proj(h, oa, ob, wa, wb, tm):
    n = h.shape[0]
    return pl.pallas_call(
        _out_proj_kernel,
        grid=(n // tm,),
        in_specs=[
            pl.BlockSpec((tm, D_MODEL), lambda i: (i, 0)),
            pl.BlockSpec((tm, A_WIDTH), lambda i: (i, 0)),
            pl.BlockSpec((tm, B_WIDTH), lambda i: (i, 0)),
            pl.BlockSpec((A_WIDTH, D_MODEL), lambda i: (0, 0)),
            pl.BlockSpec((B_WIDTH, D_MODEL), lambda i: (0, 0)),
        ],
        out_specs=pl.BlockSpec((tm, D_MODEL), lambda i: (i, 0)),
        out_shape=jax.ShapeDtypeStruct((n, D_MODEL), F32),
        compiler_params=pltpu.CompilerParams(dimension_semantics=("arbitrary",), vmem_limit_bytes=VMEM_LIMIT),
        name="out_proj",
    )(h, oa, ob, wa, wb)


def _pool_kernel(h_ref, hist_ref, g_ref, win_ref, wgrp_ref, scale_ref, wout_ref, y_ref, tail_ref, ext_ref,
                 *, tt, pos0):
    i = pl.program_id(1)
    pad = POOL_HIST + 1

    @pl.when(i == 0)
    def _():
        pos = pos0 - pad + lax.broadcasted_iota(jnp.int32, (pad, 1), 0)
        ext_ref[0:pad, :] = jnp.where(pos >= 0, hist_ref[0], 0.0)

    @pl.when(i > 0)
    def _():
        ext_ref[0:pad, :] = ext_ref[tt:tt + pad, :]

    x = h_ref[...]
    xn = x * lax.rsqrt(jnp.mean(x * x, axis=-1, keepdims=True) + EPS) * g_ref[...]
    z = _dot(xn.astype(BF16), win_ref[...])
    u = z[:, :C_WIDTH]
    gate = z[:, C_WIDTH:]
    ext_ref[pad:pad + tt, :] = u
    tail_ref[0] = u[tt - pad:, :]

    tpos = pos0 + i * tt + lax.broadcasted_iota(jnp.int32, (tt, 1), 0)
    mixed = []
    for gi, w in enumerate(POOL_SIZES):
        cols = slice(gi * C_GROUP, (gi + 1) * C_GROUP)
        s = ext_ref[:, cols]
        sh = 1
        while sh < w:
            s = s + pltpu.roll(s, sh, 0)
            sh *= 2
        cnt = jnp.minimum(tpos + 1, w).astype(F32)
        pooled = s[pad:, :] / cnt - u[:, cols]
        m = _dot(pooled.astype(BF16), wgrp_ref[gi]) * scale_ref[:, cols]
        mixed.append((m * _silu(gate[:, cols])).astype(BF16))
    y_ref[...] = x + _dot(jnp.concatenate(mixed, axis=1), wout_ref[...])


def _pool_layer(h, hist, g, win, wgrp, scale, wout, *, nb, t, tt, pos0):
    nt = t // tt
    pad = POOL_HIST + 1
    kern = functools.partial(_pool_kernel, tt=tt, pos0=pos0)
    return pl.pallas_call(
        kern,
        grid=(nb, nt),
        in_specs=[
            pl.BlockSpec((tt, D_MODEL), lambda b, i: (b * nt + i, 0)),
            pl.BlockSpec((1, pad, C_WIDTH), lambda b, i: (b, 0, 0)),
            pl.BlockSpec((1, D_MODEL), lambda b, i: (0, 0)),
            pl.BlockSpec((D_MODEL, 2 * C_WIDTH), lambda b, i: (0, 0)),
            pl.BlockSpec((len(POOL_SIZES), C_GROUP, C_GROUP), lambda b, i: (0, 0, 0)),
            pl.BlockSpec((1, C_WIDTH), lambda b, i: (0, 0)),
            pl.BlockSpec((C_WIDTH, D_MODEL), lambda b, i: (0, 0)),
        ],
        out_specs=[
            pl.BlockSpec((tt, D_MODEL), lambda b, i: (b * nt + i, 0)),
            pl.BlockSpec((1, pad, C_WIDTH), lambda b, i: (b, 0, 0)),
        ],
        out_shape=[
            jax.ShapeDtypeStruct((nb * t, D_MODEL), F32),
            jax.ShapeDtypeStruct((nb, pad, C_WIDTH), F32),
        ],
        scratch_shapes=[pltpu.VMEM((pad + tt, C_WIDTH), F32)],
        compiler_params=pltpu.CompilerParams(dimension_semantics=("arbitrary", "arbitrary"),
                                             vmem_limit_bytes=VMEM_LIMIT),
        name="pool_layer",
    )(h, hist, g, win, wgrp, scale, wout)


def _ab_weights(norm_g, w_in, q_norm, k_norm, sinks, conv_w, a_log, dt_bias, o_norm, w_out):
    aq, ak, av, ag, bqkv, bg, bbeta, balpha = jnp.split(
        w_in, [A_WIDTH, A_WIDTH + A_KV_WIDTH, A_WIDTH + 2 * A_KV_WIDTH, 2 * A_WIDTH + 2 * A_KV_WIDTH,
               2 * A_WIDTH + 2 * A_KV_WIDTH + 3 * B_WIDTH, 2 * A_WIDTH + 2 * A_KV_WIDTH + 4 * B_WIDTH,
               2 * A_WIDTH + 2 * A_KV_WIDTH + 4 * B_WIDTH + B_HEADS], axis=1)
    zpad = jnp.zeros((D_MODEL, LANES - 2 * B_HEADS), w_in.dtype)
    w = jnp.concatenate([aq, ag, bg, bqkv, ak, av, bbeta, balpha, zpad], axis=1).astype(BF16)
    lane_pad = (GATE_LANE, LANES - GATE_LANE - B_HEADS)
    gate = jnp.stack([jnp.pad(a_log.astype(F32), lane_pad), jnp.pad(dt_bias.astype(F32), lane_pad)])
    return dict(
        norm_g=norm_g.reshape(1, D_MODEL), w=w, kn=jnp.tile(k_norm, A_KV_HEADS).reshape(1, A_KV_WIDTH),
        qn=q_norm.reshape(1, A_HEAD_DIM), sinks=sinks.astype(F32), conv_w=conv_w, gate=gate,
        onorm=o_norm.reshape(1, B_HEAD_DIM), wa=w_out[:A_WIDTH].astype(BF16), wb=w_out[A_WIDTH:].astype(BF16))


def _ab_layer(h, wts, cache_k, cache_v, s0, conv_hist, *, nb, t):
    n = nb * t
    tm = min(n, 256)
    z = _in_proj(h, wts["norm_g"], wts["w"], wts["kn"], tm)
    if cache_k is None:
        tq, cq, lk, off = 256, CHUNK, WINDOW + CHUNK, 0
        kv_specs = [pl.BlockSpec((t, A_KV_WIDTH), lambda b, i: (b, Z_AK // A_KV_WIDTH)),
                    pl.BlockSpec((t, A_KV_WIDTH), lambda b, i: (b, Z_AV // A_KV_WIDTH))]
        kbuf = vbuf = z
        tc, c = 256, CHUNK
    else:
        tq, cq, lk, off = t, t, WINDOW + t, WINDOW
        k_new = z[:, Z_AK:Z_AK + A_KV_WIDTH].reshape(nb, t, A_KV_WIDTH)
        v_new = z[:, Z_AV:Z_AV + A_KV_WIDTH].reshape(nb, t, A_KV_WIDTH)
        kbuf = jnp.concatenate([cache_k.reshape(nb, WINDOW, A_KV_WIDTH), k_new], axis=1).reshape(nb * lk, A_KV_WIDTH)
        vbuf = jnp.concatenate([cache_v.reshape(nb, WINDOW, A_KV_WIDTH), v_new], axis=1).reshape(nb * lk, A_KV_WIDTH)
        kv_specs = [pl.BlockSpec((lk, A_KV_WIDTH), lambda b, i: (b, 0))] * 2
        tc, c = t, t
    o_a = _attention(z, kbuf, vbuf, kv_specs, wts["sinks"], wts["qn"], nb=nb, t=t, tq=tq, cq=cq, lk=lk, off=off)
    hist = jnp.pad(conv_hist.astype(F32), ((0, 0), (SUBLANES - (CONV_W - 1), 0), (0, 0)))
    o_b, s_new = _delta(z, hist, s0.astype(F32), wts["conv_w"], wts["gate"], wts["onorm"], nb=nb, t=t, tc=tc, c=c)
    y = _out_proj(h, o_a, o_b, wts["wa"], wts["wb"], tm)
    return y, z, s_new


def _cache_rows(z, col, nb, t, rows):
    return z[:, col:col + A_KV_WIDTH].reshape(nb, t, A_KV_HEADS, A_HEAD_DIM)[:, t - rows:]


def kernel(x_prompt, x_sample, cache_a_k, cache_a_v, state_b_s, state_b_conv, state_c_pool,
           norm_ab, w_in_ab, q_norm_a, k_norm_a, sinks_a, conv_b, a_log_b, dt_bias_b, o_norm_b, w_out_ab,
           norm_c, w_in_c, w_grp_c, scale_c, w_out_c):
    bp, tp, _ = x_prompt.shape
    bs, ts, _ = x_sample.shape
    hp = x_prompt.reshape(bp * tp, D_MODEL)
    hs = x_sample.reshape(bs * ts, D_MODEL)

    wts = _ab_weights(norm_ab[0], w_in_ab[0], q_norm_a[0], k_norm_a[0], sinks_a[0], conv_b[0], a_log_b[0],
                      dt_bias_b[0], o_norm_b[0], w_out_ab[0])
    s0 = jnp.zeros((bp, B_HEADS, B_HEAD_DIM, B_HEAD_DIM), F32)
    c0 = jnp.zeros((bp, CONV_W - 1, 3 * B_WIDTH), F32)
    hp, zp, sp = _ab_layer(hp, wts, None, None, s0, c0, nb=bp, t=tp)
    hs, zs, ss = _ab_layer(hs, wts, cache_a_k[0], cache_a_v[0], state_b_s[0], state_b_conv[0], nb=bs, t=ts)
    p_a_k = _cache_rows(zp, Z_AK, bp, tp, WINDOW)[None]
    p_a_v = _cache_rows(zp, Z_AV, bp, tp, WINDOW)[None]
    s_a_k = _cache_rows(zs, Z_AK, bs, ts, ts)[None]
    s_a_v = _cache_rows(zs, Z_AV, bs, ts, ts)[None]
    p_b_conv = zp[:, Z_BQKV:Z_BQKV + 3 * B_WIDTH].reshape(bp, tp, 3 * B_WIDTH)[:, tp - (CONV_W - 1):][None]
    s_b_conv = zs[:, Z_BQKV:Z_BQKV + 3 * B_WIDTH].reshape(bs, ts, 3 * B_WIDTH)[:, ts - (CONV_W - 1):][None]

    g_c = norm_c[0].reshape(1, D_MODEL)
    win = w_in_c[0].astype(BF16)
    wgrp = w_grp_c[0].astype(BF16)
    scale = scale_c[0].reshape(1, C_WIDTH)
    wout = w_out_c[0].astype(BF16)
    h0 = jnp.zeros((bp, POOL_HIST + 1, C_WIDTH), F32)
    hs0 = jnp.pad(state_c_pool[0].astype(F32), ((0, 0), (1, 0), (0, 0)))
    yp, tail_p = _pool_layer(hp, h0, g_c, win, wgrp, scale, wout, nb=bp, t=tp, tt=256, pos0=0)
    ys, tail_s = _pool_layer(hs, hs0, g_c, win, wgrp, scale, wout, nb=bs, t=ts, tt=ts, pos0=PAST_LEN)

    return (yp.reshape(bp, tp, D_MODEL), ys.reshape(bs, ts, D_MODEL),
            p_a_k, p_a_v, sp[None], p_b_conv, tail_p[:, 1:][None],
            s_a_k, s_a_v, ss[None], s_b_conv, tail_s[:, 1:][None])
```

```python
import functools

import jax
import jax.numpy as jnp
from jax import lax
from jax.experimental import pallas as pl
from jax.experimental.pallas import tpu as pltpu

F32 = jnp.float32
BF16 = jnp.bfloat16
HIGHEST = lax.Precision.HIGHEST

D_MODEL = 1024
CHUNK = 64
PAST_LEN = 2048
EPS = 1e-6
NEG_INF = -1e30

A_HEADS = 8
A_KV_HEADS = 2
A_HEAD_DIM = 64
A_WIDTH = A_HEADS * A_HEAD_DIM
A_KV_WIDTH = A_KV_HEADS * A_HEAD_DIM
A_REP = A_HEADS // A_KV_HEADS
WINDOW = 128

B_HEADS = 4
B_HEAD_DIM = 128
B_WIDTH = B_HEADS * B_HEAD_DIM
CONV_W = 4

POOL_SIZES = (2, 4, 8, 16)
C_WIDTH = D_MODEL
C_GROUP = C_WIDTH // len(POOL_SIZES)
POOL_HIST = max(POOL_SIZES) - 1

LANES = 128
SUBLANES = 8

Z_AQ = 0
Z_AG = A_WIDTH
Z_BG = 2 * A_WIDTH
Z_BQKV = 3 * A_WIDTH
Z_AK = Z_BQKV + 3 * B_WIDTH
Z_AV = Z_AK + A_KV_WIDTH
Z_BA = Z_AV + A_KV_WIDTH
Z_WIDTH = Z_BA + LANES
GATE_LANE = B_HEADS

VMEM_LIMIT = 48 * 1024 * 1024


def _sigmoid(x):
    return 1.0 / (1.0 + jnp.exp(-x))


def _silu(x):
    return x * _sigmoid(x)


def _softplus(x):
    return jnp.maximum(x, 0.0) + jnp.log(1.0 + jnp.exp(-jnp.abs(x)))


def _dot(a, b, precision=None):
    return jnp.dot(a, b, preferred_element_type=F32, precision=precision)


def _dot_nt(a, b, precision=None):
    return lax.dot_general(a, b, (((1,), (1,)), ((), ())), preferred_element_type=F32, precision=precision)


def _dot_tn(a, b, precision=None):
    return lax.dot_general(a, b, (((0,), (0,)), ((), ())), preferred_element_type=F32, precision=precision)


def _log2(n):
    assert n & (n - 1) == 0
    return n.bit_length() - 1


def _in_proj_kernel(x_ref, g_ref, w_ref, kn_ref, z_ref):
    x = x_ref[...]
    xn = x * lax.rsqrt(jnp.mean(x * x, axis=-1, keepdims=True) + EPS) * g_ref[...]
    z_ref[...] = _dot(xn.astype(BF16), w_ref[...])
    k = z_ref[:, Z_AK:Z_AK + A_KV_WIDTH]
    sq = k * k
    lane = lax.broadcasted_iota(jnp.int32, k.shape, 1)
    first = lane < A_HEAD_DIM
    s0 = jnp.sum(jnp.where(first, sq, 0.0), axis=-1, keepdims=True)
    s1 = jnp.sum(jnp.where(first, 0.0, sq), axis=-1, keepdims=True)
    ms = jnp.where(first, s0, s1) * (1.0 / A_HEAD_DIM)
    z_ref[:, Z_AK:Z_AK + A_KV_WIDTH] = k * lax.rsqrt(ms + EPS) * kn_ref[...]


def _in_proj(x, g, w, kn, tm):
    n = x.shape[0]
    return pl.pallas_call(
        _in_proj_kernel,
        grid=(n // tm,),
        in_specs=[
            pl.BlockSpec((tm, D_MODEL), lambda i: (i, 0)),
            pl.BlockSpec((1, D_MODEL), lambda i: (0, 0)),
            pl.BlockSpec((D_MODEL, Z_WIDTH), lambda i: (0, 0)),
            pl.BlockSpec((1, A_KV_WIDTH), lambda i: (0, 0)),
        ],
        out_specs=pl.BlockSpec((tm, Z_WIDTH), lambda i: (i, 0)),
        out_shape=jax.ShapeDtypeStruct((n, Z_WIDTH), F32),
        compiler_params=pltpu.CompilerParams(dimension_semantics=("arbitrary",), vmem_limit_bytes=VMEM_LIMIT),
        name="in_proj",
    )(x, g, w, kn)


def _attn_kernel(sinks_ref, q_ref, k_ref, v_ref, ag_ref, qn_ref, o_ref, *, cq, lk, off, tq):
    i = pl.program_id(1)
    n_sub = tq // cq
    rows = A_REP * cq
    qn = qn_ref[...]
    row1 = lax.broadcasted_iota(jnp.int32, (rows, 1), 0)
    rep = row1 >> _log2(cq)
    t = row1 & (cq - 1)
    col = lax.broadcasted_iota(jnp.int32, (rows, lk), 1)
    for cc in range(n_sub):
        cg = i * n_sub + cc
        kstart = pl.multiple_of(jnp.maximum(cg * cq + off - WINDOW, 0), SUBLANES)
        kk = k_ref[pl.ds(kstart, lk), :]
        vv = v_ref[pl.ds(kstart, lk), :]
        qc = q_ref[cc * cq:(cc + 1) * cq, :]
        kpos = kstart + col
        dist = jnp.abs(cg * cq + off + t - kpos).astype(F32)
        valid = kpos < (cg + 1) * cq + off
        outs = []
        for g in range(A_KV_HEADS):
            kg = kk[:, g * A_HEAD_DIM:(g + 1) * A_HEAD_DIM].astype(BF16)
            vg = vv[:, g * A_HEAD_DIM:(g + 1) * A_HEAD_DIM].astype(BF16)
            qs = []
            for r in range(A_REP):
                h = g * A_REP + r
                qh = qc[:, h * A_HEAD_DIM:(h + 1) * A_HEAD_DIM]
                qs.append(qh * lax.rsqrt(jnp.mean(qh * qh, axis=-1, keepdims=True) + EPS) * qn)
            qs = jnp.concatenate(qs, axis=0).astype(BF16)
            s = _dot_nt(qs, kg) * (A_HEAD_DIM ** -0.5)
            slope = jnp.zeros((rows, 1), F32)
            sk = jnp.zeros((rows, 1), F32)
            for r in range(A_REP):
                h = g * A_REP + r
                slope = jnp.where(rep == r, 2.0 ** (-8.0 * (h + 1) / A_HEADS), slope)
                sk = jnp.where(rep == r, sinks_ref[h], sk)
            s = s - slope * dist
            s = jnp.where(valid, s, NEG_INF)
            m = jnp.maximum(jnp.max(s, axis=-1, keepdims=True), sk)
            p = jnp.exp(s - m)
            den = jnp.sum(p, axis=-1, keepdims=True) + jnp.exp(sk - m)
            o = _dot(p.astype(BF16), vg) / den
            outs.extend(o[r * cq:(r + 1) * cq] for r in range(A_REP))
        ag = ag_ref[cc * cq:(cc + 1) * cq, :]
        o_ref[cc * cq:(cc + 1) * cq, :] = jnp.concatenate(outs, axis=1) * _silu(ag)


def _attention(z, kbuf, vbuf, kv_specs, sinks, qn, *, nb, t, tq, cq, lk, off):
    nt = t // tq
    kern = functools.partial(_attn_kernel, cq=cq, lk=lk, off=off, tq=tq)
    return pl.pallas_call(
        kern,
        grid=(nb, nt),
        in_specs=[
            pl.BlockSpec(memory_space=pltpu.SMEM),
            pl.BlockSpec((tq, A_WIDTH), lambda b, i: (b * nt + i, Z_AQ // A_WIDTH)),
            kv_specs[0],
            kv_specs[1],
            pl.BlockSpec((tq, A_WIDTH), lambda b, i: (b * nt + i, Z_AG // A_WIDTH)),
            pl.BlockSpec((1, A_HEAD_DIM), lambda b, i: (0, 0)),
        ],
        out_specs=pl.BlockSpec((tq, A_WIDTH), lambda b, i: (b * nt + i, 0)),
        out_shape=jax.ShapeDtypeStruct((nb * t, A_WIDTH), F32),
        compiler_params=pltpu.CompilerParams(dimension_semantics=("arbitrary", "arbitrary"),
                                             vmem_limit_bytes=VMEM_LIMIT),
        name="swa_attention",
    )(sinks, z, kbuf, vbuf, z, qn)


P_CUMSUM = "f32"
P_KK = "bf16x3"
P_INV = "bf16x3"
P_SOL = "bf16x3"
P_FOLD = "bf16x3"
P_STATE = "bf16x3"

_DIMS = {"nn": (((1,), (0,)), ((), ())), "nt": (((1,), (1,)), ((), ())), "tn": (((0,), (0,)), ((), ()))}


def _prep(a, mode):
    if mode == "f32":
        return (a,)
    hi = a.astype(BF16)
    if mode == "bf16":
        return (hi,)
    return (hi, (a - hi.astype(F32)).astype(BF16))


def _mm(ap, bp, dims="nn"):
    dn = _DIMS[dims]
    if ap[0].dtype == F32:
        return lax.dot_general(ap[0], bp[0], dn, precision=HIGHEST, preferred_element_type=F32)
    out = lax.dot_general(ap[0], bp[0], dn, preferred_element_type=F32)
    if len(ap) == 2:
        out = out + (lax.dot_general(ap[0], bp[1], dn, preferred_element_type=F32)
                     + lax.dot_general(ap[1], bp[0], dn, preferred_element_type=F32))
    return out


def _unit_lower_inverse(a_list, c, ii, jj):
    base = SUBLANES
    same = (ii >> _log2(base)) == (jj >> _log2(base))
    eye = jnp.where(ii == jj, 1.0, 0.0)
    n1 = [jnp.where(same, -a, 0.0) for a in a_list]
    x = [eye + n for n in n1]
    n1p = [_prep(n, P_INV) for n in n1]
    n2 = [_mm(p, p) for p in n1p]
    n2p = [_prep(n, P_INV) for n in n2]
    x = [xi + _mm(_prep(xi, P_INV), p) for xi, p in zip(x, n2p)]
    n4p = [_prep(_mm(p, p), P_INV) for p in n2p]
    x = [xi + _mm(_prep(xi, P_INV), p) for xi, p in zip(x, n4p)]
    s = base
    while s < c:
        sel = ((ii >> _log2(2 * s)) == (jj >> _log2(2 * s))) & ((ii >> _log2(s)) != (jj >> _log2(s)))
        xp = [_prep(xi, P_INV) for xi in x]
        xo = [_mm(p, _prep(jnp.where(sel, a, 0.0), P_INV)) for p, a in zip(xp, a_list)]
        x = [xi - _mm(_prep(o, P_INV), p) for xi, o, p in zip(x, xo, xp)]
        s *= 2
    return x


def _delta_kernel(qkv_ref, ba_ref, bg_ref, hist_ref, s0_ref, convw_ref, gate_ref, onorm_ref,
                  o_ref, sout_ref, ext_ref, st_ref, lhs_ref, add_ref, s_ref, *, c, tc):
    i = pl.program_id(1)
    hd = B_HEAD_DIM
    nh = B_HEADS
    r = nh * c
    nch = tc // c
    pad = SUBLANES

    @pl.when(i == 0)
    def _():
        ext_ref[0:pad, :] = hist_ref[0]
        s_ref[...] = s0_ref[0]

    @pl.when(i > 0)
    def _():
        ext_ref[0:pad, :] = ext_ref[tc:tc + pad, :]

    ext_ref[pad:pad + tc, :] = qkv_ref[...]

    for cc in range(nch):
        base = pad - (CONV_W - 1) + cc * c
        acc = ext_ref[base:base + c, :] * convw_ref[0:1, :]
        for j in range(1, CONV_W):
            acc = acc + ext_ref[base + j:base + j + c, :] * convw_ref[j:j + 1, :]
        y = _silu(acc)
        for part in range(3):
            for h in range(nh):
                blk = y[:, (part * nh + h) * hd:(part * nh + h + 1) * hd]
                if part < 2:
                    blk = blk * lax.rsqrt(jnp.sum(blk * blk, axis=-1, keepdims=True) + EPS)
                st_ref[part, cc, h * c:(h + 1) * c, :] = blk

    ii = lax.broadcasted_iota(jnp.int32, (r, r), 0)
    jj = lax.broadcasted_iota(jnp.int32, (r, r), 1)
    same_head = (ii >> _log2(c)) == (jj >> _log2(c))
    lower = same_head & (ii >= jj)
    strict = same_head & (ii > jj)
    ci = lax.broadcasted_iota(jnp.int32, (c, c), 0)
    cj = lax.broadcasted_iota(jnp.int32, (c, c), 1)
    tril_p = _prep(jnp.where(ci >= cj, 1.0, 0.0), P_CUMSUM)
    di = lax.broadcasted_iota(jnp.int32, (hd, hd), 0)
    dj = lax.broadcasted_iota(jnp.int32, (hd, hd), 1)
    eye_hd = di == dj
    neg_rate = -jnp.exp(gate_ref[0:1, :])
    dt_bias = gate_ref[1:2, :]
    onorm = onorm_ref[...]
    chunks = range(nch)

    def stack_cols(x, lane0):
        return jnp.concatenate([x[:, lane0 + h:lane0 + h + 1] for h in range(nh)], axis=0)

    beta, gc, gr, glast = [], [], [], []
    for cc in chunks:
        ba = ba_ref[cc * c:(cc + 1) * c, :]
        g_all = neg_rate * _softplus(ba + dt_bias)
        gcum = _mm(tril_p, _prep(g_all, P_CUMSUM))
        gcum_t = gcum.T
        beta.append(stack_cols(_sigmoid(ba), 0))
        gc.append(stack_cols(gcum, GATE_LANE))
        gr.append(jnp.concatenate([gcum_t[GATE_LANE + h:GATE_LANE + h + 1, :] for h in range(nh)], axis=1))
        glast.append(jnp.concatenate(
            [jnp.broadcast_to(gcum[c - 1:c, GATE_LANE + h:GATE_LANE + h + 1], (c, 1)) for h in range(nh)], axis=0))

    q = [st_ref[0, cc] * (hd ** -0.5) for cc in chunks]
    k = [st_ref[1, cc] for cc in chunks]
    v = [st_ref[2, cc] for cc in chunks]
    kb = [k[cc] * beta[cc] for cc in chunks]
    big = [_mm(_prep(jnp.concatenate([kb[cc], q[cc]], axis=0), P_KK), _prep(k[cc], P_KK), "nt") for cc in chunks]
    decay = [jnp.where(lower, jnp.exp(jnp.where(lower, gc[cc] - gr[cc], 0.0)), 0.0) for cc in chunks]
    a = [jnp.where(strict, big[cc][:r] * decay[cc], 0.0) for cc in chunks]
    qkd_p = [_prep(big[cc][r:] * decay[cc], P_FOLD) for cc in chunks]
    tinv = _unit_lower_inverse(a, c, ii, jj)
    eg = [jnp.exp(gc[cc]) for cc in chunks]
    rhs = [jnp.concatenate([v[cc] * beta[cc], kb[cc] * eg[cc]], axis=1) for cc in chunks]
    sol = [_mm(_prep(tinv[cc], P_SOL), _prep(rhs[cc], P_SOL)) for cc in chunks]
    fold = [_mm(qkd_p[cc], _prep(sol[cc], P_FOLD)) for cc in chunks]
    k_dec = [k[cc] * jnp.exp(glast[cc] - gc[cc]) for cc in chunks]
    for cc in chunks:
        qp = q[cc] * eg[cc] - fold[cc][:, hd:]
        for h in range(nh):
            rows = slice(h * c, (h + 1) * c)
            kt = _mm(_prep(k_dec[cc][rows], P_FOLD), _prep(sol[cc][rows], P_FOLD), "tn")
            g_tot = jnp.exp(glast[cc][h * c:h * c + 1, :])
            lhs_ref[cc, h, 0:hd, :] = jnp.where(eye_hd, g_tot, 0.0) - kt[:, hd:]
            lhs_ref[cc, h, hd:hd + c, :] = qp[rows]
            add_ref[cc, h, 0:hd, :] = kt[:, :hd]
            add_ref[cc, h, hd:hd + c, :] = fold[cc][rows, :hd]

    for cc in chunks:
        for h in range(nh):
            res = _mm(_prep(lhs_ref[cc, h], P_STATE), _prep(s_ref[h], P_STATE)) + add_ref[cc, h]
            s_ref[h] = res[:hd]
            o = res[hd:]
            on = o * lax.rsqrt(jnp.mean(o * o, axis=-1, keepdims=True) + EPS) * onorm
            bg = bg_ref[cc * c:(cc + 1) * c, h * hd:(h + 1) * hd]
            o_ref[cc * c:(cc + 1) * c, h * hd:(h + 1) * hd] = on * _silu(bg)
    sout_ref[0] = s_ref[...]


def _delta(z, hist, s0, convw, gate, onorm, *, nb, t, tc, c):
    nt = t // tc
    kern = functools.partial(_delta_kernel, c=c, tc=tc)
    return pl.pallas_call(
        kern,
        grid=(nb, nt),
        in_specs=[
            pl.BlockSpec((tc, 3 * B_WIDTH), lambda b, i: (b * nt + i, Z_BQKV // (3 * B_WIDTH))),
            pl.BlockSpec((tc, LANES), lambda b, i: (b * nt + i, Z_BA // LANES)),
            pl.BlockSpec((tc, B_WIDTH), lambda b, i: (b * nt + i, Z_BG // B_WIDTH)),
            pl.BlockSpec((1, SUBLANES, 3 * B_WIDTH), lambda b, i: (b, 0, 0)),
            pl.BlockSpec((1, B_HEADS, B_HEAD_DIM, B_HEAD_DIM), lambda b, i: (b, 0, 0, 0)),
            pl.BlockSpec((CONV_W, 3 * B_WIDTH), lambda b, i: (0, 0)),
            pl.BlockSpec((2, LANES), lambda b, i: (0, 0)),
            pl.BlockSpec((1, B_HEAD_DIM), lambda b, i: (0, 0)),
        ],
        out_specs=[
            pl.BlockSpec((tc, B_WIDTH), lambda b, i: (b * nt + i, 0)),
            pl.BlockSpec((1, B_HEADS, B_HEAD_DIM, B_HEAD_DIM), lambda b, i: (b, 0, 0, 0)),
        ],
        out_shape=[
            jax.ShapeDtypeStruct((nb * t, B_WIDTH), F32),
            jax.ShapeDtypeStruct((nb, B_HEADS, B_HEAD_DIM, B_HEAD_DIM), F32),
        ],
        scratch_shapes=[
            pltpu.VMEM((SUBLANES + tc, 3 * B_WIDTH), F32),
            pltpu.VMEM((3, tc // c, B_HEADS * c, B_HEAD_DIM), F32),
            pltpu.VMEM((tc // c, B_HEADS, B_HEAD_DIM + c, B_HEAD_DIM), F32),
            pltpu.VMEM((tc // c, B_HEADS, B_HEAD_DIM + c, B_HEAD_DIM), F32),
            pltpu.VMEM((B_HEADS, B_HEAD_DIM, B_HEAD_DIM), F32),
        ],
        compiler_params=pltpu.CompilerParams(dimension_semantics=("arbitrary", "arbitrary"),
                                             vmem_limit_bytes=VMEM_LIMIT),
        name="gated_delta",
    )(z, z, z, hist, s0, convw, gate, onorm)


def _out_proj_kernel(h_ref, oa_ref, ob_ref, wa_ref, wb_ref, y_ref):
    y_ref[...] = (h_ref[...] + _dot(oa_ref[...].astype(BF16), wa_ref[...])
                  + _dot(ob_ref[...].astype(BF16), wb_ref[...]))


def _out_proj(h, oa, ob, wa, wb, tm):
    n = h.shape[0]
    return pl.pallas_call(
        _out_proj_kernel,
        grid=(n // tm,),
        in_specs=[
            pl.BlockSpec((tm, D_MODEL), lambda i: (i, 0)),
            pl.BlockSpec((tm, A_WIDTH), lambda i: (i, 0)),
            pl.BlockSpec((tm, B_WIDTH), lambda i: (i, 0)),
            pl.BlockSpec((A_WIDTH, D_MODEL), lambda i: (0, 0)),
            pl.BlockSpec((B_WIDTH, D_MODEL), lambda i: (0, 0)),
        ],
        out_specs=pl.BlockSpec((tm, D_MODEL), lambda i: (i, 0)),
        out_shape=jax.ShapeDtypeStruct((n, D_MODEL), F32),
        compiler_params=pltpu.CompilerParams(dimension_semantics=("arbitrary",), vmem_limit_bytes=VMEM_LIMIT),
        name="out_proj",
    )(h, oa, ob, wa, wb)


def _pool_kernel(h_ref, hist_ref, g_ref, win_ref, wgrp_ref, scale_ref, wout_ref, y_ref, tail_ref, ext_ref,
                 *, tt, pos0):
    i = pl.program_id(1)
    pad = POOL_HIST + 1

    @pl.when(i == 0)
    def _():
        pos = pos0 - pad + lax.broadcasted_iota(jnp.int32, (pad, 1), 0)
        ext_ref[0:pad, :] = jnp.where(pos >= 0, hist_ref[0], 0.0)

    @pl.when(i > 0)
    def _():
        ext_ref[0:pad, :] = ext_ref[tt:tt + pad, :]

    x = h_ref[...]
    xn = x * lax.rsqrt(jnp.mean(x * x, axis=-1, keepdims=True) + EPS) * g_ref[...]
    z = _dot(xn.astype(BF16), win_ref[...])
    u = z[:, :C_WIDTH]
    gate = z[:, C_WIDTH:]
    ext_ref[pad:pad + tt, :] = u
    tail_ref[0] = u[tt - pad:, :]

    tpos = pos0 + i * tt + lax.broadcasted_iota(jnp.int32, (tt, 1), 0)
    mixed = []
    for gi, w in enumerate(POOL_SIZES):
        cols = slice(gi * C_GROUP, (gi + 1) * C_GROUP)
        s = ext_ref[:, cols]
        sh = 1
        while sh < w:
            s = s + pltpu.roll(s, sh, 0)
            sh *= 2
        cnt = jnp.minimum(tpos + 1, w).astype(F32)
        pooled = s[pad:, :] / cnt - u[:, cols]
        m = _dot(pooled.astype(BF16), wgrp_ref[gi]) * scale_ref[:, cols]
        mixed.append((m * _silu(gate[:, cols])).astype(BF16))
    y_ref[...] = x + _dot(jnp.concatenate(mixed, axis=1), wout_ref[...])


def _pool_layer(h, hist, g, win, wgrp, scale, wout, *, nb, t, tt, pos0):
    nt = t // tt
    pad = POOL_HIST + 1
    kern = functools.partial(_pool_kernel, tt=tt, pos0=pos0)
    return pl.pallas_call(
        kern,
        grid=(nb, nt),
        in_specs=[
            pl.BlockSpec((tt, D_MODEL), lambda b, i: (b * nt + i, 0)),
            pl.BlockSpec((1, pad, C_WIDTH), lambda b, i: (b, 0, 0)),
            pl.BlockSpec((1, D_MODEL), lambda b, i: (0, 0)),
            pl.BlockSpec((D_MODEL, 2 * C_WIDTH), lambda b, i: (0, 0)),
            pl.BlockSpec((len(POOL_SIZES), C_GROUP, C_GROUP), lambda b, i: (0, 0, 0)),
            pl.BlockSpec((1, C_WIDTH), lambda b, i: (0, 0)),
            pl.BlockSpec((C_WIDTH, D_MODEL), lambda b, i: (0, 0)),
        ],
        out_specs=[
            pl.BlockSpec((tt, D_MODEL), lambda b, i: (b * nt + i, 0)),
            pl.BlockSpec((1, pad, C_WIDTH), lambda b, i: (b, 0, 0)),
        ],
        out_shape=[
            jax.ShapeDtypeStruct((nb * t, D_MODEL), F32),
            jax.ShapeDtypeStruct((nb, pad, C_WIDTH), F32),
        ],
        scratch_shapes=[pltpu.VMEM((pad + tt, C_WIDTH), F32)],
        compiler_params=pltpu.CompilerParams(dimension_semantics=("arbitrary", "arbitrary"),
                                             vmem_limit_bytes=VMEM_LIMIT),
        name="pool_layer",
    )(h, hist, g, win, wgrp, scale, wout)


def _ab_weights(norm_g, w_in, q_norm, k_norm, sinks, conv_w, a_log, dt_bias, o_norm, w_out):
    aq, ak, av, ag, bqkv, bg, bbeta, balpha = jnp.split(
        w_in, [A_WIDTH, A_WIDTH + A_KV_WIDTH, A_WIDTH + 2 * A_KV_WIDTH, 2 * A_WIDTH + 2 * A_KV_WIDTH,
               2 * A_WIDTH + 2 * A_KV_WIDTH + 3 * B_WIDTH, 2 * A_WIDTH + 2 * A_KV_WIDTH + 4 * B_WIDTH,
               2 * A_WIDTH + 2 * A_KV_WIDTH + 4 * B_WIDTH + B_HEADS], axis=1)
    zpad = jnp.zeros((D_MODEL, LANES - 2 * B_HEADS), w_in.dtype)
    w = jnp.concatenate([aq, ag, bg, bqkv, ak, av, bbeta, balpha, zpad], axis=1).astype(BF16)
    lane_pad = (GATE_LANE, LANES - GATE_LANE - B_HEADS)
    gate = jnp.stack([jnp.pad(a_log.astype(F32), lane_pad), jnp.pad(dt_bias.astype(F32), lane_pad)])
    return dict(
        norm_g=norm_g.reshape(1, D_MODEL), w=w, kn=jnp.tile(k_norm, A_KV_HEADS).reshape(1, A_KV_WIDTH),
        qn=q_norm.reshape(1, A_HEAD_DIM), sinks=sinks.astype(F32), conv_w=conv_w, gate=gate,
        onorm=o_norm.reshape(1, B_HEAD_DIM), wa=w_out[:A_WIDTH].astype(BF16), wb=w_out[A_WIDTH:].astype(BF16))


def _ab_layer(h, wts, cache_k, cache_v, s0, conv_hist, *, nb, t):
    n = nb * t
    tm = min(n, 256)
    z = _in_proj(h, wts["norm_g"], wts["w"], wts["kn"], tm)
    if cache_k is None:
        tq, cq, lk, off = 256, CHUNK, WINDOW + CHUNK, 0
        kv_specs = [pl.BlockSpec((t, A_KV_WIDTH), lambda b, i: (b, Z_AK // A_KV_WIDTH)),
                    pl.BlockSpec((t, A_KV_WIDTH), lambda b, i: (b, Z_AV // A_KV_WIDTH))]
        kbuf = vbuf = z
        tc, c = 256, CHUNK
    else:
        tq, cq, lk, off = t, t, WINDOW + t, WINDOW
        k_new = z[:, Z_AK:Z_AK + A_KV_WIDTH].reshape(nb, t, A_KV_WIDTH)
        v_new = z[:, Z_AV:Z_AV + A_KV_WIDTH].reshape(nb, t, A_KV_WIDTH)
        kbuf = jnp.concatenate([cache_k.reshape(nb, WINDOW, A_KV_WIDTH), k_new], axis=1).reshape(nb * lk, A_KV_WIDTH)
        vbuf = jnp.concatenate([cache_v.reshape(nb, WINDOW, A_KV_WIDTH), v_new], axis=1).reshape(nb * lk, A_KV_WIDTH)
        kv_specs = [pl.BlockSpec((lk, A_KV_WIDTH), lambda b, i: (b, 0))] * 2
        tc, c = t, t
    o_a = _attention(z, kbuf, vbuf, kv_specs, wts["sinks"], wts["qn"], nb=nb, t=t, tq=tq, cq=cq, lk=lk, off=off)
    hist = jnp.pad(conv_hist.astype(F32), ((0, 0), (SUBLANES - (CONV_W - 1), 0), (0, 0)))
    o_b, s_new = _delta(z, hist, s0.astype(F32), wts["conv_w"], wts["gate"], wts["onorm"], nb=nb, t=t, tc=tc, c=c)
    y = _out_proj(h, o_a, o_b, wts["wa"], wts["wb"], tm)
    return y, z, s_new


def _cache_rows(z, col, nb, t, rows):
    return z[:, col:col + A_KV_WIDTH].reshape(nb, t, A_KV_HEADS, A_HEAD_DIM)[:, t - rows:]


def kernel(x_prompt, x_sample, cache_a_k, cache_a_v, state_b_s, state_b_conv, state_c_pool,
           norm_ab, w_in_ab, q_norm_a, k_norm_a, sinks_a, conv_b, a_log_b, dt_bias_b, o_norm_b, w_out_ab,
           norm_c, w_in_c, w_grp_c, scale_c, w_out_c):
    bp, tp, _ = x_prompt.shape
    bs, ts, _ = x_sample.shape
    hp = x_prompt.reshape(bp * tp, D_MODEL)
    hs = x_sample.reshape(bs * ts, D_MODEL)

    wts = _ab_weights(norm_ab[0], w_in_ab[0], q_norm_a[0], k_norm_a[0], sinks_a[0], conv_b[0], a_log_b[0],
                      dt_bias_b[0], o_norm_b[0], w_out_ab[0])
    s0 = jnp.zeros((bp, B_HEADS, B_HEAD_DIM, B_HEAD_DIM), F32)
    c0 = jnp.zeros((bp, CONV_W - 1, 3 * B_WIDTH), F32)
    hp, zp, sp = _ab_layer(hp, wts, None, None, s0, c0, nb=bp, t=tp)
    hs, zs, ss = _ab_layer(hs, wts, cache_a_k[0], cache_a_v[0], state_b_s[0], state_b_conv[0], nb=bs, t=ts)
    p_a_k = _cache_rows(zp, Z_AK, bp, tp, WINDOW)[None]
    p_a_v = _cache_rows(zp, Z_AV, bp, tp, WINDOW)[None]
    s_a_k = _cache_rows(zs, Z_AK, bs, ts, ts)[None]
    s_a_v = _cache_rows(zs, Z_AV, bs, ts, ts)[None]
    p_b_conv = zp[:, Z_BQKV:Z_BQKV + 3 * B_WIDTH].reshape(bp, tp, 3 * B_WIDTH)[:, tp - (CONV_W - 1):][None]
    s_b_conv = zs[:, Z_BQKV:Z_BQKV + 3 * B_WIDTH].reshape(bs, ts, 3 * B_WIDTH)[:, ts - (CONV_W - 1):][None]

    g_c = norm_c[0].reshape(1, D_MODEL)
    win = w_in_c[0].astype(BF16)
    wgrp = w_grp_c[0].astype(BF16)
    scale = scale_c[0].reshape(1, C_WIDTH)
    wout = w_out_c[0].astype(BF16)
    h0 = jnp.zeros((bp, POOL_HIST + 1, C_WIDTH), F32)
    hs0 = jnp.pad(state_c_pool[0].astype(F32), ((0, 0), (1, 0), (0, 0)))
    yp, tail_p = _pool_layer(hp, h0, g_c, win, wgrp, scale, wout, nb=bp, t=tp, tt=256, pos0=0)
    ys, tail_s = _pool_layer(hs, hs0, g_c, win, wgrp, scale, wout, nb=bs, t=ts, tt=ts, pos0=PAST_LEN)

    return (yp.reshape(bp, tp, D_MODEL), ys.reshape(bs, ts, D_MODEL),
            p_a_k, p_a_v, sp[None], p_b_conv, tail_p[:, 1:][None],
            s_a_k, s_a_v, ss[None], s_b_conv, tail_s[:, 1:][None])
```

```python
import functools

import jax
import jax.numpy as jnp
from jax import lax
from jax.experimental import pallas as pl
from jax.experimental.pallas import tpu as pltpu

F32 = jnp.float32
BF16 = jnp.bfloat16
HIGHEST = lax.Precision.HIGHEST

D_MODEL = 1024
CHUNK = 64
PAST_LEN = 2048
EPS = 1e-6
NEG_INF = -1e30

A_HEADS = 8
A_KV_HEADS = 2
A_HEAD_DIM = 64
A_WIDTH = A_HEADS * A_HEAD_DIM
A_KV_WIDTH = A_KV_HEADS * A_HEAD_DIM
A_REP = A_HEADS // A_KV_HEADS
WINDOW = 128
HEAD_ORDER = tuple(g * A_REP + r for r in range(A_REP) for g in range(A_KV_HEADS))

B_HEADS = 4
B_HEAD_DIM = 128
B_WIDTH = B_HEADS * B_HEAD_DIM
CONV_W = 4

POOL_SIZES = (2, 4, 8, 16)
C_WIDTH = D_MODEL
C_GROUP = C_WIDTH // len(POOL_SIZES)
POOL_HIST = max(POOL_SIZES) - 1

LANES = 128
SUBLANES = 8

Z_AQ = 0
Z_AG = A_WIDTH
Z_BG = 2 * A_WIDTH
Z_BQKV = 3 * A_WIDTH
Z_AK = Z_BQKV + 3 * B_WIDTH
Z_AV = Z_AK + A_KV_WIDTH
Z_BA = Z_AV + A_KV_WIDTH
Z_WIDTH = Z_BA + LANES
GATE_LANE = B_HEADS

VMEM_LIMIT = 48 * 1024 * 1024


def _sigmoid(x):
    return 1.0 / (1.0 + jnp.exp(-x))


def _silu(x):
    return x * _sigmoid(x)


def _softplus(x):
    return jnp.maximum(x, 0.0) + jnp.log(1.0 + jnp.exp(-jnp.abs(x)))


def _dot(a, b, precision=None):
    return jnp.dot(a, b, preferred_element_type=F32, precision=precision)


def _dot_nt(a, b, precision=None):
    return lax.dot_general(a, b, (((1,), (1,)), ((), ())), preferred_element_type=F32, precision=precision)


def _dot_tn(a, b, precision=None):
    return lax.dot_general(a, b, (((0,), (0,)), ((), ())), preferred_element_type=F32, precision=precision)


def _log2(n):
    assert n & (n - 1) == 0
    return n.bit_length() - 1


def _in_proj_kernel(x_ref, g_ref, w_ref, kn_ref, z_ref):
    x = x_ref[...]
    xn = x * lax.rsqrt(jnp.mean(x * x, axis=-1, keepdims=True) + EPS) * g_ref[...]
    z_ref[...] = _dot(xn.astype(BF16), w_ref[...])
    k = z_ref[:, Z_AK:Z_AK + A_KV_WIDTH]
    sq = k * k
    lane = lax.broadcasted_iota(jnp.int32, k.shape, 1)
    first = lane < A_HEAD_DIM
    s0 = jnp.sum(jnp.where(first, sq, 0.0), axis=-1, keepdims=True)
    s1 = jnp.sum(jnp.where(first, 0.0, sq), axis=-1, keepdims=True)
    ms = jnp.where(first, s0, s1) * (1.0 / A_HEAD_DIM)
    z_ref[:, Z_AK:Z_AK + A_KV_WIDTH] = k * lax.rsqrt(ms + EPS) * kn_ref[...]


def _in_proj(x, g, w, kn, tm):
    n = x.shape[0]
    return pl.pallas_call(
        _in_proj_kernel,
        grid=(n // tm,),
        in_specs=[
            pl.BlockSpec((tm, D_MODEL), lambda i: (i, 0)),
            pl.BlockSpec((1, D_MODEL), lambda i: (0, 0)),
            pl.BlockSpec((D_MODEL, Z_WIDTH), lambda i: (0, 0)),
            pl.BlockSpec((1, A_KV_WIDTH), lambda i: (0, 0)),
        ],
        out_specs=pl.BlockSpec((tm, Z_WIDTH), lambda i: (i, 0)),
        out_shape=jax.ShapeDtypeStruct((n, Z_WIDTH), F32),
        compiler_params=pltpu.CompilerParams(dimension_semantics=("arbitrary",), vmem_limit_bytes=VMEM_LIMIT),
        name="in_proj",
    )(x, g, w, kn)


def _attn_kernel(sinks_ref, q_ref, k_ref, v_ref, ag_ref, qn_ref, o_ref, kp_ref, vp_ref, bias_ref, *, cq, lk, off, t):
    rows = A_REP * cq
    front = WINDOW - off
    nvar = front // cq + 1
    lkt = k_ref.shape[0]
    log2e = 1.4426950408889634

    if front:
        kp_ref[0:front, :] = jnp.zeros((front, A_KV_WIDTH), BF16)
        vp_ref[0:front, :] = jnp.zeros((front, A_KV_WIDTH), BF16)
    kp_ref[front:front + lkt, :] = k_ref[...].astype(BF16)
    vp_ref[front:front + lkt, :] = v_ref[...].astype(BF16)

    row = lax.broadcasted_iota(jnp.int32, (rows, 1), 0)
    rep = row >> _log2(cq)

    @pl.when(pl.program_id(0) == 0)
    def _():
        col = lax.broadcasted_iota(jnp.int32, (rows, lk), 1)
        dist = jnp.abs((row & (cq - 1)) + WINDOW - col).astype(F32)
        for g in range(A_KV_HEADS):
            slope = jnp.zeros((rows, 1), F32)
            for r in range(A_REP):
                slope = jnp.where(rep == r, 2.0 ** (-8.0 * (g * A_REP + r + 1) / A_HEADS), slope)
            for var in range(nvar):
                bias_ref[var, g] = jnp.where(col >= front - var * cq, -slope * dist, NEG_INF) * log2e

    lane = lax.broadcasted_iota(jnp.int32, (cq, LANES), 1)
    first = lane < A_HEAD_DIM
    qn = qn_ref[...] * (A_HEAD_DIM ** -0.5 * log2e)
    sks = []
    for g in range(A_KV_HEADS):
        sk = jnp.zeros((rows, 1), F32)
        for r in range(A_REP):
            sk = jnp.where(rep == r, sinks_ref[g * A_REP + r] * log2e, sk)
        sks.append(sk)

    def chunk(cg, carry):
        r0 = pl.multiple_of(cg * cq, cq)
        qc = q_ref[pl.ds(r0, cq), :]
        kk = kp_ref[pl.ds(r0, lk), :]
        vv = vp_ref[pl.ds(r0, lk), :]
        var = jnp.minimum(cg, nvar - 1)
        blocks = []
        for r in range(A_REP):
            x = qc[:, r * LANES:(r + 1) * LANES]
            sq = x * x
            s_lo = jnp.sum(jnp.where(first, sq, 0.0), axis=-1, keepdims=True)
            s_hi = jnp.sum(jnp.where(first, 0.0, sq), axis=-1, keepdims=True)
            ms = jnp.where(first, s_lo, s_hi) * (1.0 / A_HEAD_DIM)
            blocks.append(x * lax.rsqrt(ms + EPS) * qn)
        outs = []
        for g in range(A_KV_HEADS):
            mine = first if g == 0 else jnp.logical_not(first)
            lhs = jnp.concatenate([jnp.where(mine, b, 0.0) for b in blocks], axis=0).astype(BF16)
            s = _dot_nt(lhs, kk) + bias_ref[var, g]
            m = jnp.maximum(jnp.max(s, axis=-1, keepdims=True), sks[g])
            p = jnp.exp2(s - m)
            den = jnp.sum(p, axis=-1, keepdims=True) + jnp.exp2(sks[g] - m)
            outs.append(_dot(p.astype(BF16), vv) * (1.0 / den))
        tile = jnp.concatenate(
            [jnp.where(first, outs[0][r * cq:(r + 1) * cq], outs[1][r * cq:(r + 1) * cq]) for r in range(A_REP)], axis=1)
        o_ref[pl.ds(r0, cq), :] = tile * _silu(ag_ref[pl.ds(r0, cq), :])
        return carry

    n_chunks = t // cq
    lax.fori_loop(0, n_chunks, chunk, 0, unroll=min(n_chunks, 4))


def _attention(z, kbuf, vbuf, kv_specs, sinks, qn, *, nb, t, cq, lk, off):
    front = WINDOW - off
    lkt = kv_specs[0].block_shape[0]
    kern = functools.partial(_attn_kernel, cq=cq, lk=lk, off=off, t=t)
    return pl.pallas_call(
        kern,
        grid=(nb,),
        in_specs=[
            pl.BlockSpec(memory_space=pltpu.SMEM),
            pl.BlockSpec((t, A_WIDTH), lambda b: (b, Z_AQ // A_WIDTH)),
            kv_specs[0],
            kv_specs[1],
            pl.BlockSpec((t, A_WIDTH), lambda b: (b, Z_AG // A_WIDTH)),
            pl.BlockSpec((1, LANES), lambda b: (0, 0)),
        ],
        out_specs=pl.BlockSpec((t, A_WIDTH), lambda b: (b, 0)),
        out_shape=jax.ShapeDtypeStruct((nb * t, A_WIDTH), F32),
        scratch_shapes=[
            pltpu.VMEM((front + lkt, A_KV_WIDTH), BF16),
            pltpu.VMEM((front + lkt, A_KV_WIDTH), BF16),
            pltpu.VMEM((front // cq + 1, A_KV_HEADS, A_REP * cq, lk), F32),
        ],
        compiler_params=pltpu.CompilerParams(dimension_semantics=("arbitrary",), vmem_limit_bytes=VMEM_LIMIT),
        name="swa_attention",
    )(sinks, z, kbuf, vbuf, z, qn)


P_CUMSUM = "f32"
P_KK = "bf16x3"
P_INV = "bf16x3"
P_SOL = "bf16x3"
P_FOLD = "bf16x3"
P_STATE = "bf16x3"

_DIMS = {"nn": (((1,), (0,)), ((), ())), "nt": (((1,), (1,)), ((), ())), "tn": (((0,), (0,)), ((), ()))}


def _prep(a, mode):
    if mode == "f32":
        return (a,)
    hi = a.astype(BF16)
    if mode == "bf16":
        return (hi,)
    return (hi, (a - hi.astype(F32)).astype(BF16))


def _mm(ap, bp, dims="nn"):
    dn = _DIMS[dims]
    if ap[0].dtype == F32:
        return lax.dot_general(ap[0], bp[0], dn, precision=HIGHEST, preferred_element_type=F32)
    out = lax.dot_general(ap[0], bp[0], dn, preferred_element_type=F32)
    if len(ap) == 2:
        out = out + (lax.dot_general(ap[0], bp[1], dn, preferred_element_type=F32)
                     + lax.dot_general(ap[1], bp[0], dn, preferred_element_type=F32))
    return out


def _unit_lower_inverse(a_list, c, ii, jj):
    base = SUBLANES
    same = (ii >> _log2(base)) == (jj >> _log2(base))
    eye = jnp.where(ii == jj, 1.0, 0.0)
    n1 = [jnp.where(same, -a, 0.0) for a in a_list]
    x = [eye + n for n in n1]
    n1p = [_prep(n, P_INV) for n in n1]
    n2 = [_mm(p, p) for p in n1p]
    n2p = [_prep(n, P_INV) for n in n2]
    x = [xi + _mm(_prep(xi, P_INV), p) for xi, p in zip(x, n2p)]
    n4p = [_prep(_mm(p, p), P_INV) for p in n2p]
    x = [xi + _mm(_prep(xi, P_INV), p) for xi, p in zip(x, n4p)]
    s = base
    while s < c:
        sel = ((ii >> _log2(2 * s)) == (jj >> _log2(2 * s))) & ((ii >> _log2(s)) != (jj >> _log2(s)))
        xp = [_prep(xi, P_INV) for xi in x]
        xo = [_mm(p, _prep(jnp.where(sel, a, 0.0), P_INV)) for p, a in zip(xp, a_list)]
        x = [xi - _mm(_prep(o, P_INV), p) for xi, o, p in zip(x, xo, xp)]
        s *= 2
    return x


def _delta_kernel(qkv_ref, ba_ref, bg_ref, hist_ref, s0_ref, convw_ref, gate_ref, onorm_ref,
                  o_ref, sout_ref, ext_ref, st_ref, lhs_ref, add_ref, s_ref, *, c, tc):
    i = pl.program_id(1)
    hd = B_HEAD_DIM
    nh = B_HEADS
    r = nh * c
    nch = tc // c
    pad = SUBLANES

    @pl.when(i == 0)
    def _():
        ext_ref[0:pad, :] = hist_ref[0]
        s_ref[...] = s0_ref[0]

    @pl.when(i > 0)
    def _():
        ext_ref[0:pad, :] = ext_ref[tc:tc + pad, :]

    ext_ref[pad:pad + tc, :] = qkv_ref[...]

    for cc in range(nch):
        base = pad - (CONV_W - 1) + cc * c
        acc = ext_ref[base:base + c, :] * convw_ref[0:1, :]
        for j in range(1, CONV_W):
            acc = acc + ext_ref[base + j:base + j + c, :] * convw_ref[j:j + 1, :]
        y = _silu(acc)
        for part in range(3):
            for h in range(nh):
                blk = y[:, (part * nh + h) * hd:(part * nh + h + 1) * hd]
                if part < 2:
                    blk = blk * lax.rsqrt(jnp.sum(blk * blk, axis=-1, keepdims=True) + EPS)
                st_ref[part, cc, h * c:(h + 1) * c, :] = blk

    ii = lax.broadcasted_iota(jnp.int32, (r, r), 0)
    jj = lax.broadcasted_iota(jnp.int32, (r, r), 1)
    same_head = (ii >> _log2(c)) == (jj >> _log2(c))
    lower = same_head & (ii >= jj)
    strict = same_head & (ii > jj)
    ci = lax.broadcasted_iota(jnp.int32, (c, c), 0)
    cj = lax.broadcasted_iota(jnp.int32, (c, c), 1)
    tril_p = _prep(jnp.where(ci >= cj, 1.0, 0.0), P_CUMSUM)
    di = lax.broadcasted_iota(jnp.int32, (hd, hd), 0)
    dj = lax.broadcasted_iota(jnp.int32, (hd, hd), 1)
    eye_hd = di == dj
    neg_rate = -jnp.exp(gate_ref[0:1, :])
    dt_bias = gate_ref[1:2, :]
    onorm = onorm_ref[...]
    chunks = range(nch)

    def stack_cols(x, lane0):
        return jnp.concatenate([x[:, lane0 + h:lane0 + h + 1] for h in range(nh)], axis=0)

    beta, gc, gr, glast = [], [], [], []
    for cc in chunks:
        ba = ba_ref[cc * c:(cc + 1) * c, :]
        g_all = neg_rate * _softplus(ba + dt_bias)
        gcum = _mm(tril_p, _prep(g_all, P_CUMSUM))
        gcum_t = gcum.T
        beta.append(stack_cols(_sigmoid(ba), 0))
        gc.append(stack_cols(gcum, GATE_LANE))
        gr.append(jnp.concatenate([gcum_t[GATE_LANE + h:GATE_LANE + h + 1, :] for h in range(nh)], axis=1))
        glast.append(jnp.concatenate(
            [jnp.broadcast_to(gcum[c - 1:c, GATE_LANE + h:GATE_LANE + h + 1], (c, 1)) for h in range(nh)], axis=0))

    q = [st_ref[0, cc] * (hd ** -0.5) for cc in chunks]
    k = [st_ref[1, cc] for cc in chunks]
    v = [st_ref[2, cc] for cc in chunks]
    kb = [k[cc] * beta[cc] for cc in chunks]
    big = [_mm(_prep(jnp.concatenate([kb[cc], q[cc]], axis=0), P_KK), _prep(k[cc], P_KK), "nt") for cc in chunks]
    decay = [jnp.where(lower, jnp.exp(jnp.where(lower, gc[cc] - gr[cc], 0.0)), 0.0) for cc in chunks]
    a = [jnp.where(strict, big[cc][:r] * decay[cc], 0.0) for cc in chunks]
    qkd_p = [_prep(big[cc][r:] * decay[cc], P_FOLD) for cc in chunks]
    tinv = _unit_lower_inverse(a, c, ii, jj)
    eg = [jnp.exp(gc[cc]) for cc in chunks]
    rhs = [jnp.concatenate([v[cc] * beta[cc], kb[cc] * eg[cc]], axis=1) for cc in chunks]
    sol = [_mm(_prep(tinv[cc], P_SOL), _prep(rhs[cc], P_SOL)) for cc in chunks]
    fold = [_mm(qkd_p[cc], _prep(sol[cc], P_FOLD)) for cc in chunks]
    k_dec = [k[cc] * jnp.exp(glast[cc] - gc[cc]) for cc in chunks]
    for cc in chunks:
        qp = q[cc] * eg[cc] - fold[cc][:, hd:]
        for h in range(nh):
            rows = slice(h * c, (h + 1) * c)
            kt = _mm(_prep(k_dec[cc][rows], P_FOLD), _prep(sol[cc][rows], P_FOLD), "tn")
            g_tot = jnp.exp(glast[cc][h * c:h * c + 1, :])
            lhs_ref[cc, h, 0:hd, :] = jnp.where(eye_hd, g_tot, 0.0) - kt[:, hd:]
            lhs_ref[cc, h, hd:hd + c, :] = qp[rows]
            add_ref[cc, h, 0:hd, :] = kt[:, :hd]
            add_ref[cc, h, hd:hd + c, :] = fold[cc][rows, :hd]

    for cc in chunks:
        for h in range(nh):
            res = _mm(_prep(lhs_ref[cc, h], P_STATE), _prep(s_ref[h], P_STATE)) + add_ref[cc, h]
            s_ref[h] = res[:hd]
            o = res[hd:]
            on = o * lax.rsqrt(jnp.mean(o * o, axis=-1, keepdims=True) + EPS) * onorm
            bg = bg_ref[cc * c:(cc + 1) * c, h * hd:(h + 1) * hd]
            o_ref[cc * c:(cc + 1) * c, h * hd:(h + 1) * hd] = on * _silu(bg)
    sout_ref[0] = s_ref[...]


def _delta(z, hist, s0, convw, gate, onorm, *, nb, t, tc, c):
    nt = t // tc
    kern = functools.partial(_delta_kernel, c=c, tc=tc)
    return pl.pallas_call(
        kern,
        grid=(nb, nt),
        in_specs=[
            pl.BlockSpec((tc, 3 * B_WIDTH), lambda b, i: (b * nt + i, Z_BQKV // (3 * B_WIDTH))),
            pl.BlockSpec((tc, LANES), lambda b, i: (b * nt + i, Z_BA // LANES)),
            pl.BlockSpec((tc, B_WIDTH), lambda b, i: (b * nt + i, Z_BG // B_WIDTH)),
            pl.BlockSpec((1, SUBLANES, 3 * B_WIDTH), lambda b, i: (b, 0, 0)),
            pl.BlockSpec((1, B_HEADS, B_HEAD_DIM, B_HEAD_DIM), lambda b, i: (b, 0, 0, 0)),
            pl.BlockSpec((CONV_W, 3 * B_WIDTH), lambda b, i: (0, 0)),
            pl.BlockSpec((2, LANES), lambda b, i: (0, 0)),
            pl.BlockSpec((1, B_HEAD_DIM), lambda b, i: (0, 0)),
        ],
        out_specs=[
            pl.BlockSpec((tc, B_WIDTH), lambda b, i: (b * nt + i, 0)),
            pl.BlockSpec((1, B_HEADS, B_HEAD_DIM, B_HEAD_DIM), lambda b, i: (b, 0, 0, 0)),
        ],
        out_shape=[
            jax.ShapeDtypeStruct((nb * t, B_WIDTH), F32),
            jax.ShapeDtypeStruct((nb, B_HEADS, B_HEAD_DIM, B_HEAD_DIM), F32),
        ],
        scratch_shapes=[
            pltpu.VMEM((SUBLANES + tc, 3 * B_WIDTH), F32),
            pltpu.VMEM((3, tc // c, B_HEADS * c, B_HEAD_DIM), F32),
            pltpu.VMEM((tc // c, B_HEADS, B_HEAD_DIM + c, B_HEAD_DIM), F32),
            pltpu.VMEM((tc // c, B_HEADS, B_HEAD_DIM + c, B_HEAD_DIM), F32),
            pltpu.VMEM((B_HEADS, B_HEAD_DIM, B_HEAD_DIM), F32),
        ],
        compiler_params=pltpu.CompilerParams(dimension_semantics=("arbitrary", "arbitrary"),
                                             vmem_limit_bytes=VMEM_LIMIT),
        name="gated_delta",
    )(z, z, z, hist, s0, convw, gate, onorm)


def _out_proj_kernel(h_ref, oa_ref, ob_ref, wa_ref, wb_ref, y_ref):
    y_ref[...] = (h_ref[...] + _dot(oa_ref[...].astype(BF16), wa_ref[...])
                  + _dot(ob_ref[...].astype(BF16), wb_ref[...]))


def _out_proj(h, oa, ob, wa, wb, tm):
    n = h.shape[0]
    return pl.pallas_call(
        _out_proj_kernel,
        grid=(n // tm,),
        in_specs=[
            pl.BlockSpec((tm, D_MODEL), lambda i: (i, 0)),
            pl.BlockSpec((tm, A_WIDTH), lambda i: (i, 0)),
            pl.BlockSpec((tm, B_WIDTH), lambda i: (i, 0)),
            pl.BlockSpec((A_WIDTH, D_MODEL), lambda i: (0, 0)),
            pl.BlockSpec((B_WIDTH, D_MODEL), lambda i: (0, 0)),
        ],
        out_specs=pl.BlockSpec((tm, D_MODEL), lambda i: (i, 0)),
        out_shape=jax.ShapeDtypeStruct((n, D_MODEL), F32),
        compiler_params=pltpu.CompilerParams(dimension_semantics=("arbitrary",), vmem_limit_bytes=VMEM_LIMIT),
        name="out_proj",
    )(h, oa, ob, wa, wb)


def _pool_kernel(h_ref, hist_ref, g_ref, win_ref, wgrp_ref, scale_ref, wout_ref, y_ref, tail_ref, ext_ref,
                 *, tt, pos0):
    i = pl.program_id(1)
    pad = POOL_HIST + 1

    @pl.when(i == 0)
    def _():
        pos = pos0 - pad + lax.broadcasted_iota(jnp.int32, (pad, 1), 0)
        ext_ref[0:pad, :] = jnp.where(pos >= 0, hist_ref[0], 0.0)

    @pl.when(i > 0)
    def _():
        ext_ref[0:pad, :] = ext_ref[tt:tt + pad, :]

    x = h_ref[...]
    xn = x * lax.rsqrt(jnp.mean(x * x, axis=-1, keepdims=True) + EPS) * g_ref[...]
    z = _dot(xn.astype(BF16), win_ref[...])
    u = z[:, :C_WIDTH]
    gate = z[:, C_WIDTH:]
    ext_ref[pad:pad + tt, :] = u
    tail_ref[0] = u[tt - pad:, :]

    tpos = pos0 + i * tt + lax.broadcasted_iota(jnp.int32, (tt, 1), 0)
    mixed = []
    for gi, w in enumerate(POOL_SIZES):
        cols = slice(gi * C_GROUP, (gi + 1) * C_GROUP)
        s = ext_ref[:, cols]
        sh = 1
        while sh < w:
            s = s + pltpu.roll(s, sh, 0)
            sh *= 2
        cnt = jnp.minimum(tpos + 1, w).astype(F32)
        pooled = s[pad:, :] / cnt - u[:, cols]
        m = _dot(pooled.astype(BF16), wgrp_ref[gi]) * scale_ref[:, cols]
        mixed.append((m * _silu(gate[:, cols])).astype(BF16))
    y_ref[...] = x + _dot(jnp.concatenate(mixed, axis=1), wout_ref[...])


def _pool_layer(h, hist, g, win, wgrp, scale, wout, *, nb, t, tt, pos0):
    nt = t // tt
    pad = POOL_HIST + 1
    kern = functools.partial(_pool_kernel, tt=tt, pos0=pos0)
    return pl.pallas_call(
        kern,
        grid=(nb, nt),
        in_specs=[
            pl.BlockSpec((tt, D_MODEL), lambda b, i: (b * nt + i, 0)),
            pl.BlockSpec((1, pad, C_WIDTH), lambda b, i: (b, 0, 0)),
            pl.BlockSpec((1, D_MODEL), lambda b, i: (0, 0)),
            pl.BlockSpec((D_MODEL, 2 * C_WIDTH), lambda b, i: (0, 0)),
            pl.BlockSpec((len(POOL_SIZES), C_GROUP, C_GROUP), lambda b, i: (0, 0, 0)),
            pl.BlockSpec((1, C_WIDTH), lambda b, i: (0, 0)),
            pl.BlockSpec((C_WIDTH, D_MODEL), lambda b, i: (0, 0)),
        ],
        out_specs=[
            pl.BlockSpec((tt, D_MODEL), lambda b, i: (b * nt + i, 0)),
            pl.BlockSpec((1, pad, C_WIDTH), lambda b, i: (b, 0, 0)),
        ],
        out_shape=[
            jax.ShapeDtypeStruct((nb * t, D_MODEL), F32),
            jax.ShapeDtypeStruct((nb, pad, C_WIDTH), F32),
        ],
        scratch_shapes=[pltpu.VMEM((pad + tt, C_WIDTH), F32)],
        compiler_params=pltpu.CompilerParams(dimension_semantics=("arbitrary", "arbitrary"),
                                             vmem_limit_bytes=VMEM_LIMIT),
        name="pool_layer",
    )(h, hist, g, win, wgrp, scale, wout)


def _ab_weights(norm_g, w_in, q_norm, k_norm, sinks, conv_w, a_log, dt_bias, o_norm, w_out):
    aq, ak, av, ag, bqkv, bg, bbeta, balpha = jnp.split(
        w_in, [A_WIDTH, A_WIDTH + A_KV_WIDTH, A_WIDTH + 2 * A_KV_WIDTH, 2 * A_WIDTH + 2 * A_KV_WIDTH,
               2 * A_WIDTH + 2 * A_KV_WIDTH + 3 * B_WIDTH, 2 * A_WIDTH + 2 * A_KV_WIDTH + 4 * B_WIDTH,
               2 * A_WIDTH + 2 * A_KV_WIDTH + 4 * B_WIDTH + B_HEADS], axis=1)
    zpad = jnp.zeros((D_MODEL, LANES - 2 * B_HEADS), w_in.dtype)
    order = jnp.array(HEAD_ORDER)
    aq = aq.reshape(D_MODEL, A_HEADS, A_HEAD_DIM)[:, order].reshape(D_MODEL, A_WIDTH)
    ag = ag.reshape(D_MODEL, A_HEADS, A_HEAD_DIM)[:, order].reshape(D_MODEL, A_WIDTH)
    wa = w_out[:A_WIDTH].reshape(A_HEADS, A_HEAD_DIM, D_MODEL)[order].reshape(A_WIDTH, D_MODEL)
    w = jnp.concatenate([aq, ag, bg, bqkv, ak, av, bbeta, balpha, zpad], axis=1).astype(BF16)
    lane_pad = (GATE_LANE, LANES - GATE_LANE - B_HEADS)
    gate = jnp.stack([jnp.pad(a_log.astype(F32), lane_pad), jnp.pad(dt_bias.astype(F32), lane_pad)])
    return dict(
        norm_g=norm_g.reshape(1, D_MODEL), w=w, kn=jnp.tile(k_norm, A_KV_HEADS).reshape(1, A_KV_WIDTH),
        qn=jnp.tile(q_norm, A_KV_HEADS).reshape(1, LANES), sinks=sinks.astype(F32), conv_w=conv_w, gate=gate,
        onorm=o_norm.reshape(1, B_HEAD_DIM), wa=wa.astype(BF16), wb=w_out[A_WIDTH:].astype(BF16))


def _ab_layer(h, wts, cache_k, cache_v, s0, conv_hist, *, nb, t):
    n = nb * t
    tm = min(n, 256)
    z = _in_proj(h, wts["norm_g"], wts["w"], wts["kn"], tm)
    if cache_k is None:
        cq, lk, off = CHUNK, WINDOW + CHUNK, 0
        kv_specs = [pl.BlockSpec((t, A_KV_WIDTH), lambda b: (b, Z_AK // A_KV_WIDTH)),
                    pl.BlockSpec((t, A_KV_WIDTH), lambda b: (b, Z_AV // A_KV_WIDTH))]
        kbuf = vbuf = z
        tc, c = 256, CHUNK
    else:
        cq, lk, off = t, WINDOW + t, WINDOW
        k_new = z[:, Z_AK:Z_AK + A_KV_WIDTH].reshape(nb, t, A_KV_WIDTH)
        v_new = z[:, Z_AV:Z_AV + A_KV_WIDTH].reshape(nb, t, A_KV_WIDTH)
        kbuf = jnp.concatenate([cache_k.reshape(nb, WINDOW, A_KV_WIDTH), k_new], axis=1).reshape(nb * lk, A_KV_WIDTH)
        vbuf = jnp.concatenate([cache_v.reshape(nb, WINDOW, A_KV_WIDTH), v_new], axis=1).reshape(nb * lk, A_KV_WIDTH)
        kv_specs = [pl.BlockSpec((lk, A_KV_WIDTH), lambda b: (b, 0))] * 2
        tc, c = t, t
    o_a = _attention(z, kbuf, vbuf, kv_specs, wts["sinks"], wts["qn"], nb=nb, t=t, cq=cq, lk=lk, off=off)
    hist = jnp.pad(conv_hist.astype(F32), ((0, 0), (SUBLANES - (CONV_W - 1), 0), (0, 0)))
    o_b, s_new = _delta(z, hist, s0.astype(F32), wts["conv_w"], wts["gate"], wts["onorm"], nb=nb, t=t, tc=tc, c=c)
    y = _out_proj(h, o_a, o_b, wts["wa"], wts["wb"], tm)
    return y, z, s_new


def _last_rows(z, col, width, nb, t, rows):
    return z.reshape(nb, t, Z_WIDTH)[:, t - rows:, col:col + width]


def _cache_rows(z, col, nb, t, rows):
    return _last_rows(z, col, A_KV_WIDTH, nb, t, rows).reshape(nb, rows, A_KV_HEADS, A_HEAD_DIM)


def kernel(x_prompt, x_sample, cache_a_k, cache_a_v, state_b_s, state_b_conv, state_c_pool,
           norm_ab, w_in_ab, q_norm_a, k_norm_a, sinks_a, conv_b, a_log_b, dt_bias_b, o_norm_b, w_out_ab,
           norm_c, w_in_c, w_grp_c, scale_c, w_out_c):
    bp, tp, _ = x_prompt.shape
    bs, ts, _ = x_sample.shape
    hp = x_prompt.reshape(bp * tp, D_MODEL)
    hs = x_sample.reshape(bs * ts, D_MODEL)

    wts = _ab_weights(norm_ab[0], w_in_ab[0], q_norm_a[0], k_norm_a[0], sinks_a[0], conv_b[0], a_log_b[0],
                      dt_bias_b[0], o_norm_b[0], w_out_ab[0])
    s0 = jnp.zeros((bp, B_HEADS, B_HEAD_DIM, B_HEAD_DIM), F32)
    c0 = jnp.zeros((bp, CONV_W - 1, 3 * B_WIDTH), F32)
    hp, zp, sp = _ab_layer(hp, wts, None, None, s0, c0, nb=bp, t=tp)
    hs, zs, ss = _ab_layer(hs, wts, cache_a_k[0], cache_a_v[0], state_b_s[0], state_b_conv[0], nb=bs, t=ts)
    p_a_k = _cache_rows(zp, Z_AK, bp, tp, WINDOW)[None]
    p_a_v = _cache_rows(zp, Z_AV, bp, tp, WINDOW)[None]
    s_a_k = _cache_rows(zs, Z_AK, bs, ts, ts)[None]
    s_a_v = _cache_rows(zs, Z_AV, bs, ts, ts)[None]
    p_b_conv = _last_rows(zp, Z_BQKV, 3 * B_WIDTH, bp, tp, CONV_W - 1)[None]
    s_b_conv = _last_rows(zs, Z_BQKV, 3 * B_WIDTH, bs, ts, CONV_W - 1)[None]

    g_c = norm_c[0].reshape(1, D_MODEL)
    win = w_in_c[0].astype(BF16)
    wgrp = w_grp_c[0].astype(BF16)
    scale = scale_c[0].reshape(1, C_WIDTH)
    wout = w_out_c[0].astype(BF16)
    h0 = jnp.zeros((bp, POOL_HIST + 1, C_WIDTH), F32)
    hs0 = jnp.pad(state_c_pool[0].astype(F32), ((0, 0), (1, 0), (0, 0)))
    yp, tail_p = _pool_layer(hp, h0, g_c, win, wgrp, scale, wout, nb=bp, t=tp, tt=256, pos0=0)
    ys, tail_s = _pool_layer(hs, hs0, g_c, win, wgrp, scale, wout, nb=bs, t=ts, tt=ts, pos0=PAST_LEN)

    return (yp.reshape(bp, tp, D_MODEL), ys.reshape(bs, ts, D_MODEL),
            p_a_k, p_a_v, sp[None], p_b_conv, tail_p[:, 1:][None],
            s_a_k, s_a_v, ss[None], s_b_conv, tail_s[:, 1:][None])
```

```python
import functools

import jax
import jax.numpy as jnp
from jax import lax
from jax.experimental import pallas as pl
from jax.experimental.pallas import tpu as pltpu

F32 = jnp.float32
BF16 = jnp.bfloat16
HIGHEST = lax.Precision.HIGHEST

D_MODEL = 1024
CHUNK = 64
PAST_LEN = 2048
EPS = 1e-6
NEG_INF = -1e30

A_HEADS = 8
A_KV_HEADS = 2
A_HEAD_DIM = 64
A_WIDTH = A_HEADS * A_HEAD_DIM
A_KV_WIDTH = A_KV_HEADS * A_HEAD_DIM
A_REP = A_HEADS // A_KV_HEADS
WINDOW = 128
HEAD_ORDER = tuple(g * A_REP + r for r in range(A_REP) for g in range(A_KV_HEADS))

B_HEADS = 4
B_HEAD_DIM = 128
B_WIDTH = B_HEADS * B_HEAD_DIM
CONV_W = 4

POOL_SIZES = (2, 4, 8, 16)
C_WIDTH = D_MODEL
C_GROUP = C_WIDTH // len(POOL_SIZES)
POOL_HIST = max(POOL_SIZES) - 1

LANES = 128
SUBLANES = 8

Z_AQ = 0
Z_AG = A_WIDTH
Z_BG = 2 * A_WIDTH
Z_BQKV = 3 * A_WIDTH
Z_AK = Z_BQKV + 3 * B_WIDTH
Z_AV = Z_AK + A_KV_WIDTH
Z_BA = Z_AV + A_KV_WIDTH
Z_WIDTH = Z_BA + LANES
GATE_LANE = B_HEADS

VMEM_LIMIT = 48 * 1024 * 1024


def _sigmoid(x):
    return 1.0 / (1.0 + jnp.exp(-x))


def _silu(x):
    return x * _sigmoid(x)


def _softplus(x):
    return jnp.maximum(x, 0.0) + jnp.log(1.0 + jnp.exp(-jnp.abs(x)))


def _dot(a, b, precision=None):
    return jnp.dot(a, b, preferred_element_type=F32, precision=precision)


def _dot_nt(a, b, precision=None):
    return lax.dot_general(a, b, (((1,), (1,)), ((), ())), preferred_element_type=F32, precision=precision)


def _dot_tn(a, b, precision=None):
    return lax.dot_general(a, b, (((0,), (0,)), ((), ())), preferred_element_type=F32, precision=precision)


def _log2(n):
    assert n & (n - 1) == 0
    return n.bit_length() - 1


def _in_proj_kernel(x_ref, g_ref, w_ref, kn_ref, z_ref):
    x = x_ref[...]
    xn = x * lax.rsqrt(jnp.mean(x * x, axis=-1, keepdims=True) + EPS) * g_ref[...]
    z_ref[...] = _dot(xn.astype(BF16), w_ref[...])
    k = z_ref[:, Z_AK:Z_AK + A_KV_WIDTH]
    sq = k * k
    lane = lax.broadcasted_iota(jnp.int32, k.shape, 1)
    first = lane < A_HEAD_DIM
    s0 = jnp.sum(jnp.where(first, sq, 0.0), axis=-1, keepdims=True)
    s1 = jnp.sum(jnp.where(first, 0.0, sq), axis=-1, keepdims=True)
    ms = jnp.where(first, s0, s1) * (1.0 / A_HEAD_DIM)
    z_ref[:, Z_AK:Z_AK + A_KV_WIDTH] = k * lax.rsqrt(ms + EPS) * kn_ref[...]


def _in_proj(x, g, w, kn, tm):
    n = x.shape[0]
    return pl.pallas_call(
        _in_proj_kernel,
        grid=(n // tm,),
        in_specs=[
            pl.BlockSpec((tm, D_MODEL), lambda i: (i, 0)),
            pl.BlockSpec((1, D_MODEL), lambda i: (0, 0)),
            pl.BlockSpec((D_MODEL, Z_WIDTH), lambda i: (0, 0)),
            pl.BlockSpec((1, A_KV_WIDTH), lambda i: (0, 0)),
        ],
        out_specs=pl.BlockSpec((tm, Z_WIDTH), lambda i: (i, 0)),
        out_shape=jax.ShapeDtypeStruct((n, Z_WIDTH), F32),
        compiler_params=pltpu.CompilerParams(dimension_semantics=("arbitrary",), vmem_limit_bytes=VMEM_LIMIT),
        name="in_proj",
    )(x, g, w, kn)


def _attn_kernel(sinks_ref, q_ref, k_ref, v_ref, ag_ref, qn_ref, o_ref, kp_ref, vp_ref, bias_ref, *, cq, lk, off, t):
    rows = A_REP * cq
    front = WINDOW - off
    nvar = front // cq + 1
    lkt = k_ref.shape[0]
    log2e = 1.4426950408889634

    if front:
        kp_ref[0:front, :] = jnp.zeros((front, A_KV_WIDTH), BF16)
        vp_ref[0:front, :] = jnp.zeros((front, A_KV_WIDTH), BF16)
    kp_ref[front:front + lkt, :] = k_ref[...].astype(BF16)
    vp_ref[front:front + lkt, :] = v_ref[...].astype(BF16)

    row = lax.broadcasted_iota(jnp.int32, (rows, 1), 0)
    rep = row >> _log2(cq)

    @pl.when(pl.program_id(0) == 0)
    def _():
        col = lax.broadcasted_iota(jnp.int32, (rows, lk), 1)
        dist = jnp.abs((row & (cq - 1)) + WINDOW - col).astype(F32)
        for g in range(A_KV_HEADS):
            slope = jnp.zeros((rows, 1), F32)
            for r in range(A_REP):
                slope = jnp.where(rep == r, 2.0 ** (-8.0 * (g * A_REP + r + 1) / A_HEADS), slope)
            for var in range(nvar):
                bias_ref[var, g] = jnp.where(col >= front - var * cq, -slope * dist, NEG_INF) * log2e

    lane = lax.broadcasted_iota(jnp.int32, (cq, LANES), 1)
    first = lane < A_HEAD_DIM
    qn = qn_ref[...] * (A_HEAD_DIM ** -0.5 * log2e)
    sks = []
    for g in range(A_KV_HEADS):
        sk = jnp.zeros((rows, 1), F32)
        for r in range(A_REP):
            sk = jnp.where(rep == r, sinks_ref[g * A_REP + r] * log2e, sk)
        sks.append(sk)

    def chunk(cg, carry):
        r0 = pl.multiple_of(cg * cq, cq)
        qc = q_ref[pl.ds(r0, cq), :]
        kk = kp_ref[pl.ds(r0, lk), :]
        vv = vp_ref[pl.ds(r0, lk), :]
        var = jnp.minimum(cg, nvar - 1)
        blocks = []
        for r in range(A_REP):
            x = qc[:, r * LANES:(r + 1) * LANES]
            sq = x * x
            s_lo = jnp.sum(jnp.where(first, sq, 0.0), axis=-1, keepdims=True)
            s_hi = jnp.sum(jnp.where(first, 0.0, sq), axis=-1, keepdims=True)
            ms = jnp.where(first, s_lo, s_hi) * (1.0 / A_HEAD_DIM)
            blocks.append(x * lax.rsqrt(ms + EPS) * qn)
        outs = []
        for g in range(A_KV_HEADS):
            mine = first if g == 0 else jnp.logical_not(first)
            lhs = jnp.concatenate([jnp.where(mine, b, 0.0) for b in blocks], axis=0).astype(BF16)
            s = _dot_nt(lhs, kk) + bias_ref[var, g]
            m = jnp.maximum(jnp.max(s, axis=-1, keepdims=True), sks[g])
            p = jnp.exp2(s - m)
            den = jnp.sum(p, axis=-1, keepdims=True) + jnp.exp2(sks[g] - m)
            outs.append(_dot(p.astype(BF16), vv) * (1.0 / den))
        tile = jnp.concatenate(
            [jnp.where(first, outs[0][r * cq:(r + 1) * cq], outs[1][r * cq:(r + 1) * cq]) for r in range(A_REP)], axis=1)
        o_ref[pl.ds(r0, cq), :] = tile * _silu(ag_ref[pl.ds(r0, cq), :])
        return carry

    n_chunks = t // cq
    lax.fori_loop(0, n_chunks, chunk, 0, unroll=min(n_chunks, 4))


def _attention(z, kbuf, vbuf, kv_specs, sinks, qn, *, nb, t, cq, lk, off):
    front = WINDOW - off
    lkt = kv_specs[0].block_shape[0]
    kern = functools.partial(_attn_kernel, cq=cq, lk=lk, off=off, t=t)
    return pl.pallas_call(
        kern,
        grid=(nb,),
        in_specs=[
            pl.BlockSpec(memory_space=pltpu.SMEM),
            pl.BlockSpec((t, A_WIDTH), lambda b: (b, Z_AQ // A_WIDTH)),
            kv_specs[0],
            kv_specs[1],
            pl.BlockSpec((t, A_WIDTH), lambda b: (b, Z_AG // A_WIDTH)),
            pl.BlockSpec((1, LANES), lambda b: (0, 0)),
        ],
        out_specs=pl.BlockSpec((t, A_WIDTH), lambda b: (b, 0)),
        out_shape=jax.ShapeDtypeStruct((nb * t, A_WIDTH), F32),
        scratch_shapes=[
            pltpu.VMEM((front + lkt, A_KV_WIDTH), BF16),
            pltpu.VMEM((front + lkt, A_KV_WIDTH), BF16),
            pltpu.VMEM((front // cq + 1, A_KV_HEADS, A_REP * cq, lk), F32),
        ],
        compiler_params=pltpu.CompilerParams(dimension_semantics=("arbitrary",), vmem_limit_bytes=VMEM_LIMIT),
        name="swa_attention",
    )(sinks, z, kbuf, vbuf, z, qn)


P_CUMSUM = "f32"
P_KK = "bf16"
P_INV = "bf16"
P_SOL = "bf16"
P_FOLD = "bf16"
P_STATE = "bf16"

_DIMS = {"nn": (((1,), (0,)), ((), ())), "nt": (((1,), (1,)), ((), ())), "tn": (((0,), (0,)), ((), ()))}


def _prep(a, mode):
    if mode == "f32":
        return (a,)
    hi = a.astype(BF16)
    if mode == "bf16":
        return (hi,)
    return (hi, (a - hi.astype(F32)).astype(BF16))


def _mm(ap, bp, dims="nn"):
    dn = _DIMS[dims]
    if ap[0].dtype == F32:
        return lax.dot_general(ap[0], bp[0], dn, precision=HIGHEST, preferred_element_type=F32)
    out = lax.dot_general(ap[0], bp[0], dn, preferred_element_type=F32)
    if len(ap) == 2:
        out = out + (lax.dot_general(ap[0], bp[1], dn, preferred_element_type=F32)
                     + lax.dot_general(ap[1], bp[0], dn, preferred_element_type=F32))
    return out


def _unit_lower_inverse(a_list, c, ii, jj):
    base = SUBLANES
    same = (ii >> _log2(base)) == (jj >> _log2(base))
    eye = jnp.where(ii == jj, 1.0, 0.0)
    n1 = [jnp.where(same, -a, 0.0) for a in a_list]
    x = [eye + n for n in n1]
    n1p = [_prep(n, P_INV) for n in n1]
    n2 = [_mm(p, p) for p in n1p]
    n2p = [_prep(n, P_INV) for n in n2]
    x = [xi + _mm(_prep(xi, P_INV), p) for xi, p in zip(x, n2p)]
    n4p = [_prep(_mm(p, p), P_INV) for p in n2p]
    x = [xi + _mm(_prep(xi, P_INV), p) for xi, p in zip(x, n4p)]
    s = base
    while s < c:
        sel = ((ii >> _log2(2 * s)) == (jj >> _log2(2 * s))) & ((ii >> _log2(s)) != (jj >> _log2(s)))
        xp = [_prep(xi, P_INV) for xi in x]
        xo = [_mm(p, _prep(jnp.where(sel, a, 0.0), P_INV)) for p, a in zip(xp, a_list)]
        x = [xi - _mm(_prep(o, P_INV), p) for xi, o, p in zip(x, xo, xp)]
        s *= 2
    return x


def _delta_kernel(qkv_ref, ba_ref, bg_ref, hist_ref, s0_ref, convw_ref, gate_ref, onorm_ref,
                  o_ref, sout_ref, ext_ref, st_ref, lhs_ref, add_ref, s_ref, *, c, tc):
    i = pl.program_id(1)
    hd = B_HEAD_DIM
    nh = B_HEADS
    r = nh * c
    nch = tc // c
    pad = SUBLANES

    @pl.when(i == 0)
    def _():
        ext_ref[0:pad, :] = hist_ref[0]
        s_ref[...] = s0_ref[0]

    @pl.when(i > 0)
    def _():
        ext_ref[0:pad, :] = ext_ref[tc:tc + pad, :]

    ext_ref[pad:pad + tc, :] = qkv_ref[...]

    for cc in range(nch):
        base = pad - (CONV_W - 1) + cc * c
        acc = ext_ref[base:base + c, :] * convw_ref[0:1, :]
        for j in range(1, CONV_W):
            acc = acc + ext_ref[base + j:base + j + c, :] * convw_ref[j:j + 1, :]
        y = _silu(acc)
        for part in range(3):
            for h in range(nh):
                blk = y[:, (part * nh + h) * hd:(part * nh + h + 1) * hd]
                if part < 2:
                    blk = blk * lax.rsqrt(jnp.sum(blk * blk, axis=-1, keepdims=True) + EPS)
                st_ref[part, cc, h * c:(h + 1) * c, :] = blk

    ii = lax.broadcasted_iota(jnp.int32, (r, r), 0)
    jj = lax.broadcasted_iota(jnp.int32, (r, r), 1)
    same_head = (ii >> _log2(c)) == (jj >> _log2(c))
    lower = same_head & (ii >= jj)
    strict = same_head & (ii > jj)
    ci = lax.broadcasted_iota(jnp.int32, (c, c), 0)
    cj = lax.broadcasted_iota(jnp.int32, (c, c), 1)
    tril_p = _prep(jnp.where(ci >= cj, 1.0, 0.0), P_CUMSUM)
    di = lax.broadcasted_iota(jnp.int32, (hd, hd), 0)
    dj = lax.broadcasted_iota(jnp.int32, (hd, hd), 1)
    eye_hd = di == dj
    neg_rate = -jnp.exp(gate_ref[0:1, :])
    dt_bias = gate_ref[1:2, :]
    onorm = onorm_ref[...]
    chunks = range(nch)

    def stack_cols(x, lane0):
        return jnp.concatenate([x[:, lane0 + h:lane0 + h + 1] for h in range(nh)], axis=0)

    beta, gc, gr, glast = [], [], [], []
    for cc in chunks:
        ba = ba_ref[cc * c:(cc + 1) * c, :]
        g_all = neg_rate * _softplus(ba + dt_bias)
        gcum = _mm(tril_p, _prep(g_all, P_CUMSUM))
        gcum_t = gcum.T
        beta.append(stack_cols(_sigmoid(ba), 0))
        gc.append(stack_cols(gcum, GATE_LANE))
        gr.append(jnp.concatenate([gcum_t[GATE_LANE + h:GATE_LANE + h + 1, :] for h in range(nh)], axis=1))
        glast.append(jnp.concatenate(
            [jnp.broadcast_to(gcum[c - 1:c, GATE_LANE + h:GATE_LANE + h + 1], (c, 1)) for h in range(nh)], axis=0))

    q = [st_ref[0, cc] * (hd ** -0.5) for cc in chunks]
    k = [st_ref[1, cc] for cc in chunks]
    v = [st_ref[2, cc] for cc in chunks]
    kb = [k[cc] * beta[cc] for cc in chunks]
    big = [_mm(_prep(jnp.concatenate([kb[cc], q[cc]], axis=0), P_KK), _prep(k[cc], P_KK), "nt") for cc in chunks]
    decay = [jnp.where(lower, jnp.exp(jnp.where(lower, gc[cc] - gr[cc], 0.0)), 0.0) for cc in chunks]
    a = [jnp.where(strict, big[cc][:r] * decay[cc], 0.0) for cc in chunks]
    qkd_p = [_prep(big[cc][r:] * decay[cc], P_FOLD) for cc in chunks]
    tinv = _unit_lower_inverse(a, c, ii, jj)
    eg = [jnp.exp(gc[cc]) for cc in chunks]
    rhs = [jnp.concatenate([v[cc] * beta[cc], kb[cc] * eg[cc]], axis=1) for cc in chunks]
    sol = [_mm(_prep(tinv[cc], P_SOL), _prep(rhs[cc], P_SOL)) for cc in chunks]
    fold = [_mm(qkd_p[cc], _prep(sol[cc], P_FOLD)) for cc in chunks]
    k_dec = [k[cc] * jnp.exp(glast[cc] - gc[cc]) for cc in chunks]
    for cc in chunks:
        qp = q[cc] * eg[cc] - fold[cc][:, hd:]
        for h in range(nh):
            rows = slice(h * c, (h + 1) * c)
            kt = _mm(_prep(k_dec[cc][rows], P_FOLD), _prep(sol[cc][rows], P_FOLD), "tn")
            g_tot = jnp.exp(glast[cc][h * c:h * c + 1, :])
            lhs_ref[cc, h, 0:hd, :] = jnp.where(eye_hd, g_tot, 0.0) - kt[:, hd:]
            lhs_ref[cc, h, hd:hd + c, :] = qp[rows]
            add_ref[cc, h, 0:hd, :] = kt[:, :hd]
            add_ref[cc, h, hd:hd + c, :] = fold[cc][rows, :hd]

    for cc in chunks:
        for h in range(nh):
            res = _mm(_prep(lhs_ref[cc, h], P_STATE), _prep(s_ref[h], P_STATE)) + add_ref[cc, h]
            s_ref[h] = res[:hd]
            o = res[hd:]
            on = o * lax.rsqrt(jnp.mean(o * o, axis=-1, keepdims=True) + EPS) * onorm
            bg = bg_ref[cc * c:(cc + 1) * c, h * hd:(h + 1) * hd]
            o_ref[cc * c:(cc + 1) * c, h * hd:(h + 1) * hd] = on * _silu(bg)
    sout_ref[0] = s_ref[...]


def _delta(z, hist, s0, convw, gate, onorm, *, nb, t, tc, c):
    nt = t // tc
    kern = functools.partial(_delta_kernel, c=c, tc=tc)
    return pl.pallas_call(
        kern,
        grid=(nb, nt),
        in_specs=[
            pl.BlockSpec((tc, 3 * B_WIDTH), lambda b, i: (b * nt + i, Z_BQKV // (3 * B_WIDTH))),
            pl.BlockSpec((tc, LANES), lambda b, i: (b * nt + i, Z_BA // LANES)),
            pl.BlockSpec((tc, B_WIDTH), lambda b, i: (b * nt + i, Z_BG // B_WIDTH)),
            pl.BlockSpec((1, SUBLANES, 3 * B_WIDTH), lambda b, i: (b, 0, 0)),
            pl.BlockSpec((1, B_HEADS, B_HEAD_DIM, B_HEAD_DIM), lambda b, i: (b, 0, 0, 0)),
            pl.BlockSpec((CONV_W, 3 * B_WIDTH), lambda b, i: (0, 0)),
            pl.BlockSpec((2, LANES), lambda b, i: (0, 0)),
            pl.BlockSpec((1, B_HEAD_DIM), lambda b, i: (0, 0)),
        ],
        out_specs=[
            pl.BlockSpec((tc, B_WIDTH), lambda b, i: (b * nt + i, 0)),
            pl.BlockSpec((1, B_HEADS, B_HEAD_DIM, B_HEAD_DIM), lambda b, i: (b, 0, 0, 0)),
        ],
        out_shape=[
            jax.ShapeDtypeStruct((nb * t, B_WIDTH), F32),
            jax.ShapeDtypeStruct((nb, B_HEADS, B_HEAD_DIM, B_HEAD_DIM), F32),
        ],
        scratch_shapes=[
            pltpu.VMEM((SUBLANES + tc, 3 * B_WIDTH), F32),
            pltpu.VMEM((3, tc // c, B_HEADS * c, B_HEAD_DIM), F32),
            pltpu.VMEM((tc // c, B_HEADS, B_HEAD_DIM + c, B_HEAD_DIM), F32),
            pltpu.VMEM((tc // c, B_HEADS, B_HEAD_DIM + c, B_HEAD_DIM), F32),
            pltpu.VMEM((B_HEADS, B_HEAD_DIM, B_HEAD_DIM), F32),
        ],
        compiler_params=pltpu.CompilerParams(dimension_semantics=("arbitrary", "arbitrary"),
                                             vmem_limit_bytes=VMEM_LIMIT),
        name="gated_delta",
    )(z, z, z, hist, s0, convw, gate, onorm)


def _out_proj_kernel(h_ref, oa_ref, ob_ref, wa_ref, wb_ref, y_ref):
    y_ref[...] = (h_ref[...] + _dot(oa_ref[...].astype(BF16), wa_ref[...])
                  + _dot(ob_ref[...].astype(BF16), wb_ref[...]))


def _out_proj(h, oa, ob, wa, wb, tm):
    n = h.shape[0]
    return pl.pallas_call(
        _out_proj_kernel,
        grid=(n // tm,),
        in_specs=[
            pl.BlockSpec((tm, D_MODEL), lambda i: (i, 0)),
            pl.BlockSpec((tm, A_WIDTH), lambda i: (i, 0)),
            pl.BlockSpec((tm, B_WIDTH), lambda i: (i, 0)),
            pl.BlockSpec((A_WIDTH, D_MODEL), lambda i: (0, 0)),
            pl.BlockSpec((B_WIDTH, D_MODEL), lambda i: (0, 0)),
        ],
        out_specs=pl.BlockSpec((tm, D_MODEL), lambda i: (i, 0)),
        out_shape=jax.ShapeDtypeStruct((n, D_MODEL), F32),
        compiler_params=pltpu.CompilerParams(dimension_semantics=("arbitrary",), vmem_limit_bytes=VMEM_LIMIT),
        name="out_proj",
    )(h, oa, ob, wa, wb)


def _pool_kernel(h_ref, hist_ref, g_ref, win_ref, wgrp_ref, scale_ref, wout_ref, y_ref, tail_ref, ext_ref,
                 *, tt, pos0):
    i = pl.program_id(1)
    pad = POOL_HIST + 1

    @pl.when(i == 0)
    def _():
        pos = pos0 - pad + lax.broadcasted_iota(jnp.int32, (pad, 1), 0)
        ext_ref[0:pad, :] = jnp.where(pos >= 0, hist_ref[0], 0.0)

    @pl.when(i > 0)
    def _():
        ext_ref[0:pad, :] = ext_ref[tt:tt + pad, :]

    x = h_ref[...]
    xn = x * lax.rsqrt(jnp.mean(x * x, axis=-1, keepdims=True) + EPS) * g_ref[...]
    z = _dot(xn.astype(BF16), win_ref[...])
    u = z[:, :C_WIDTH]
    gate = z[:, C_WIDTH:]
    ext_ref[pad:pad + tt, :] = u
    tail_ref[0] = u[tt - pad:, :]

    tpos = pos0 + i * tt + lax.broadcasted_iota(jnp.int32, (tt, 1), 0)
    mixed = []
    for gi, w in enumerate(POOL_SIZES):
        cols = slice(gi * C_GROUP, (gi + 1) * C_GROUP)
        s = ext_ref[:, cols]
        sh = 1
        while sh < w:
            s = s + pltpu.roll(s, sh, 0)
            sh *= 2
        cnt = jnp.minimum(tpos + 1, w).astype(F32)
        pooled = s[pad:, :] / cnt - u[:, cols]
        m = _dot(pooled.astype(BF16), wgrp_ref[gi]) * scale_ref[:, cols]
        mixed.append((m * _silu(gate[:, cols])).astype(BF16))
    y_ref[...] = x + _dot(jnp.concatenate(mixed, axis=1), wout_ref[...])


def _pool_layer(h, hist, g, win, wgrp, scale, wout, *, nb, t, tt, pos0):
    nt = t // tt
    pad = POOL_HIST + 1
    kern = functools.partial(_pool_kernel, tt=tt, pos0=pos0)
    return pl.pallas_call(
        kern,
        grid=(nb, nt),
        in_specs=[
            pl.BlockSpec((tt, D_MODEL), lambda b, i: (b * nt + i, 0)),
            pl.BlockSpec((1, pad, C_WIDTH), lambda b, i: (b, 0, 0)),
            pl.BlockSpec((1, D_MODEL), lambda b, i: (0, 0)),
            pl.BlockSpec((D_MODEL, 2 * C_WIDTH), lambda b, i: (0, 0)),
            pl.BlockSpec((len(POOL_SIZES), C_GROUP, C_GROUP), lambda b, i: (0, 0, 0)),
            pl.BlockSpec((1, C_WIDTH), lambda b, i: (0, 0)),
            pl.BlockSpec((C_WIDTH, D_MODEL), lambda b, i: (0, 0)),
        ],
        out_specs=[
            pl.BlockSpec((tt, D_MODEL), lambda b, i: (b * nt + i, 0)),
            pl.BlockSpec((1, pad, C_WIDTH), lambda b, i: (b, 0, 0)),
        ],
        out_shape=[
            jax.ShapeDtypeStruct((nb * t, D_MODEL), F32),
            jax.ShapeDtypeStruct((nb, pad, C_WIDTH), F32),
        ],
        scratch_shapes=[pltpu.VMEM((pad + tt, C_WIDTH), F32)],
        compiler_params=pltpu.CompilerParams(dimension_semantics=("arbitrary", "arbitrary"),
                                             vmem_limit_bytes=VMEM_LIMIT),
        name="pool_layer",
    )(h, hist, g, win, wgrp, scale, wout)


def _ab_weights(norm_g, w_in, q_norm, k_norm, sinks, conv_w, a_log, dt_bias, o_norm, w_out):
    aq, ak, av, ag, bqkv, bg, bbeta, balpha = jnp.split(
        w_in, [A_WIDTH, A_WIDTH + A_KV_WIDTH, A_WIDTH + 2 * A_KV_WIDTH, 2 * A_WIDTH + 2 * A_KV_WIDTH,
               2 * A_WIDTH + 2 * A_KV_WIDTH + 3 * B_WIDTH, 2 * A_WIDTH + 2 * A_KV_WIDTH + 4 * B_WIDTH,
               2 * A_WIDTH + 2 * A_KV_WIDTH + 4 * B_WIDTH + B_HEADS], axis=1)
    zpad = jnp.zeros((D_MODEL, LANES - 2 * B_HEADS), w_in.dtype)
    order = jnp.array(HEAD_ORDER)
    aq = aq.reshape(D_MODEL, A_HEADS, A_HEAD_DIM)[:, order].reshape(D_MODEL, A_WIDTH)
    ag = ag.reshape(D_MODEL, A_HEADS, A_HEAD_DIM)[:, order].reshape(D_MODEL, A_WIDTH)
    wa = w_out[:A_WIDTH].reshape(A_HEADS, A_HEAD_DIM, D_MODEL)[order].reshape(A_WIDTH, D_MODEL)
    w = jnp.concatenate([aq, ag, bg, bqkv, ak, av, bbeta, balpha, zpad], axis=1).astype(BF16)
    lane_pad = (GATE_LANE, LANES - GATE_LANE - B_HEADS)
    gate = jnp.stack([jnp.pad(a_log.astype(F32), lane_pad), jnp.pad(dt_bias.astype(F32), lane_pad)])
    return dict(
        norm_g=norm_g.reshape(1, D_MODEL), w=w, kn=jnp.tile(k_norm, A_KV_HEADS).reshape(1, A_KV_WIDTH),
        qn=jnp.tile(q_norm, A_KV_HEADS).reshape(1, LANES), sinks=sinks.astype(F32), conv_w=conv_w, gate=gate,
        onorm=o_norm.reshape(1, B_HEAD_DIM), wa=wa.astype(BF16), wb=w_out[A_WIDTH:].astype(BF16))


def _ab_layer(h, wts, cache_k, cache_v, s0, conv_hist, *, nb, t):
    n = nb * t
    tm = min(n, 256)
    z = _in_proj(h, wts["norm_g"], wts["w"], wts["kn"], tm)
    if cache_k is None:
        cq, lk, off = CHUNK, WINDOW + CHUNK, 0
        kv_specs = [pl.BlockSpec((t, A_KV_WIDTH), lambda b: (b, Z_AK // A_KV_WIDTH)),
                    pl.BlockSpec((t, A_KV_WIDTH), lambda b: (b, Z_AV // A_KV_WIDTH))]
        kbuf = vbuf = z
        tc, c = 256, CHUNK
    else:
        cq, lk, off = t, WINDOW + t, WINDOW
        k_new = z[:, Z_AK:Z_AK + A_KV_WIDTH].reshape(nb, t, A_KV_WIDTH)
        v_new = z[:, Z_AV:Z_AV + A_KV_WIDTH].reshape(nb, t, A_KV_WIDTH)
        kbuf = jnp.concatenate([cache_k.reshape(nb, WINDOW, A_KV_WIDTH), k_new], axis=1).reshape(nb * lk, A_KV_WIDTH)
        vbuf = jnp.concatenate([cache_v.reshape(nb, WINDOW, A_KV_WIDTH), v_new], axis=1).reshape(nb * lk, A_KV_WIDTH)
        kv_specs = [pl.BlockSpec((lk, A_KV_WIDTH), lambda b: (b, 0))] * 2
        tc, c = t, t
    o_a = _attention(z, kbuf, vbuf, kv_specs, wts["sinks"], wts["qn"], nb=nb, t=t, cq=cq, lk=lk, off=off)
    hist = jnp.pad(conv_hist.astype(F32), ((0, 0), (SUBLANES - (CONV_W - 1), 0), (0, 0)))
    o_b, s_new = _delta(z, hist, s0.astype(F32), wts["conv_w"], wts["gate"], wts["onorm"], nb=nb, t=t, tc=tc, c=c)
    y = _out_proj(h, o_a, o_b, wts["wa"], wts["wb"], tm)
    return y, z, s_new


def _last_rows(z, col, width, nb, t, rows):
    return z.reshape(nb, t, Z_WIDTH)[:, t - rows:, col:col + width]


def _cache_rows(z, col, nb, t, rows):
    return _last_rows(z, col, A_KV_WIDTH, nb, t, rows).reshape(nb, rows, A_KV_HEADS, A_HEAD_DIM)


def kernel(x_prompt, x_sample, cache_a_k, cache_a_v, state_b_s, state_b_conv, state_c_pool,
           norm_ab, w_in_ab, q_norm_a, k_norm_a, sinks_a, conv_b, a_log_b, dt_bias_b, o_norm_b, w_out_ab,
           norm_c, w_in_c, w_grp_c, scale_c, w_out_c):
    bp, tp, _ = x_prompt.shape
    bs, ts, _ = x_sample.shape
    hp = x_prompt.reshape(bp * tp, D_MODEL)
    hs = x_sample.reshape(bs * ts, D_MODEL)

    wts = _ab_weights(norm_ab[0], w_in_ab[0], q_norm_a[0], k_norm_a[0], sinks_a[0], conv_b[0], a_log_b[0],
                      dt_bias_b[0], o_norm_b[0], w_out_ab[0])
    s0 = jnp.zeros((bp, B_HEADS, B_HEAD_DIM, B_HEAD_DIM), F32)
    c0 = jnp.zeros((bp, CONV_W - 1, 3 * B_WIDTH), F32)
    hp, zp, sp = _ab_layer(hp, wts, None, None, s0, c0, nb=bp, t=tp)
    hs, zs, ss = _ab_layer(hs, wts, cache_a_k[0], cache_a_v[0], state_b_s[0], state_b_conv[0], nb=bs, t=ts)
    p_a_k = _cache_rows(zp, Z_AK, bp, tp, WINDOW)[None]
    p_a_v = _cache_rows(zp, Z_AV, bp, tp, WINDOW)[None]
    s_a_k = _cache_rows(zs, Z_AK, bs, ts, ts)[None]
    s_a_v = _cache_rows(zs, Z_AV, bs, ts, ts)[None]
    p_b_conv = _last_rows(zp, Z_BQKV, 3 * B_WIDTH, bp, tp, CONV_W - 1)[None]
    s_b_conv = _last_rows(zs, Z_BQKV, 3 * B_WIDTH, bs, ts, CONV_W - 1)[None]

    g_c = norm_c[0].reshape(1, D_MODEL)
    win = w_in_c[0].astype(BF16)
    wgrp = w_grp_c[0].astype(BF16)
    scale = scale_c[0].reshape(1, C_WIDTH)
    wout = w_out_c[0].astype(BF16)
    h0 = jnp.zeros((bp, POOL_HIST + 1, C_WIDTH), F32)
    hs0 = jnp.pad(state_c_pool[0].astype(F32), ((0, 0), (1, 0), (0, 0)))
    yp, tail_p = _pool_layer(hp, h0, g_c, win, wgrp, scale, wout, nb=bp, t=tp, tt=256, pos0=0)
    ys, tail_s = _pool_layer(hs, hs0, g_c, win, wgrp, scale, wout, nb=bs, t=ts, tt=ts, pos0=PAST_LEN)

    return (yp.reshape(bp, tp, D_MODEL), ys.reshape(bs, ts, D_MODEL),
            p_a_k, p_a_v, sp[None], p_b_conv, tail_p[:, 1:][None],
            s_a_k, s_a_v, ss[None], s_b_conv, tail_s[:, 1:][None])
```

```python
import functools

import jax
import jax.numpy as jnp
from jax import lax
from jax.experimental import pallas as pl
from jax.experimental.pallas import tpu as pltpu

F32 = jnp.float32
BF16 = jnp.bfloat16
HIGHEST = lax.Precision.HIGHEST

D_MODEL = 1024
CHUNK = 64
PAST_LEN = 2048
EPS = 1e-6
NEG_INF = -1e30

A_HEADS = 8
A_KV_HEADS = 2
A_HEAD_DIM = 64
A_WIDTH = A_HEADS * A_HEAD_DIM
A_KV_WIDTH = A_KV_HEADS * A_HEAD_DIM
A_REP = A_HEADS // A_KV_HEADS
WINDOW = 128
HEAD_ORDER = tuple(g * A_REP + r for r in range(A_REP) for g in range(A_KV_HEADS))

B_HEADS = 4
B_HEAD_DIM = 128
B_WIDTH = B_HEADS * B_HEAD_DIM
CONV_W = 4

POOL_SIZES = (2, 4, 8, 16)
C_WIDTH = D_MODEL
C_GROUP = C_WIDTH // len(POOL_SIZES)
POOL_HIST = max(POOL_SIZES) - 1

LANES = 128
SUBLANES = 8

Z_AQ = 0
Z_AG = A_WIDTH
Z_BG = 2 * A_WIDTH
Z_BQKV = 3 * A_WIDTH
Z_AK = Z_BQKV + 3 * B_WIDTH
Z_AV = Z_AK + A_KV_WIDTH
Z_BA = Z_AV + A_KV_WIDTH
Z_WIDTH = Z_BA + LANES
GATE_LANE = B_HEADS

VMEM_LIMIT = 48 * 1024 * 1024


def _sigmoid(x):
    return 1.0 / (1.0 + jnp.exp(-x))


def _silu(x):
    return x * _sigmoid(x)


def _softplus(x):
    return jnp.maximum(x, 0.0) + jnp.log(1.0 + jnp.exp(-jnp.abs(x)))


def _dot(a, b, precision=None):
    return jnp.dot(a, b, preferred_element_type=F32, precision=precision)


def _dot_nt(a, b, precision=None):
    return lax.dot_general(a, b, (((1,), (1,)), ((), ())), preferred_element_type=F32, precision=precision)


def _dot_tn(a, b, precision=None):
    return lax.dot_general(a, b, (((0,), (0,)), ((), ())), preferred_element_type=F32, precision=precision)


def _log2(n):
    assert n & (n - 1) == 0
    return n.bit_length() - 1


def _in_proj_kernel(x_ref, g_ref, w_ref, kn_ref, z_ref):
    x = x_ref[...]
    xn = x * lax.rsqrt(jnp.mean(x * x, axis=-1, keepdims=True) + EPS) * g_ref[...]
    z_ref[...] = _dot(xn.astype(BF16), w_ref[...])
    k = z_ref[:, Z_AK:Z_AK + A_KV_WIDTH]
    sq = k * k
    lane = lax.broadcasted_iota(jnp.int32, k.shape, 1)
    first = lane < A_HEAD_DIM
    s0 = jnp.sum(jnp.where(first, sq, 0.0), axis=-1, keepdims=True)
    s1 = jnp.sum(jnp.where(first, 0.0, sq), axis=-1, keepdims=True)
    ms = jnp.where(first, s0, s1) * (1.0 / A_HEAD_DIM)
    z_ref[:, Z_AK:Z_AK + A_KV_WIDTH] = k * lax.rsqrt(ms + EPS) * kn_ref[...]


def _in_proj(x, g, w, kn, tm):
    n = x.shape[0]
    return pl.pallas_call(
        _in_proj_kernel,
        grid=(n // tm,),
        in_specs=[
            pl.BlockSpec((tm, D_MODEL), lambda i: (i, 0)),
            pl.BlockSpec((1, D_MODEL), lambda i: (0, 0)),
            pl.BlockSpec((D_MODEL, Z_WIDTH), lambda i: (0, 0)),
            pl.BlockSpec((1, A_KV_WIDTH), lambda i: (0, 0)),
        ],
        out_specs=pl.BlockSpec((tm, Z_WIDTH), lambda i: (i, 0)),
        out_shape=jax.ShapeDtypeStruct((n, Z_WIDTH), F32),
        compiler_params=pltpu.CompilerParams(dimension_semantics=("arbitrary",), vmem_limit_bytes=VMEM_LIMIT),
        name="in_proj",
    )(x, g, w, kn)


def _attn_kernel(sinks_ref, q_ref, k_ref, v_ref, ag_ref, qn_ref, o_ref, kp_ref, vp_ref, bias_ref, *, cq, lk, off, t):
    rows = A_REP * cq
    front = WINDOW - off
    nvar = front // cq + 1
    lkt = k_ref.shape[0]
    log2e = 1.4426950408889634

    if front:
        kp_ref[0:front, :] = jnp.zeros((front, A_KV_WIDTH), BF16)
        vp_ref[0:front, :] = jnp.zeros((front, A_KV_WIDTH), BF16)
    kp_ref[front:front + lkt, :] = k_ref[...].astype(BF16)
    vp_ref[front:front + lkt, :] = v_ref[...].astype(BF16)

    row = lax.broadcasted_iota(jnp.int32, (rows, 1), 0)
    rep = row >> _log2(cq)

    @pl.when(pl.program_id(0) == 0)
    def _():
        col = lax.broadcasted_iota(jnp.int32, (rows, lk), 1)
        dist = jnp.abs((row & (cq - 1)) + WINDOW - col).astype(F32)
        for g in range(A_KV_HEADS):
            slope = jnp.zeros((rows, 1), F32)
            for r in range(A_REP):
                slope = jnp.where(rep == r, 2.0 ** (-8.0 * (g * A_REP + r + 1) / A_HEADS), slope)
            for var in range(nvar):
                bias_ref[var, g] = jnp.where(col >= front - var * cq, -slope * dist, NEG_INF) * log2e

    lane = lax.broadcasted_iota(jnp.int32, (cq, LANES), 1)
    first = lane < A_HEAD_DIM
    qn = qn_ref[...] * (A_HEAD_DIM ** -0.5 * log2e)
    sks = []
    for g in range(A_KV_HEADS):
        sk = jnp.zeros((rows, 1), F32)
        for r in range(A_REP):
            sk = jnp.where(rep == r, sinks_ref[g * A_REP + r] * log2e, sk)
        sks.append(sk)

    def chunk(cg, carry):
        r0 = pl.multiple_of(cg * cq, cq)
        qc = q_ref[pl.ds(r0, cq), :]
        kk = kp_ref[pl.ds(r0, lk), :]
        vv = vp_ref[pl.ds(r0, lk), :]
        var = jnp.minimum(cg, nvar - 1)
        blocks = []
        for r in range(A_REP):
            x = qc[:, r * LANES:(r + 1) * LANES]
            sq = x * x
            s_lo = jnp.sum(jnp.where(first, sq, 0.0), axis=-1, keepdims=True)
            s_hi = jnp.sum(jnp.where(first, 0.0, sq), axis=-1, keepdims=True)
            ms = jnp.where(first, s_lo, s_hi) * (1.0 / A_HEAD_DIM)
            blocks.append(x * lax.rsqrt(ms + EPS) * qn)
        outs = []
        for g in range(A_KV_HEADS):
            mine = first if g == 0 else jnp.logical_not(first)
            lhs = jnp.concatenate([jnp.where(mine, b, 0.0) for b in blocks], axis=0).astype(BF16)
            s = _dot_nt(lhs, kk) + bias_ref[var, g]
            m = jnp.maximum(jnp.max(s, axis=-1, keepdims=True), sks[g])
            p = jnp.exp2(s - m)
            den = jnp.sum(p, axis=-1, keepdims=True) + jnp.exp2(sks[g] - m)
            outs.append(_dot(p.astype(BF16), vv) * (1.0 / den))
        tile = jnp.concatenate(
            [jnp.where(first, outs[0][r * cq:(r + 1) * cq], outs[1][r * cq:(r + 1) * cq]) for r in range(A_REP)], axis=1)
        o_ref[pl.ds(r0, cq), :] = tile * _silu(ag_ref[pl.ds(r0, cq), :])
        return carry

    n_chunks = t // cq
    lax.fori_loop(0, n_chunks, chunk, 0, unroll=min(n_chunks, 4))


def _attention(z, kbuf, vbuf, kv_specs, sinks, qn, *, nb, t, cq, lk, off):
    front = WINDOW - off
    lkt = kv_specs[0].block_shape[0]
    kern = functools.partial(_attn_kernel, cq=cq, lk=lk, off=off, t=t)
    return pl.pallas_call(
        kern,
        grid=(nb,),
        in_specs=[
            pl.BlockSpec(memory_space=pltpu.SMEM),
            pl.BlockSpec((t, A_WIDTH), lambda b: (b, Z_AQ // A_WIDTH)),
            kv_specs[0],
            kv_specs[1],
            pl.BlockSpec((t, A_WIDTH), lambda b: (b, Z_AG // A_WIDTH)),
            pl.BlockSpec((1, LANES), lambda b: (0, 0)),
        ],
        out_specs=pl.BlockSpec((t, A_WIDTH), lambda b: (b, 0)),
        out_shape=jax.ShapeDtypeStruct((nb * t, A_WIDTH), F32),
        scratch_shapes=[
            pltpu.VMEM((front + lkt, A_KV_WIDTH), BF16),
            pltpu.VMEM((front + lkt, A_KV_WIDTH), BF16),
            pltpu.VMEM((front // cq + 1, A_KV_HEADS, A_REP * cq, lk), F32),
        ],
        compiler_params=pltpu.CompilerParams(dimension_semantics=("arbitrary",), vmem_limit_bytes=VMEM_LIMIT),
        name="swa_attention",
    )(sinks, z, kbuf, vbuf, z, qn)


def _bf(x):
    return x.astype(BF16)


def _unit_lower_inverse(a_list, c, ri, lj, expand):
    base = SUBLANES
    same = (ri >> _log2(base)) == (lj >> _log2(base))
    eye = jnp.where(ri == lj, 1.0, 0.0)
    n1 = [jnp.where(same, -a, 0.0) for a in a_list]
    x = [eye + n for n in n1]
    n1b = [_bf(n) for n in n1]
    n2b = [_bf(_dot(nb, expand(nb))) for nb in n1b]
    n2e = [expand(nb) for nb in n2b]
    x = [xi + _dot(_bf(xi), ne) for xi, ne in zip(x, n2e)]
    n4e = [expand(_bf(_dot(nb, ne))) for nb, ne in zip(n2b, n2e)]
    x = [xi + _dot(_bf(xi), ne) for xi, ne in zip(x, n4e)]
    s = base
    while s < c:
        sel = ((ri >> _log2(2 * s)) == (lj >> _log2(2 * s))) & ((ri >> _log2(s)) != (lj >> _log2(s)))
        xb = [_bf(xi) for xi in x]
        xo = [_dot(b, expand(_bf(jnp.where(sel, a, 0.0)))) for b, a in zip(xb, a_list)]
        x = [xi - _dot(_bf(o), expand(b)) for xi, o, b in zip(x, xo, xb)]
        s *= 2
    return x


def _delta_kernel(qkv_ref, ba_ref, bg_ref, hist_ref, s0_ref, convw_ref, gate_ref, onorm_ref,
                  o_ref, sout_ref, ext_ref, st_ref, lhs_ref, add_ref, s_ref, *, c, tc):
    i = pl.program_id(1)
    hd = B_HEAD_DIM
    nh = B_HEADS
    r = nh * c
    nch = tc // c
    pad = SUBLANES

    @pl.when(i == 0)
    def _():
        ext_ref[0:pad, :] = hist_ref[0]
        s_ref[...] = s0_ref[0]

    @pl.when(i > 0)
    def _():
        ext_ref[0:pad, :] = ext_ref[tc:tc + pad, :]

    ext_ref[pad:pad + tc, :] = qkv_ref[...]

    for cc in range(nch):
        base = pad - (CONV_W - 1) + cc * c
        acc = ext_ref[base:base + c, :] * convw_ref[0:1, :]
        for j in range(1, CONV_W):
            acc = acc + ext_ref[base + j:base + j + c, :] * convw_ref[j:j + 1, :]
        y = _silu(acc)
        for part in range(3):
            for h in range(nh):
                blk = y[:, (part * nh + h) * hd:(part * nh + h + 1) * hd]
                if part < 2:
                    blk = blk * lax.rsqrt(jnp.sum(blk * blk, axis=-1, keepdims=True) + EPS)
                st_ref[part, cc, h * c:(h + 1) * c, :] = blk

    ri = lax.broadcasted_iota(jnp.int32, (c, r), 0)
    li = lax.broadcasted_iota(jnp.int32, (c, r), 1)
    lj = li & (c - 1)
    lh = li >> _log2(c)
    lower = ri >= lj
    strict = ri > lj
    head_sel = [jnp.where(lh == h, 1.0, 0.0).astype(BF16) for h in range(nh)]

    def expand(xb):
        return jnp.concatenate([xb * m for m in head_sel], axis=0)

    def lanes(x_st):
        return jnp.concatenate([x_st[h * c:(h + 1) * c] for h in range(nh)], axis=1)

    def head_blocks(x_st):
        zero = jnp.zeros((c, hd), x_st.dtype)
        return jnp.concatenate(
            [jnp.concatenate([x_st[h * c:(h + 1) * c] if h2 == h else zero for h2 in range(nh)], axis=1)
             for h in range(nh)], axis=0)

    ci = lax.broadcasted_iota(jnp.int32, (c, c), 0)
    cj = lax.broadcasted_iota(jnp.int32, (c, c), 1)
    tril = jnp.where(ci >= cj, 1.0, 0.0)
    di = lax.broadcasted_iota(jnp.int32, (hd, hd), 0)
    dj = lax.broadcasted_iota(jnp.int32, (hd, hd), 1)
    eye_hd = di == dj
    neg_rate = -jnp.exp(gate_ref[0:1, :])
    dt_bias = gate_ref[1:2, :]
    onorm = onorm_ref[...]
    chunks = range(nch)

    def stack_cols(x, lane0):
        return jnp.concatenate([x[:, lane0 + h:lane0 + h + 1] for h in range(nh)], axis=0)

    beta, gc, gcc, gr, glast = [], [], [], [], []
    for cc in chunks:
        ba = ba_ref[cc * c:(cc + 1) * c, :]
        g_all = neg_rate * _softplus(ba + dt_bias)
        gcum = _dot(tril, g_all, HIGHEST)
        gcum_t = gcum.T
        beta.append(stack_cols(_sigmoid(ba), 0))
        gc.append(stack_cols(gcum, GATE_LANE))
        gcc_h = jnp.zeros((c, r), F32)
        for h in range(nh):
            gcc_h = jnp.where(lh == h, gcum[:, GATE_LANE + h:GATE_LANE + h + 1], gcc_h)
        gcc.append(gcc_h)
        gr.append(jnp.concatenate([gcum_t[GATE_LANE + h:GATE_LANE + h + 1, :] for h in range(nh)], axis=1))
        glast.append(jnp.concatenate(
            [jnp.broadcast_to(gcum[c - 1:c, GATE_LANE + h:GATE_LANE + h + 1], (c, 1)) for h in range(nh)], axis=0))

    q = [st_ref[0, cc] * (hd ** -0.5) for cc in chunks]
    k = [st_ref[1, cc] for cc in chunks]
    v = [st_ref[2, cc] for cc in chunks]
    kb = [k[cc] * beta[cc] for cc in chunks]
    big = [_dot_nt(_bf(jnp.concatenate([lanes(kb[cc]), lanes(q[cc])], axis=0)), head_blocks(_bf(k[cc])))
           for cc in chunks]
    decay = [jnp.where(lower, jnp.exp(jnp.where(lower, gcc[cc] - gr[cc], 0.0)), 0.0) for cc in chunks]
    a = [jnp.where(strict, big[cc][:c] * decay[cc], 0.0) for cc in chunks]
    qkd = [expand(_bf(big[cc][c:] * decay[cc])) for cc in chunks]
    tinv = _unit_lower_inverse(a, c, ri, lj, expand)
    eg = [jnp.exp(gc[cc]) for cc in chunks]
    rhs = [_bf(jnp.concatenate([v[cc] * beta[cc], kb[cc] * eg[cc]], axis=1)) for cc in chunks]
    sol = [_dot(expand(_bf(tinv[cc])), rhs[cc]) for cc in chunks]
    solb = [_bf(s) for s in sol]
    fold = [_dot(qkd[cc], solb[cc]) for cc in chunks]
    k_dec = [_bf(k[cc] * jnp.exp(glast[cc] - gc[cc])) for cc in chunks]
    for cc in chunks:
        qp = q[cc] * eg[cc] - fold[cc][:, hd:]
        for h in range(nh):
            rows = slice(h * c, (h + 1) * c)
            kt = _dot_tn(k_dec[cc][rows], solb[cc][rows])
            g_tot = jnp.exp(glast[cc][h * c:h * c + 1, :])
            lhs_ref[cc, h, 0:hd, :] = _bf(jnp.where(eye_hd, g_tot, 0.0) - kt[:, hd:])
            lhs_ref[cc, h, hd:hd + c, :] = _bf(qp[rows])
            add_ref[cc, h, 0:hd, :] = kt[:, :hd]
            add_ref[cc, h, hd:hd + c, :] = fold[cc][rows, :hd]

    for cc in chunks:
        for h in range(nh):
            res = _dot(lhs_ref[cc, h], _bf(s_ref[h])) + add_ref[cc, h]
            s_ref[h] = res[:hd]
            o = res[hd:]
            on = o * lax.rsqrt(jnp.mean(o * o, axis=-1, keepdims=True) + EPS) * onorm
            bg = bg_ref[cc * c:(cc + 1) * c, h * hd:(h + 1) * hd]
            o_ref[cc * c:(cc + 1) * c, h * hd:(h + 1) * hd] = on * _silu(bg)
    sout_ref[0] = s_ref[...]


def _delta(z, hist, s0, convw, gate, onorm, *, nb, t, tc, c):
    nt = t // tc
    kern = functools.partial(_delta_kernel, c=c, tc=tc)
    return pl.pallas_call(
        kern,
        grid=(nb, nt),
        in_specs=[
            pl.BlockSpec((tc, 3 * B_WIDTH), lambda b, i: (b * nt + i, Z_BQKV // (3 * B_WIDTH))),
            pl.BlockSpec((tc, LANES), lambda b, i: (b * nt + i, Z_BA // LANES)),
            pl.BlockSpec((tc, B_WIDTH), lambda b, i: (b * nt + i, Z_BG // B_WIDTH)),
            pl.BlockSpec((1, SUBLANES, 3 * B_WIDTH), lambda b, i: (b, 0, 0)),
            pl.BlockSpec((1, B_HEADS, B_HEAD_DIM, B_HEAD_DIM), lambda b, i: (b, 0, 0, 0)),
            pl.BlockSpec((CONV_W, 3 * B_WIDTH), lambda b, i: (0, 0)),
            pl.BlockSpec((2, LANES), lambda b, i: (0, 0)),
            pl.BlockSpec((1, B_HEAD_DIM), lambda b, i: (0, 0)),
        ],
        out_specs=[
            pl.BlockSpec((tc, B_WIDTH), lambda b, i: (b * nt + i, 0)),
            pl.BlockSpec((1, B_HEADS, B_HEAD_DIM, B_HEAD_DIM), lambda b, i: (b, 0, 0, 0)),
        ],
        out_shape=[
            jax.ShapeDtypeStruct((nb * t, B_WIDTH), F32),
            jax.ShapeDtypeStruct((nb, B_HEADS, B_HEAD_DIM, B_HEAD_DIM), F32),
        ],
        scratch_shapes=[
            pltpu.VMEM((SUBLANES + tc, 3 * B_WIDTH), F32),
            pltpu.VMEM((3, tc // c, B_HEADS * c, B_HEAD_DIM), F32),
            pltpu.VMEM((tc // c, B_HEADS, B_HEAD_DIM + c, B_HEAD_DIM), BF16),
            pltpu.VMEM((tc // c, B_HEADS, B_HEAD_DIM + c, B_HEAD_DIM), F32),
            pltpu.VMEM((B_HEADS, B_HEAD_DIM, B_HEAD_DIM), F32),
        ],
        compiler_params=pltpu.CompilerParams(dimension_semantics=("arbitrary", "arbitrary"),
                                             vmem_limit_bytes=VMEM_LIMIT),
        name="gated_delta",
    )(z, z, z, hist, s0, convw, gate, onorm)


def _out_proj_kernel(h_ref, oa_ref, ob_ref, wa_ref, wb_ref, y_ref):
    y_ref[...] = (h_ref[...] + _dot(oa_ref[...].astype(BF16), wa_ref[...])
                  + _dot(ob_ref[...].astype(BF16), wb_ref[...]))


def _out_proj(h, oa, ob, wa, wb, tm):
    n = h.shape[0]
    return pl.pallas_call(
        _out_proj_kernel,
        grid=(n // tm,),
        in_specs=[
            pl.BlockSpec((tm, D_MODEL), lambda i: (i, 0)),
            pl.BlockSpec((tm, A_WIDTH), lambda i: (i, 0)),
            pl.BlockSpec((tm, B_WIDTH), lambda i: (i, 0)),
            pl.BlockSpec((A_WIDTH, D_MODEL), lambda i: (0, 0)),
            pl.BlockSpec((B_WIDTH, D_MODEL), lambda i: (0, 0)),
        ],
        out_specs=pl.BlockSpec((tm, D_MODEL), lambda i: (i, 0)),
        out_shape=jax.ShapeDtypeStruct((n, D_MODEL), F32),
        compiler_params=pltpu.CompilerParams(dimension_semantics=("arbitrary",), vmem_limit_bytes=VMEM_LIMIT),
        name="out_proj",
    )(h, oa, ob, wa, wb)


def _pool_kernel(h_ref, hist_ref, g_ref, win_ref, wgrp_ref, scale_ref, wout_ref, y_ref, tail_ref, ext_ref,
                 *, tt, pos0):
    i = pl.program_id(1)
    pad = POOL_HIST + 1

    @pl.when(i == 0)
    def _():
        pos = pos0 - pad + lax.broadcasted_iota(jnp.int32, (pad, 1), 0)
        ext_ref[0:pad, :] = jnp.where(pos >= 0, hist_ref[0], 0.0)

    @pl.when(i > 0)
    def _():
        ext_ref[0:pad, :] = ext_ref[tt:tt + pad, :]

    x = h_ref[...]
    xn = x * lax.rsqrt(jnp.mean(x * x, axis=-1, keepdims=True) + EPS) * g_ref[...]
    z = _dot(xn.astype(BF16), win_ref[...])
    u = z[:, :C_WIDTH]
    gate = z[:, C_WIDTH:]
    ext_ref[pad:pad + tt, :] = u
    tail_ref[0] = u[tt - pad:, :]

    tpos = pos0 + i * tt + lax.broadcasted_iota(jnp.int32, (tt, 1), 0)
    mixed = []
    for gi, w in enumerate(POOL_SIZES):
        cols = slice(gi * C_GROUP, (gi + 1) * C_GROUP)
        s = ext_ref[:, cols]
        sh = 1
        while sh < w:
            s = s + pltpu.roll(s, sh, 0)
            sh *= 2
        cnt = jnp.minimum(tpos + 1, w).astype(F32)
        pooled = s[pad:, :] / cnt - u[:, cols]
        m = _dot(pooled.astype(BF16), wgrp_ref[gi]) * scale_ref[:, cols]
        mixed.append((m * _silu(gate[:, cols])).astype(BF16))
    y_ref[...] = x + _dot(jnp.concatenate(mixed, axis=1), wout_ref[...])


def _pool_layer(h, hist, g, win, wgrp, scale, wout, *, nb, t, tt, pos0):
    nt = t // tt
    pad = POOL_HIST + 1
    kern = functools.partial(_pool_kernel, tt=tt, pos0=pos0)
    return pl.pallas_call(
        kern,
        grid=(nb, nt),
        in_specs=[
            pl.BlockSpec((tt, D_MODEL), lambda b, i: (b * nt + i, 0)),
            pl.BlockSpec((1, pad, C_WIDTH), lambda b, i: (b, 0, 0)),
            pl.BlockSpec((1, D_MODEL), lambda b, i: (0, 0)),
            pl.BlockSpec((D_MODEL, 2 * C_WIDTH), lambda b, i: (0, 0)),
            pl.BlockSpec((len(POOL_SIZES), C_GROUP, C_GROUP), lambda b, i: (0, 0, 0)),
            pl.BlockSpec((1, C_WIDTH), lambda b, i: (0, 0)),
            pl.BlockSpec((C_WIDTH, D_MODEL), lambda b, i: (0, 0)),
        ],
        out_specs=[
            pl.BlockSpec((tt, D_MODEL), lambda b, i: (b * nt + i, 0)),
            pl.BlockSpec((1, pad, C_WIDTH), lambda b, i: (b, 0, 0)),
        ],
        out_shape=[
            jax.ShapeDtypeStruct((nb * t, D_MODEL), F32),
            jax.ShapeDtypeStruct((nb, pad, C_WIDTH), F32),
        ],
        scratch_shapes=[pltpu.VMEM((pad + tt, C_WIDTH), F32)],
        compiler_params=pltpu.CompilerParams(dimension_semantics=("arbitrary", "arbitrary"),
                                             vmem_limit_bytes=VMEM_LIMIT),
        name="pool_layer",
    )(h, hist, g, win, wgrp, scale, wout)


def _ab_weights(norm_g, w_in, q_norm, k_norm, sinks, conv_w, a_log, dt_bias, o_norm, w_out):
    aq, ak, av, ag, bqkv, bg, bbeta, balpha = jnp.split(
        w_in, [A_WIDTH, A_WIDTH + A_KV_WIDTH, A_WIDTH + 2 * A_KV_WIDTH, 2 * A_WIDTH + 2 * A_KV_WIDTH,
               2 * A_WIDTH + 2 * A_KV_WIDTH + 3 * B_WIDTH, 2 * A_WIDTH + 2 * A_KV_WIDTH + 4 * B_WIDTH,
               2 * A_WIDTH + 2 * A_KV_WIDTH + 4 * B_WIDTH + B_HEADS], axis=1)
    zpad = jnp.zeros((D_MODEL, LANES - 2 * B_HEADS), w_in.dtype)
    order = jnp.array(HEAD_ORDER)
    aq = aq.reshape(D_MODEL, A_HEADS, A_HEAD_DIM)[:, order].reshape(D_MODEL, A_WIDTH)
    ag = ag.reshape(D_MODEL, A_HEADS, A_HEAD_DIM)[:, order].reshape(D_MODEL, A_WIDTH)
    wa = w_out[:A_WIDTH].reshape(A_HEADS, A_HEAD_DIM, D_MODEL)[order].reshape(A_WIDTH, D_MODEL)
    w = jnp.concatenate([aq, ag, bg, bqkv, ak, av, bbeta, balpha, zpad], axis=1).astype(BF16)
    lane_pad = (GATE_LANE, LANES - GATE_LANE - B_HEADS)
    gate = jnp.stack([jnp.pad(a_log.astype(F32), lane_pad), jnp.pad(dt_bias.astype(F32), lane_pad)])
    return dict(
        norm_g=norm_g.reshape(1, D_MODEL), w=w, kn=jnp.tile(k_norm, A_KV_HEADS).reshape(1, A_KV_WIDTH),
        qn=jnp.tile(q_norm, A_KV_HEADS).reshape(1, LANES), sinks=sinks.astype(F32), conv_w=conv_w, gate=gate,
        onorm=o_norm.reshape(1, B_HEAD_DIM), wa=wa.astype(BF16), wb=w_out[A_WIDTH:].astype(BF16))


def _ab_layer(h, wts, cache_k, cache_v, s0, conv_hist, *, nb, t):
    n = nb * t
    tm = min(n, 512)
    z = _in_proj(h, wts["norm_g"], wts["w"], wts["kn"], tm)
    if cache_k is None:
        cq, lk, off = CHUNK, WINDOW + CHUNK, 0
        kv_specs = [pl.BlockSpec((t, A_KV_WIDTH), lambda b: (b, Z_AK // A_KV_WIDTH)),
                    pl.BlockSpec((t, A_KV_WIDTH), lambda b: (b, Z_AV // A_KV_WIDTH))]
        kbuf = vbuf = z
        tc, c = 256, CHUNK
    else:
        cq, lk, off = t, WINDOW + t, WINDOW
        k_new = z[:, Z_AK:Z_AK + A_KV_WIDTH].reshape(nb, t, A_KV_WIDTH)
        v_new = z[:, Z_AV:Z_AV + A_KV_WIDTH].reshape(nb, t, A_KV_WIDTH)
        kbuf = jnp.concatenate([cache_k.reshape(nb, WINDOW, A_KV_WIDTH), k_new], axis=1).reshape(nb * lk, A_KV_WIDTH)
        vbuf = jnp.concatenate([cache_v.reshape(nb, WINDOW, A_KV_WIDTH), v_new], axis=1).reshape(nb * lk, A_KV_WIDTH)
        kv_specs = [pl.BlockSpec((lk, A_KV_WIDTH), lambda b: (b, 0))] * 2
        tc, c = t, t
    o_a = _attention(z, kbuf, vbuf, kv_specs, wts["sinks"], wts["qn"], nb=nb, t=t, cq=cq, lk=lk, off=off)
    hist = jnp.pad(conv_hist.astype(F32), ((0, 0), (SUBLANES - (CONV_W - 1), 0), (0, 0)))
    o_b, s_new = _delta(z, hist, s0.astype(F32), wts["conv_w"], wts["gate"], wts["onorm"], nb=nb, t=t, tc=tc, c=c)
    y = _out_proj(h, o_a, o_b, wts["wa"], wts["wb"], tm)
    return y, z, s_new


def _last_rows(z, col, width, nb, t, rows):
    return z.reshape(nb, t, Z_WIDTH)[:, t - rows:, col:col + width]


def _cache_rows(z, col, nb, t, rows):
    return _last_rows(z, col, A_KV_WIDTH, nb, t, rows).reshape(nb, rows, A_KV_HEADS, A_HEAD_DIM)


def kernel(x_prompt, x_sample, cache_a_k, cache_a_v, state_b_s, state_b_conv, state_c_pool,
           norm_ab, w_in_ab, q_norm_a, k_norm_a, sinks_a, conv_b, a_log_b, dt_bias_b, o_norm_b, w_out_ab,
           norm_c, w_in_c, w_grp_c, scale_c, w_out_c):
    bp, tp, _ = x_prompt.shape
    bs, ts, _ = x_sample.shape
    hp = x_prompt.reshape(bp * tp, D_MODEL)
    hs = x_sample.reshape(bs * ts, D_MODEL)

    wts = _ab_weights(norm_ab[0], w_in_ab[0], q_norm_a[0], k_norm_a[0], sinks_a[0], conv_b[0], a_log_b[0],
                      dt_bias_b[0], o_norm_b[0], w_out_ab[0])
    s0 = jnp.zeros((bp, B_HEADS, B_HEAD_DIM, B_HEAD_DIM), F32)
    c0 = jnp.zeros((bp, CONV_W - 1, 3 * B_WIDTH), F32)
    hp, zp, sp = _ab_layer(hp, wts, None, None, s0, c0, nb=bp, t=tp)
    hs, zs, ss = _ab_layer(hs, wts, cache_a_k[0], cache_a_v[0], state_b_s[0], state_b_conv[0], nb=bs, t=ts)
    p_a_k = _cache_rows(zp, Z_AK, bp, tp, WINDOW)[None]
    p_a_v = _cache_rows(zp, Z_AV, bp, tp, WINDOW)[None]
    s_a_k = _cache_rows(zs, Z_AK, bs, ts, ts)[None]
    s_a_v = _cache_rows(zs, Z_AV, bs, ts, ts)[None]
    p_b_conv = _last_rows(zp, Z_BQKV, 3 * B_WIDTH, bp, tp, CONV_W - 1)[None]
    s_b_conv = _last_rows(zs, Z_BQKV, 3 * B_WIDTH, bs, ts, CONV_W - 1)[None]

    g_c = norm_c[0].reshape(1, D_MODEL)
    win = w_in_c[0].astype(BF16)
    wgrp = w_grp_c[0].astype(BF16)
    scale = scale_c[0].reshape(1, C_WIDTH)
    wout = w_out_c[0].astype(BF16)
    h0 = jnp.zeros((bp, POOL_HIST + 1, C_WIDTH), F32)
    hs0 = jnp.pad(state_c_pool[0].astype(F32), ((0, 0), (1, 0), (0, 0)))
    yp, tail_p = _pool_layer(hp, h0, g_c, win, wgrp, scale, wout, nb=bp, t=tp, tt=512, pos0=0)
    ys, tail_s = _pool_layer(hs, hs0, g_c, win, wgrp, scale, wout, nb=bs, t=ts, tt=ts, pos0=PAST_LEN)

    return (yp.reshape(bp, tp, D_MODEL), ys.reshape(bs, ts, D_MODEL),
            p_a_k, p_a_v, sp[None], p_b_conv, tail_p[:, 1:][None],
            s_a_k, s_a_v, ss[None], s_b_conv, tail_s[:, 1:][None])
```

```python
import functools

import jax
import jax.numpy as jnp
from jax import lax
from jax.experimental import pallas as pl
from jax.experimental.pallas import tpu as pltpu

F32 = jnp.float32
BF16 = jnp.bfloat16
HIGHEST = lax.Precision.HIGHEST

D_MODEL = 1024
CHUNK = 64
PAST_LEN = 2048
EPS = 1e-6
NEG_INF = -1e30
LOG2E = 1.4426950408889634

A_HEADS = 8
A_KV_HEADS = 2
A_HEAD_DIM = 64
A_WIDTH = A_HEADS * A_HEAD_DIM
A_KV_WIDTH = A_KV_HEADS * A_HEAD_DIM
A_REP = A_HEADS // A_KV_HEADS
WINDOW = 128
HEAD_ORDER = tuple(g * A_REP + r for r in range(A_REP) for g in range(A_KV_HEADS))

B_HEADS = 4
B_HEAD_DIM = 128
B_WIDTH = B_HEADS * B_HEAD_DIM
CONV_W = 4

POOL_SIZES = (2, 4, 8, 16)
C_WIDTH = D_MODEL
C_GROUP = C_WIDTH // len(POOL_SIZES)
POOL_HIST = max(POOL_SIZES) - 1

LANES = 128
SUBLANES = 8

Z_AQ = 0
Z_AG = A_WIDTH
Z_BG = 2 * A_WIDTH
Z_BQKV = 3 * A_WIDTH
Z_AK = Z_BQKV + 3 * B_WIDTH
Z_AV = Z_AK + A_KV_WIDTH
Z_BA = Z_AV + A_KV_WIDTH
Z_WIDTH = Z_BA + LANES
GATE_LANE = B_HEADS

VMEM_LIMIT = 48 * 1024 * 1024


def _sigmoid(x):
    return 1.0 / (1.0 + jnp.exp(-x))


def _silu(x):
    return x * _sigmoid(x)


def _softplus(x):
    return jnp.maximum(x, 0.0) + jnp.log(1.0 + jnp.exp(-jnp.abs(x)))


def _dot(a, b, precision=None):
    return jnp.dot(a, b, preferred_element_type=F32, precision=precision)


def _dot_nt(a, b, precision=None):
    return lax.dot_general(a, b, (((1,), (1,)), ((), ())), preferred_element_type=F32, precision=precision)


def _dot_tn(a, b, precision=None):
    return lax.dot_general(a, b, (((0,), (0,)), ((), ())), preferred_element_type=F32, precision=precision)


def _log2(n):
    assert n & (n - 1) == 0
    return n.bit_length() - 1


def _in_proj_kernel(x_ref, g_ref, w_ref, kn_ref, z_ref):
    x = x_ref[...]
    xn = x * lax.rsqrt(jnp.mean(x * x, axis=-1, keepdims=True) + EPS) * g_ref[...]
    z_ref[...] = _dot(xn.astype(BF16), w_ref[...])
    k = z_ref[:, Z_AK:Z_AK + A_KV_WIDTH]
    sq = k * k
    lane = lax.broadcasted_iota(jnp.int32, k.shape, 1)
    first = lane < A_HEAD_DIM
    s0 = jnp.sum(jnp.where(first, sq, 0.0), axis=-1, keepdims=True)
    s1 = jnp.sum(jnp.where(first, 0.0, sq), axis=-1, keepdims=True)
    ms = jnp.where(first, s0, s1) * (1.0 / A_HEAD_DIM)
    z_ref[:, Z_AK:Z_AK + A_KV_WIDTH] = k * lax.rsqrt(ms + EPS) * kn_ref[...]


def _in_proj(x, g, w, kn, tm):
    n = x.shape[0]
    return pl.pallas_call(
        _in_proj_kernel,
        grid=(n // tm,),
        in_specs=[
            pl.BlockSpec((tm, D_MODEL), lambda i: (i, 0)),
            pl.BlockSpec((1, D_MODEL), lambda i: (0, 0)),
            pl.BlockSpec((D_MODEL, Z_WIDTH), lambda i: (0, 0)),
            pl.BlockSpec((1, A_KV_WIDTH), lambda i: (0, 0)),
        ],
        out_specs=pl.BlockSpec((tm, Z_WIDTH), lambda i: (i, 0)),
        out_shape=jax.ShapeDtypeStruct((n, Z_WIDTH), F32),
        compiler_params=pltpu.CompilerParams(dimension_semantics=("arbitrary",), vmem_limit_bytes=VMEM_LIMIT),
        name="in_proj",
    )(x, g, w, kn)


def _attn_kernel(sinks_ref, q_ref, k_ref, v_ref, ag_ref, qn_ref, o_ref, kp_ref, vp_ref, bias_ref, *, bt, cq, lk, off, t):
    rows = A_REP * cq
    front = WINDOW - off
    nvar = front // cq + 1
    lkt = k_ref.shape[0] // bt
    n_chunks = t // cq

    for b in range(bt):
        if front:
            kp_ref[b, 0:front, :] = jnp.zeros((front, A_KV_WIDTH), BF16)
            vp_ref[b, 0:front, :] = jnp.zeros((front, A_KV_WIDTH), BF16)
        kp_ref[b, front:front + lkt, :] = k_ref[b * lkt:(b + 1) * lkt, :].astype(BF16)
        vp_ref[b, front:front + lkt, :] = v_ref[b * lkt:(b + 1) * lkt, :].astype(BF16)

    row = lax.broadcasted_iota(jnp.int32, (rows, 1), 0)
    rep = row >> _log2(cq)

    @pl.when(pl.program_id(0) == 0)
    def _():
        col = lax.broadcasted_iota(jnp.int32, (rows, lk), 1)
        dist = jnp.abs((row & (cq - 1)) + WINDOW - col).astype(F32)
        for g in range(A_KV_HEADS):
            slope = jnp.zeros((rows, 1), F32)
            for r in range(A_REP):
                slope = jnp.where(rep == r, 2.0 ** (-8.0 * (g * A_REP + r + 1) / A_HEADS), slope)
            for var in range(nvar):
                bias_ref[var, g] = jnp.where(col >= front - var * cq, -slope * dist, NEG_INF) * LOG2E

    lane = lax.broadcasted_iota(jnp.int32, (cq, LANES), 1)
    first = lane < A_HEAD_DIM
    qn = qn_ref[...] * (A_HEAD_DIM ** -0.5 * LOG2E)
    sks = []
    for g in range(A_KV_HEADS):
        sk = jnp.zeros((rows, 1), F32)
        for r in range(A_REP):
            sk = jnp.where(rep == r, sinks_ref[g * A_REP + r] * LOG2E, sk)
        sks.append(sk)

    def chunk(b, cg):
        k0 = cg * cq if isinstance(cg, int) else pl.multiple_of(cg * cq, cq)
        r0 = b * t + k0
        qc = q_ref[pl.ds(r0, cq), :]
        kk = kp_ref[b, pl.ds(k0, lk), :]
        vv = vp_ref[b, pl.ds(k0, lk), :]
        var = min(cg, nvar - 1) if isinstance(cg, int) else jnp.minimum(cg, nvar - 1)
        blocks = []
        for r in range(A_REP):
            x = qc[:, r * LANES:(r + 1) * LANES]
            sq = x * x
            s_lo = jnp.sum(jnp.where(first, sq, 0.0), axis=-1, keepdims=True)
            s_hi = jnp.sum(jnp.where(first, 0.0, sq), axis=-1, keepdims=True)
            ms = jnp.where(first, s_lo, s_hi) * (1.0 / A_HEAD_DIM)
            blocks.append(x * lax.rsqrt(ms + EPS) * qn)
        outs = []
        for g in range(A_KV_HEADS):
            mine = first if g == 0 else jnp.logical_not(first)
            lhs = jnp.concatenate([jnp.where(mine, blk, 0.0) for blk in blocks], axis=0).astype(BF16)
            s = _dot_nt(lhs, kk) + bias_ref[var, g]
            m = jnp.maximum(jnp.max(s, axis=-1, keepdims=True), sks[g])
            p = jnp.exp2(s - m)
            den = jnp.sum(p, axis=-1, keepdims=True) + jnp.exp2(sks[g] - m)
            outs.append(_dot(p.astype(BF16), vv) * (1.0 / den))
        tile = jnp.concatenate(
            [jnp.where(first, outs[0][r * cq:(r + 1) * cq], outs[1][r * cq:(r + 1) * cq]) for r in range(A_REP)], axis=1)
        o_ref[pl.ds(r0, cq), :] = (tile * _silu(ag_ref[pl.ds(r0, cq), :])).astype(BF16)

    for b in range(bt):
        if n_chunks == 1:
            chunk(b, 0)
        else:
            lax.fori_loop(0, n_chunks, lambda cg, carry, b=b: (chunk(b, cg), carry)[1], 0, unroll=min(n_chunks, 4))


def _attention(z, kbuf, vbuf, kv_specs, sinks, qn, *, nb, bt, t, cq, lk, off):
    front = WINDOW - off
    lkt = kv_specs[0].block_shape[0] // bt
    kern = functools.partial(_attn_kernel, bt=bt, cq=cq, lk=lk, off=off, t=t)
    return pl.pallas_call(
        kern,
        grid=(nb // bt,),
        in_specs=[
            pl.BlockSpec(memory_space=pltpu.SMEM),
            pl.BlockSpec((bt * t, A_WIDTH), lambda b: (b, Z_AQ // A_WIDTH)),
            kv_specs[0],
            kv_specs[1],
            pl.BlockSpec((bt * t, A_WIDTH), lambda b: (b, Z_AG // A_WIDTH)),
            pl.BlockSpec((1, LANES), lambda b: (0, 0)),
        ],
        out_specs=pl.BlockSpec((bt * t, A_WIDTH), lambda b: (b, 0)),
        out_shape=jax.ShapeDtypeStruct((nb * t, A_WIDTH), BF16),
        scratch_shapes=[
            pltpu.VMEM((bt, front + lkt, A_KV_WIDTH), BF16),
            pltpu.VMEM((bt, front + lkt, A_KV_WIDTH), BF16),
            pltpu.VMEM((front // cq + 1, A_KV_HEADS, A_REP * cq, lk), F32),
        ],
        compiler_params=pltpu.CompilerParams(dimension_semantics=("arbitrary",), vmem_limit_bytes=VMEM_LIMIT),
        name="swa_attention",
    )(sinks, z, kbuf, vbuf, z, qn)


def _bf(x):
    return x.astype(BF16)


def _unit_lower_inverse(a_list, c, ri, lj, expand):
    base = SUBLANES
    same = (ri >> _log2(base)) == (lj >> _log2(base))
    eye = jnp.where(ri == lj, 1.0, 0.0)
    n1 = [jnp.where(same, -a, 0.0) for a in a_list]
    x = [eye + n for n in n1]
    n1b = [_bf(n) for n in n1]
    n2b = [_bf(_dot(nb, expand(nb))) for nb in n1b]
    n2e = [expand(nb) for nb in n2b]
    x = [xi + _dot(_bf(xi), ne) for xi, ne in zip(x, n2e)]
    n4e = [expand(_bf(_dot(nb, ne))) for nb, ne in zip(n2b, n2e)]
    x = [xi + _dot(_bf(xi), ne) for xi, ne in zip(x, n4e)]
    s = base
    while s < c:
        sel = ((ri >> _log2(2 * s)) == (lj >> _log2(2 * s))) & ((ri >> _log2(s)) != (lj >> _log2(s)))
        xb = [_bf(xi) for xi in x]
        xo = [_dot(b, expand(_bf(jnp.where(sel, a, 0.0)))) for b, a in zip(xb, a_list)]
        x = [xi - _dot(_bf(o), expand(b)) for xi, o, b in zip(x, xo, xb)]
        s *= 2
    return x


def _delta_kernel(qkv_ref, ba_ref, bg_ref, hist_ref, s0_ref, convw_ref, gate_ref, onorm_ref,
                  o_ref, sout_ref, ext_ref, st_ref, lhs_ref, add_ref, s_ref, *, bt, c, tc):
    i = pl.program_id(1)
    hd = B_HEAD_DIM
    nh = B_HEADS
    r = nh * c
    nch = tc // c
    pad = SUBLANES
    units = [(b, cc) for b in range(bt) for cc in range(nch)]

    @pl.when(i == 0)
    def _():
        ext_ref[:, 0:pad, :] = hist_ref[...]
        s_ref[...] = s0_ref[...]

    @pl.when(i > 0)
    def _():
        ext_ref[:, 0:pad, :] = ext_ref[:, tc:tc + pad, :]

    for b in range(bt):
        ext_ref[b, pad:pad + tc, :] = qkv_ref[b * tc:(b + 1) * tc, :]

    for u, (b, cc) in enumerate(units):
        base = pad - (CONV_W - 1) + cc * c
        acc = ext_ref[b, base:base + c, :] * convw_ref[0:1, :]
        for j in range(1, CONV_W):
            acc = acc + ext_ref[b, base + j:base + j + c, :] * convw_ref[j:j + 1, :]
        y = _silu(acc)
        for part in range(3):
            for h in range(nh):
                blk = y[:, (part * nh + h) * hd:(part * nh + h + 1) * hd]
                if part < 2:
                    blk = blk * lax.rsqrt(jnp.sum(blk * blk, axis=-1, keepdims=True) + EPS)
                st_ref[part, u, h * c:(h + 1) * c, :] = blk

    ri = lax.broadcasted_iota(jnp.int32, (c, r), 0)
    li = lax.broadcasted_iota(jnp.int32, (c, r), 1)
    lj = li & (c - 1)
    lh = li >> _log2(c)
    lower = ri >= lj
    strict = ri > lj
    head_sel = [jnp.where(lh == h, 1.0, 0.0).astype(BF16) for h in range(nh)]

    def expand(xb):
        return jnp.concatenate([xb * m for m in head_sel], axis=0)

    def lanes(x_st):
        return jnp.concatenate([x_st[h * c:(h + 1) * c] for h in range(nh)], axis=1)

    def head_blocks(x_st):
        zero = jnp.zeros((c, hd), x_st.dtype)
        return jnp.concatenate(
            [jnp.concatenate([x_st[h * c:(h + 1) * c] if h2 == h else zero for h2 in range(nh)], axis=1)
             for h in range(nh)], axis=0)

    ci = lax.broadcasted_iota(jnp.int32, (c, c), 0)
    cj = lax.broadcasted_iota(jnp.int32, (c, c), 1)
    tril = jnp.where(ci >= cj, 1.0, 0.0)
    di = lax.broadcasted_iota(jnp.int32, (hd, hd), 0)
    dj = lax.broadcasted_iota(jnp.int32, (hd, hd), 1)
    eye_hd = di == dj
    neg_rate = -jnp.exp(gate_ref[0:1, :])
    dt_bias = gate_ref[1:2, :]
    onorm = onorm_ref[...]
    chunks = range(len(units))

    def stack_cols(x, lane0):
        return jnp.concatenate([x[:, lane0 + h:lane0 + h + 1] for h in range(nh)], axis=0)

    beta, gc, gcc, gr, glast = [], [], [], [], []
    for b, cc in units:
        ba = ba_ref[b * tc + cc * c:b * tc + (cc + 1) * c, :]
        g_all = neg_rate * _softplus(ba + dt_bias)
        gcum = _dot(tril, g_all, HIGHEST)
        gcum_t = gcum.T
        beta.append(stack_cols(_sigmoid(ba), 0))
        gc.append(stack_cols(gcum, GATE_LANE))
        gcc_h = jnp.zeros((c, r), F32)
        for h in range(nh):
            gcc_h = jnp.where(lh == h, gcum[:, GATE_LANE + h:GATE_LANE + h + 1], gcc_h)
        gcc.append(gcc_h)
        gr.append(jnp.concatenate([gcum_t[GATE_LANE + h:GATE_LANE + h + 1, :] for h in range(nh)], axis=1))
        glast.append(jnp.concatenate(
            [jnp.broadcast_to(gcum[c - 1:c, GATE_LANE + h:GATE_LANE + h + 1], (c, 1)) for h in range(nh)], axis=0))

    q = [st_ref[0, cc] * (hd ** -0.5) for cc in chunks]
    k = [st_ref[1, cc] for cc in chunks]
    v = [st_ref[2, cc] for cc in chunks]
    kb = [k[cc] * beta[cc] for cc in chunks]
    big = [_dot_nt(_bf(jnp.concatenate([lanes(kb[cc]), lanes(q[cc])], axis=0)), head_blocks(_bf(k[cc])))
           for cc in chunks]
    decay = [jnp.where(lower, jnp.exp(jnp.where(lower, gcc[cc] - gr[cc], 0.0)), 0.0) for cc in chunks]
    a = [jnp.where(strict, big[cc][:c] * decay[cc], 0.0) for cc in chunks]
    qkd = [expand(_bf(big[cc][c:] * decay[cc])) for cc in chunks]
    tinv = _unit_lower_inverse(a, c, ri, lj, expand)
    eg = [jnp.exp(gc[cc]) for cc in chunks]
    rhs = [_bf(jnp.concatenate([v[cc] * beta[cc], kb[cc] * eg[cc]], axis=1)) for cc in chunks]
    sol = [_dot(expand(_bf(tinv[cc])), rhs[cc]) for cc in chunks]
    solb = [_bf(s) for s in sol]
    fold = [_dot(qkd[cc], solb[cc]) for cc in chunks]
    k_dec = [_bf(k[cc] * jnp.exp(glast[cc] - gc[cc])) for cc in chunks]
    for cc in chunks:
        qp = q[cc] * eg[cc] - fold[cc][:, hd:]
        for h in range(nh):
            rows = slice(h * c, (h + 1) * c)
            kt = _dot_tn(k_dec[cc][rows], solb[cc][rows])
            g_tot = jnp.exp(glast[cc][h * c:h * c + 1, :])
            lhs_ref[cc, h, 0:hd, :] = _bf(jnp.where(eye_hd, g_tot, 0.0) - kt[:, hd:])
            lhs_ref[cc, h, hd:hd + c, :] = _bf(qp[rows])
            add_ref[cc, h, 0:hd, :] = kt[:, :hd]
            add_ref[cc, h, hd:hd + c, :] = fold[cc][rows, :hd]

    for u, (b, cc) in enumerate(units):
        rows = slice(b * tc + cc * c, b * tc + (cc + 1) * c)
        for h in range(nh):
            res = _dot(lhs_ref[u, h], _bf(s_ref[b, h])) + add_ref[u, h]
            s_ref[b, h] = res[:hd]
            o = res[hd:]
            on = o * lax.rsqrt(jnp.mean(o * o, axis=-1, keepdims=True) + EPS) * onorm
            bg = bg_ref[rows, h * hd:(h + 1) * hd]
            o_ref[rows, h * hd:(h + 1) * hd] = (on * _silu(bg)).astype(BF16)
    sout_ref[...] = s_ref[...]


def _delta(z, hist, s0, convw, gate, onorm, *, nb, bt, t, tc, c):
    nt = t // tc
    units = bt * (tc // c)
    state = (bt, B_HEADS, B_HEAD_DIM, B_HEAD_DIM)
    kern = functools.partial(_delta_kernel, bt=bt, c=c, tc=tc)
    return pl.pallas_call(
        kern,
        grid=(nb // bt, nt),
        in_specs=[
            pl.BlockSpec((bt * tc, 3 * B_WIDTH), lambda b, i: (b * nt + i, Z_BQKV // (3 * B_WIDTH))),
            pl.BlockSpec((bt * tc, LANES), lambda b, i: (b * nt + i, Z_BA // LANES)),
            pl.BlockSpec((bt * tc, B_WIDTH), lambda b, i: (b * nt + i, Z_BG // B_WIDTH)),
            pl.BlockSpec((bt, SUBLANES, 3 * B_WIDTH), lambda b, i: (b, 0, 0)),
            pl.BlockSpec(state, lambda b, i: (b, 0, 0, 0)),
            pl.BlockSpec((CONV_W, 3 * B_WIDTH), lambda b, i: (0, 0)),
            pl.BlockSpec((2, LANES), lambda b, i: (0, 0)),
            pl.BlockSpec((1, B_HEAD_DIM), lambda b, i: (0, 0)),
        ],
        out_specs=[
            pl.BlockSpec((bt * tc, B_WIDTH), lambda b, i: (b * nt + i, 0)),
            pl.BlockSpec(state, lambda b, i: (b, 0, 0, 0)),
        ],
        out_shape=[
            jax.ShapeDtypeStruct((nb * t, B_WIDTH), BF16),
            jax.ShapeDtypeStruct((nb, B_HEADS, B_HEAD_DIM, B_HEAD_DIM), F32),
        ],
        scratch_shapes=[
            pltpu.VMEM((bt, SUBLANES + tc, 3 * B_WIDTH), F32),
            pltpu.VMEM((3, units, B_HEADS * c, B_HEAD_DIM), F32),
            pltpu.VMEM((units, B_HEADS, B_HEAD_DIM + c, B_HEAD_DIM), BF16),
            pltpu.VMEM((units, B_HEADS, B_HEAD_DIM + c, B_HEAD_DIM), F32),
            pltpu.VMEM(state, F32),
        ],
        compiler_params=pltpu.CompilerParams(dimension_semantics=("arbitrary", "arbitrary"),
                                             vmem_limit_bytes=VMEM_LIMIT),
        name="gated_delta",
    )(z, z, z, hist, s0, convw, gate, onorm)


def _pool_kernel(h_ref, oa_ref, ob_ref, wa_ref, wb_ref, hist_ref, g_ref, win_ref, wgrp_ref, scale_ref, wout_ref,
                 y_ref, tail_ref, ext_ref, *, bt, tt, pos0):
    i = pl.program_id(1)
    pad = POOL_HIST + 1
    rows = bt * tt

    @pl.when(i == 0)
    def _():
        pos = pos0 - pad + lax.broadcasted_iota(jnp.int32, (1, pad, 1), 1)
        ext_ref[:, 0:pad, :] = jnp.where(pos >= 0, hist_ref[...], 0.0)

    @pl.when(i > 0)
    def _():
        ext_ref[:, 0:pad, :] = ext_ref[:, tt:tt + pad, :]

    x = h_ref[...] + _dot(oa_ref[...], wa_ref[...]) + _dot(ob_ref[...], wb_ref[...])
    xn = x * lax.rsqrt(jnp.mean(x * x, axis=-1, keepdims=True) + EPS) * g_ref[...]
    z = _dot(xn.astype(BF16), win_ref[...])
    u = z[:, :C_WIDTH]
    gate = z[:, C_WIDTH:]
    ext_ref[:, pad:pad + tt, :] = u.reshape(bt, tt, C_WIDTH)
    tail_ref[...] = ext_ref[:, tt:tt + pad, :]

    tpos = pos0 + i * tt + (lax.broadcasted_iota(jnp.int32, (rows, 1), 0) & (tt - 1))
    mixed = []
    for gi, w in enumerate(POOL_SIZES):
        cols = slice(gi * C_GROUP, (gi + 1) * C_GROUP)
        s = ext_ref[:, :, cols].reshape(bt * (pad + tt), C_GROUP)
        sh = 1
        while sh < w:
            s = s + pltpu.roll(s, sh, 0)
            sh *= 2
        s = s.reshape(bt, pad + tt, C_GROUP)[:, pad:, :].reshape(rows, C_GROUP)
        cnt = jnp.minimum(tpos + 1, w).astype(F32)
        pooled = s / cnt - u[:, cols]
        m = _dot(pooled.astype(BF16), wgrp_ref[gi]) * scale_ref[:, cols]
        mixed.append((m * _silu(gate[:, cols])).astype(BF16))
    y_ref[...] = x + _dot(jnp.concatenate(mixed, axis=1), wout_ref[...])


def _pool_layer(h, oa, ob, wa, wb, hist, g, win, wgrp, scale, wout, *, nb, bt, t, tt, pos0):
    nt = t // tt
    pad = POOL_HIST + 1
    rows = bt * tt
    kern = functools.partial(_pool_kernel, bt=bt, tt=tt, pos0=pos0)
    tile = lambda b, i: (b * nt + i, 0)
    const2 = lambda b, i: (0, 0)
    return pl.pallas_call(
        kern,
        grid=(nb // bt, nt),
        in_specs=[
            pl.BlockSpec((rows, D_MODEL), tile),
            pl.BlockSpec((rows, A_WIDTH), tile),
            pl.BlockSpec((rows, B_WIDTH), tile),
            pl.BlockSpec((A_WIDTH, D_MODEL), const2),
            pl.BlockSpec((B_WIDTH, D_MODEL), const2),
            pl.BlockSpec((bt, pad, C_WIDTH), lambda b, i: (b, 0, 0)),
            pl.BlockSpec((1, D_MODEL), const2),
            pl.BlockSpec((D_MODEL, 2 * C_WIDTH), const2),
            pl.BlockSpec((len(POOL_SIZES), C_GROUP, C_GROUP), lambda b, i: (0, 0, 0)),
            pl.BlockSpec((1, C_WIDTH), const2),
            pl.BlockSpec((C_WIDTH, D_MODEL), const2),
        ],
        out_specs=[
            pl.BlockSpec((rows, D_MODEL), tile),
            pl.BlockSpec((bt, pad, C_WIDTH), lambda b, i: (b, 0, 0)),
        ],
        out_shape=[
            jax.ShapeDtypeStruct((nb * t, D_MODEL), F32),
            jax.ShapeDtypeStruct((nb, pad, C_WIDTH), F32),
        ],
        scratch_shapes=[pltpu.VMEM((bt, pad + tt, C_WIDTH), F32)],
        compiler_params=pltpu.CompilerParams(dimension_semantics=("arbitrary", "arbitrary"),
                                             vmem_limit_bytes=VMEM_LIMIT),
        name="out_proj_pool_layer",
    )(h, oa, ob, wa, wb, hist, g, win, wgrp, scale, wout)


def _ab_weights(norm_g, w_in, q_norm, k_norm, sinks, conv_w, a_log, dt_bias, o_norm, w_out):
    aq, ak, av, ag, bqkv, bg, bbeta, balpha = jnp.split(
        w_in, [A_WIDTH, A_WIDTH + A_KV_WIDTH, A_WIDTH + 2 * A_KV_WIDTH, 2 * A_WIDTH + 2 * A_KV_WIDTH,
               2 * A_WIDTH + 2 * A_KV_WIDTH + 3 * B_WIDTH, 2 * A_WIDTH + 2 * A_KV_WIDTH + 4 * B_WIDTH,
               2 * A_WIDTH + 2 * A_KV_WIDTH + 4 * B_WIDTH + B_HEADS], axis=1)
    zpad = jnp.zeros((D_MODEL, LANES - 2 * B_HEADS), w_in.dtype)
    order = jnp.array(HEAD_ORDER)
    aq = aq.reshape(D_MODEL, A_HEADS, A_HEAD_DIM)[:, order].reshape(D_MODEL, A_WIDTH)
    ag = ag.reshape(D_MODEL, A_HEADS, A_HEAD_DIM)[:, order].reshape(D_MODEL, A_WIDTH)
    wa = w_out[:A_WIDTH].reshape(A_HEADS, A_HEAD_DIM, D_MODEL)[order].reshape(A_WIDTH, D_MODEL)
    w = jnp.concatenate([aq, ag, bg, bqkv, ak, av, bbeta, balpha, zpad], axis=1).astype(BF16)
    lane_pad = (GATE_LANE, LANES - GATE_LANE - B_HEADS)
    gate = jnp.stack([jnp.pad(a_log.astype(F32), lane_pad), jnp.pad(dt_bias.astype(F32), lane_pad)])
    return dict(
        norm_g=norm_g.reshape(1, D_MODEL), w=w, kn=jnp.tile(k_norm, A_KV_HEADS).reshape(1, A_KV_WIDTH),
        qn=jnp.tile(q_norm, A_KV_HEADS).reshape(1, LANES), sinks=sinks.astype(F32), conv_w=conv_w, gate=gate,
        onorm=o_norm.reshape(1, B_HEAD_DIM), wa=wa.astype(BF16), wb=w_out[A_WIDTH:].astype(BF16))


def _group_tiles(t):
    if t >= CHUNK:
        return dict(proj_rows=512, seqs_attn=1, seqs_delta=1, delta_tokens=256, chunk=CHUNK, seqs_pool=1, pool_tokens=512)
    return dict(proj_rows=512, seqs_attn=8, seqs_delta=8, delta_tokens=t, chunk=t, seqs_pool=512 // t, pool_tokens=t)


def _ab_mixers(h, wts, cache_k, cache_v, s0, conv_hist, *, nb, t):
    tl = _group_tiles(t)
    z = _in_proj(h, wts["norm_g"], wts["w"], wts["kn"], min(nb * t, tl["proj_rows"]))
    bt = tl["seqs_attn"]
    if cache_k is None:
        cq, lk, off = CHUNK, WINDOW + CHUNK, 0
        kv_specs = [pl.BlockSpec((bt * t, A_KV_WIDTH), lambda b: (b, Z_AK // A_KV_WIDTH)),
                    pl.BlockSpec((bt * t, A_KV_WIDTH), lambda b: (b, Z_AV // A_KV_WIDTH))]
        kbuf = vbuf = z
    else:
        cq, lk, off = t, WINDOW + t, WINDOW
        k_new = z[:, Z_AK:Z_AK + A_KV_WIDTH].reshape(nb, t, A_KV_WIDTH)
        v_new = z[:, Z_AV:Z_AV + A_KV_WIDTH].reshape(nb, t, A_KV_WIDTH)
        kbuf = jnp.concatenate([cache_k.reshape(nb, WINDOW, A_KV_WIDTH), k_new], axis=1).reshape(nb * lk, A_KV_WIDTH)
        vbuf = jnp.concatenate([cache_v.reshape(nb, WINDOW, A_KV_WIDTH), v_new], axis=1).reshape(nb * lk, A_KV_WIDTH)
        kv_specs = [pl.BlockSpec((bt * lk, A_KV_WIDTH), lambda b: (b, 0))] * 2
    o_a = _attention(z, kbuf, vbuf, kv_specs, wts["sinks"], wts["qn"], nb=nb, bt=bt, t=t, cq=cq, lk=lk, off=off)
    hist = jnp.pad(conv_hist.astype(F32), ((0, 0), (SUBLANES - (CONV_W - 1), 0), (0, 0)))
    o_b, s_new = _delta(z, hist, s0.astype(F32), wts["conv_w"], wts["gate"], wts["onorm"], nb=nb,
                        bt=tl["seqs_delta"], t=t, tc=tl["delta_tokens"], c=tl["chunk"])
    return o_a, o_b, z, s_new


def _last_rows(z, col, width, nb, t, rows):
    return z.reshape(nb, t, Z_WIDTH)[:, t - rows:, col:col + width]


def _cache_rows(z, col, nb, t, rows):
    return _last_rows(z, col, A_KV_WIDTH, nb, t, rows).reshape(nb, rows, A_KV_HEADS, A_HEAD_DIM)


def kernel(x_prompt, x_sample, cache_a_k, cache_a_v, state_b_s, state_b_conv, state_c_pool,
           norm_ab, w_in_ab, q_norm_a, k_norm_a, sinks_a, conv_b, a_log_b, dt_bias_b, o_norm_b, w_out_ab,
           norm_c, w_in_c, w_grp_c, scale_c, w_out_c):
    bp, tp, _ = x_prompt.shape
    bs, ts, _ = x_sample.shape
    hp = x_prompt.reshape(bp * tp, D_MODEL)
    hs = x_sample.reshape(bs * ts, D_MODEL)

    wts = _ab_weights(norm_ab[0], w_in_ab[0], q_norm_a[0], k_norm_a[0], sinks_a[0], conv_b[0], a_log_b[0],
                      dt_bias_b[0], o_norm_b[0], w_out_ab[0])
    s0 = jnp.zeros((bp, B_HEADS, B_HEAD_DIM, B_HEAD_DIM), F32)
    c0 = jnp.zeros((bp, CONV_W - 1, 3 * B_WIDTH), F32)
    oap, obp, zp, sp = _ab_mixers(hp, wts, None, None, s0, c0, nb=bp, t=tp)
    oas, obs, zs, ss = _ab_mixers(hs, wts, cache_a_k[0], cache_a_v[0], state_b_s[0], state_b_conv[0], nb=bs, t=ts)
    p_a_k = _cache_rows(zp, Z_AK, bp, tp, WINDOW)[None]
    p_a_v = _cache_rows(zp, Z_AV, bp, tp, WINDOW)[None]
    s_a_k = _cache_rows(zs, Z_AK, bs, ts, ts)[None]
    s_a_v = _cache_rows(zs, Z_AV, bs, ts, ts)[None]
    p_b_conv = _last_rows(zp, Z_BQKV, 3 * B_WIDTH, bp, tp, CONV_W - 1)[None]
    s_b_conv = _last_rows(zs, Z_BQKV, 3 * B_WIDTH, bs, ts, CONV_W - 1)[None]

    g_c = norm_c[0].reshape(1, D_MODEL)
    win = w_in_c[0].astype(BF16)
    wgrp = w_grp_c[0].astype(BF16)
    scale = scale_c[0].reshape(1, C_WIDTH)
    wout = w_out_c[0].astype(BF16)
    h0 = jnp.zeros((bp, POOL_HIST + 1, C_WIDTH), F32)
    hs0 = jnp.pad(state_c_pool[0].astype(F32), ((0, 0), (1, 0), (0, 0)))
    tlp, tls = _group_tiles(tp), _group_tiles(ts)
    yp, tail_p = _pool_layer(hp, oap, obp, wts["wa"], wts["wb"], h0, g_c, win, wgrp, scale, wout, nb=bp,
                             bt=tlp["seqs_pool"], t=tp, tt=tlp["pool_tokens"], pos0=0)
    ys, tail_s = _pool_layer(hs, oas, obs, wts["wa"], wts["wb"], hs0, g_c, win, wgrp, scale, wout, nb=bs,
                             bt=tls["seqs_pool"], t=ts, tt=tls["pool_tokens"], pos0=PAST_LEN)

    return (yp.reshape(bp, tp, D_MODEL), ys.reshape(bs, ts, D_MODEL),
            p_a_k, p_a_v, sp[None], p_b_conv, tail_p[:, 1:][None],
            s_a_k, s_a_v, ss[None], s_b_conv, tail_s[:, 1:][None])
```

```python
import functools

import jax
import jax.numpy as jnp
from jax import lax
from jax.experimental import pallas as pl
from jax.experimental.pallas import tpu as pltpu

F32 = jnp.float32
BF16 = jnp.bfloat16
HIGHEST = lax.Precision.HIGHEST

D_MODEL = 1024
CHUNK = 64
PAST_LEN = 2048
EPS = 1e-6
NEG_INF = -1e30
LOG2E = 1.4426950408889634

A_HEADS = 8
A_KV_HEADS = 2
A_HEAD_DIM = 64
A_WIDTH = A_HEADS * A_HEAD_DIM
A_KV_WIDTH = A_KV_HEADS * A_HEAD_DIM
A_REP = A_HEADS // A_KV_HEADS
WINDOW = 128
HEAD_ORDER = tuple(g * A_REP + r for r in range(A_REP) for g in range(A_KV_HEADS))

B_HEADS = 4
B_HEAD_DIM = 128
B_WIDTH = B_HEADS * B_HEAD_DIM
CONV_W = 4

POOL_SIZES = (2, 4, 8, 16)
C_WIDTH = D_MODEL
C_GROUP = C_WIDTH // len(POOL_SIZES)
POOL_HIST = max(POOL_SIZES) - 1

LANES = 128
SUBLANES = 8

Z_AQ = 0
Z_AG = A_WIDTH
Z_BG = 2 * A_WIDTH
Z_BQKV = 3 * A_WIDTH
Z_AK = Z_BQKV + 3 * B_WIDTH
Z_AV = Z_AK + A_KV_WIDTH
Z_BA = Z_AV + A_KV_WIDTH
Z_WIDTH = Z_BA + LANES
GATE_LANE = B_HEADS

VMEM_LIMIT = 48 * 1024 * 1024


def _sigmoid(x):
    return 1.0 / (1.0 + jnp.exp(-x))


def _silu(x):
    return x * _sigmoid(x)


def _softplus(x):
    return jnp.maximum(x, 0.0) + jnp.log(1.0 + jnp.exp(-jnp.abs(x)))


def _dot(a, b, precision=None):
    return jnp.dot(a, b, preferred_element_type=F32, precision=precision)


def _dot_nt(a, b, precision=None):
    return lax.dot_general(a, b, (((1,), (1,)), ((), ())), preferred_element_type=F32, precision=precision)


def _dot_tn(a, b, precision=None):
    return lax.dot_general(a, b, (((0,), (0,)), ((), ())), preferred_element_type=F32, precision=precision)


def _log2(n):
    assert n & (n - 1) == 0
    return n.bit_length() - 1


def _in_proj_kernel(x_ref, g_ref, w_ref, kn_ref, z_ref):
    x = x_ref[...]
    xn = x * lax.rsqrt(jnp.mean(x * x, axis=-1, keepdims=True) + EPS) * g_ref[...]
    z_ref[...] = _dot(xn.astype(BF16), w_ref[...])
    k = z_ref[:, Z_AK:Z_AK + A_KV_WIDTH]
    sq = k * k
    lane = lax.broadcasted_iota(jnp.int32, k.shape, 1)
    first = lane < A_HEAD_DIM
    s0 = jnp.sum(jnp.where(first, sq, 0.0), axis=-1, keepdims=True)
    s1 = jnp.sum(jnp.where(first, 0.0, sq), axis=-1, keepdims=True)
    ms = jnp.where(first, s0, s1) * (1.0 / A_HEAD_DIM)
    z_ref[:, Z_AK:Z_AK + A_KV_WIDTH] = k * lax.rsqrt(ms + EPS) * kn_ref[...]


def _in_proj(x, g, w, kn, tm):
    n = x.shape[0]
    return pl.pallas_call(
        _in_proj_kernel,
        grid=(n // tm,),
        in_specs=[
            pl.BlockSpec((tm, D_MODEL), lambda i: (i, 0)),
            pl.BlockSpec((1, D_MODEL), lambda i: (0, 0)),
            pl.BlockSpec((D_MODEL, Z_WIDTH), lambda i: (0, 0)),
            pl.BlockSpec((1, A_KV_WIDTH), lambda i: (0, 0)),
        ],
        out_specs=pl.BlockSpec((tm, Z_WIDTH), lambda i: (i, 0)),
        out_shape=jax.ShapeDtypeStruct((n, Z_WIDTH), F32),
        compiler_params=pltpu.CompilerParams(dimension_semantics=("arbitrary",), vmem_limit_bytes=VMEM_LIMIT),
        name="in_proj",
    )(x, g, w, kn)


def _attn_kernel(sinks_ref, q_ref, k_ref, v_ref, ag_ref, qn_ref, o_ref, kp_ref, vp_ref, bias_ref, *, bt, cq, lk, off, t):
    rows = A_REP * cq
    front = WINDOW - off
    nvar = front // cq + 1
    lkt = k_ref.shape[0] // bt
    n_chunks = t // cq

    for b in range(bt):
        if front:
            kp_ref[b, 0:front, :] = jnp.zeros((front, A_KV_WIDTH), BF16)
            vp_ref[b, 0:front, :] = jnp.zeros((front, A_KV_WIDTH), BF16)
        kp_ref[b, front:front + lkt, :] = k_ref[b * lkt:(b + 1) * lkt, :].astype(BF16)
        vp_ref[b, front:front + lkt, :] = v_ref[b * lkt:(b + 1) * lkt, :].astype(BF16)

    row = lax.broadcasted_iota(jnp.int32, (rows, 1), 0)
    rep = row >> _log2(cq)

    @pl.when(pl.program_id(0) == 0)
    def _():
        col = lax.broadcasted_iota(jnp.int32, (rows, lk), 1)
        dist = jnp.abs((row & (cq - 1)) + WINDOW - col).astype(F32)
        for g in range(A_KV_HEADS):
            slope = jnp.zeros((rows, 1), F32)
            for r in range(A_REP):
                slope = jnp.where(rep == r, 2.0 ** (-8.0 * (g * A_REP + r + 1) / A_HEADS), slope)
            for var in range(nvar):
                bias_ref[var, g] = jnp.where(col >= front - var * cq, -slope * dist, NEG_INF) * LOG2E

    lane = lax.broadcasted_iota(jnp.int32, (cq, LANES), 1)
    first = lane < A_HEAD_DIM
    qn = qn_ref[...] * (A_HEAD_DIM ** -0.5 * LOG2E)
    sks = []
    for g in range(A_KV_HEADS):
        sk = jnp.zeros((rows, 1), F32)
        for r in range(A_REP):
            sk = jnp.where(rep == r, sinks_ref[g * A_REP + r] * LOG2E, sk)
        sks.append(sk)

    def chunk(b, cg):
        k0 = cg * cq if isinstance(cg, int) else pl.multiple_of(cg * cq, cq)
        r0 = b * t + k0
        qc = q_ref[pl.ds(r0, cq), :]
        kk = kp_ref[b, pl.ds(k0, lk), :]
        vv = vp_ref[b, pl.ds(k0, lk), :]
        var = min(cg, nvar - 1) if isinstance(cg, int) else jnp.minimum(cg, nvar - 1)
        blocks = []
        for r in range(A_REP):
            x = qc[:, r * LANES:(r + 1) * LANES]
            sq = x * x
            s_lo = jnp.sum(jnp.where(first, sq, 0.0), axis=-1, keepdims=True)
            s_hi = jnp.sum(jnp.where(first, 0.0, sq), axis=-1, keepdims=True)
            ms = jnp.where(first, s_lo, s_hi) * (1.0 / A_HEAD_DIM)
            blocks.append(x * lax.rsqrt(ms + EPS) * qn)
        outs = []
        for g in range(A_KV_HEADS):
            mine = first if g == 0 else jnp.logical_not(first)
            lhs = jnp.concatenate([jnp.where(mine, blk, 0.0) for blk in blocks], axis=0).astype(BF16)
            s = _dot_nt(lhs, kk) + bias_ref[var, g]
            m = jnp.maximum(jnp.max(s, axis=-1, keepdims=True), sks[g])
            p = jnp.exp2(s - m)
            den = jnp.sum(p, axis=-1, keepdims=True) + jnp.exp2(sks[g] - m)
            outs.append(_dot(p.astype(BF16), vv) * (1.0 / den))
        tile = jnp.concatenate(
            [jnp.where(first, outs[0][r * cq:(r + 1) * cq], outs[1][r * cq:(r + 1) * cq]) for r in range(A_REP)], axis=1)
        o_ref[pl.ds(r0, cq), :] = (tile * _silu(ag_ref[pl.ds(r0, cq), :])).astype(BF16)

    for b in range(bt):
        if n_chunks == 1:
            chunk(b, 0)
        else:
            lax.fori_loop(0, n_chunks, lambda cg, carry, b=b: (chunk(b, cg), carry)[1], 0, unroll=min(n_chunks, 8))


def _attention(z, kbuf, vbuf, kv_specs, sinks, qn, *, nb, bt, t, cq, lk, off):
    front = WINDOW - off
    lkt = kv_specs[0].block_shape[0] // bt
    kern = functools.partial(_attn_kernel, bt=bt, cq=cq, lk=lk, off=off, t=t)
    return pl.pallas_call(
        kern,
        grid=(nb // bt,),
        in_specs=[
            pl.BlockSpec(memory_space=pltpu.SMEM),
            pl.BlockSpec((bt * t, A_WIDTH), lambda b: (b, Z_AQ // A_WIDTH)),
            kv_specs[0],
            kv_specs[1],
            pl.BlockSpec((bt * t, A_WIDTH), lambda b: (b, Z_AG // A_WIDTH)),
            pl.BlockSpec((1, LANES), lambda b: (0, 0)),
        ],
        out_specs=pl.BlockSpec((bt * t, A_WIDTH), lambda b: (b, 0)),
        out_shape=jax.ShapeDtypeStruct((nb * t, A_WIDTH), BF16),
        scratch_shapes=[
            pltpu.VMEM((bt, front + lkt, A_KV_WIDTH), BF16),
            pltpu.VMEM((bt, front + lkt, A_KV_WIDTH), BF16),
            pltpu.VMEM((front // cq + 1, A_KV_HEADS, A_REP * cq, lk), F32),
        ],
        compiler_params=pltpu.CompilerParams(dimension_semantics=("arbitrary",), vmem_limit_bytes=VMEM_LIMIT),
        name="swa_attention",
    )(sinks, z, kbuf, vbuf, z, qn)


def _bf(x):
    return x.astype(BF16)


def _unit_lower_inverse(a_list, c, ri, lj, expand):
    base = SUBLANES
    same = (ri >> _log2(base)) == (lj >> _log2(base))
    eye = jnp.where(ri == lj, 1.0, 0.0)
    n1 = [jnp.where(same, -a, 0.0) for a in a_list]
    x = [eye + n for n in n1]
    n1b = [_bf(n) for n in n1]
    n2b = [_bf(_dot(nb, expand(nb))) for nb in n1b]
    n2e = [expand(nb) for nb in n2b]
    x = [xi + _dot(_bf(xi), ne) for xi, ne in zip(x, n2e)]
    n4e = [expand(_bf(_dot(nb, ne))) for nb, ne in zip(n2b, n2e)]
    x = [xi + _dot(_bf(xi), ne) for xi, ne in zip(x, n4e)]
    s = base
    while s < c:
        sel = ((ri >> _log2(2 * s)) == (lj >> _log2(2 * s))) & ((ri >> _log2(s)) != (lj >> _log2(s)))
        xb = [_bf(xi) for xi in x]
        xo = [_dot(b, expand(_bf(jnp.where(sel, a, 0.0)))) for b, a in zip(xb, a_list)]
        x = [xi - _dot(_bf(o), expand(b)) for xi, o, b in zip(x, xo, xb)]
        s *= 2
    return x


def _delta_kernel(qkv_ref, ba_ref, bg_ref, hist_ref, s0_ref, convw_ref, gate_ref, onorm_ref,
                  o_ref, sout_ref, ext_ref, st_ref, lhs_ref, add_ref, s_ref, *, bt, c, tc):
    i = pl.program_id(1)
    hd = B_HEAD_DIM
    nh = B_HEADS
    r = nh * c
    nch = tc // c
    pad = SUBLANES
    units = [(b, cc) for b in range(bt) for cc in range(nch)]

    @pl.when(i == 0)
    def _():
        ext_ref[:, 0:pad, :] = hist_ref[...]
        s_ref[...] = s0_ref[...]

    @pl.when(i > 0)
    def _():
        ext_ref[:, 0:pad, :] = ext_ref[:, tc:tc + pad, :]

    for b in range(bt):
        ext_ref[b, pad:pad + tc, :] = qkv_ref[b * tc:(b + 1) * tc, :]

    for u, (b, cc) in enumerate(units):
        base = pad - (CONV_W - 1) + cc * c
        acc = ext_ref[b, base:base + c, :] * convw_ref[0:1, :]
        for j in range(1, CONV_W):
            acc = acc + ext_ref[b, base + j:base + j + c, :] * convw_ref[j:j + 1, :]
        y = _silu(acc)
        for part in range(3):
            for h in range(nh):
                blk = y[:, (part * nh + h) * hd:(part * nh + h + 1) * hd]
                if part < 2:
                    blk = blk * lax.rsqrt(jnp.sum(blk * blk, axis=-1, keepdims=True) + EPS)
                st_ref[part, u, h * c:(h + 1) * c, :] = blk

    ri = lax.broadcasted_iota(jnp.int32, (c, r), 0)
    li = lax.broadcasted_iota(jnp.int32, (c, r), 1)
    lj = li & (c - 1)
    lh = li >> _log2(c)
    lower = ri >= lj
    strict = ri > lj
    head_sel = [jnp.where(lh == h, 1.0, 0.0).astype(BF16) for h in range(nh)]

    def expand(xb):
        return jnp.concatenate([xb * m for m in head_sel], axis=0)

    def lanes(x_st):
        return jnp.concatenate([x_st[h * c:(h + 1) * c] for h in range(nh)], axis=1)

    def head_blocks(x_st):
        zero = jnp.zeros((c, hd), x_st.dtype)
        return jnp.concatenate(
            [jnp.concatenate([x_st[h * c:(h + 1) * c] if h2 == h else zero for h2 in range(nh)], axis=1)
             for h in range(nh)], axis=0)

    ci = lax.broadcasted_iota(jnp.int32, (c, c), 0)
    cj = lax.broadcasted_iota(jnp.int32, (c, c), 1)
    tril = jnp.where(ci >= cj, 1.0, 0.0)
    di = lax.broadcasted_iota(jnp.int32, (hd, hd), 0)
    dj = lax.broadcasted_iota(jnp.int32, (hd, hd), 1)
    eye_hd = di == dj
    neg_rate = -jnp.exp(gate_ref[0:1, :])
    dt_bias = gate_ref[1:2, :]
    onorm = onorm_ref[...]

    def stack_cols(x, lane0):
        return jnp.concatenate([x[:, lane0 + h:lane0 + h + 1] for h in range(nh)], axis=0)

    beta, gc, gcc, gr, glast = [], [], [], [], []
    for b, cc in units:
        ba = ba_ref[b * tc + cc * c:b * tc + (cc + 1) * c, :]
        g_all = neg_rate * _softplus(ba + dt_bias)
        gcum = _dot(tril, g_all, HIGHEST)
        gcum_t = gcum.T
        beta.append(stack_cols(_sigmoid(ba), 0))
        gc.append(stack_cols(gcum, GATE_LANE))
        gcc_h = jnp.zeros((c, r), F32)
        for h in range(nh):
            gcc_h = jnp.where(lh == h, gcum[:, GATE_LANE + h:GATE_LANE + h + 1], gcc_h)
        gcc.append(gcc_h)
        gr.append(jnp.concatenate([gcum_t[GATE_LANE + h:GATE_LANE + h + 1, :] for h in range(nh)], axis=1))
        glast.append(jnp.concatenate(
            [jnp.broadcast_to(gcum[c - 1:c, GATE_LANE + h:GATE_LANE + h + 1], (c, 1)) for h in range(nh)], axis=0))

    def fold_units(us):
        n = range(len(us))
        q = [st_ref[0, u] * (hd ** -0.5) for u in us]
        k = [st_ref[1, u] for u in us]
        v = [st_ref[2, u] for u in us]
        kb = [k[j] * beta[u] for j, u in enumerate(us)]
        big = [_dot_nt(_bf(jnp.concatenate([lanes(kb[j]), lanes(q[j])], axis=0)), head_blocks(_bf(k[j]))) for j in n]
        decay = [jnp.where(lower, jnp.exp(jnp.where(lower, gcc[u] - gr[u], 0.0)), 0.0) for u in us]
        a = [jnp.where(strict, big[j][:c] * decay[j], 0.0) for j in n]
        qkd = [expand(_bf(big[j][c:] * decay[j])) for j in n]
        tinv = _unit_lower_inverse(a, c, ri, lj, expand)
        eg = [jnp.exp(gc[u]) for u in us]
        rhs = [_bf(jnp.concatenate([v[j] * beta[u], kb[j] * eg[j]], axis=1)) for j, u in enumerate(us)]
        sol = [_dot(expand(_bf(tinv[j])), rhs[j]) for j in n]
        solb = [_bf(s) for s in sol]
        fold = [_dot(qkd[j], solb[j]) for j in n]
        k_dec = [_bf(k[j] * jnp.exp(glast[u] - gc[u])) for j, u in enumerate(us)]
        for j, u in enumerate(us):
            qp = q[j] * eg[j] - fold[j][:, hd:]
            for h in range(nh):
                rows = slice(h * c, (h + 1) * c)
                kt = _dot_tn(k_dec[j][rows], solb[j][rows])
                g_tot = jnp.exp(glast[u][h * c:h * c + 1, :])
                lhs_ref[u, h, 0:hd, :] = _bf(jnp.where(eye_hd, g_tot, 0.0) - kt[:, hd:])
                lhs_ref[u, h, hd:hd + c, :] = _bf(qp[rows])
                add_ref[u, h, 0:hd, :] = kt[:, :hd]
                add_ref[u, h, hd:hd + c, :] = fold[j][rows, :hd]

    fold_units(list(range(len(units))))

    for u, (b, cc) in enumerate(units):
        rows = slice(b * tc + cc * c, b * tc + (cc + 1) * c)
        for h in range(nh):
            res = _dot(lhs_ref[u, h], _bf(s_ref[b, h])) + add_ref[u, h]
            s_ref[b, h] = res[:hd]
            o = res[hd:]
            on = o * lax.rsqrt(jnp.mean(o * o, axis=-1, keepdims=True) + EPS) * onorm
            bg = bg_ref[rows, h * hd:(h + 1) * hd]
            o_ref[rows, h * hd:(h + 1) * hd] = (on * _silu(bg)).astype(BF16)
    sout_ref[...] = s_ref[...]


def _delta(z, hist, s0, convw, gate, onorm, *, nb, bt, t, tc, c):
    nt = t // tc
    units = bt * (tc // c)
    state = (bt, B_HEADS, B_HEAD_DIM, B_HEAD_DIM)
    kern = functools.partial(_delta_kernel, bt=bt, c=c, tc=tc)
    return pl.pallas_call(
        kern,
        grid=(nb // bt, nt),
        in_specs=[
            pl.BlockSpec((bt * tc, 3 * B_WIDTH), lambda b, i: (b * nt + i, Z_BQKV // (3 * B_WIDTH))),
            pl.BlockSpec((bt * tc, LANES), lambda b, i: (b * nt + i, Z_BA // LANES)),
            pl.BlockSpec((bt * tc, B_WIDTH), lambda b, i: (b * nt + i, Z_BG // B_WIDTH)),
            pl.BlockSpec((bt, SUBLANES, 3 * B_WIDTH), lambda b, i: (b, 0, 0)),
            pl.BlockSpec(state, lambda b, i: (b, 0, 0, 0)),
            pl.BlockSpec((CONV_W, 3 * B_WIDTH), lambda b, i: (0, 0)),
            pl.BlockSpec((2, LANES), lambda b, i: (0, 0)),
            pl.BlockSpec((1, B_HEAD_DIM), lambda b, i: (0, 0)),
        ],
        out_specs=[
            pl.BlockSpec((bt * tc, B_WIDTH), lambda b, i: (b * nt + i, 0)),
            pl.BlockSpec(state, lambda b, i: (b, 0, 0, 0)),
        ],
        out_shape=[
            jax.ShapeDtypeStruct((nb * t, B_WIDTH), BF16),
            jax.ShapeDtypeStruct((nb, B_HEADS, B_HEAD_DIM, B_HEAD_DIM), F32),
        ],
        scratch_shapes=[
            pltpu.VMEM((bt, SUBLANES + tc, 3 * B_WIDTH), F32),
            pltpu.VMEM((3, units, B_HEADS * c, B_HEAD_DIM), F32),
            pltpu.VMEM((units, B_HEADS, B_HEAD_DIM + c, B_HEAD_DIM), BF16),
            pltpu.VMEM((units, B_HEADS, B_HEAD_DIM + c, B_HEAD_DIM), F32),
            pltpu.VMEM(state, F32),
        ],
        compiler_params=pltpu.CompilerParams(dimension_semantics=("arbitrary", "arbitrary"),
                                             vmem_limit_bytes=VMEM_LIMIT),
        name="gated_delta",
    )(z, z, z, hist, s0, convw, gate, onorm)


def _pool_kernel(h_ref, oa_ref, ob_ref, wa_ref, wb_ref, hist_ref, g_ref, win_ref, wgrp_ref, scale_ref, wout_ref,
                 y_ref, tail_ref, ext_ref, *, bt, tt, pos0):
    i = pl.program_id(1)
    pad = POOL_HIST + 1
    rows = bt * tt

    @pl.when(i == 0)
    def _():
        pos = pos0 - pad + lax.broadcasted_iota(jnp.int32, (1, pad, 1), 1)
        ext_ref[:, 0:pad, :] = jnp.where(pos >= 0, hist_ref[...], 0.0)

    @pl.when(i > 0)
    def _():
        ext_ref[:, 0:pad, :] = ext_ref[:, tt:tt + pad, :]

    x = h_ref[...] + _dot(oa_ref[...], wa_ref[...]) + _dot(ob_ref[...], wb_ref[...])
    xn = x * lax.rsqrt(jnp.mean(x * x, axis=-1, keepdims=True) + EPS) * g_ref[...]
    z = _dot(xn.astype(BF16), win_ref[...])
    u = z[:, :C_WIDTH]
    gate = z[:, C_WIDTH:]
    ext_ref[:, pad:pad + tt, :] = u.reshape(bt, tt, C_WIDTH)
    tail_ref[...] = ext_ref[:, tt:tt + pad, :]

    tpos = pos0 + i * tt + (lax.broadcasted_iota(jnp.int32, (rows, 1), 0) & (tt - 1))
    mixed = []
    for gi, w in enumerate(POOL_SIZES):
        cols = slice(gi * C_GROUP, (gi + 1) * C_GROUP)
        s = ext_ref[:, :, cols].reshape(bt * (pad + tt), C_GROUP)
        sh = 1
        while sh < w:
            s = s + pltpu.roll(s, sh, 0)
            sh *= 2
        s = s.reshape(bt, pad + tt, C_GROUP)[:, pad:, :].reshape(rows, C_GROUP)
        cnt = jnp.minimum(tpos + 1, w).astype(F32)
        pooled = s / cnt - u[:, cols]
        m = _dot(pooled.astype(BF16), wgrp_ref[gi]) * scale_ref[:, cols]
        mixed.append((m * _silu(gate[:, cols])).astype(BF16))
    y_ref[...] = x + _dot(jnp.concatenate(mixed, axis=1), wout_ref[...])


def _pool_layer(h, oa, ob, wa, wb, hist, g, win, wgrp, scale, wout, *, nb, bt, t, tt, pos0):
    nt = t // tt
    pad = POOL_HIST + 1
    rows = bt * tt
    kern = functools.partial(_pool_kernel, bt=bt, tt=tt, pos0=pos0)
    tile = lambda b, i: (b * nt + i, 0)
    const2 = lambda b, i: (0, 0)
    return pl.pallas_call(
        kern,
        grid=(nb // bt, nt),
        in_specs=[
            pl.BlockSpec((rows, D_MODEL), tile),
            pl.BlockSpec((rows, A_WIDTH), tile),
            pl.BlockSpec((rows, B_WIDTH), tile),
            pl.BlockSpec((A_WIDTH, D_MODEL), const2),
            pl.BlockSpec((B_WIDTH, D_MODEL), const2),
            pl.BlockSpec((bt, pad, C_WIDTH), lambda b, i: (b, 0, 0)),
            pl.BlockSpec((1, D_MODEL), const2),
            pl.BlockSpec((D_MODEL, 2 * C_WIDTH), const2),
            pl.BlockSpec((len(POOL_SIZES), C_GROUP, C_GROUP), lambda b, i: (0, 0, 0)),
            pl.BlockSpec((1, C_WIDTH), const2),
            pl.BlockSpec((C_WIDTH, D_MODEL), const2),
        ],
        out_specs=[
            pl.BlockSpec((rows, D_MODEL), tile),
            pl.BlockSpec((bt, pad, C_WIDTH), lambda b, i: (b, 0, 0)),
        ],
        out_shape=[
            jax.ShapeDtypeStruct((nb * t, D_MODEL), F32),
            jax.ShapeDtypeStruct((nb, pad, C_WIDTH), F32),
        ],
        scratch_shapes=[pltpu.VMEM((bt, pad + tt, C_WIDTH), F32)],
        compiler_params=pltpu.CompilerParams(dimension_semantics=("arbitrary", "arbitrary"),
                                             vmem_limit_bytes=VMEM_LIMIT),
        name="out_proj_pool_layer",
    )(h, oa, ob, wa, wb, hist, g, win, wgrp, scale, wout)


def _ab_weights(norm_g, w_in, q_norm, k_norm, sinks, conv_w, a_log, dt_bias, o_norm, w_out):
    aq, ak, av, ag, bqkv, bg, bbeta, balpha = jnp.split(
        w_in, [A_WIDTH, A_WIDTH + A_KV_WIDTH, A_WIDTH + 2 * A_KV_WIDTH, 2 * A_WIDTH + 2 * A_KV_WIDTH,
               2 * A_WIDTH + 2 * A_KV_WIDTH + 3 * B_WIDTH, 2 * A_WIDTH + 2 * A_KV_WIDTH + 4 * B_WIDTH,
               2 * A_WIDTH + 2 * A_KV_WIDTH + 4 * B_WIDTH + B_HEADS], axis=1)
    zpad = jnp.zeros((D_MODEL, LANES - 2 * B_HEADS), w_in.dtype)
    order = jnp.array(HEAD_ORDER)
    aq = aq.reshape(D_MODEL, A_HEADS, A_HEAD_DIM)[:, order].reshape(D_MODEL, A_WIDTH)
    ag = ag.reshape(D_MODEL, A_HEADS, A_HEAD_DIM)[:, order].reshape(D_MODEL, A_WIDTH)
    wa = w_out[:A_WIDTH].reshape(A_HEADS, A_HEAD_DIM, D_MODEL)[order].reshape(A_WIDTH, D_MODEL)
    w = jnp.concatenate([aq, ag, bg, bqkv, ak, av, bbeta, balpha, zpad], axis=1).astype(BF16)
    lane_pad = (GATE_LANE, LANES - GATE_LANE - B_HEADS)
    gate = jnp.stack([jnp.pad(a_log.astype(F32), lane_pad), jnp.pad(dt_bias.astype(F32), lane_pad)])
    return dict(
        norm_g=norm_g.reshape(1, D_MODEL), w=w, kn=jnp.tile(k_norm, A_KV_HEADS).reshape(1, A_KV_WIDTH),
        qn=jnp.tile(q_norm, A_KV_HEADS).reshape(1, LANES), sinks=sinks.astype(F32), conv_w=conv_w, gate=gate,
        onorm=o_norm.reshape(1, B_HEAD_DIM), wa=wa.astype(BF16), wb=w_out[A_WIDTH:].astype(BF16))


def _group_tiles(t):
    if t >= CHUNK:
        return dict(proj_rows=512, seqs_attn=1, seqs_delta=1, delta_tokens=512, chunk=CHUNK, seqs_pool=1, pool_tokens=512)
    return dict(proj_rows=512, seqs_attn=8, seqs_delta=16, delta_tokens=t, chunk=t, seqs_pool=512 // t, pool_tokens=t)


def _ab_mixers(h, wts, cache_k, cache_v, s0, conv_hist, *, nb, t):
    tl = _group_tiles(t)
    z = _in_proj(h, wts["norm_g"], wts["w"], wts["kn"], min(nb * t, tl["proj_rows"]))
    bt = tl["seqs_attn"]
    if cache_k is None:
        cq, lk, off = CHUNK, WINDOW + CHUNK, 0
        kv_specs = [pl.BlockSpec((bt * t, A_KV_WIDTH), lambda b: (b, Z_AK // A_KV_WIDTH)),
                    pl.BlockSpec((bt * t, A_KV_WIDTH), lambda b: (b, Z_AV // A_KV_WIDTH))]
        kbuf = vbuf = z
    else:
        cq, lk, off = t, WINDOW + t, WINDOW
        k_new = z[:, Z_AK:Z_AK + A_KV_WIDTH].reshape(nb, t, A_KV_WIDTH)
        v_new = z[:, Z_AV:Z_AV + A_KV_WIDTH].reshape(nb, t, A_KV_WIDTH)
        kbuf = jnp.concatenate([cache_k.reshape(nb, WINDOW, A_KV_WIDTH), k_new], axis=1).reshape(nb * lk, A_KV_WIDTH)
        vbuf = jnp.concatenate([cache_v.reshape(nb, WINDOW, A_KV_WIDTH), v_new], axis=1).reshape(nb * lk, A_KV_WIDTH)
        kv_specs = [pl.BlockSpec((bt * lk, A_KV_WIDTH), lambda b: (b, 0))] * 2
    o_a = _attention(z, kbuf, vbuf, kv_specs, wts["sinks"], wts["qn"], nb=nb, bt=bt, t=t, cq=cq, lk=lk, off=off)
    hist = jnp.pad(conv_hist.astype(F32), ((0, 0), (SUBLANES - (CONV_W - 1), 0), (0, 0)))
    o_b, s_new = _delta(z, hist, s0.astype(F32), wts["conv_w"], wts["gate"], wts["onorm"], nb=nb,
                        bt=tl["seqs_delta"], t=t, tc=tl["delta_tokens"], c=tl["chunk"])
    return o_a, o_b, z, s_new


def _last_rows(z, col, width, nb, t, rows):
    return z.reshape(nb, t, Z_WIDTH)[:, t - rows:, col:col + width]


def _cache_rows(z, col, nb, t, rows):
    return _last_rows(z, col, A_KV_WIDTH, nb, t, rows).reshape(nb, rows, A_KV_HEADS, A_HEAD_DIM)


def kernel(x_prompt, x_sample, cache_a_k, cache_a_v, state_b_s, state_b_conv, state_c_pool,
           norm_ab, w_in_ab, q_norm_a, k_norm_a, sinks_a, conv_b, a_log_b, dt_bias_b, o_norm_b, w_out_ab,
           norm_c, w_in_c, w_grp_c, scale_c, w_out_c):
    bp, tp, _ = x_prompt.shape
    bs, ts, _ = x_sample.shape
    hp = x_prompt.reshape(bp * tp, D_MODEL)
    hs = x_sample.reshape(bs * ts, D_MODEL)

    wts = _ab_weights(norm_ab[0], w_in_ab[0], q_norm_a[0], k_norm_a[0], sinks_a[0], conv_b[0], a_log_b[0],
                      dt_bias_b[0], o_norm_b[0], w_out_ab[0])
    s0 = jnp.zeros((bp, B_HEADS, B_HEAD_DIM, B_HEAD_DIM), F32)
    c0 = jnp.zeros((bp, CONV_W - 1, 3 * B_WIDTH), F32)
    oap, obp, zp, sp = _ab_mixers(hp, wts, None, None, s0, c0, nb=bp, t=tp)
    oas, obs, zs, ss = _ab_mixers(hs, wts, cache_a_k[0], cache_a_v[0], state_b_s[0], state_b_conv[0], nb=bs, t=ts)
    p_a_k = _cache_rows(zp, Z_AK, bp, tp, WINDOW)[None]
    p_a_v = _cache_rows(zp, Z_AV, bp, tp, WINDOW)[None]
    s_a_k = _cache_rows(zs, Z_AK, bs, ts, ts)[None]
    s_a_v = _cache_rows(zs, Z_AV, bs, ts, ts)[None]
    p_b_conv = _last_rows(zp, Z_BQKV, 3 * B_WIDTH, bp, tp, CONV_W - 1)[None]
    s_b_conv = _last_rows(zs, Z_BQKV, 3 * B_WIDTH, bs, ts, CONV_W - 1)[None]

    g_c = norm_c[0].reshape(1, D_MODEL)
    win = w_in_c[0].astype(BF16)
    wgrp = w_grp_c[0].astype(BF16)
    scale = scale_c[0].reshape(1, C_WIDTH)
    wout = w_out_c[0].astype(BF16)
    h0 = jnp.zeros((bp, POOL_HIST + 1, C_WIDTH), F32)
    hs0 = jnp.pad(state_c_pool[0].astype(F32), ((0, 0), (1, 0), (0, 0)))
    tlp, tls = _group_tiles(tp), _group_tiles(ts)
    yp, tail_p = _pool_layer(hp, oap, obp, wts["wa"], wts["wb"], h0, g_c, win, wgrp, scale, wout, nb=bp,
                             bt=tlp["seqs_pool"], t=tp, tt=tlp["pool_tokens"], pos0=0)
    ys, tail_s = _pool_layer(hs, oas, obs, wts["wa"], wts["wb"], hs0, g_c, win, wgrp, scale, wout, nb=bs,
                             bt=tls["seqs_pool"], t=ts, tt=tls["pool_tokens"], pos0=PAST_LEN)

    return (yp.reshape(bp, tp, D_MODEL), ys.reshape(bs, ts, D_MODEL),
            p_a_k, p_a_v, sp[None], p_b_conv, tail_p[:, 1:][None],
            s_a_k, s_a_v, ss[None], s_b_conv, tail_s[:, 1:][None])
```

```python
import functools

import jax
import jax.numpy as jnp
import numpy as np
from jax import lax
from jax.experimental import pallas as pl
from jax.experimental.pallas import tpu as pltpu

F32 = jnp.float32
BF16 = jnp.bfloat16
HIGHEST = lax.Precision.HIGHEST

D_MODEL = 1024
CHUNK = 64
PAST_LEN = 2048
EPS = 1e-6
NEG_INF = -1e30
LOG2E = 1.4426950408889634

A_HEADS = 8
A_KV_HEADS = 2
A_HEAD_DIM = 64
A_WIDTH = A_HEADS * A_HEAD_DIM
A_KV_WIDTH = A_KV_HEADS * A_HEAD_DIM
A_REP = A_HEADS // A_KV_HEADS
WINDOW = 128
HEAD_ORDER = tuple(g * A_REP + r for r in range(A_REP) for g in range(A_KV_HEADS))

B_HEADS = 4
B_HEAD_DIM = 128
B_WIDTH = B_HEADS * B_HEAD_DIM
CONV_W = 4

POOL_SIZES = (2, 4, 8, 16)
C_WIDTH = D_MODEL
C_GROUP = C_WIDTH // len(POOL_SIZES)
POOL_HIST = max(POOL_SIZES) - 1

LANES = 128
SUBLANES = 8

Z_AQ = 0
Z_AG = A_WIDTH
Z_BG = 2 * A_WIDTH
Z_BQKV = 3 * A_WIDTH
Z_AK = Z_BQKV + 3 * B_WIDTH
Z_AV = Z_AK + A_KV_WIDTH
Z_BA = Z_AV + A_KV_WIDTH
Z_WIDTH = Z_BA + LANES
GATE_LANE = B_HEADS

VMEM_LIMIT = 48 * 1024 * 1024


def _sigmoid(x):
    return 1.0 / (1.0 + jnp.exp(-x))


def _silu(x):
    return x * _sigmoid(x)


def _softplus(x):
    return jnp.maximum(x, 0.0) + jnp.log(1.0 + jnp.exp(-jnp.abs(x)))


def _dot(a, b, precision=None):
    return jnp.dot(a, b, preferred_element_type=F32, precision=precision)


def _dot_nt(a, b, precision=None):
    return lax.dot_general(a, b, (((1,), (1,)), ((), ())), preferred_element_type=F32, precision=precision)


def _dot_tn(a, b, precision=None):
    return lax.dot_general(a, b, (((0,), (0,)), ((), ())), preferred_element_type=F32, precision=precision)


def _log2(n):
    assert n & (n - 1) == 0
    return n.bit_length() - 1


def _in_proj_kernel(x_ref, g_ref, w_ref, kn_ref, z_ref):
    x = x_ref[...]
    xn = x * lax.rsqrt(jnp.mean(x * x, axis=-1, keepdims=True) + EPS) * g_ref[...]
    z_ref[...] = _dot(xn.astype(BF16), w_ref[...])
    k = z_ref[:, Z_AK:Z_AK + A_KV_WIDTH]
    sq = k * k
    lane = lax.broadcasted_iota(jnp.int32, k.shape, 1)
    first = lane < A_HEAD_DIM
    s0 = jnp.sum(jnp.where(first, sq, 0.0), axis=-1, keepdims=True)
    s1 = jnp.sum(jnp.where(first, 0.0, sq), axis=-1, keepdims=True)
    ms = jnp.where(first, s0, s1) * (1.0 / A_HEAD_DIM)
    z_ref[:, Z_AK:Z_AK + A_KV_WIDTH] = k * lax.rsqrt(ms + EPS) * kn_ref[...]


def _in_proj(x, g, w, kn, tm):
    n = x.shape[0]
    return pl.pallas_call(
        _in_proj_kernel,
        grid=(n // tm,),
        in_specs=[
            pl.BlockSpec((tm, D_MODEL), lambda i: (i, 0)),
            pl.BlockSpec((1, D_MODEL), lambda i: (0, 0)),
            pl.BlockSpec((D_MODEL, Z_WIDTH), lambda i: (0, 0)),
            pl.BlockSpec((1, A_KV_WIDTH), lambda i: (0, 0)),
        ],
        out_specs=pl.BlockSpec((tm, Z_WIDTH), lambda i: (i, 0)),
        out_shape=jax.ShapeDtypeStruct((n, Z_WIDTH), F32),
        compiler_params=pltpu.CompilerParams(dimension_semantics=("arbitrary",), vmem_limit_bytes=VMEM_LIMIT),
        name="in_proj",
    )(x, g, w, kn)


def _attn_kernel(sinks_ref, q_ref, k_ref, v_ref, ag_ref, qn_ref, o_ref, kp_ref, vp_ref, bias_ref, *, bt, cq, lk, off, t):
    rows = A_REP * cq
    front = WINDOW - off
    nvar = front // cq + 1
    lkt = k_ref.shape[0] // bt
    n_chunks = t // cq

    for b in range(bt):
        if front:
            kp_ref[b, 0:front, :] = jnp.zeros((front, A_KV_WIDTH), BF16)
            vp_ref[b, 0:front, :] = jnp.zeros((front, A_KV_WIDTH), BF16)
        kp_ref[b, front:front + lkt, :] = k_ref[b * lkt:(b + 1) * lkt, :].astype(BF16)
        vp_ref[b, front:front + lkt, :] = v_ref[b * lkt:(b + 1) * lkt, :].astype(BF16)

    row = lax.broadcasted_iota(jnp.int32, (rows, 1), 0)
    rep = row >> _log2(cq)

    @pl.when(pl.program_id(0) == 0)
    def _():
        col = lax.broadcasted_iota(jnp.int32, (rows, lk), 1)
        dist = jnp.abs((row & (cq - 1)) + WINDOW - col).astype(F32)
        for g in range(A_KV_HEADS):
            slope = jnp.zeros((rows, 1), F32)
            for r in range(A_REP):
                slope = jnp.where(rep == r, 2.0 ** (-8.0 * (g * A_REP + r + 1) / A_HEADS), slope)
            for var in range(nvar):
                bias_ref[var, g] = jnp.where(col >= front - var * cq, -slope * dist, NEG_INF) * LOG2E

    lane = lax.broadcasted_iota(jnp.int32, (cq, LANES), 1)
    first = lane < A_HEAD_DIM
    qn = qn_ref[...] * (A_HEAD_DIM ** -0.5 * LOG2E)
    sks = []
    for g in range(A_KV_HEADS):
        sk = jnp.zeros((rows, 1), F32)
        for r in range(A_REP):
            sk = jnp.where(rep == r, sinks_ref[g * A_REP + r] * LOG2E, sk)
        sks.append(sk)

    def chunk(b, cg):
        k0 = cg * cq if isinstance(cg, int) else pl.multiple_of(cg * cq, cq)
        r0 = b * t + k0
        qc = q_ref[pl.ds(r0, cq), :]
        kk = kp_ref[b, pl.ds(k0, lk), :]
        vv = vp_ref[b, pl.ds(k0, lk), :]
        var = min(cg, nvar - 1) if isinstance(cg, int) else jnp.minimum(cg, nvar - 1)
        blocks = []
        for r in range(A_REP):
            x = qc[:, r * LANES:(r + 1) * LANES]
            sq = x * x
            s_lo = jnp.sum(jnp.where(first, sq, 0.0), axis=-1, keepdims=True)
            s_hi = jnp.sum(jnp.where(first, 0.0, sq), axis=-1, keepdims=True)
            ms = jnp.where(first, s_lo, s_hi) * (1.0 / A_HEAD_DIM)
            blocks.append(x * lax.rsqrt(ms + EPS) * qn)
        outs = []
        for g in range(A_KV_HEADS):
            mine = first if g == 0 else jnp.logical_not(first)
            lhs = jnp.concatenate([jnp.where(mine, blk, 0.0) for blk in blocks], axis=0).astype(BF16)
            s = _dot_nt(lhs, kk) + bias_ref[var, g]
            m = jnp.maximum(jnp.max(s, axis=-1, keepdims=True), sks[g])
            p = jnp.exp2(s - m)
            den = jnp.sum(p, axis=-1, keepdims=True) + jnp.exp2(sks[g] - m)
            outs.append(_dot(p.astype(BF16), vv) * (1.0 / den))
        tile = jnp.concatenate(
            [jnp.where(first, outs[0][r * cq:(r + 1) * cq], outs[1][r * cq:(r + 1) * cq]) for r in range(A_REP)], axis=1)
        o_ref[pl.ds(r0, cq), :] = (tile * _silu(ag_ref[pl.ds(r0, cq), :])).astype(BF16)

    for b in range(bt):
        if n_chunks == 1:
            chunk(b, 0)
        else:
            lax.fori_loop(0, n_chunks, lambda cg, carry, b=b: (chunk(b, cg), carry)[1], 0, unroll=min(n_chunks, 8))


def _attention(z, kbuf, vbuf, kv_specs, sinks, qn, *, nb, bt, t, cq, lk, off):
    front = WINDOW - off
    lkt = kv_specs[0].block_shape[0] // bt
    kern = functools.partial(_attn_kernel, bt=bt, cq=cq, lk=lk, off=off, t=t)
    return pl.pallas_call(
        kern,
        grid=(nb // bt,),
        in_specs=[
            pl.BlockSpec(memory_space=pltpu.SMEM),
            pl.BlockSpec((bt * t, A_WIDTH), lambda b: (b, Z_AQ // A_WIDTH)),
            kv_specs[0],
            kv_specs[1],
            pl.BlockSpec((bt * t, A_WIDTH), lambda b: (b, Z_AG // A_WIDTH)),
            pl.BlockSpec((1, LANES), lambda b: (0, 0)),
        ],
        out_specs=pl.BlockSpec((bt * t, A_WIDTH), lambda b: (b, 0)),
        out_shape=jax.ShapeDtypeStruct((nb * t, A_WIDTH), BF16),
        scratch_shapes=[
            pltpu.VMEM((bt, front + lkt, A_KV_WIDTH), BF16),
            pltpu.VMEM((bt, front + lkt, A_KV_WIDTH), BF16),
            pltpu.VMEM((front // cq + 1, A_KV_HEADS, A_REP * cq, lk), F32),
        ],
        compiler_params=pltpu.CompilerParams(dimension_semantics=("arbitrary",), vmem_limit_bytes=VMEM_LIMIT),
        name="swa_attention",
    )(sinks, z, kbuf, vbuf, z, qn)


def _bf(x):
    return x.astype(BF16)


def _unit_lower_inverse(a_list, c, ri, lj, expand):
    base = SUBLANES
    same = (ri >> _log2(base)) == (lj >> _log2(base))
    eye = jnp.where(ri == lj, 1.0, 0.0)
    n1 = [jnp.where(same, -a, 0.0) for a in a_list]
    x = [eye + n for n in n1]
    n1b = [_bf(n) for n in n1]
    n2b = [_bf(_dot(nb, expand(nb))) for nb in n1b]
    n2e = [expand(nb) for nb in n2b]
    x = [xi + _dot(_bf(xi), ne) for xi, ne in zip(x, n2e)]
    n4e = [expand(_bf(_dot(nb, ne))) for nb, ne in zip(n2b, n2e)]
    x = [xi + _dot(_bf(xi), ne) for xi, ne in zip(x, n4e)]
    s = base
    while s < c:
        sel = ((ri >> _log2(2 * s)) == (lj >> _log2(2 * s))) & ((ri >> _log2(s)) != (lj >> _log2(s)))
        xb = [_bf(xi) for xi in x]
        xo = [_dot(b, expand(_bf(jnp.where(sel, a, 0.0)))) for b, a in zip(xb, a_list)]
        x = [xi - _dot(_bf(o), expand(b)) for xi, o, b in zip(x, xo, xb)]
        s *= 2
    return x


def _delta_kernel(qkv_ref, ba_ref, bg_ref, hist_ref, s0_ref, convw_ref, gate_ref, onorm_ref,
                  o_ref, sout_ref, ext_ref, st_ref, lhs_ref, add_ref, s_ref, *, bt, c, tc):
    i = pl.program_id(1)
    hd = B_HEAD_DIM
    nh = B_HEADS
    r = nh * c
    nch = tc // c
    pad = SUBLANES
    units = [(b, cc) for b in range(bt) for cc in range(nch)]

    @pl.when(i == 0)
    def _():
        ext_ref[:, 0:pad, :] = hist_ref[...]
        s_ref[...] = s0_ref[...]

    @pl.when(i > 0)
    def _():
        ext_ref[:, 0:pad, :] = ext_ref[:, tc:tc + pad, :]

    for b in range(bt):
        ext_ref[b, pad:pad + tc, :] = qkv_ref[b * tc:(b + 1) * tc, :]

    for u, (b, cc) in enumerate(units):
        base = pad - (CONV_W - 1) + cc * c
        acc = ext_ref[b, base:base + c, :] * convw_ref[0:1, :]
        for j in range(1, CONV_W):
            acc = acc + ext_ref[b, base + j:base + j + c, :] * convw_ref[j:j + 1, :]
        y = _silu(acc)
        for part in range(3):
            for h in range(nh):
                blk = y[:, (part * nh + h) * hd:(part * nh + h + 1) * hd]
                if part < 2:
                    blk = blk * lax.rsqrt(jnp.sum(blk * blk, axis=-1, keepdims=True) + EPS)
                st_ref[part, u, h * c:(h + 1) * c, :] = blk

    ri = lax.broadcasted_iota(jnp.int32, (c, r), 0)
    li = lax.broadcasted_iota(jnp.int32, (c, r), 1)
    lj = li & (c - 1)
    lh = li >> _log2(c)
    lower = ri >= lj
    strict = ri > lj
    head_sel = [jnp.where(lh == h, 1.0, 0.0).astype(BF16) for h in range(nh)]

    def expand(xb):
        return jnp.concatenate([xb * m for m in head_sel], axis=0)

    def lanes(x_st):
        return jnp.concatenate([x_st[h * c:(h + 1) * c] for h in range(nh)], axis=1)

    def head_blocks(x_st):
        zero = jnp.zeros((c, hd), x_st.dtype)
        return jnp.concatenate(
            [jnp.concatenate([x_st[h * c:(h + 1) * c] if h2 == h else zero for h2 in range(nh)], axis=1)
             for h in range(nh)], axis=0)

    ci = lax.broadcasted_iota(jnp.int32, (c, c), 0)
    cj = lax.broadcasted_iota(jnp.int32, (c, c), 1)
    tril = jnp.where(ci >= cj, 1.0, 0.0)
    di = lax.broadcasted_iota(jnp.int32, (hd, hd), 0)
    dj = lax.broadcasted_iota(jnp.int32, (hd, hd), 1)
    eye_hd = di == dj
    neg_rate = -jnp.exp(gate_ref[0:1, :])
    dt_bias = gate_ref[1:2, :]
    onorm = onorm_ref[...]

    def stack_cols(x, lane0):
        return jnp.concatenate([x[:, lane0 + h:lane0 + h + 1] for h in range(nh)], axis=0)

    beta, gc, gcc, gr, glast = [], [], [], [], []
    for b, cc in units:
        ba = ba_ref[b * tc + cc * c:b * tc + (cc + 1) * c, :]
        g_all = neg_rate * _softplus(ba + dt_bias)
        gcum = _dot(tril, g_all, HIGHEST)
        gcum_t = gcum.T
        beta.append(stack_cols(_sigmoid(ba), 0))
        gc.append(stack_cols(gcum, GATE_LANE))
        gcc_h = jnp.zeros((c, r), F32)
        for h in range(nh):
            gcc_h = jnp.where(lh == h, gcum[:, GATE_LANE + h:GATE_LANE + h + 1], gcc_h)
        gcc.append(gcc_h)
        gr.append(jnp.concatenate([gcum_t[GATE_LANE + h:GATE_LANE + h + 1, :] for h in range(nh)], axis=1))
        glast.append(jnp.concatenate(
            [jnp.broadcast_to(gcum[c - 1:c, GATE_LANE + h:GATE_LANE + h + 1], (c, 1)) for h in range(nh)], axis=0))

    def fold_units(us):
        n = range(len(us))
        q = [st_ref[0, u] * (hd ** -0.5) for u in us]
        k = [st_ref[1, u] for u in us]
        v = [st_ref[2, u] for u in us]
        kb = [k[j] * beta[u] for j, u in enumerate(us)]
        big = [_dot_nt(_bf(jnp.concatenate([lanes(kb[j]), lanes(q[j])], axis=0)), head_blocks(_bf(k[j]))) for j in n]
        decay = [jnp.where(lower, jnp.exp(jnp.where(lower, gcc[u] - gr[u], 0.0)), 0.0) for u in us]
        a = [jnp.where(strict, big[j][:c] * decay[j], 0.0) for j in n]
        qkd = [expand(_bf(big[j][c:] * decay[j])) for j in n]
        tinv = _unit_lower_inverse(a, c, ri, lj, expand)
        eg = [jnp.exp(gc[u]) for u in us]
        rhs = [_bf(jnp.concatenate([v[j] * beta[u], kb[j] * eg[j]], axis=1)) for j, u in enumerate(us)]
        sol = [_dot(expand(_bf(tinv[j])), rhs[j]) for j in n]
        solb = [_bf(s) for s in sol]
        fold = [_dot(qkd[j], solb[j]) for j in n]
        k_dec = [_bf(k[j] * jnp.exp(glast[u] - gc[u])) for j, u in enumerate(us)]
        for j, u in enumerate(us):
            qp = q[j] * eg[j] - fold[j][:, hd:]
            for h in range(nh):
                rows = slice(h * c, (h + 1) * c)
                kt = _dot_tn(k_dec[j][rows], solb[j][rows])
                g_tot = jnp.exp(glast[u][h * c:h * c + 1, :])
                lhs_ref[u, h, 0:hd, :] = _bf(jnp.where(eye_hd, g_tot, 0.0) - kt[:, hd:])
                lhs_ref[u, h, hd:hd + c, :] = _bf(qp[rows])
                add_ref[u, h, 0:hd, :] = kt[:, :hd]
                add_ref[u, h, hd:hd + c, :] = fold[j][rows, :hd]

    fold_units(list(range(len(units))))

    for u, (b, cc) in enumerate(units):
        rows = slice(b * tc + cc * c, b * tc + (cc + 1) * c)
        for h in range(nh):
            res = _dot(lhs_ref[u, h], _bf(s_ref[b, h])) + add_ref[u, h]
            s_ref[b, h] = res[:hd]
            o = res[hd:]
            on = o * lax.rsqrt(jnp.mean(o * o, axis=-1, keepdims=True) + EPS) * onorm
            bg = bg_ref[rows, h * hd:(h + 1) * hd]
            o_ref[rows, h * hd:(h + 1) * hd] = (on * _silu(bg)).astype(BF16)
    sout_ref[...] = s_ref[...]


def _delta(z, hist, s0, convw, gate, onorm, *, nb, bt, t, tc, c):
    nt = t // tc
    units = bt * (tc // c)
    state = (bt, B_HEADS, B_HEAD_DIM, B_HEAD_DIM)
    kern = functools.partial(_delta_kernel, bt=bt, c=c, tc=tc)
    return pl.pallas_call(
        kern,
        grid=(nb // bt, nt),
        in_specs=[
            pl.BlockSpec((bt * tc, 3 * B_WIDTH), lambda b, i: (b * nt + i, Z_BQKV // (3 * B_WIDTH))),
            pl.BlockSpec((bt * tc, LANES), lambda b, i: (b * nt + i, Z_BA // LANES)),
            pl.BlockSpec((bt * tc, B_WIDTH), lambda b, i: (b * nt + i, Z_BG // B_WIDTH)),
            pl.BlockSpec((bt, SUBLANES, 3 * B_WIDTH), lambda b, i: (b, 0, 0)),
            pl.BlockSpec(state, lambda b, i: (b, 0, 0, 0)),
            pl.BlockSpec((CONV_W, 3 * B_WIDTH), lambda b, i: (0, 0)),
            pl.BlockSpec((2, LANES), lambda b, i: (0, 0)),
            pl.BlockSpec((1, B_HEAD_DIM), lambda b, i: (0, 0)),
        ],
        out_specs=[
            pl.BlockSpec((bt * tc, B_WIDTH), lambda b, i: (b * nt + i, 0)),
            pl.BlockSpec(state, lambda b, i: (b, 0, 0, 0)),
        ],
        out_shape=[
            jax.ShapeDtypeStruct((nb * t, B_WIDTH), BF16),
            jax.ShapeDtypeStruct((nb, B_HEADS, B_HEAD_DIM, B_HEAD_DIM), F32),
        ],
        scratch_shapes=[
            pltpu.VMEM((bt, SUBLANES + tc, 3 * B_WIDTH), F32),
            pltpu.VMEM((3, units, B_HEADS * c, B_HEAD_DIM), F32),
            pltpu.VMEM((units, B_HEADS, B_HEAD_DIM + c, B_HEAD_DIM), BF16),
            pltpu.VMEM((units, B_HEADS, B_HEAD_DIM + c, B_HEAD_DIM), F32),
            pltpu.VMEM(state, F32),
        ],
        compiler_params=pltpu.CompilerParams(dimension_semantics=("arbitrary", "arbitrary"),
                                             vmem_limit_bytes=VMEM_LIMIT),
        name="gated_delta",
    )(z, z, z, hist, s0, convw, gate, onorm)


def _pool_kernel(h_ref, oa_ref, ob_ref, wa_ref, wb_ref, hist_ref, g_ref, win_ref, wgrp_ref, scale_ref, wout_ref,
                 y_ref, tail_ref, ext_ref, *, bt, tt, pos0):
    i = pl.program_id(1)
    pad = POOL_HIST + 1
    rows = bt * tt

    @pl.when(i == 0)
    def _():
        pos = pos0 - pad + lax.broadcasted_iota(jnp.int32, (1, pad, 1), 1)
        ext_ref[:, 0:pad, :] = jnp.where(pos >= 0, hist_ref[...], 0.0)

    @pl.when(i > 0)
    def _():
        ext_ref[:, 0:pad, :] = ext_ref[:, tt:tt + pad, :]

    x = h_ref[...] + _dot(oa_ref[...], wa_ref[...]) + _dot(ob_ref[...], wb_ref[...])
    xn = x * lax.rsqrt(jnp.mean(x * x, axis=-1, keepdims=True) + EPS) * g_ref[...]
    z = _dot(xn.astype(BF16), win_ref[...])
    u = z[:, :C_WIDTH]
    gate = z[:, C_WIDTH:]
    ext_ref[:, pad:pad + tt, :] = u.reshape(bt, tt, C_WIDTH)
    tail_ref[...] = ext_ref[:, tt:tt + pad, :]

    tpos = pos0 + i * tt + (lax.broadcasted_iota(jnp.int32, (rows, 1), 0) & (tt - 1))
    mixed = []
    for gi, w in enumerate(POOL_SIZES):
        cols = slice(gi * C_GROUP, (gi + 1) * C_GROUP)
        s = ext_ref[:, :, cols].reshape(bt * (pad + tt), C_GROUP)
        sh = 1
        while sh < w:
            s = s + pltpu.roll(s, sh, 0)
            sh *= 2
        s = s.reshape(bt, pad + tt, C_GROUP)[:, pad:, :].reshape(rows, C_GROUP)
        cnt = jnp.minimum(tpos + 1, w).astype(F32)
        pooled = s / cnt - u[:, cols]
        m = _dot(pooled.astype(BF16), wgrp_ref[gi]) * scale_ref[:, cols]
        mixed.append((m * _silu(gate[:, cols])).astype(BF16))
    y_ref[...] = x + _dot(jnp.concatenate(mixed, axis=1), wout_ref[...])


def _pool_layer(h, oa, ob, wa, wb, hist, g, win, wgrp, scale, wout, *, nb, bt, t, tt, pos0):
    nt = t // tt
    pad = POOL_HIST + 1
    rows = bt * tt
    kern = functools.partial(_pool_kernel, bt=bt, tt=tt, pos0=pos0)
    tile = lambda b, i: (b * nt + i, 0)
    const2 = lambda b, i: (0, 0)
    return pl.pallas_call(
        kern,
        grid=(nb // bt, nt),
        in_specs=[
            pl.BlockSpec((rows, D_MODEL), tile),
            pl.BlockSpec((rows, A_WIDTH), tile),
            pl.BlockSpec((rows, B_WIDTH), tile),
            pl.BlockSpec((A_WIDTH, D_MODEL), const2),
            pl.BlockSpec((B_WIDTH, D_MODEL), const2),
            pl.BlockSpec((bt, pad, C_WIDTH), lambda b, i: (b, 0, 0)),
            pl.BlockSpec((1, D_MODEL), const2),
            pl.BlockSpec((D_MODEL, 2 * C_WIDTH), const2),
            pl.BlockSpec((len(POOL_SIZES), C_GROUP, C_GROUP), lambda b, i: (0, 0, 0)),
            pl.BlockSpec((1, C_WIDTH), const2),
            pl.BlockSpec((C_WIDTH, D_MODEL), const2),
        ],
        out_specs=[
            pl.BlockSpec((rows, D_MODEL), tile),
            pl.BlockSpec((bt, pad, C_WIDTH), lambda b, i: (b, 0, 0)),
        ],
        out_shape=[
            jax.ShapeDtypeStruct((nb * t, D_MODEL), F32),
            jax.ShapeDtypeStruct((nb, pad, C_WIDTH), F32),
        ],
        scratch_shapes=[pltpu.VMEM((bt, pad + tt, C_WIDTH), F32)],
        compiler_params=pltpu.CompilerParams(dimension_semantics=("arbitrary", "arbitrary"),
                                             vmem_limit_bytes=VMEM_LIMIT),
        name="out_proj_pool_layer",
    )(h, oa, ob, wa, wb, hist, g, win, wgrp, scale, wout)


def _in_proj_weight(w_in):
    ch = A_HEAD_DIM
    aq0 = 0
    ak0 = aq0 + A_WIDTH // ch
    av0 = ak0 + A_KV_WIDTH // ch
    ag0 = av0 + A_KV_WIDTH // ch
    bqkv0 = ag0 + A_WIDTH // ch
    bg0 = bqkv0 + 3 * B_WIDTH // ch
    ba0 = bg0 + B_WIDTH // ch
    zero = ba0 + 1
    order = ([aq0 + h for h in HEAD_ORDER] + [ag0 + h for h in HEAD_ORDER] + list(range(bg0, ba0))
             + list(range(bqkv0, bg0)) + list(range(ak0, ag0)) + [ba0, zero])
    assert len(order) * ch == Z_WIDTH
    wp = jnp.pad(w_in, ((0, 0), (0, (zero + 1) * ch - w_in.shape[1]))).reshape(D_MODEL, zero + 1, ch)
    return wp[:, np.array(order)].reshape(D_MODEL, Z_WIDTH).astype(BF16)


def _ab_weights(norm_g, w_in, q_norm, k_norm, sinks, conv_w, a_log, dt_bias, o_norm, w_out):
    w = _in_proj_weight(w_in)
    wa = w_out[:A_WIDTH].reshape(A_HEADS, A_HEAD_DIM, D_MODEL)[np.array(HEAD_ORDER)].reshape(A_WIDTH, D_MODEL)
    lane_pad = (GATE_LANE, LANES - GATE_LANE - B_HEADS)
    gate = jnp.stack([jnp.pad(a_log.astype(F32), lane_pad), jnp.pad(dt_bias.astype(F32), lane_pad)])
    return dict(
        norm_g=norm_g.reshape(1, D_MODEL), w=w, kn=jnp.tile(k_norm, A_KV_HEADS).reshape(1, A_KV_WIDTH),
        qn=jnp.tile(q_norm, A_KV_HEADS).reshape(1, LANES), sinks=sinks.astype(F32), conv_w=conv_w, gate=gate,
        onorm=o_norm.reshape(1, B_HEAD_DIM), wa=wa.astype(BF16), wb=w_out[A_WIDTH:].astype(BF16))


def _group_tiles(t):
    if t >= CHUNK:
        return dict(proj_rows=512, seqs_attn=1, seqs_delta=1, delta_tokens=512, chunk=CHUNK, seqs_pool=1, pool_tokens=512)
    return dict(proj_rows=512, seqs_attn=8, seqs_delta=16, delta_tokens=t, chunk=t, seqs_pool=512 // t, pool_tokens=t)


def _ab_mixers(h, wts, cache_k, cache_v, s0, conv_hist, *, nb, t):
    tl = _group_tiles(t)
    z = _in_proj(h, wts["norm_g"], wts["w"], wts["kn"], min(nb * t, tl["proj_rows"]))
    bt = tl["seqs_attn"]
    if cache_k is None:
        cq, lk, off = CHUNK, WINDOW + CHUNK, 0
        kv_specs = [pl.BlockSpec((bt * t, A_KV_WIDTH), lambda b: (b, Z_AK // A_KV_WIDTH)),
                    pl.BlockSpec((bt * t, A_KV_WIDTH), lambda b: (b, Z_AV // A_KV_WIDTH))]
        kbuf = vbuf = z
    else:
        cq, lk, off = t, WINDOW + t, WINDOW
        k_new = z[:, Z_AK:Z_AK + A_KV_WIDTH].reshape(nb, t, A_KV_WIDTH)
        v_new = z[:, Z_AV:Z_AV + A_KV_WIDTH].reshape(nb, t, A_KV_WIDTH)
        kbuf = jnp.concatenate([cache_k.reshape(nb, WINDOW, A_KV_WIDTH), k_new], axis=1).reshape(nb * lk, A_KV_WIDTH)
        vbuf = jnp.concatenate([cache_v.reshape(nb, WINDOW, A_KV_WIDTH), v_new], axis=1).reshape(nb * lk, A_KV_WIDTH)
        kv_specs = [pl.BlockSpec((bt * lk, A_KV_WIDTH), lambda b: (b, 0))] * 2
    o_a = _attention(z, kbuf, vbuf, kv_specs, wts["sinks"], wts["qn"], nb=nb, bt=bt, t=t, cq=cq, lk=lk, off=off)
    hist = jnp.pad(conv_hist.astype(F32), ((0, 0), (SUBLANES - (CONV_W - 1), 0), (0, 0)))
    o_b, s_new = _delta(z, hist, s0.astype(F32), wts["conv_w"], wts["gate"], wts["onorm"], nb=nb,
                        bt=tl["seqs_delta"], t=t, tc=tl["delta_tokens"], c=tl["chunk"])
    return o_a, o_b, z, s_new


def _last_rows(z, col, width, nb, t, rows):
    return z.reshape(nb, t, Z_WIDTH)[:, t - rows:, col:col + width]


def _cache_rows(z, col, nb, t, rows):
    return _last_rows(z, col, A_KV_WIDTH, nb, t, rows).reshape(nb, rows, A_KV_HEADS, A_HEAD_DIM)


def kernel(x_prompt, x_sample, cache_a_k, cache_a_v, state_b_s, state_b_conv, state_c_pool,
           norm_ab, w_in_ab, q_norm_a, k_norm_a, sinks_a, conv_b, a_log_b, dt_bias_b, o_norm_b, w_out_ab,
           norm_c, w_in_c, w_grp_c, scale_c, w_out_c):
    bp, tp, _ = x_prompt.shape
    bs, ts, _ = x_sample.shape
    hp = x_prompt.reshape(bp * tp, D_MODEL)
    hs = x_sample.reshape(bs * ts, D_MODEL)

    wts = _ab_weights(norm_ab[0], w_in_ab[0], q_norm_a[0], k_norm_a[0], sinks_a[0], conv_b[0], a_log_b[0],
                      dt_bias_b[0], o_norm_b[0], w_out_ab[0])
    s0 = jnp.zeros((bp, B_HEADS, B_HEAD_DIM, B_HEAD_DIM), F32)
    c0 = jnp.zeros((bp, CONV_W - 1, 3 * B_WIDTH), F32)
    oap, obp, zp, sp = _ab_mixers(hp, wts, None, None, s0, c0, nb=bp, t=tp)
    oas, obs, zs, ss = _ab_mixers(hs, wts, cache_a_k[0], cache_a_v[0], state_b_s[0], state_b_conv[0], nb=bs, t=ts)
    p_a_k = _cache_rows(zp, Z_AK, bp, tp, WINDOW)[None]
    p_a_v = _cache_rows(zp, Z_AV, bp, tp, WINDOW)[None]
    s_a_k = _cache_rows(zs, Z_AK, bs, ts, ts)[None]
    s_a_v = _cache_rows(zs, Z_AV, bs, ts, ts)[None]
    p_b_conv = _last_rows(zp, Z_BQKV, 3 * B_WIDTH, bp, tp, CONV_W - 1)[None]
    s_b_conv = _last_rows(zs, Z_BQKV, 3 * B_WIDTH, bs, ts, CONV_W - 1)[None]

    g_c = norm_c[0].reshape(1, D_MODEL)
    win = w_in_c[0].astype(BF16)
    wgrp = w_grp_c[0].astype(BF16)
    scale = scale_c[0].reshape(1, C_WIDTH)
    wout = w_out_c[0].astype(BF16)
    h0 = jnp.zeros((bp, POOL_HIST + 1, C_WIDTH), F32)
    hs0 = jnp.pad(state_c_pool[0].astype(F32), ((0, 0), (1, 0), (0, 0)))
    tlp, tls = _group_tiles(tp), _group_tiles(ts)
    yp, tail_p = _pool_layer(hp, oap, obp, wts["wa"], wts["wb"], h0, g_c, win, wgrp, scale, wout, nb=bp,
                             bt=tlp["seqs_pool"], t=tp, tt=tlp["pool_tokens"], pos0=0)
    ys, tail_s = _pool_layer(hs, oas, obs, wts["wa"], wts["wb"], hs0, g_c, win, wgrp, scale, wout, nb=bs,
                             bt=tls["seqs_pool"], t=ts, tt=tls["pool_tokens"], pos0=PAST_LEN)

    return (yp.reshape(bp, tp, D_MODEL), ys.reshape(bs, ts, D_MODEL),
            p_a_k, p_a_v, sp[None], p_b_conv, tail_p[:, 1:][None],
            s_a_k, s_a_v, ss[None], s_b_conv, tail_s[:, 1:][None])
```

```python
import functools

import jax
import jax.numpy as jnp
from jax import lax
from jax.experimental import pallas as pl
from jax.experimental.pallas import tpu as pltpu

F32 = jnp.float32
BF16 = jnp.bfloat16
HIGHEST = lax.Precision.HIGHEST

D_MODEL = 1024
CHUNK = 64
PAST_LEN = 2048
EPS = 1e-6
NEG_INF = -1e30
LOG2E = 1.4426950408889634

A_HEADS = 8
A_KV_HEADS = 2
A_HEAD_DIM = 64
A_WIDTH = A_HEADS * A_HEAD_DIM
A_KV_WIDTH = A_KV_HEADS * A_HEAD_DIM
A_REP = A_HEADS // A_KV_HEADS
WINDOW = 128

B_HEADS = 4
B_HEAD_DIM = 128
B_WIDTH = B_HEADS * B_HEAD_DIM
CONV_W = 4

POOL_SIZES = (2, 4, 8, 16)
C_WIDTH = D_MODEL
C_GROUP = C_WIDTH // len(POOL_SIZES)
POOL_HIST = max(POOL_SIZES) - 1

LANES = 128
SUBLANES = 8

Z_AQ = 0
Z_AK = Z_AQ + A_WIDTH
Z_AV = Z_AK + A_KV_WIDTH
Z_AG = Z_AV + A_KV_WIDTH
Z_BQKV = Z_AG + A_WIDTH
Z_BG = Z_BQKV + 3 * B_WIDTH
Z_BA = Z_BG + B_WIDTH
Z_WIDTH = Z_BA + LANES
GATE_LANE = B_HEADS


def _z_window(rows, width, col, row_block):
    return pl.BlockSpec((pl.Element(rows), pl.Element(width)), lambda *idx: (row_block(*idx) * rows, col))

VMEM_LIMIT = 48 * 1024 * 1024


def _sigmoid(x):
    return 1.0 / (1.0 + jnp.exp(-x))


def _silu(x):
    return x * _sigmoid(x)


def _softplus(x):
    return jnp.maximum(x, 0.0) + jnp.log(1.0 + jnp.exp(-jnp.abs(x)))


def _dot(a, b, precision=None):
    return jnp.dot(a, b, preferred_element_type=F32, precision=precision)


def _dot_nt(a, b, precision=None):
    return lax.dot_general(a, b, (((1,), (1,)), ((), ())), preferred_element_type=F32, precision=precision)


def _dot_tn(a, b, precision=None):
    return lax.dot_general(a, b, (((0,), (0,)), ((), ())), preferred_element_type=F32, precision=precision)


def _log2(n):
    assert n & (n - 1) == 0
    return n.bit_length() - 1


def _in_proj_kernel(x_ref, g_ref, w_ref, kn_ref, z_ref):
    x = x_ref[...]
    xn = x * lax.rsqrt(jnp.mean(x * x, axis=-1, keepdims=True) + EPS) * g_ref[...]
    z_ref[...] = _dot(xn.astype(BF16), w_ref[...])
    k = z_ref[:, Z_AK:Z_AK + A_KV_WIDTH]
    sq = k * k
    lane = lax.broadcasted_iota(jnp.int32, k.shape, 1)
    first = lane < A_HEAD_DIM
    s0 = jnp.sum(jnp.where(first, sq, 0.0), axis=-1, keepdims=True)
    s1 = jnp.sum(jnp.where(first, 0.0, sq), axis=-1, keepdims=True)
    ms = jnp.where(first, s0, s1) * (1.0 / A_HEAD_DIM)
    z_ref[:, Z_AK:Z_AK + A_KV_WIDTH] = k * lax.rsqrt(ms + EPS) * kn_ref[...]


def _in_proj(x, g, w, kn, tm):
    n = x.shape[0]
    return pl.pallas_call(
        _in_proj_kernel,
        grid=(n // tm,),
        in_specs=[
            pl.BlockSpec((tm, D_MODEL), lambda i: (i, 0)),
            pl.BlockSpec((1, D_MODEL), lambda i: (0, 0)),
            pl.BlockSpec((D_MODEL, Z_WIDTH), lambda i: (0, 0)),
            pl.BlockSpec((1, A_KV_WIDTH), lambda i: (0, 0)),
        ],
        out_specs=pl.BlockSpec((tm, Z_WIDTH), lambda i: (i, 0)),
        out_shape=jax.ShapeDtypeStruct((n, Z_WIDTH), F32),
        compiler_params=pltpu.CompilerParams(dimension_semantics=("arbitrary",), vmem_limit_bytes=VMEM_LIMIT),
        name="in_proj",
    )(x, g, w, kn)


A_HALVES = LANES // A_HEAD_DIM
A_QBLOCKS = A_WIDTH // LANES
A_COPIES = A_KV_HEADS


def _attn_half(r, c):
    g = r * A_HALVES // A_REP
    return (g + c) % A_HALVES


def _attn_tables(sinks_ref, bias_ref, *, cq, lk, front, fill):
    rows = A_QBLOCKS * cq
    nvar = front // cq + 1
    row = lax.broadcasted_iota(jnp.int32, (rows, 1), 0)
    blk = row >> _log2(cq)

    def per_row(value):
        out = []
        for c in range(A_COPIES):
            col = jnp.zeros((rows, 1), F32)
            for r in range(A_QBLOCKS):
                col = jnp.where(blk == r, value(r * A_HALVES + _attn_half(r, c)), col)
            out.append(col)
        return out

    @pl.when(fill)
    def _():
        col = lax.broadcasted_iota(jnp.int32, (rows, lk), 1)
        dist = jnp.abs((row & (cq - 1)) + WINDOW - col).astype(F32)
        slopes = per_row(lambda h: 2.0 ** (-8.0 * (h + 1) / A_HEADS))
        for c in range(A_COPIES):
            for var in range(nvar):
                bias_ref[var, c] = jnp.where(col >= front - var * cq, -slopes[c] * dist, NEG_INF) * LOG2E

    first = lax.broadcasted_iota(jnp.int32, (cq, LANES), 1) < A_HEAD_DIM
    sks = per_row(lambda h: sinks_ref[h] * LOG2E)
    return first, sks, nvar


def _attn_chunk(qc, kv, biases, sks, first, qn, gate, cq):
    blocks = []
    for r in range(A_QBLOCKS):
        x = qc[:, r * LANES:(r + 1) * LANES]
        sq = x * x
        s_lo = jnp.sum(jnp.where(first, sq, 0.0), axis=-1, keepdims=True)
        s_hi = jnp.sum(jnp.where(first, 0.0, sq), axis=-1, keepdims=True)
        ms = jnp.where(first, s_lo, s_hi) * (1.0 / A_HEAD_DIM)
        blocks.append(x * lax.rsqrt(ms + EPS) * qn)
    outs = []
    for c in range(A_COPIES):
        lhs = jnp.concatenate(
            [jnp.where(first if _attn_half(r, c) == 0 else jnp.logical_not(first), blocks[r], 0.0)
             for r in range(A_QBLOCKS)], axis=0)
        kk, vv = kv[c]
        sc = _dot_nt(lhs.astype(BF16), kk) + biases[c]
        m = jnp.maximum(jnp.max(sc, axis=-1, keepdims=True), sks[c])
        p = jnp.exp2(sc - m)
        den = jnp.sum(p, axis=-1, keepdims=True) + jnp.exp2(sks[c] - m)
        outs.append(_dot(p.astype(BF16), vv) * (1.0 / den))
    tile = []
    for r in range(A_QBLOCKS):
        rows = slice(r * cq, (r + 1) * cq)
        low = 0 if _attn_half(r, 0) == 0 else 1
        tile.append(jnp.where(first, outs[low][rows], outs[1 - low][rows]))
    return (jnp.concatenate(tile, axis=1) * _silu(gate)).astype(BF16)


def _attn_kernel(sinks_ref, q_ref, k_ref, v_ref, ag_ref, qn_ref, o_ref, kp_ref, vp_ref, bias_ref, *, bt, cq, lk, off, t):
    front = WINDOW - off
    lkt = k_ref.shape[0] // bt
    n_chunks = t // cq

    for b in range(bt):
        for src, dst in ((k_ref, kp_ref), (v_ref, vp_ref)):
            x = src[b * lkt:(b + 1) * lkt, :]
            if front:
                dst[:, b, 0:front, :] = jnp.zeros((A_COPIES, front, A_KV_WIDTH), BF16)
            dst[0, b, front:front + lkt, :] = x.astype(BF16)
            dst[1, b, front:front + lkt, :] = pltpu.roll(x, A_HEAD_DIM, 1).astype(BF16)

    first, sks, nvar = _attn_tables(sinks_ref, bias_ref, cq=cq, lk=lk, front=front, fill=pl.program_id(0) == 0)
    qn = qn_ref[...] * (A_HEAD_DIM ** -0.5 * LOG2E)

    def chunk(b, cg):
        k0 = cg * cq if isinstance(cg, int) else pl.multiple_of(cg * cq, cq)
        r0 = b * t + k0
        var = min(cg, nvar - 1) if isinstance(cg, int) else jnp.minimum(cg, nvar - 1)
        kv = [(kp_ref[c, b, pl.ds(k0, lk), :], vp_ref[c, b, pl.ds(k0, lk), :]) for c in range(A_COPIES)]
        o_ref[pl.ds(r0, cq), :] = _attn_chunk(
            q_ref[pl.ds(r0, cq), :], kv, [bias_ref[var, c] for c in range(A_COPIES)], sks, first, qn,
            ag_ref[pl.ds(r0, cq), :], cq)

    for b in range(bt):
        if n_chunks == 1:
            chunk(b, 0)
        else:
            lax.fori_loop(0, n_chunks, lambda cg, carry, b=b: (chunk(b, cg), carry)[1], 0, unroll=min(n_chunks, 8))


def _attention(z, kbuf, vbuf, kv_specs, sinks, qn, *, nb, bt, t, cq, lk, off):
    front = WINDOW - off
    lkt = kbuf.shape[0] // nb
    kern = functools.partial(_attn_kernel, bt=bt, cq=cq, lk=lk, off=off, t=t)
    return pl.pallas_call(
        kern,
        grid=(nb // bt,),
        in_specs=[
            pl.BlockSpec(memory_space=pltpu.SMEM),
            _z_window(bt * t, A_WIDTH, Z_AQ, lambda b: b),
            kv_specs[0],
            kv_specs[1],
            _z_window(bt * t, A_WIDTH, Z_AG, lambda b: b),
            pl.BlockSpec((1, LANES), lambda b: (0, 0)),
        ],
        out_specs=pl.BlockSpec((bt * t, A_WIDTH), lambda b: (b, 0)),
        out_shape=jax.ShapeDtypeStruct((nb * t, A_WIDTH), BF16),
        scratch_shapes=[
            pltpu.VMEM((A_COPIES, bt, front + lkt, A_KV_WIDTH), BF16),
            pltpu.VMEM((A_COPIES, bt, front + lkt, A_KV_WIDTH), BF16),
            pltpu.VMEM((front // cq + 1, A_COPIES, A_QBLOCKS * cq, lk), F32),
        ],
        compiler_params=pltpu.CompilerParams(dimension_semantics=("arbitrary",), vmem_limit_bytes=VMEM_LIMIT),
        name="swa_attention",
    )(sinks, z, kbuf, vbuf, z, qn)


def _bf(x):
    return x.astype(BF16)


def _unit_lower_inverse(a_list, c, ri, lj, expand):
    base = SUBLANES
    same = (ri >> _log2(base)) == (lj >> _log2(base))
    eye = jnp.where(ri == lj, 1.0, 0.0)
    n1 = [jnp.where(same, -a, 0.0) for a in a_list]
    x = [eye + n for n in n1]
    n1b = [_bf(n) for n in n1]
    n2b = [_bf(_dot(nb, expand(nb))) for nb in n1b]
    n2e = [expand(nb) for nb in n2b]
    x = [xi + _dot(_bf(xi), ne) for xi, ne in zip(x, n2e)]
    n4e = [expand(_bf(_dot(nb, ne))) for nb, ne in zip(n2b, n2e)]
    x = [xi + _dot(_bf(xi), ne) for xi, ne in zip(x, n4e)]
    s = base
    while s < c:
        sel = ((ri >> _log2(2 * s)) == (lj >> _log2(2 * s))) & ((ri >> _log2(s)) != (lj >> _log2(s)))
        xb = [_bf(xi) for xi in x]
        xo = [_dot(b, expand(_bf(jnp.where(sel, a, 0.0)))) for b, a in zip(xb, a_list)]
        x = [xi - _dot(_bf(o), expand(b)) for xi, o, b in zip(x, xo, xb)]
        s *= 2
    return x


def _delta_kernel(qkv_ref, ba_ref, bg_ref, hist_ref, s0_ref, convw_ref, gate_ref, onorm_ref,
                  o_ref, sout_ref, ext_ref, st_ref, lhs_ref, add_ref, s_ref, *, bt, c, tc):
    i = pl.program_id(1)
    hd = B_HEAD_DIM
    nh = B_HEADS
    r = nh * c
    nch = tc // c
    pad = SUBLANES
    units = [(b, cc) for b in range(bt) for cc in range(nch)]

    @pl.when(i == 0)
    def _():
        ext_ref[:, 0:pad, :] = hist_ref[...]
        s_ref[...] = s0_ref[...]

    @pl.when(i > 0)
    def _():
        ext_ref[:, 0:pad, :] = ext_ref[:, tc:tc + pad, :]

    for b in range(bt):
        ext_ref[b, pad:pad + tc, :] = qkv_ref[b * tc:(b + 1) * tc, :]

    for u, (b, cc) in enumerate(units):
        base = pad - (CONV_W - 1) + cc * c
        acc = ext_ref[b, base:base + c, :] * convw_ref[0:1, :]
        for j in range(1, CONV_W):
            acc = acc + ext_ref[b, base + j:base + j + c, :] * convw_ref[j:j + 1, :]
        y = _silu(acc)
        for part in range(3):
            for h in range(nh):
                blk = y[:, (part * nh + h) * hd:(part * nh + h + 1) * hd]
                if part < 2:
                    blk = blk * lax.rsqrt(jnp.sum(blk * blk, axis=-1, keepdims=True) + EPS)
                st_ref[part, u, h * c:(h + 1) * c, :] = blk

    ri = lax.broadcasted_iota(jnp.int32, (c, r), 0)
    li = lax.broadcasted_iota(jnp.int32, (c, r), 1)
    lj = li & (c - 1)
    lh = li >> _log2(c)
    lower = ri >= lj
    strict = ri > lj
    head_sel = [jnp.where(lh == h, 1.0, 0.0).astype(BF16) for h in range(nh)]

    def expand(xb):
        return jnp.concatenate([xb * m for m in head_sel], axis=0)

    def lanes(x_st):
        return jnp.concatenate([x_st[h * c:(h + 1) * c] for h in range(nh)], axis=1)

    def head_blocks(x_st):
        zero = jnp.zeros((c, hd), x_st.dtype)
        return jnp.concatenate(
            [jnp.concatenate([x_st[h * c:(h + 1) * c] if h2 == h else zero for h2 in range(nh)], axis=1)
             for h in range(nh)], axis=0)

    ci = lax.broadcasted_iota(jnp.int32, (c, c), 0)
    cj = lax.broadcasted_iota(jnp.int32, (c, c), 1)
    tril = jnp.where(ci >= cj, 1.0, 0.0)
    di = lax.broadcasted_iota(jnp.int32, (hd, hd), 0)
    dj = lax.broadcasted_iota(jnp.int32, (hd, hd), 1)
    eye_hd = di == dj
    neg_rate = -jnp.exp(gate_ref[0:1, :])
    dt_bias = gate_ref[1:2, :]
    onorm = onorm_ref[...]

    def stack_cols(x, lane0):
        return jnp.concatenate([x[:, lane0 + h:lane0 + h + 1] for h in range(nh)], axis=0)

    beta, gc, gcc, gr, glast = [], [], [], [], []
    for b, cc in units:
        ba = ba_ref[b * tc + cc * c:b * tc + (cc + 1) * c, :]
        g_all = neg_rate * _softplus(ba + dt_bias)
        gcum = _dot(tril, g_all, HIGHEST)
        gcum_t = gcum.T
        beta.append(stack_cols(_sigmoid(ba), 0))
        gc.append(stack_cols(gcum, GATE_LANE))
        gcc_h = jnp.zeros((c, r), F32)
        for h in range(nh):
            gcc_h = jnp.where(lh == h, gcum[:, GATE_LANE + h:GATE_LANE + h + 1], gcc_h)
        gcc.append(gcc_h)
        gr.append(jnp.concatenate([gcum_t[GATE_LANE + h:GATE_LANE + h + 1, :] for h in range(nh)], axis=1))
        glast.append(jnp.concatenate(
            [jnp.broadcast_to(gcum[c - 1:c, GATE_LANE + h:GATE_LANE + h + 1], (c, 1)) for h in range(nh)], axis=0))

    def fold_units(us):
        n = range(len(us))
        q = [st_ref[0, u] * (hd ** -0.5) for u in us]
        k = [st_ref[1, u] for u in us]
        v = [st_ref[2, u] for u in us]
        kb = [k[j] * beta[u] for j, u in enumerate(us)]
        big = [_dot_nt(_bf(jnp.concatenate([lanes(kb[j]), lanes(q[j])], axis=0)), head_blocks(_bf(k[j]))) for j in n]
        decay = [jnp.where(lower, jnp.exp(jnp.where(lower, gcc[u] - gr[u], 0.0)), 0.0) for u in us]
        a = [jnp.where(strict, big[j][:c] * decay[j], 0.0) for j in n]
        qkd = [expand(_bf(big[j][c:] * decay[j])) for j in n]
        tinv = _unit_lower_inverse(a, c, ri, lj, expand)
        eg = [jnp.exp(gc[u]) for u in us]
        rhs = [_bf(jnp.concatenate([v[j] * beta[u], kb[j] * eg[j]], axis=1)) for j, u in enumerate(us)]
        sol = [_dot(expand(_bf(tinv[j])), rhs[j]) for j in n]
        solb = [_bf(s) for s in sol]
        fold = [_dot(qkd[j], solb[j]) for j in n]
        k_dec = [_bf(k[j] * jnp.exp(glast[u] - gc[u])) for j, u in enumerate(us)]
        for j, u in enumerate(us):
            qp = q[j] * eg[j] - fold[j][:, hd:]
            for h in range(nh):
                rows = slice(h * c, (h + 1) * c)
                kt = _dot_tn(k_dec[j][rows], solb[j][rows])
                g_tot = jnp.exp(glast[u][h * c:h * c + 1, :])
                lhs_ref[u, h, 0:hd, :] = _bf(jnp.where(eye_hd, g_tot, 0.0) - kt[:, hd:])
                lhs_ref[u, h, hd:hd + c, :] = _bf(qp[rows])
                add_ref[u, h, 0:hd, :] = kt[:, :hd]
                add_ref[u, h, hd:hd + c, :] = fold[j][rows, :hd]

    fold_units(list(range(len(units))))

    for u, (b, cc) in enumerate(units):
        rows = slice(b * tc + cc * c, b * tc + (cc + 1) * c)
        for h in range(nh):
            res = _dot(lhs_ref[u, h], _bf(s_ref[b, h])) + add_ref[u, h]
            s_ref[b, h] = res[:hd]
            o = res[hd:]
            on = o * lax.rsqrt(jnp.mean(o * o, axis=-1, keepdims=True) + EPS) * onorm
            bg = bg_ref[rows, h * hd:(h + 1) * hd]
            o_ref[rows, h * hd:(h + 1) * hd] = (on * _silu(bg)).astype(BF16)
    sout_ref[...] = s_ref[...]


def _delta(z, hist, s0, convw, gate, onorm, *, nb, bt, t, tc, c):
    nt = t // tc
    units = bt * (tc // c)
    state = (bt, B_HEADS, B_HEAD_DIM, B_HEAD_DIM)
    kern = functools.partial(_delta_kernel, bt=bt, c=c, tc=tc)
    return pl.pallas_call(
        kern,
        grid=(nb // bt, nt),
        in_specs=[
            _z_window(bt * tc, 3 * B_WIDTH, Z_BQKV, lambda b, i: b * nt + i),
            _z_window(bt * tc, LANES, Z_BA, lambda b, i: b * nt + i),
            _z_window(bt * tc, B_WIDTH, Z_BG, lambda b, i: b * nt + i),
            pl.BlockSpec((bt, SUBLANES, 3 * B_WIDTH), lambda b, i: (b, 0, 0)),
            pl.BlockSpec(state, lambda b, i: (b, 0, 0, 0)),
            pl.BlockSpec((CONV_W, 3 * B_WIDTH), lambda b, i: (0, 0)),
            pl.BlockSpec((2, LANES), lambda b, i: (0, 0)),
            pl.BlockSpec((1, B_HEAD_DIM), lambda b, i: (0, 0)),
        ],
        out_specs=[
            pl.BlockSpec((bt * tc, B_WIDTH), lambda b, i: (b * nt + i, 0)),
            pl.BlockSpec(state, lambda b, i: (b, 0, 0, 0)),
        ],
        out_shape=[
            jax.ShapeDtypeStruct((nb * t, B_WIDTH), BF16),
            jax.ShapeDtypeStruct((nb, B_HEADS, B_HEAD_DIM, B_HEAD_DIM), F32),
        ],
        scratch_shapes=[
            pltpu.VMEM((bt, SUBLANES + tc, 3 * B_WIDTH), F32),
            pltpu.VMEM((3, units, B_HEADS * c, B_HEAD_DIM), F32),
            pltpu.VMEM((units, B_HEADS, B_HEAD_DIM + c, B_HEAD_DIM), BF16),
            pltpu.VMEM((units, B_HEADS, B_HEAD_DIM + c, B_HEAD_DIM), F32),
            pltpu.VMEM(state, F32),
        ],
        compiler_params=pltpu.CompilerParams(dimension_semantics=("arbitrary", "arbitrary"),
                                             vmem_limit_bytes=VMEM_LIMIT),
        name="gated_delta",
    )(z, z, z, hist, s0, convw, gate, onorm)


def _pool_kernel(h_ref, oa_ref, ob_ref, wa_ref, wb_ref, hist_ref, g_ref, win_ref, wgrp_ref, scale_ref, wout_ref,
                 y_ref, tail_ref, ext_ref, *, bt, tt, pos0):
    i = pl.program_id(1)
    pad = POOL_HIST + 1
    rows = bt * tt

    @pl.when(i == 0)
    def _():
        pos = pos0 - pad + lax.broadcasted_iota(jnp.int32, (1, pad, 1), 1)
        ext_ref[:, 0:pad, :] = jnp.where(pos >= 0, hist_ref[...], 0.0)

    @pl.when(i > 0)
    def _():
        ext_ref[:, 0:pad, :] = ext_ref[:, tt:tt + pad, :]

    x = h_ref[...] + _dot(oa_ref[...], wa_ref[...]) + _dot(ob_ref[...], wb_ref[...])
    xn = x * lax.rsqrt(jnp.mean(x * x, axis=-1, keepdims=True) + EPS) * g_ref[...]
    z = _dot(xn.astype(BF16), win_ref[...])
    u = z[:, :C_WIDTH]
    gate = z[:, C_WIDTH:]
    ext_ref[:, pad:pad + tt, :] = u.reshape(bt, tt, C_WIDTH)
    tail_ref[...] = ext_ref[:, tt:tt + pad, :]

    tpos = pos0 + i * tt + (lax.broadcasted_iota(jnp.int32, (rows, 1), 0) & (tt - 1))
    mixed = []
    for gi, w in enumerate(POOL_SIZES):
        cols = slice(gi * C_GROUP, (gi + 1) * C_GROUP)
        s = ext_ref[:, :, cols].reshape(bt * (pad + tt), C_GROUP)
        sh = 1
        while sh < w:
            s = s + pltpu.roll(s, sh, 0)
            sh *= 2
        s = s.reshape(bt, pad + tt, C_GROUP)[:, pad:, :].reshape(rows, C_GROUP)
        cnt = jnp.minimum(tpos + 1, w).astype(F32)
        pooled = s / cnt - u[:, cols]
        m = _dot(pooled.astype(BF16), wgrp_ref[gi]) * scale_ref[:, cols]
        mixed.append((m * _silu(gate[:, cols])).astype(BF16))
    y_ref[...] = x + _dot(jnp.concatenate(mixed, axis=1), wout_ref[...])


def _pool_layer(h, oa, ob, wa, wb, hist, g, win, wgrp, scale, wout, *, nb, bt, t, tt, pos0):
    nt = t // tt
    pad = POOL_HIST + 1
    rows = bt * tt
    kern = functools.partial(_pool_kernel, bt=bt, tt=tt, pos0=pos0)
    tile = lambda b, i: (b * nt + i, 0)
    const2 = lambda b, i: (0, 0)
    return pl.pallas_call(
        kern,
        grid=(nb // bt, nt),
        in_specs=[
            pl.BlockSpec((rows, D_MODEL), tile),
            pl.BlockSpec((rows, A_WIDTH), tile),
            pl.BlockSpec((rows, B_WIDTH), tile),
            pl.BlockSpec((A_WIDTH, D_MODEL), const2),
            pl.BlockSpec((B_WIDTH, D_MODEL), const2),
            pl.BlockSpec((bt, pad, C_WIDTH), lambda b, i: (b, 0, 0)),
            pl.BlockSpec((1, D_MODEL), const2),
            pl.BlockSpec((D_MODEL, 2 * C_WIDTH), const2),
            pl.BlockSpec((len(POOL_SIZES), C_GROUP, C_GROUP), lambda b, i: (0, 0, 0)),
            pl.BlockSpec((1, C_WIDTH), const2),
            pl.BlockSpec((C_WIDTH, D_MODEL), const2),
        ],
        out_specs=[
            pl.BlockSpec((rows, D_MODEL), tile),
            pl.BlockSpec((bt, pad, C_WIDTH), lambda b, i: (b, 0, 0)),
        ],
        out_shape=[
            jax.ShapeDtypeStruct((nb * t, D_MODEL), F32),
            jax.ShapeDtypeStruct((nb, pad, C_WIDTH), F32),
        ],
        scratch_shapes=[pltpu.VMEM((bt, pad + tt, C_WIDTH), F32)],
        compiler_params=pltpu.CompilerParams(dimension_semantics=("arbitrary", "arbitrary"),
                                             vmem_limit_bytes=VMEM_LIMIT),
        name="out_proj_pool_layer",
    )(h, oa, ob, wa, wb, hist, g, win, wgrp, scale, wout)


def _ab_weights(norm_g, w_in, q_norm, k_norm, sinks, conv_w, a_log, dt_bias, o_norm, w_out):
    w = jnp.pad(w_in, ((0, 0), (0, Z_WIDTH - w_in.shape[1]))).astype(BF16)
    lane_pad = (GATE_LANE, LANES - GATE_LANE - B_HEADS)
    gate = jnp.stack([jnp.pad(a_log.astype(F32), lane_pad), jnp.pad(dt_bias.astype(F32), lane_pad)])
    return dict(
        norm_g=norm_g.reshape(1, D_MODEL), w=w, kn=jnp.tile(k_norm, A_KV_HEADS).reshape(1, A_KV_WIDTH),
        qn=jnp.tile(q_norm, A_KV_HEADS).reshape(1, LANES), sinks=sinks.astype(F32), conv_w=conv_w, gate=gate,
        onorm=o_norm.reshape(1, B_HEAD_DIM), wa=w_out[:A_WIDTH].astype(BF16), wb=w_out[A_WIDTH:].astype(BF16))


def _group_tiles(t):
    if t >= CHUNK:
        return dict(proj_rows=512, seqs_attn=1, seqs_delta=1, delta_tokens=512, chunk=CHUNK, seqs_pool=1, pool_tokens=512)
    return dict(proj_rows=512, seqs_attn=8, seqs_delta=16, delta_tokens=t, chunk=t, seqs_pool=512 // t, pool_tokens=t)


def _ab_mixers(h, wts, cache_k, cache_v, s0, conv_hist, *, nb, t):
    tl = _group_tiles(t)
    z = _in_proj(h, wts["norm_g"], wts["w"], wts["kn"], min(nb * t, tl["proj_rows"]))
    bt = tl["seqs_attn"]
    if cache_k is None:
        cq, lk, off = CHUNK, WINDOW + CHUNK, 0
        kv_specs = [_z_window(bt * t, A_KV_WIDTH, Z_AK, lambda b: b), _z_window(bt * t, A_KV_WIDTH, Z_AV, lambda b: b)]
        kbuf = vbuf = z
    else:
        cq, lk, off = t, WINDOW + t, WINDOW
        k_new = z[:, Z_AK:Z_AK + A_KV_WIDTH].reshape(nb, t, A_KV_WIDTH)
        v_new = z[:, Z_AV:Z_AV + A_KV_WIDTH].reshape(nb, t, A_KV_WIDTH)
        kbuf = jnp.concatenate([cache_k.reshape(nb, WINDOW, A_KV_WIDTH), k_new], axis=1).reshape(nb * lk, A_KV_WIDTH)
        vbuf = jnp.concatenate([cache_v.reshape(nb, WINDOW, A_KV_WIDTH), v_new], axis=1).reshape(nb * lk, A_KV_WIDTH)
        kv_specs = [pl.BlockSpec((bt * lk, A_KV_WIDTH), lambda b: (b, 0))] * 2
    o_a = _attention(z, kbuf, vbuf, kv_specs, wts["sinks"], wts["qn"], nb=nb, bt=bt, t=t, cq=cq, lk=lk, off=off)
    hist = jnp.pad(conv_hist.astype(F32), ((0, 0), (SUBLANES - (CONV_W - 1), 0), (0, 0)))
    o_b, s_new = _delta(z, hist, s0.astype(F32), wts["conv_w"], wts["gate"], wts["onorm"], nb=nb,
                        bt=tl["seqs_delta"], t=t, tc=tl["delta_tokens"], c=tl["chunk"])
    return o_a, o_b, z, s_new


def _last_rows(z, col, width, nb, t, rows):
    return z.reshape(nb, t, Z_WIDTH)[:, t - rows:, col:col + width]


def _cache_rows(z, col, nb, t, rows):
    return _last_rows(z, col, A_KV_WIDTH, nb, t, rows).reshape(nb, rows, A_KV_HEADS, A_HEAD_DIM)


def kernel(x_prompt, x_sample, cache_a_k, cache_a_v, state_b_s, state_b_conv, state_c_pool,
           norm_ab, w_in_ab, q_norm_a, k_norm_a, sinks_a, conv_b, a_log_b, dt_bias_b, o_norm_b, w_out_ab,
           norm_c, w_in_c, w_grp_c, scale_c, w_out_c):
    bp, tp, _ = x_prompt.shape
    bs, ts, _ = x_sample.shape
    hp = x_prompt.reshape(bp * tp, D_MODEL)
    hs = x_sample.reshape(bs * ts, D_MODEL)

    wts = _ab_weights(norm_ab[0], w_in_ab[0], q_norm_a[0], k_norm_a[0], sinks_a[0], conv_b[0], a_log_b[0],
                      dt_bias_b[0], o_norm_b[0], w_out_ab[0])
    s0 = jnp.zeros((bp, B_HEADS, B_HEAD_DIM, B_HEAD_DIM), F32)
    c0 = jnp.zeros((bp, CONV_W - 1, 3 * B_WIDTH), F32)
    oap, obp, zp, sp = _ab_mixers(hp, wts, None, None, s0, c0, nb=bp, t=tp)
    oas, obs, zs, ss = _ab_mixers(hs, wts, cache_a_k[0], cache_a_v[0], state_b_s[0], state_b_conv[0], nb=bs, t=ts)
    p_a_k = _cache_rows(zp, Z_AK, bp, tp, WINDOW)[None]
    p_a_v = _cache_rows(zp, Z_AV, bp, tp, WINDOW)[None]
    s_a_k = _cache_rows(zs, Z_AK, bs, ts, ts)[None]
    s_a_v = _cache_rows(zs, Z_AV, bs, ts, ts)[None]
    p_b_conv = _last_rows(zp, Z_BQKV, 3 * B_WIDTH, bp, tp, CONV_W - 1)[None]
    s_b_conv = _last_rows(zs, Z_BQKV, 3 * B_WIDTH, bs, ts, CONV_W - 1)[None]

    g_c = norm_c[0].reshape(1, D_MODEL)
    win = w_in_c[0].astype(BF16)
    wgrp = w_grp_c[0].astype(BF16)
    scale = scale_c[0].reshape(1, C_WIDTH)
    wout = w_out_c[0].astype(BF16)
    h0 = jnp.zeros((bp, POOL_HIST + 1, C_WIDTH), F32)
    hs0 = jnp.pad(state_c_pool[0].astype(F32), ((0, 0), (1, 0), (0, 0)))
    tlp, tls = _group_tiles(tp), _group_tiles(ts)
    yp, tail_p = _pool_layer(hp, oap, obp, wts["wa"], wts["wb"], h0, g_c, win, wgrp, scale, wout, nb=bp,
                             bt=tlp["seqs_pool"], t=tp, tt=tlp["pool_tokens"], pos0=0)
    ys, tail_s = _pool_layer(hs, oas, obs, wts["wa"], wts["wb"], hs0, g_c, win, wgrp, scale, wout, nb=bs,
                             bt=tls["seqs_pool"], t=ts, tt=tls["pool_tokens"], pos0=PAST_LEN)

    return (yp.reshape(bp, tp, D_MODEL), ys.reshape(bs, ts, D_MODEL),
            p_a_k, p_a_v, sp[None], p_b_conv, tail_p[:, 1:][None],
            s_a_k, s_a_v, ss[None], s_b_conv, tail_s[:, 1:][None])
```

```python
import functools

import jax
import jax.numpy as jnp
from jax import lax
from jax.experimental import pallas as pl
from jax.experimental.pallas import tpu as pltpu

F32 = jnp.float32
BF16 = jnp.bfloat16
HIGHEST = lax.Precision.HIGHEST

D_MODEL = 1024
CHUNK = 64
PAST_LEN = 2048
EPS = 1e-6
NEG_INF = -1e30
LOG2E = 1.4426950408889634

A_HEADS = 8
A_KV_HEADS = 2
A_HEAD_DIM = 64
A_WIDTH = A_HEADS * A_HEAD_DIM
A_KV_WIDTH = A_KV_HEADS * A_HEAD_DIM
A_REP = A_HEADS // A_KV_HEADS
WINDOW = 128

B_HEADS = 4
B_HEAD_DIM = 128
B_WIDTH = B_HEADS * B_HEAD_DIM
CONV_W = 4

POOL_SIZES = (2, 4, 8, 16)
C_WIDTH = D_MODEL
C_GROUP = C_WIDTH // len(POOL_SIZES)
POOL_HIST = max(POOL_SIZES) - 1

LANES = 128
SUBLANES = 8

Z_AQ = 0
Z_AK = Z_AQ + A_WIDTH
Z_AV = Z_AK + A_KV_WIDTH
Z_AG = Z_AV + A_KV_WIDTH
Z_BQKV = Z_AG + A_WIDTH
Z_BG = Z_BQKV + 3 * B_WIDTH
Z_BA = Z_BG + B_WIDTH
Z_WIDTH = Z_BA + LANES
GATE_LANE = B_HEADS


def _z_window(rows, width, col, row_block):
    return pl.BlockSpec((pl.Element(rows), pl.Element(width)), lambda *idx: (row_block(*idx) * rows, col))

VMEM_LIMIT = 48 * 1024 * 1024


def _sigmoid(x):
    return 1.0 / (1.0 + jnp.exp(-x))


def _silu(x):
    return x * _sigmoid(x)


def _softplus(x):
    return jnp.maximum(x, 0.0) + jnp.log(1.0 + jnp.exp(-jnp.abs(x)))


def _dot(a, b, precision=None):
    return jnp.dot(a, b, preferred_element_type=F32, precision=precision)


def _dot_nt(a, b, precision=None):
    return lax.dot_general(a, b, (((1,), (1,)), ((), ())), preferred_element_type=F32, precision=precision)


def _dot_tn(a, b, precision=None):
    return lax.dot_general(a, b, (((0,), (0,)), ((), ())), preferred_element_type=F32, precision=precision)


def _log2(n):
    assert n & (n - 1) == 0
    return n.bit_length() - 1


def _in_proj_kernel(x_ref, g_ref, w_ref, kn_ref, z_ref):
    x = x_ref[...]
    xn = x * lax.rsqrt(jnp.mean(x * x, axis=-1, keepdims=True) + EPS) * g_ref[...]
    n_in = w_ref.shape[1]
    z_ref[:, Z_BA:Z_WIDTH] = jnp.zeros((x.shape[0], Z_WIDTH - Z_BA), F32)
    z_ref[:, 0:n_in] = _dot(xn.astype(BF16), w_ref[...])
    k = z_ref[:, Z_AK:Z_AK + A_KV_WIDTH]
    sq = k * k
    lane = lax.broadcasted_iota(jnp.int32, k.shape, 1)
    first = lane < A_HEAD_DIM
    s0 = jnp.sum(jnp.where(first, sq, 0.0), axis=-1, keepdims=True)
    s1 = jnp.sum(jnp.where(first, 0.0, sq), axis=-1, keepdims=True)
    ms = jnp.where(first, s0, s1) * (1.0 / A_HEAD_DIM)
    z_ref[:, Z_AK:Z_AK + A_KV_WIDTH] = k * lax.rsqrt(ms + EPS) * kn_ref[...]


def _in_proj(x, g, w, kn, tm):
    n = x.shape[0]
    return pl.pallas_call(
        _in_proj_kernel,
        grid=(n // tm,),
        in_specs=[
            pl.BlockSpec((tm, D_MODEL), lambda i: (i, 0)),
            pl.BlockSpec((1, D_MODEL), lambda i: (0, 0)),
            pl.BlockSpec(w.shape, lambda i: (0, 0)),
            pl.BlockSpec((1, A_KV_WIDTH), lambda i: (0, 0)),
        ],
        out_specs=pl.BlockSpec((tm, Z_WIDTH), lambda i: (i, 0)),
        out_shape=jax.ShapeDtypeStruct((n, Z_WIDTH), F32),
        compiler_params=pltpu.CompilerParams(dimension_semantics=("arbitrary",), vmem_limit_bytes=VMEM_LIMIT),
        name="in_proj",
    )(x, g, w, kn)


A_HALVES = LANES // A_HEAD_DIM
A_QBLOCKS = A_WIDTH // LANES
A_COPIES = A_KV_HEADS


def _attn_half(r, c):
    g = r * A_HALVES // A_REP
    return (g + c) % A_HALVES


def _attn_tables(sinks_ref, bias_ref, *, cq, lk, front, fill):
    rows = A_QBLOCKS * cq
    nvar = front // cq + 1
    row = lax.broadcasted_iota(jnp.int32, (rows, 1), 0)
    blk = row >> _log2(cq)

    def per_row(value):
        out = []
        for c in range(A_COPIES):
            col = jnp.zeros((rows, 1), F32)
            for r in range(A_QBLOCKS):
                col = jnp.where(blk == r, value(r * A_HALVES + _attn_half(r, c)), col)
            out.append(col)
        return out

    @pl.when(fill)
    def _():
        col = lax.broadcasted_iota(jnp.int32, (rows, lk), 1)
        dist = jnp.abs((row & (cq - 1)) + WINDOW - col).astype(F32)
        slopes = per_row(lambda h: 2.0 ** (-8.0 * (h + 1) / A_HEADS))
        for c in range(A_COPIES):
            for var in range(nvar):
                bias_ref[var, c] = jnp.where(col >= front - var * cq, -slopes[c] * dist, NEG_INF) * LOG2E

    first = lax.broadcasted_iota(jnp.int32, (cq, LANES), 1) < A_HEAD_DIM
    sks = per_row(lambda h: sinks_ref[h] * LOG2E)
    return first, sks, nvar


def _attn_chunk(qc, kv, biases, sks, first, qn, gate, cq):
    blocks = []
    for r in range(A_QBLOCKS):
        x = qc[:, r * LANES:(r + 1) * LANES]
        sq = x * x
        s_lo = jnp.sum(jnp.where(first, sq, 0.0), axis=-1, keepdims=True)
        s_hi = jnp.sum(jnp.where(first, 0.0, sq), axis=-1, keepdims=True)
        ms = jnp.where(first, s_lo, s_hi) * (1.0 / A_HEAD_DIM)
        blocks.append(x * lax.rsqrt(ms + EPS) * qn)
    outs = []
    for c in range(A_COPIES):
        lhs = jnp.concatenate(
            [jnp.where(first if _attn_half(r, c) == 0 else jnp.logical_not(first), blocks[r], 0.0)
             for r in range(A_QBLOCKS)], axis=0)
        kk, vv = kv[c]
        sc = _dot_nt(lhs.astype(BF16), kk) + biases[c]
        m = jnp.maximum(jnp.max(sc, axis=-1, keepdims=True), sks[c])
        p = jnp.exp2(sc - m)
        den = jnp.sum(p, axis=-1, keepdims=True) + jnp.exp2(sks[c] - m)
        outs.append(_dot(p.astype(BF16), vv) * (1.0 / den))
    tile = []
    for r in range(A_QBLOCKS):
        rows = slice(r * cq, (r + 1) * cq)
        low = 0 if _attn_half(r, 0) == 0 else 1
        tile.append(jnp.where(first, outs[low][rows], outs[1 - low][rows]))
    return (jnp.concatenate(tile, axis=1) * _silu(gate)).astype(BF16)


def _attn_kernel(sinks_ref, q_ref, k_ref, v_ref, ag_ref, qn_ref, o_ref, kp_ref, vp_ref, bias_ref, *, bt, cq, lk, off, t):
    front = WINDOW - off
    lkt = k_ref.shape[0] // bt
    n_chunks = t // cq

    for b in range(bt):
        for src, dst in ((k_ref, kp_ref), (v_ref, vp_ref)):
            x = src[b * lkt:(b + 1) * lkt, :]
            if front:
                dst[:, b, 0:front, :] = jnp.zeros((A_COPIES, front, A_KV_WIDTH), BF16)
            dst[0, b, front:front + lkt, :] = x.astype(BF16)
            dst[1, b, front:front + lkt, :] = pltpu.roll(x, A_HEAD_DIM, 1).astype(BF16)

    first, sks, nvar = _attn_tables(sinks_ref, bias_ref, cq=cq, lk=lk, front=front, fill=pl.program_id(0) == 0)
    qn = qn_ref[...] * (A_HEAD_DIM ** -0.5 * LOG2E)

    def chunk(b, cg):
        k0 = cg * cq if isinstance(cg, int) else pl.multiple_of(cg * cq, cq)
        r0 = b * t + k0
        var = min(cg, nvar - 1) if isinstance(cg, int) else jnp.minimum(cg, nvar - 1)
        kv = [(kp_ref[c, b, pl.ds(k0, lk), :], vp_ref[c, b, pl.ds(k0, lk), :]) for c in range(A_COPIES)]
        o_ref[pl.ds(r0, cq), :] = _attn_chunk(
            q_ref[pl.ds(r0, cq), :], kv, [bias_ref[var, c] for c in range(A_COPIES)], sks, first, qn,
            ag_ref[pl.ds(r0, cq), :], cq)

    for b in range(bt):
        if n_chunks == 1:
            chunk(b, 0)
        else:
            lax.fori_loop(0, n_chunks, lambda cg, carry, b=b: (chunk(b, cg), carry)[1], 0, unroll=min(n_chunks, 8))


def _attention(z, kbuf, vbuf, kv_specs, sinks, qn, *, nb, bt, t, cq, lk, off):
    front = WINDOW - off
    lkt = kbuf.shape[0] // nb
    kern = functools.partial(_attn_kernel, bt=bt, cq=cq, lk=lk, off=off, t=t)
    return pl.pallas_call(
        kern,
        grid=(nb // bt,),
        in_specs=[
            pl.BlockSpec(memory_space=pltpu.SMEM),
            _z_window(bt * t, A_WIDTH, Z_AQ, lambda b: b),
            kv_specs[0],
            kv_specs[1],
            _z_window(bt * t, A_WIDTH, Z_AG, lambda b: b),
            pl.BlockSpec((1, LANES), lambda b: (0, 0)),
        ],
        out_specs=pl.BlockSpec((bt * t, A_WIDTH), lambda b: (b, 0)),
        out_shape=jax.ShapeDtypeStruct((nb * t, A_WIDTH), BF16),
        scratch_shapes=[
            pltpu.VMEM((A_COPIES, bt, front + lkt, A_KV_WIDTH), BF16),
            pltpu.VMEM((A_COPIES, bt, front + lkt, A_KV_WIDTH), BF16),
            pltpu.VMEM((front // cq + 1, A_COPIES, A_QBLOCKS * cq, lk), F32),
        ],
        compiler_params=pltpu.CompilerParams(dimension_semantics=("arbitrary",), vmem_limit_bytes=VMEM_LIMIT),
        name="swa_attention",
    )(sinks, z, kbuf, vbuf, z, qn)


def _bf(x):
    return x.astype(BF16)


def _unit_lower_inverse(a_list, c, ri, lj, expand):
    base = SUBLANES
    same = (ri >> _log2(base)) == (lj >> _log2(base))
    eye = jnp.where(ri == lj, 1.0, 0.0)
    n1 = [jnp.where(same, -a, 0.0) for a in a_list]
    x = [eye + n for n in n1]
    n1b = [_bf(n) for n in n1]
    n2b = [_bf(_dot(nb, expand(nb))) for nb in n1b]
    n2e = [expand(nb) for nb in n2b]
    x = [xi + _dot(_bf(xi), ne) for xi, ne in zip(x, n2e)]
    n4e = [expand(_bf(_dot(nb, ne))) for nb, ne in zip(n2b, n2e)]
    x = [xi + _dot(_bf(xi), ne) for xi, ne in zip(x, n4e)]
    s = base
    while s < c:
        sel = ((ri >> _log2(2 * s)) == (lj >> _log2(2 * s))) & ((ri >> _log2(s)) != (lj >> _log2(s)))
        xb = [_bf(xi) for xi in x]
        xo = [_dot(b, expand(_bf(jnp.where(sel, a, 0.0)))) for b, a in zip(xb, a_list)]
        x = [xi - _dot(_bf(o), expand(b)) for xi, o, b in zip(x, xo, xb)]
        s *= 2
    return x


def _delta_kernel(qkv_ref, ba_ref, bg_ref, hist_ref, s0_ref, convw_ref, gate_ref, onorm_ref,
                  o_ref, sout_ref, ext_ref, st_ref, lhs_ref, add_ref, s_ref, *, bt, c, tc):
    i = pl.program_id(1)
    hd = B_HEAD_DIM
    nh = B_HEADS
    r = nh * c
    nch = tc // c
    pad = SUBLANES
    units = [(b, cc) for b in range(bt) for cc in range(nch)]

    @pl.when(i == 0)
    def _():
        ext_ref[:, 0:pad, :] = hist_ref[...]
        s_ref[...] = s0_ref[...]

    @pl.when(i > 0)
    def _():
        ext_ref[:, 0:pad, :] = ext_ref[:, tc:tc + pad, :]

    for b in range(bt):
        ext_ref[b, pad:pad + tc, :] = qkv_ref[b * tc:(b + 1) * tc, :]

    for u, (b, cc) in enumerate(units):
        base = pad - (CONV_W - 1) + cc * c
        acc = ext_ref[b, base:base + c, :] * convw_ref[0:1, :]
        for j in range(1, CONV_W):
            acc = acc + ext_ref[b, base + j:base + j + c, :] * convw_ref[j:j + 1, :]
        y = _silu(acc)
        for part in range(3):
            for h in range(nh):
                blk = y[:, (part * nh + h) * hd:(part * nh + h + 1) * hd]
                if part < 2:
                    blk = blk * lax.rsqrt(jnp.sum(blk * blk, axis=-1, keepdims=True) + EPS)
                st_ref[part, u, h * c:(h + 1) * c, :] = blk

    ri = lax.broadcasted_iota(jnp.int32, (c, r), 0)
    li = lax.broadcasted_iota(jnp.int32, (c, r), 1)
    lj = li & (c - 1)
    lh = li >> _log2(c)
    lower = ri >= lj
    strict = ri > lj
    head_sel = [jnp.where(lh == h, 1.0, 0.0).astype(BF16) for h in range(nh)]

    def expand(xb):
        return jnp.concatenate([xb * m for m in head_sel], axis=0)

    def lanes(x_st):
        return jnp.concatenate([x_st[h * c:(h + 1) * c] for h in range(nh)], axis=1)

    def head_blocks(x_st):
        zero = jnp.zeros((c, hd), x_st.dtype)
        return jnp.concatenate(
            [jnp.concatenate([x_st[h * c:(h + 1) * c] if h2 == h else zero for h2 in range(nh)], axis=1)
             for h in range(nh)], axis=0)

    ci = lax.broadcasted_iota(jnp.int32, (c, c), 0)
    cj = lax.broadcasted_iota(jnp.int32, (c, c), 1)
    tril = jnp.where(ci >= cj, 1.0, 0.0)
    di = lax.broadcasted_iota(jnp.int32, (hd, hd), 0)
    dj = lax.broadcasted_iota(jnp.int32, (hd, hd), 1)
    eye_hd = di == dj
    neg_rate = -jnp.exp(gate_ref[0:1, :])
    dt_bias = gate_ref[1:2, :]
    onorm = onorm_ref[...]

    def stack_cols(x, lane0):
        return jnp.concatenate([x[:, lane0 + h:lane0 + h + 1] for h in range(nh)], axis=0)

    beta, gc, gcc, gr, glast = [], [], [], [], []
    for b, cc in units:
        ba = ba_ref[b * tc + cc * c:b * tc + (cc + 1) * c, :]
        g_all = neg_rate * _softplus(ba + dt_bias)
        gcum = _dot(tril, g_all, HIGHEST)
        gcum_t = gcum.T
        beta.append(stack_cols(_sigmoid(ba), 0))
        gc.append(stack_cols(gcum, GATE_LANE))
        gcc_h = jnp.zeros((c, r), F32)
        for h in range(nh):
            gcc_h = jnp.where(lh == h, gcum[:, GATE_LANE + h:GATE_LANE + h + 1], gcc_h)
        gcc.append(gcc_h)
        gr.append(jnp.concatenate([gcum_t[GATE_LANE + h:GATE_LANE + h + 1, :] for h in range(nh)], axis=1))
        glast.append(jnp.concatenate(
            [jnp.broadcast_to(gcum[c - 1:c, GATE_LANE + h:GATE_LANE + h + 1], (c, 1)) for h in range(nh)], axis=0))

    def fold_units(us):
        n = range(len(us))
        q = [st_ref[0, u] * (hd ** -0.5) for u in us]
        k = [st_ref[1, u] for u in us]
        v = [st_ref[2, u] for u in us]
        kb = [k[j] * beta[u] for j, u in enumerate(us)]
        big = [_dot_nt(_bf(jnp.concatenate([lanes(kb[j]), lanes(q[j])], axis=0)), head_blocks(_bf(k[j]))) for j in n]
        decay = [jnp.where(lower, jnp.exp(jnp.where(lower, gcc[u] - gr[u], 0.0)), 0.0) for u in us]
        a = [jnp.where(strict, big[j][:c] * decay[j], 0.0) for j in n]
        qkd = [expand(_bf(big[j][c:] * decay[j])) for j in n]
        tinv = _unit_lower_inverse(a, c, ri, lj, expand)
        eg = [jnp.exp(gc[u]) for u in us]
        rhs = [_bf(jnp.concatenate([v[j] * beta[u], kb[j] * eg[j]], axis=1)) for j, u in enumerate(us)]
        sol = [_dot(expand(_bf(tinv[j])), rhs[j]) for j in n]
        solb = [_bf(s) for s in sol]
        fold = [_dot(qkd[j], solb[j]) for j in n]
        k_dec = [_bf(k[j] * jnp.exp(glast[u] - gc[u])) for j, u in enumerate(us)]
        for j, u in enumerate(us):
            qp = q[j] * eg[j] - fold[j][:, hd:]
            for h in range(nh):
                rows = slice(h * c, (h + 1) * c)
                kt = _dot_tn(k_dec[j][rows], solb[j][rows])
                g_tot = jnp.exp(glast[u][h * c:h * c + 1, :])
                lhs_ref[u, h, 0:hd, :] = _bf(jnp.where(eye_hd, g_tot, 0.0) - kt[:, hd:])
                lhs_ref[u, h, hd:hd + c, :] = _bf(qp[rows])
                add_ref[u, h, 0:hd, :] = kt[:, :hd]
                add_ref[u, h, hd:hd + c, :] = fold[j][rows, :hd]

    fold_units(list(range(len(units))))

    for u, (b, cc) in enumerate(units):
        rows = slice(b * tc + cc * c, b * tc + (cc + 1) * c)
        for h in range(nh):
            res = _dot(lhs_ref[u, h], _bf(s_ref[b, h])) + add_ref[u, h]
            s_ref[b, h] = res[:hd]
            o = res[hd:]
            on = o * lax.rsqrt(jnp.mean(o * o, axis=-1, keepdims=True) + EPS) * onorm
            bg = bg_ref[rows, h * hd:(h + 1) * hd]
            o_ref[rows, h * hd:(h + 1) * hd] = (on * _silu(bg)).astype(BF16)
    sout_ref[...] = s_ref[...]


def _delta(z, hist, s0, convw, gate, onorm, *, nb, bt, t, tc, c):
    nt = t // tc
    units = bt * (tc // c)
    state = (bt, B_HEADS, B_HEAD_DIM, B_HEAD_DIM)
    kern = functools.partial(_delta_kernel, bt=bt, c=c, tc=tc)
    return pl.pallas_call(
        kern,
        grid=(nb // bt, nt),
        in_specs=[
            _z_window(bt * tc, 3 * B_WIDTH, Z_BQKV, lambda b, i: b * nt + i),
            _z_window(bt * tc, LANES, Z_BA, lambda b, i: b * nt + i),
            _z_window(bt * tc, B_WIDTH, Z_BG, lambda b, i: b * nt + i),
            pl.BlockSpec((bt, SUBLANES, 3 * B_WIDTH), lambda b, i: (b, 0, 0)),
            pl.BlockSpec(state, lambda b, i: (b, 0, 0, 0)),
            pl.BlockSpec((CONV_W, 3 * B_WIDTH), lambda b, i: (0, 0)),
            pl.BlockSpec((2, LANES), lambda b, i: (0, 0)),
            pl.BlockSpec((1, B_HEAD_DIM), lambda b, i: (0, 0)),
        ],
        out_specs=[
            pl.BlockSpec((bt * tc, B_WIDTH), lambda b, i: (b * nt + i, 0)),
            pl.BlockSpec(state, lambda b, i: (b, 0, 0, 0)),
        ],
        out_shape=[
            jax.ShapeDtypeStruct((nb * t, B_WIDTH), BF16),
            jax.ShapeDtypeStruct((nb, B_HEADS, B_HEAD_DIM, B_HEAD_DIM), F32),
        ],
        scratch_shapes=[
            pltpu.VMEM((bt, SUBLANES + tc, 3 * B_WIDTH), F32),
            pltpu.VMEM((3, units, B_HEADS * c, B_HEAD_DIM), F32),
            pltpu.VMEM((units, B_HEADS, B_HEAD_DIM + c, B_HEAD_DIM), BF16),
            pltpu.VMEM((units, B_HEADS, B_HEAD_DIM + c, B_HEAD_DIM), F32),
            pltpu.VMEM(state, F32),
        ],
        compiler_params=pltpu.CompilerParams(dimension_semantics=("arbitrary", "arbitrary"),
                                             vmem_limit_bytes=VMEM_LIMIT),
        name="gated_delta",
    )(z, z, z, hist, s0, convw, gate, onorm)


def _pool_kernel(h_ref, oa_ref, ob_ref, wa_ref, wb_ref, hist_ref, g_ref, win_ref, wgrp_ref, scale_ref, wout_ref,
                 y_ref, tail_ref, ext_ref, *, bt, tt, pos0):
    i = pl.program_id(1)
    pad = POOL_HIST + 1
    rows = bt * tt

    @pl.when(i == 0)
    def _():
        pos = pos0 - pad + lax.broadcasted_iota(jnp.int32, (1, pad, 1), 1)
        ext_ref[:, 0:pad, :] = jnp.where(pos >= 0, hist_ref[...], 0.0)

    @pl.when(i > 0)
    def _():
        ext_ref[:, 0:pad, :] = ext_ref[:, tt:tt + pad, :]

    x = h_ref[...] + _dot(oa_ref[...], wa_ref[...]) + _dot(ob_ref[...], wb_ref[...])
    xn = x * lax.rsqrt(jnp.mean(x * x, axis=-1, keepdims=True) + EPS) * g_ref[...]
    z = _dot(xn.astype(BF16), win_ref[...])
    u = z[:, :C_WIDTH]
    gate = z[:, C_WIDTH:]
    ext_ref[:, pad:pad + tt, :] = u.reshape(bt, tt, C_WIDTH)
    tail_ref[...] = ext_ref[:, tt:tt + pad, :]

    tpos = pos0 + i * tt + (lax.broadcasted_iota(jnp.int32, (rows, 1), 0) & (tt - 1))
    mixed = []
    for gi, w in enumerate(POOL_SIZES):
        cols = slice(gi * C_GROUP, (gi + 1) * C_GROUP)
        s = ext_ref[:, :, cols].reshape(bt * (pad + tt), C_GROUP)
        sh = 1
        while sh < w:
            s = s + pltpu.roll(s, sh, 0)
            sh *= 2
        s = s.reshape(bt, pad + tt, C_GROUP)[:, pad:, :].reshape(rows, C_GROUP)
        cnt = jnp.minimum(tpos + 1, w).astype(F32)
        pooled = s / cnt - u[:, cols]
        m = _dot(pooled.astype(BF16), wgrp_ref[gi]) * scale_ref[:, cols]
        mixed.append((m * _silu(gate[:, cols])).astype(BF16))
    y_ref[...] = x + _dot(jnp.concatenate(mixed, axis=1), wout_ref[...])


def _pool_layer(h, oa, ob, wa, wb, hist, g, win, wgrp, scale, wout, *, nb, bt, t, tt, pos0):
    nt = t // tt
    pad = POOL_HIST + 1
    rows = bt * tt
    kern = functools.partial(_pool_kernel, bt=bt, tt=tt, pos0=pos0)
    tile = lambda b, i: (b * nt + i, 0)
    const2 = lambda b, i: (0, 0)
    return pl.pallas_call(
        kern,
        grid=(nb // bt, nt),
        in_specs=[
            pl.BlockSpec((rows, D_MODEL), tile),
            pl.BlockSpec((rows, A_WIDTH), tile),
            pl.BlockSpec((rows, B_WIDTH), tile),
            pl.BlockSpec((A_WIDTH, D_MODEL), const2),
            pl.BlockSpec((B_WIDTH, D_MODEL), const2),
            pl.BlockSpec((bt, pad, C_WIDTH), lambda b, i: (b, 0, 0)),
            pl.BlockSpec((1, D_MODEL), const2),
            pl.BlockSpec((D_MODEL, 2 * C_WIDTH), const2),
            pl.BlockSpec((len(POOL_SIZES), C_GROUP, C_GROUP), lambda b, i: (0, 0, 0)),
            pl.BlockSpec((1, C_WIDTH), const2),
            pl.BlockSpec((C_WIDTH, D_MODEL), const2),
        ],
        out_specs=[
            pl.BlockSpec((rows, D_MODEL), tile),
            pl.BlockSpec((bt, pad, C_WIDTH), lambda b, i: (b, 0, 0)),
        ],
        out_shape=[
            jax.ShapeDtypeStruct((nb * t, D_MODEL), F32),
            jax.ShapeDtypeStruct((nb, pad, C_WIDTH), F32),
        ],
        scratch_shapes=[pltpu.VMEM((bt, pad + tt, C_WIDTH), F32)],
        compiler_params=pltpu.CompilerParams(dimension_semantics=("arbitrary", "arbitrary"),
                                             vmem_limit_bytes=VMEM_LIMIT),
        name="out_proj_pool_layer",
    )(h, oa, ob, wa, wb, hist, g, win, wgrp, scale, wout)


def _ab_weights(norm_g, w_in, q_norm, k_norm, sinks, conv_w, a_log, dt_bias, o_norm, w_out):
    w = w_in.astype(BF16)
    lane_pad = (GATE_LANE, LANES - GATE_LANE - B_HEADS)
    gate = jnp.stack([jnp.pad(a_log.astype(F32), lane_pad), jnp.pad(dt_bias.astype(F32), lane_pad)])
    return dict(
        norm_g=norm_g.reshape(1, D_MODEL), w=w, kn=jnp.tile(k_norm, A_KV_HEADS).reshape(1, A_KV_WIDTH),
        qn=jnp.tile(q_norm, A_KV_HEADS).reshape(1, LANES), sinks=sinks.astype(F32), conv_w=conv_w, gate=gate,
        onorm=o_norm.reshape(1, B_HEAD_DIM), wa=w_out[:A_WIDTH].astype(BF16), wb=w_out[A_WIDTH:].astype(BF16))


def _group_tiles(t):
    if t >= CHUNK:
        return dict(proj_rows=512, seqs_attn=1, seqs_delta=1, delta_tokens=512, chunk=CHUNK, seqs_pool=1, pool_tokens=512)
    return dict(proj_rows=512, seqs_attn=32, seqs_delta=16, delta_tokens=t, chunk=t, seqs_pool=512 // t, pool_tokens=t)


def _ab_mixers(h, wts, cache_k, cache_v, s0, conv_hist, *, nb, t):
    tl = _group_tiles(t)
    z = _in_proj(h, wts["norm_g"], wts["w"], wts["kn"], min(nb * t, tl["proj_rows"]))
    bt = tl["seqs_attn"]
    if cache_k is None:
        cq, lk, off = CHUNK, WINDOW + CHUNK, 0
        kv_specs = [_z_window(bt * t, A_KV_WIDTH, Z_AK, lambda b: b), _z_window(bt * t, A_KV_WIDTH, Z_AV, lambda b: b)]
        kbuf = vbuf = z
    else:
        cq, lk, off = t, WINDOW + t, WINDOW
        k_new = z[:, Z_AK:Z_AK + A_KV_WIDTH].reshape(nb, t, A_KV_WIDTH)
        v_new = z[:, Z_AV:Z_AV + A_KV_WIDTH].reshape(nb, t, A_KV_WIDTH)
        kbuf = jnp.concatenate([cache_k.reshape(nb, WINDOW, A_KV_WIDTH), k_new], axis=1).reshape(nb * lk, A_KV_WIDTH)
        vbuf = jnp.concatenate([cache_v.reshape(nb, WINDOW, A_KV_WIDTH), v_new], axis=1).reshape(nb * lk, A_KV_WIDTH)
        kv_specs = [pl.BlockSpec((bt * lk, A_KV_WIDTH), lambda b: (b, 0))] * 2
    o_a = _attention(z, kbuf, vbuf, kv_specs, wts["sinks"], wts["qn"], nb=nb, bt=bt, t=t, cq=cq, lk=lk, off=off)
    hist = jnp.pad(conv_hist.astype(F32), ((0, 0), (SUBLANES - (CONV_W - 1), 0), (0, 0)))
    o_b, s_new = _delta(z, hist, s0.astype(F32), wts["conv_w"], wts["gate"], wts["onorm"], nb=nb,
                        bt=tl["seqs_delta"], t=t, tc=tl["delta_tokens"], c=tl["chunk"])
    return o_a, o_b, z, s_new


def _last_rows(z, col, width, nb, t, rows):
    return z.reshape(nb, t, Z_WIDTH)[:, t - rows:, col:col + width]


def _cache_rows(z, col, nb, t, rows):
    return _last_rows(z, col, A_KV_WIDTH, nb, t, rows).reshape(nb, rows, A_KV_HEADS, A_HEAD_DIM)


def kernel(x_prompt, x_sample, cache_a_k, cache_a_v, state_b_s, state_b_conv, state_c_pool,
           norm_ab, w_in_ab, q_norm_a, k_norm_a, sinks_a, conv_b, a_log_b, dt_bias_b, o_norm_b, w_out_ab,
           norm_c, w_in_c, w_grp_c, scale_c, w_out_c):
    bp, tp, _ = x_prompt.shape
    bs, ts, _ = x_sample.shape
    hp = x_prompt.reshape(bp * tp, D_MODEL)
    hs = x_sample.reshape(bs * ts, D_MODEL)

    wts = _ab_weights(norm_ab[0], w_in_ab[0], q_norm_a[0], k_norm_a[0], sinks_a[0], conv_b[0], a_log_b[0],
                      dt_bias_b[0], o_norm_b[0], w_out_ab[0])
    s0 = jnp.zeros((bp, B_HEADS, B_HEAD_DIM, B_HEAD_DIM), F32)
    c0 = jnp.zeros((bp, CONV_W - 1, 3 * B_WIDTH), F32)
    oap, obp, zp, sp = _ab_mixers(hp, wts, None, None, s0, c0, nb=bp, t=tp)
    oas, obs, zs, ss = _ab_mixers(hs, wts, cache_a_k[0], cache_a_v[0], state_b_s[0], state_b_conv[0], nb=bs, t=ts)
    p_a_k = _cache_rows(zp, Z_AK, bp, tp, WINDOW)[None]
    p_a_v = _cache_rows(zp, Z_AV, bp, tp, WINDOW)[None]
    s_a_k = _cache_rows(zs, Z_AK, bs, ts, ts)[None]
    s_a_v = _cache_rows(zs, Z_AV, bs, ts, ts)[None]
    p_b_conv = _last_rows(zp, Z_BQKV, 3 * B_WIDTH, bp, tp, CONV_W - 1)[None]
    s_b_conv = _last_rows(zs, Z_BQKV, 3 * B_WIDTH, bs, ts, CONV_W - 1)[None]

    g_c = norm_c[0].reshape(1, D_MODEL)
    win = w_in_c[0].astype(BF16)
    wgrp = w_grp_c[0].astype(BF16)
    scale = scale_c[0].reshape(1, C_WIDTH)
    wout = w_out_c[0].astype(BF16)
    h0 = jnp.zeros((bp, POOL_HIST + 1, C_WIDTH), F32)
    hs0 = jnp.pad(state_c_pool[0].astype(F32), ((0, 0), (1, 0), (0, 0)))
    tlp, tls = _group_tiles(tp), _group_tiles(ts)
    yp, tail_p = _pool_layer(hp, oap, obp, wts["wa"], wts["wb"], h0, g_c, win, wgrp, scale, wout, nb=bp,
                             bt=tlp["seqs_pool"], t=tp, tt=tlp["pool_tokens"], pos0=0)
    ys, tail_s = _pool_layer(hs, oas, obs, wts["wa"], wts["wb"], hs0, g_c, win, wgrp, scale, wout, nb=bs,
                             bt=tls["seqs_pool"], t=ts, tt=tls["pool_tokens"], pos0=PAST_LEN)

    return (yp.reshape(bp, tp, D_MODEL), ys.reshape(bs, ts, D_MODEL),
            p_a_k, p_a_v, sp[None], p_b_conv, tail_p[:, 1:][None],
            s_a_k, s_a_v, ss[None], s_b_conv, tail_s[:, 1:][None])
```

```python
import functools

import jax
import jax.numpy as jnp
from jax import lax
from jax.experimental import pallas as pl
from jax.experimental.pallas import tpu as pltpu

F32 = jnp.float32
BF16 = jnp.bfloat16
HIGHEST = lax.Precision.HIGHEST

D_MODEL = 1024
CHUNK = 64
PAST_LEN = 2048
EPS = 1e-6
NEG_INF = -1e30
LOG2E = 1.4426950408889634

A_HEADS = 8
A_KV_HEADS = 2
A_HEAD_DIM = 64
A_WIDTH = A_HEADS * A_HEAD_DIM
A_KV_WIDTH = A_KV_HEADS * A_HEAD_DIM
A_REP = A_HEADS // A_KV_HEADS
WINDOW = 128

B_HEADS = 4
B_HEAD_DIM = 128
B_WIDTH = B_HEADS * B_HEAD_DIM
CONV_W = 4

POOL_SIZES = (2, 4, 8, 16)
C_WIDTH = D_MODEL
C_GROUP = C_WIDTH // len(POOL_SIZES)
POOL_HIST = max(POOL_SIZES) - 1

LANES = 128
SUBLANES = 8

Z_AQ = 0
Z_AK = Z_AQ + A_WIDTH
Z_AV = Z_AK + A_KV_WIDTH
Z_AG = Z_AV + A_KV_WIDTH
Z_BQKV = Z_AG + A_WIDTH
Z_BG = Z_BQKV + 3 * B_WIDTH
Z_BA = Z_BG + B_WIDTH
Z_WIDTH = Z_BA + LANES
GATE_LANE = B_HEADS


def _z_window(rows, width, col, row_block):
    return pl.BlockSpec((pl.Element(rows), pl.Element(width)), lambda *idx: (row_block(*idx) * rows, col))

VMEM_LIMIT = 48 * 1024 * 1024


def _sigmoid(x):
    return 1.0 / (1.0 + jnp.exp(-x))


def _silu(x):
    return x * _sigmoid(x)


def _softplus(x):
    return jnp.maximum(x, 0.0) + jnp.log(1.0 + jnp.exp(-jnp.abs(x)))


def _dot(a, b, precision=None):
    return jnp.dot(a, b, preferred_element_type=F32, precision=precision)


def _dot_nt(a, b, precision=None):
    return lax.dot_general(a, b, (((1,), (1,)), ((), ())), preferred_element_type=F32, precision=precision)


def _dot_tn(a, b, precision=None):
    return lax.dot_general(a, b, (((0,), (0,)), ((), ())), preferred_element_type=F32, precision=precision)


def _log2(n):
    assert n & (n - 1) == 0
    return n.bit_length() - 1


def _in_proj_kernel(x_ref, g_ref, w_ref, kn_ref, z_ref):
    x = x_ref[...]
    xn = x * lax.rsqrt(jnp.mean(x * x, axis=-1, keepdims=True) + EPS) * g_ref[...]
    n_in = w_ref.shape[1]
    z_ref[:, Z_BA:Z_WIDTH] = jnp.zeros((x.shape[0], Z_WIDTH - Z_BA), F32)
    z_ref[:, 0:n_in] = _dot(xn.astype(BF16), w_ref[...])
    k = z_ref[:, Z_AK:Z_AK + A_KV_WIDTH]
    sq = k * k
    lane = lax.broadcasted_iota(jnp.int32, k.shape, 1)
    first = lane < A_HEAD_DIM
    s0 = jnp.sum(jnp.where(first, sq, 0.0), axis=-1, keepdims=True)
    s1 = jnp.sum(jnp.where(first, 0.0, sq), axis=-1, keepdims=True)
    ms = jnp.where(first, s0, s1) * (1.0 / A_HEAD_DIM)
    z_ref[:, Z_AK:Z_AK + A_KV_WIDTH] = k * lax.rsqrt(ms + EPS) * kn_ref[...]


def _in_proj(x, g, w, kn, tm):
    n = x.shape[0]
    return pl.pallas_call(
        _in_proj_kernel,
        grid=(n // tm,),
        in_specs=[
            pl.BlockSpec((tm, D_MODEL), lambda i: (i, 0)),
            pl.BlockSpec((1, D_MODEL), lambda i: (0, 0)),
            pl.BlockSpec(w.shape, lambda i: (0, 0)),
            pl.BlockSpec((1, A_KV_WIDTH), lambda i: (0, 0)),
        ],
        out_specs=pl.BlockSpec((tm, Z_WIDTH), lambda i: (i, 0)),
        out_shape=jax.ShapeDtypeStruct((n, Z_WIDTH), F32),
        compiler_params=pltpu.CompilerParams(dimension_semantics=("arbitrary",), vmem_limit_bytes=VMEM_LIMIT),
        name="in_proj",
    )(x, g, w, kn)


A_HALVES = LANES // A_HEAD_DIM
A_QBLOCKS = A_WIDTH // LANES
A_COPIES = A_KV_HEADS


def _attn_half(r, c):
    g = r * A_HALVES // A_REP
    return (g + c) % A_HALVES


def _attn_tables(sinks_ref, bias_ref, *, cq, lk, front, fill):
    rows = A_QBLOCKS * cq
    nvar = front // cq + 1
    row = lax.broadcasted_iota(jnp.int32, (rows, 1), 0)
    blk = row >> _log2(cq)

    def per_row(value):
        out = []
        for c in range(A_COPIES):
            col = jnp.zeros((rows, 1), F32)
            for r in range(A_QBLOCKS):
                col = jnp.where(blk == r, value(r * A_HALVES + _attn_half(r, c)), col)
            out.append(col)
        return out

    @pl.when(fill)
    def _():
        col = lax.broadcasted_iota(jnp.int32, (rows, lk), 1)
        dist = jnp.abs((row & (cq - 1)) + WINDOW - col).astype(F32)
        slopes = per_row(lambda h: 2.0 ** (-8.0 * (h + 1) / A_HEADS))
        for c in range(A_COPIES):
            for var in range(nvar):
                bias_ref[var, c] = jnp.where(col >= front - var * cq, -slopes[c] * dist, NEG_INF) * LOG2E

    first = lax.broadcasted_iota(jnp.int32, (cq, LANES), 1) < A_HEAD_DIM
    sks = per_row(lambda h: sinks_ref[h] * LOG2E)
    return first, sks, nvar


def _attn_chunk(qc, kv, biases, sks, first, qn, gate, cq):
    blocks = []
    for r in range(A_QBLOCKS):
        x = qc[:, r * LANES:(r + 1) * LANES]
        sq = x * x
        s_lo = jnp.sum(jnp.where(first, sq, 0.0), axis=-1, keepdims=True)
        s_hi = jnp.sum(jnp.where(first, 0.0, sq), axis=-1, keepdims=True)
        ms = jnp.where(first, s_lo, s_hi) * (1.0 / A_HEAD_DIM)
        blocks.append(x * lax.rsqrt(ms + EPS) * qn)
    outs = []
    for c in range(A_COPIES):
        lhs = jnp.concatenate(
            [jnp.where(first if _attn_half(r, c) == 0 else jnp.logical_not(first), blocks[r], 0.0)
             for r in range(A_QBLOCKS)], axis=0)
        kk, vv = kv[c]
        sc = _dot_nt(lhs.astype(BF16), kk) + biases[c]
        m = jnp.maximum(jnp.max(sc, axis=-1, keepdims=True), sks[c])
        p = jnp.exp2(sc - m)
        den = jnp.sum(p, axis=-1, keepdims=True) + jnp.exp2(sks[c] - m)
        outs.append(_dot(p.astype(BF16), vv) * (1.0 / den))
    tile = []
    for r in range(A_QBLOCKS):
        rows = slice(r * cq, (r + 1) * cq)
        low = 0 if _attn_half(r, 0) == 0 else 1
        tile.append(jnp.where(first, outs[low][rows], outs[1 - low][rows]))
    return (jnp.concatenate(tile, axis=1) * _silu(gate)).astype(BF16)


def _attn_kernel(sinks_ref, q_ref, k_ref, v_ref, ag_ref, qn_ref, o_ref, kp_ref, vp_ref, bias_ref, *, bt, cq, lk, off, t):
    front = WINDOW - off
    lkt = k_ref.shape[0] // bt
    n_chunks = t // cq

    for b in range(bt):
        for src, dst in ((k_ref, kp_ref), (v_ref, vp_ref)):
            x = src[b * lkt:(b + 1) * lkt, :]
            if front:
                dst[:, b, 0:front, :] = jnp.zeros((A_COPIES, front, A_KV_WIDTH), BF16)
            dst[0, b, front:front + lkt, :] = x.astype(BF16)
            dst[1, b, front:front + lkt, :] = pltpu.roll(x, A_HEAD_DIM, 1).astype(BF16)

    first, sks, nvar = _attn_tables(sinks_ref, bias_ref, cq=cq, lk=lk, front=front, fill=pl.program_id(0) == 0)
    qn = qn_ref[...] * (A_HEAD_DIM ** -0.5 * LOG2E)

    def chunk(b, cg):
        k0 = cg * cq if isinstance(cg, int) else pl.multiple_of(cg * cq, cq)
        r0 = b * t + k0
        var = min(cg, nvar - 1) if isinstance(cg, int) else jnp.minimum(cg, nvar - 1)
        kv = [(kp_ref[c, b, pl.ds(k0, lk), :], vp_ref[c, b, pl.ds(k0, lk), :]) for c in range(A_COPIES)]
        o_ref[pl.ds(r0, cq), :] = _attn_chunk(
            q_ref[pl.ds(r0, cq), :], kv, [bias_ref[var, c] for c in range(A_COPIES)], sks, first, qn,
            ag_ref[pl.ds(r0, cq), :], cq)

    for b in range(bt):
        if n_chunks == 1:
            chunk(b, 0)
        else:
            lax.fori_loop(0, n_chunks, lambda cg, carry, b=b: (chunk(b, cg), carry)[1], 0, unroll=min(n_chunks, 8))


def _attention(z, kbuf, vbuf, kv_specs, sinks, qn, *, nb, bt, t, cq, lk, off):
    front = WINDOW - off
    lkt = kbuf.shape[0] // nb
    kern = functools.partial(_attn_kernel, bt=bt, cq=cq, lk=lk, off=off, t=t)
    return pl.pallas_call(
        kern,
        grid=(nb // bt,),
        in_specs=[
            pl.BlockSpec(memory_space=pltpu.SMEM),
            _z_window(bt * t, A_WIDTH, Z_AQ, lambda b: b),
            kv_specs[0],
            kv_specs[1],
            _z_window(bt * t, A_WIDTH, Z_AG, lambda b: b),
            pl.BlockSpec((1, LANES), lambda b: (0, 0)),
        ],
        out_specs=pl.BlockSpec((bt * t, A_WIDTH), lambda b: (b, 0)),
        out_shape=jax.ShapeDtypeStruct((nb * t, A_WIDTH), BF16),
        scratch_shapes=[
            pltpu.VMEM((A_COPIES, bt, front + lkt, A_KV_WIDTH), BF16),
            pltpu.VMEM((A_COPIES, bt, front + lkt, A_KV_WIDTH), BF16),
            pltpu.VMEM((front // cq + 1, A_COPIES, A_QBLOCKS * cq, lk), F32),
        ],
        compiler_params=pltpu.CompilerParams(dimension_semantics=("arbitrary",), vmem_limit_bytes=VMEM_LIMIT),
        name="swa_attention",
    )(sinks, z, kbuf, vbuf, z, qn)


def _bf(x):
    return x.astype(BF16)


def _unit_lower_inverse(a_list, c, ri, lj, expand):
    base = SUBLANES
    same = (ri >> _log2(base)) == (lj >> _log2(base))
    eye = jnp.where(ri == lj, 1.0, 0.0)
    n1 = [jnp.where(same, -a, 0.0) for a in a_list]
    x = [eye + n for n in n1]
    n1b = [_bf(n) for n in n1]
    n2b = [_bf(_dot(nb, expand(nb))) for nb in n1b]
    n2e = [expand(nb) for nb in n2b]
    x = [xi + _dot(_bf(xi), ne) for xi, ne in zip(x, n2e)]
    n4e = [expand(_bf(_dot(nb, ne))) for nb, ne in zip(n2b, n2e)]
    x = [xi + _dot(_bf(xi), ne) for xi, ne in zip(x, n4e)]
    s = base
    while s < c:
        sel = ((ri >> _log2(2 * s)) == (lj >> _log2(2 * s))) & ((ri >> _log2(s)) != (lj >> _log2(s)))
        xb = [_bf(xi) for xi in x]
        xo = [_dot(b, expand(_bf(jnp.where(sel, a, 0.0)))) for b, a in zip(xb, a_list)]
        x = [xi - _dot(_bf(o), expand(b)) for xi, o, b in zip(x, xo, xb)]
        s *= 2
    return x


def _delta_kernel(qkv_ref, ba_ref, bg_ref, hist_ref, s0_ref, convw_ref, gate_ref, onorm_ref,
                  o_ref, sout_ref, ext_ref, st_ref, lhs_ref, add_ref, s_ref, *, bt, c, tc):
    i = pl.program_id(1)
    hd = B_HEAD_DIM
    nh = B_HEADS
    r = nh * c
    nch = tc // c
    pad = SUBLANES
    units = [(b, cc) for b in range(bt) for cc in range(nch)]

    @pl.when(i == 0)
    def _():
        ext_ref[:, 0:pad, :] = hist_ref[...]
        s_ref[...] = s0_ref[...]

    @pl.when(i > 0)
    def _():
        ext_ref[:, 0:pad, :] = ext_ref[:, tc:tc + pad, :]

    for b in range(bt):
        ext_ref[b, pad:pad + tc, :] = qkv_ref[b * tc:(b + 1) * tc, :]

    for u, (b, cc) in enumerate(units):
        e = ext_ref[b, cc * c:cc * c + pad + c, :]
        acc = e * convw_ref[0:1, :]
        for j in range(1, CONV_W):
            acc = pltpu.roll(acc, 1, 0) + e * convw_ref[j:j + 1, :]
        y = _silu(acc[pad:, :])
        for part in range(3):
            for h in range(nh):
                blk = y[:, (part * nh + h) * hd:(part * nh + h + 1) * hd]
                if part < 2:
                    blk = blk * lax.rsqrt(jnp.sum(blk * blk, axis=-1, keepdims=True) + EPS)
                st_ref[part, u, h * c:(h + 1) * c, :] = blk

    ri = lax.broadcasted_iota(jnp.int32, (c, r), 0)
    li = lax.broadcasted_iota(jnp.int32, (c, r), 1)
    lj = li & (c - 1)
    lh = li >> _log2(c)
    lower = ri >= lj
    strict = ri > lj
    head_sel = [jnp.where(lh == h, 1.0, 0.0).astype(BF16) for h in range(nh)]

    def expand(xb):
        return jnp.concatenate([xb * m for m in head_sel], axis=0)

    def lanes(x_st):
        return jnp.concatenate([x_st[h * c:(h + 1) * c] for h in range(nh)], axis=1)

    def head_blocks(x_st):
        zero = jnp.zeros((c, hd), x_st.dtype)
        return jnp.concatenate(
            [jnp.concatenate([x_st[h * c:(h + 1) * c] if h2 == h else zero for h2 in range(nh)], axis=1)
             for h in range(nh)], axis=0)

    ci = lax.broadcasted_iota(jnp.int32, (c, c), 0)
    cj = lax.broadcasted_iota(jnp.int32, (c, c), 1)
    tril = jnp.where(ci >= cj, 1.0, 0.0)
    di = lax.broadcasted_iota(jnp.int32, (hd, hd), 0)
    dj = lax.broadcasted_iota(jnp.int32, (hd, hd), 1)
    eye_hd = di == dj
    neg_rate = -jnp.exp(gate_ref[0:1, :])
    dt_bias = gate_ref[1:2, :]
    onorm = onorm_ref[...]

    def stack_cols(x, lane0):
        return jnp.concatenate([x[:, lane0 + h:lane0 + h + 1] for h in range(nh)], axis=0)

    beta, gc, gcc, gr, glast = [], [], [], [], []
    for b, cc in units:
        ba = ba_ref[b * tc + cc * c:b * tc + (cc + 1) * c, :]
        g_all = neg_rate * _softplus(ba + dt_bias)
        gcum = _dot(tril, g_all, HIGHEST)
        gcum_t = gcum.T
        beta.append(stack_cols(_sigmoid(ba), 0))
        gc.append(stack_cols(gcum, GATE_LANE))
        gcc_h = jnp.zeros((c, r), F32)
        for h in range(nh):
            gcc_h = jnp.where(lh == h, gcum[:, GATE_LANE + h:GATE_LANE + h + 1], gcc_h)
        gcc.append(gcc_h)
        gr.append(jnp.concatenate([gcum_t[GATE_LANE + h:GATE_LANE + h + 1, :] for h in range(nh)], axis=1))
        glast.append(jnp.concatenate(
            [jnp.broadcast_to(gcum[c - 1:c, GATE_LANE + h:GATE_LANE + h + 1], (c, 1)) for h in range(nh)], axis=0))

    def fold_units(us):
        n = range(len(us))
        q = [st_ref[0, u] * (hd ** -0.5) for u in us]
        k = [st_ref[1, u] for u in us]
        v = [st_ref[2, u] for u in us]
        kb = [k[j] * beta[u] for j, u in enumerate(us)]
        big = [_dot_nt(_bf(jnp.concatenate([lanes(kb[j]), lanes(q[j])], axis=0)), head_blocks(_bf(k[j]))) for j in n]
        decay = [jnp.where(lower, jnp.exp(jnp.where(lower, gcc[u] - gr[u], 0.0)), 0.0) for u in us]
        a = [jnp.where(strict, big[j][:c] * decay[j], 0.0) for j in n]
        qkd = [expand(_bf(big[j][c:] * decay[j])) for j in n]
        tinv = _unit_lower_inverse(a, c, ri, lj, expand)
        eg = [jnp.exp(gc[u]) for u in us]
        rhs = [_bf(jnp.concatenate([v[j] * beta[u], kb[j] * eg[j]], axis=1)) for j, u in enumerate(us)]
        sol = [_dot(expand(_bf(tinv[j])), rhs[j]) for j in n]
        solb = [_bf(s) for s in sol]
        fold = [_dot(qkd[j], solb[j]) for j in n]
        k_dec = [_bf(k[j] * jnp.exp(glast[u] - gc[u])) for j, u in enumerate(us)]
        for j, u in enumerate(us):
            qp = q[j] * eg[j] - fold[j][:, hd:]
            for h in range(nh):
                rows = slice(h * c, (h + 1) * c)
                kt = _dot_tn(k_dec[j][rows], solb[j][rows])
                g_tot = jnp.exp(glast[u][h * c:h * c + 1, :])
                lhs_ref[u, h, 0:hd, :] = _bf(jnp.where(eye_hd, g_tot, 0.0) - kt[:, hd:])
                lhs_ref[u, h, hd:hd + c, :] = _bf(qp[rows])
                add_ref[u, h, 0:hd, :] = kt[:, :hd]
                add_ref[u, h, hd:hd + c, :] = fold[j][rows, :hd]

    fold_units(list(range(len(units))))

    for u, (b, cc) in enumerate(units):
        rows = slice(b * tc + cc * c, b * tc + (cc + 1) * c)
        for h in range(nh):
            res = _dot(lhs_ref[u, h], _bf(s_ref[b, h])) + add_ref[u, h]
            s_ref[b, h] = res[:hd]
            o = res[hd:]
            on = o * lax.rsqrt(jnp.mean(o * o, axis=-1, keepdims=True) + EPS) * onorm
            bg = bg_ref[rows, h * hd:(h + 1) * hd]
            o_ref[rows, h * hd:(h + 1) * hd] = (on * _silu(bg)).astype(BF16)
    sout_ref[...] = s_ref[...]


def _delta(z, hist, s0, convw, gate, onorm, *, nb, bt, t, tc, c):
    nt = t // tc
    units = bt * (tc // c)
    state = (bt, B_HEADS, B_HEAD_DIM, B_HEAD_DIM)
    kern = functools.partial(_delta_kernel, bt=bt, c=c, tc=tc)
    return pl.pallas_call(
        kern,
        grid=(nb // bt, nt),
        in_specs=[
            _z_window(bt * tc, 3 * B_WIDTH, Z_BQKV, lambda b, i: b * nt + i),
            _z_window(bt * tc, LANES, Z_BA, lambda b, i: b * nt + i),
            _z_window(bt * tc, B_WIDTH, Z_BG, lambda b, i: b * nt + i),
            pl.BlockSpec((bt, SUBLANES, 3 * B_WIDTH), lambda b, i: (b, 0, 0)),
            pl.BlockSpec(state, lambda b, i: (b, 0, 0, 0)),
            pl.BlockSpec((CONV_W, 3 * B_WIDTH), lambda b, i: (0, 0)),
            pl.BlockSpec((2, LANES), lambda b, i: (0, 0)),
            pl.BlockSpec((1, B_HEAD_DIM), lambda b, i: (0, 0)),
        ],
        out_specs=[
            pl.BlockSpec((bt * tc, B_WIDTH), lambda b, i: (b * nt + i, 0)),
            pl.BlockSpec(state, lambda b, i: (b, 0, 0, 0)),
        ],
        out_shape=[
            jax.ShapeDtypeStruct((nb * t, B_WIDTH), BF16),
            jax.ShapeDtypeStruct((nb, B_HEADS, B_HEAD_DIM, B_HEAD_DIM), F32),
        ],
        scratch_shapes=[
            pltpu.VMEM((bt, SUBLANES + tc, 3 * B_WIDTH), F32),
            pltpu.VMEM((3, units, B_HEADS * c, B_HEAD_DIM), F32),
            pltpu.VMEM((units, B_HEADS, B_HEAD_DIM + c, B_HEAD_DIM), BF16),
            pltpu.VMEM((units, B_HEADS, B_HEAD_DIM + c, B_HEAD_DIM), F32),
            pltpu.VMEM(state, F32),
        ],
        compiler_params=pltpu.CompilerParams(dimension_semantics=("arbitrary", "arbitrary"),
                                             vmem_limit_bytes=VMEM_LIMIT),
        name="gated_delta",
    )(z, z, z, hist, s0, convw, gate, onorm)


def _pool_kernel(h_ref, oa_ref, ob_ref, wa_ref, wb_ref, hist_ref, g_ref, win_ref, wgrp_ref, scale_ref, wout_ref,
                 y_ref, tail_ref, ext_ref, *, bt, tt, pos0):
    i = pl.program_id(1)
    pad = POOL_HIST + 1
    rows = bt * tt

    @pl.when(i == 0)
    def _():
        pos = pos0 - pad + lax.broadcasted_iota(jnp.int32, (1, pad, 1), 1)
        ext_ref[:, 0:pad, :] = jnp.where(pos >= 0, hist_ref[...], 0.0)

    @pl.when(i > 0)
    def _():
        ext_ref[:, 0:pad, :] = ext_ref[:, tt:tt + pad, :]

    x = h_ref[...] + _dot(oa_ref[...], wa_ref[...]) + _dot(ob_ref[...], wb_ref[...])
    xn = x * lax.rsqrt(jnp.mean(x * x, axis=-1, keepdims=True) + EPS) * g_ref[...]
    z = _dot(xn.astype(BF16), win_ref[...])
    u = z[:, :C_WIDTH]
    gate = z[:, C_WIDTH:]
    ext_ref[:, pad:pad + tt, :] = u.reshape(bt, tt, C_WIDTH)
    tail_ref[...] = ext_ref[:, tt:tt + pad, :]

    tpos = pos0 + i * tt + (lax.broadcasted_iota(jnp.int32, (rows, 1), 0) & (tt - 1))
    mixed = []
    for gi, w in enumerate(POOL_SIZES):
        cols = slice(gi * C_GROUP, (gi + 1) * C_GROUP)
        s = ext_ref[:, :, cols].reshape(bt * (pad + tt), C_GROUP)
        sh = 1
        while sh < w:
            s = s + pltpu.roll(s, sh, 0)
            sh *= 2
        s = s.reshape(bt, pad + tt, C_GROUP)[:, pad:, :].reshape(rows, C_GROUP)
        cnt = jnp.minimum(tpos + 1, w).astype(F32)
        pooled = s / cnt - u[:, cols]
        m = _dot(pooled.astype(BF16), wgrp_ref[gi]) * scale_ref[:, cols]
        mixed.append((m * _silu(gate[:, cols])).astype(BF16))
    y_ref[...] = x + _dot(jnp.concatenate(mixed, axis=1), wout_ref[...])


def _pool_layer(h, oa, ob, wa, wb, hist, g, win, wgrp, scale, wout, *, nb, bt, t, tt, pos0):
    nt = t // tt
    pad = POOL_HIST + 1
    rows = bt * tt
    kern = functools.partial(_pool_kernel, bt=bt, tt=tt, pos0=pos0)
    tile = lambda b, i: (b * nt + i, 0)
    const2 = lambda b, i: (0, 0)
    return pl.pallas_call(
        kern,
        grid=(nb // bt, nt),
        in_specs=[
            pl.BlockSpec((rows, D_MODEL), tile),
            pl.BlockSpec((rows, A_WIDTH), tile),
            pl.BlockSpec((rows, B_WIDTH), tile),
            pl.BlockSpec((A_WIDTH, D_MODEL), const2),
            pl.BlockSpec((B_WIDTH, D_MODEL), const2),
            pl.BlockSpec((bt, pad, C_WIDTH), lambda b, i: (b, 0, 0)),
            pl.BlockSpec((1, D_MODEL), const2),
            pl.BlockSpec((D_MODEL, 2 * C_WIDTH), const2),
            pl.BlockSpec((len(POOL_SIZES), C_GROUP, C_GROUP), lambda b, i: (0, 0, 0)),
            pl.BlockSpec((1, C_WIDTH), const2),
            pl.BlockSpec((C_WIDTH, D_MODEL), const2),
        ],
        out_specs=[
            pl.BlockSpec((rows, D_MODEL), tile),
            pl.BlockSpec((bt, pad, C_WIDTH), lambda b, i: (b, 0, 0)),
        ],
        out_shape=[
            jax.ShapeDtypeStruct((nb * t, D_MODEL), F32),
            jax.ShapeDtypeStruct((nb, pad, C_WIDTH), F32),
        ],
        scratch_shapes=[pltpu.VMEM((bt, pad + tt, C_WIDTH), F32)],
        compiler_params=pltpu.CompilerParams(dimension_semantics=("arbitrary", "arbitrary"),
                                             vmem_limit_bytes=VMEM_LIMIT),
        name="out_proj_pool_layer",
    )(h, oa, ob, wa, wb, hist, g, win, wgrp, scale, wout)


def _ab_weights(norm_g, w_in, q_norm, k_norm, sinks, conv_w, a_log, dt_bias, o_norm, w_out):
    w = w_in.astype(BF16)
    lane_pad = (GATE_LANE, LANES - GATE_LANE - B_HEADS)
    gate = jnp.stack([jnp.pad(a_log.astype(F32), lane_pad), jnp.pad(dt_bias.astype(F32), lane_pad)])
    return dict(
        norm_g=norm_g.reshape(1, D_MODEL), w=w, kn=jnp.tile(k_norm, A_KV_HEADS).reshape(1, A_KV_WIDTH),
        qn=jnp.tile(q_norm, A_KV_HEADS).reshape(1, LANES), sinks=sinks.astype(F32), conv_w=conv_w, gate=gate,
        onorm=o_norm.reshape(1, B_HEAD_DIM), wa=w_out[:A_WIDTH].astype(BF16), wb=w_out[A_WIDTH:].astype(BF16))


def _group_tiles(t):
    if t >= CHUNK:
        return dict(proj_rows=512, seqs_attn=1, seqs_delta=1, delta_tokens=512, chunk=CHUNK, seqs_pool=1, pool_tokens=512)
    return dict(proj_rows=256, seqs_attn=32, seqs_delta=16, delta_tokens=t, chunk=t, seqs_pool=256 // t, pool_tokens=t)


def _ab_mixers(h, wts, cache_k, cache_v, s0, conv_hist, *, nb, t):
    tl = _group_tiles(t)
    z = _in_proj(h, wts["norm_g"], wts["w"], wts["kn"], min(nb * t, tl["proj_rows"]))
    bt = tl["seqs_attn"]
    if cache_k is None:
        cq, lk, off = CHUNK, WINDOW + CHUNK, 0
        kv_specs = [_z_window(bt * t, A_KV_WIDTH, Z_AK, lambda b: b), _z_window(bt * t, A_KV_WIDTH, Z_AV, lambda b: b)]
        kbuf = vbuf = z
    else:
        cq, lk, off = t, WINDOW + t, WINDOW
        k_new = z[:, Z_AK:Z_AK + A_KV_WIDTH].reshape(nb, t, A_KV_WIDTH)
        v_new = z[:, Z_AV:Z_AV + A_KV_WIDTH].reshape(nb, t, A_KV_WIDTH)
        kbuf = jnp.concatenate([cache_k.reshape(nb, WINDOW, A_KV_WIDTH), k_new], axis=1).reshape(nb * lk, A_KV_WIDTH)
        vbuf = jnp.concatenate([cache_v.reshape(nb, WINDOW, A_KV_WIDTH), v_new], axis=1).reshape(nb * lk, A_KV_WIDTH)
        kv_specs = [pl.BlockSpec((bt * lk, A_KV_WIDTH), lambda b: (b, 0))] * 2
    o_a = _attention(z, kbuf, vbuf, kv_specs, wts["sinks"], wts["qn"], nb=nb, bt=bt, t=t, cq=cq, lk=lk, off=off)
    hist = jnp.pad(conv_hist.astype(F32), ((0, 0), (SUBLANES - (CONV_W - 1), 0), (0, 0)))
    o_b, s_new = _delta(z, hist, s0.astype(F32), wts["conv_w"], wts["gate"], wts["onorm"], nb=nb,
                        bt=tl["seqs_delta"], t=t, tc=tl["delta_tokens"], c=tl["chunk"])
    return o_a, o_b, z, s_new


def _last_rows(z, col, width, nb, t, rows):
    return z.reshape(nb, t, Z_WIDTH)[:, t - rows:, col:col + width]


def _cache_rows(z, col, nb, t, rows):
    return _last_rows(z, col, A_KV_WIDTH, nb, t, rows).reshape(nb, rows, A_KV_HEADS, A_HEAD_DIM)


def kernel(x_prompt, x_sample, cache_a_k, cache_a_v, state_b_s, state_b_conv, state_c_pool,
           norm_ab, w_in_ab, q_norm_a, k_norm_a, sinks_a, conv_b, a_log_b, dt_bias_b, o_norm_b, w_out_ab,
           norm_c, w_in_c, w_grp_c, scale_c, w_out_c):
    bp, tp, _ = x_prompt.shape
    bs, ts, _ = x_sample.shape
    hp = x_prompt.reshape(bp * tp, D_MODEL)
    hs = x_sample.reshape(bs * ts, D_MODEL)

    wts = _ab_weights(norm_ab[0], w_in_ab[0], q_norm_a[0], k_norm_a[0], sinks_a[0], conv_b[0], a_log_b[0],
                      dt_bias_b[0], o_norm_b[0], w_out_ab[0])
    s0 = jnp.zeros((bp, B_HEADS, B_HEAD_DIM, B_HEAD_DIM), F32)
    c0 = jnp.zeros((bp, CONV_W - 1, 3 * B_WIDTH), F32)
    oap, obp, zp, sp = _ab_mixers(hp, wts, None, None, s0, c0, nb=bp, t=tp)
    oas, obs, zs, ss = _ab_mixers(hs, wts, cache_a_k[0], cache_a_v[0], state_b_s[0], state_b_conv[0], nb=bs, t=ts)
    p_a_k = _cache_rows(zp, Z_AK, bp, tp, WINDOW)[None]
    p_a_v = _cache_rows(zp, Z_AV, bp, tp, WINDOW)[None]
    s_a_k = _cache_rows(zs, Z_AK, bs, ts, ts)[None]
    s_a_v = _cache_rows(zs, Z_AV, bs, ts, ts)[None]
    p_b_conv = _last_rows(zp, Z_BQKV, 3 * B_WIDTH, bp, tp, CONV_W - 1)[None]
    s_b_conv = _last_rows(zs, Z_BQKV, 3 * B_WIDTH, bs, ts, CONV_W - 1)[None]

    g_c = norm_c[0].reshape(1, D_MODEL)
    win = w_in_c[0].astype(BF16)
    wgrp = w_grp_c[0].astype(BF16)
    scale = scale_c[0].reshape(1, C_WIDTH)
    wout = w_out_c[0].astype(BF16)
    h0 = jnp.zeros((bp, POOL_HIST + 1, C_WIDTH), F32)
    hs0 = jnp.pad(state_c_pool[0].astype(F32), ((0, 0), (1, 0), (0, 0)))
    tlp, tls = _group_tiles(tp), _group_tiles(ts)
    yp, tail_p = _pool_layer(hp, oap, obp, wts["wa"], wts["wb"], h0, g_c, win, wgrp, scale, wout, nb=bp,
                             bt=tlp["seqs_pool"], t=tp, tt=tlp["pool_tokens"], pos0=0)
    ys, tail_s = _pool_layer(hs, oas, obs, wts["wa"], wts["wb"], hs0, g_c, win, wgrp, scale, wout, nb=bs,
                             bt=tls["seqs_pool"], t=ts, tt=tls["pool_tokens"], pos0=PAST_LEN)

    return (yp.reshape(bp, tp, D_MODEL), ys.reshape(bs, ts, D_MODEL),
            p_a_k, p_a_v, sp[None], p_b_conv, tail_p[:, 1:][None],
            s_a_k, s_a_v, ss[None], s_b_conv, tail_s[:, 1:][None])
```

```python
import functools

import jax
import jax.numpy as jnp
from jax import lax
from jax.experimental import pallas as pl
from jax.experimental.pallas import tpu as pltpu

F32 = jnp.float32
BF16 = jnp.bfloat16
HIGHEST = lax.Precision.HIGHEST

D_MODEL = 1024
CHUNK = 64
PAST_LEN = 2048
EPS = 1e-6
NEG_INF = -1e30
LOG2E = 1.4426950408889634

A_HEADS = 8
A_KV_HEADS = 2
A_HEAD_DIM = 64
A_WIDTH = A_HEADS * A_HEAD_DIM
A_KV_WIDTH = A_KV_HEADS * A_HEAD_DIM
A_REP = A_HEADS // A_KV_HEADS
WINDOW = 128

B_HEADS = 4
B_HEAD_DIM = 128
B_WIDTH = B_HEADS * B_HEAD_DIM
CONV_W = 4

POOL_SIZES = (2, 4, 8, 16)
C_WIDTH = D_MODEL
C_GROUP = C_WIDTH // len(POOL_SIZES)
POOL_HIST = max(POOL_SIZES) - 1

LANES = 128
SUBLANES = 8

Z_AQ = 0
Z_AK = Z_AQ + A_WIDTH
Z_AV = Z_AK + A_KV_WIDTH
Z_AG = Z_AV + A_KV_WIDTH
Z_BQKV = Z_AG + A_WIDTH
Z_BG = Z_BQKV + 3 * B_WIDTH
Z_BA = Z_BG + B_WIDTH
Z_WIDTH = Z_BA + LANES
GATE_LANE = B_HEADS


def _z_window(rows, width, col, row_block):
    return pl.BlockSpec((pl.Element(rows), pl.Element(width)), lambda *idx: (row_block(*idx) * rows, col))

VMEM_LIMIT = 48 * 1024 * 1024


def _sigmoid(x):
    return 1.0 / (1.0 + jnp.exp(-x))


def _silu(x):
    return x * _sigmoid(x)


def _softplus(x):
    return jnp.maximum(x, 0.0) + jnp.log(1.0 + jnp.exp(-jnp.abs(x)))


def _dot(a, b, precision=None):
    return jnp.dot(a, b, preferred_element_type=F32, precision=precision)


def _dot_nt(a, b, precision=None):
    return lax.dot_general(a, b, (((1,), (1,)), ((), ())), preferred_element_type=F32, precision=precision)


def _dot_tn(a, b, precision=None):
    return lax.dot_general(a, b, (((0,), (0,)), ((), ())), preferred_element_type=F32, precision=precision)


def _log2(n):
    assert n & (n - 1) == 0
    return n.bit_length() - 1


def _in_proj_kernel(x_ref, g_ref, w_ref, kn_ref, z_ref):
    x = x_ref[...]
    xn = x * lax.rsqrt(jnp.mean(x * x, axis=-1, keepdims=True) + EPS) * g_ref[...]
    n_in = w_ref.shape[1]
    z_ref[:, Z_BA:Z_WIDTH] = jnp.zeros((x.shape[0], Z_WIDTH - Z_BA), F32)
    z_ref[:, 0:n_in] = _dot(xn.astype(BF16), w_ref[...])
    k = z_ref[:, Z_AK:Z_AK + A_KV_WIDTH]
    sq = k * k
    lane = lax.broadcasted_iota(jnp.int32, k.shape, 1)
    first = lane < A_HEAD_DIM
    s0 = jnp.sum(jnp.where(first, sq, 0.0), axis=-1, keepdims=True)
    s1 = jnp.sum(jnp.where(first, 0.0, sq), axis=-1, keepdims=True)
    ms = jnp.where(first, s0, s1) * (1.0 / A_HEAD_DIM)
    z_ref[:, Z_AK:Z_AK + A_KV_WIDTH] = k * lax.rsqrt(ms + EPS) * kn_ref[...]


def _in_proj(x, g, w, kn, tm):
    n = x.shape[0]
    return pl.pallas_call(
        _in_proj_kernel,
        grid=(n // tm,),
        in_specs=[
            pl.BlockSpec((tm, D_MODEL), lambda i: (i, 0)),
            pl.BlockSpec((1, D_MODEL), lambda i: (0, 0)),
            pl.BlockSpec(w.shape, lambda i: (0, 0), pipeline_mode=pl.Buffered(1)),
            pl.BlockSpec((1, A_KV_WIDTH), lambda i: (0, 0)),
        ],
        out_specs=pl.BlockSpec((tm, Z_WIDTH), lambda i: (i, 0)),
        out_shape=jax.ShapeDtypeStruct((n, Z_WIDTH), F32),
        compiler_params=pltpu.CompilerParams(dimension_semantics=("arbitrary",), vmem_limit_bytes=VMEM_LIMIT),
        name="in_proj",
    )(x, g, w, kn)


A_HALVES = LANES // A_HEAD_DIM
A_QBLOCKS = A_WIDTH // LANES
A_COPIES = A_KV_HEADS


def _attn_half(r, c):
    g = r * A_HALVES // A_REP
    return (g + c) % A_HALVES


def _attn_tables(sinks_ref, bias_ref, *, cq, lk, front, fill):
    rows = A_QBLOCKS * cq
    nvar = front // cq + 1
    row = lax.broadcasted_iota(jnp.int32, (rows, 1), 0)
    blk = row >> _log2(cq)

    def per_row(value):
        out = []
        for c in range(A_COPIES):
            col = jnp.zeros((rows, 1), F32)
            for r in range(A_QBLOCKS):
                col = jnp.where(blk == r, value(r * A_HALVES + _attn_half(r, c)), col)
            out.append(col)
        return out

    @pl.when(fill)
    def _():
        col = lax.broadcasted_iota(jnp.int32, (rows, lk), 1)
        dist = jnp.abs((row & (cq - 1)) + WINDOW - col).astype(F32)
        slopes = per_row(lambda h: 2.0 ** (-8.0 * (h + 1) / A_HEADS))
        for c in range(A_COPIES):
            for var in range(nvar):
                bias_ref[var, c] = jnp.where(col >= front - var * cq, -slopes[c] * dist, NEG_INF) * LOG2E

    first = lax.broadcasted_iota(jnp.int32, (cq, LANES), 1) < A_HEAD_DIM
    sks = per_row(lambda h: sinks_ref[h] * LOG2E)
    return first, sks, nvar


def _attn_chunk(qc, kv, biases, sks, first, qn, gate, cq):
    blocks = []
    for r in range(A_QBLOCKS):
        x = qc[:, r * LANES:(r + 1) * LANES]
        sq = x * x
        s_lo = jnp.sum(jnp.where(first, sq, 0.0), axis=-1, keepdims=True)
        s_hi = jnp.sum(jnp.where(first, 0.0, sq), axis=-1, keepdims=True)
        ms = jnp.where(first, s_lo, s_hi) * (1.0 / A_HEAD_DIM)
        blocks.append(x * lax.rsqrt(ms + EPS) * qn)
    outs = []
    for c in range(A_COPIES):
        lhs = jnp.concatenate(
            [jnp.where(first if _attn_half(r, c) == 0 else jnp.logical_not(first), blocks[r], 0.0)
             for r in range(A_QBLOCKS)], axis=0)
        kk, vv = kv[c]
        sc = _dot_nt(lhs.astype(BF16), kk) + biases[c]
        m = jnp.maximum(jnp.max(sc, axis=-1, keepdims=True), sks[c])
        p = jnp.exp2(sc - m)
        den = jnp.sum(p, axis=-1, keepdims=True) + jnp.exp2(sks[c] - m)
        outs.append(_dot(p.astype(BF16), vv) * (1.0 / den))
    tile = []
    for r in range(A_QBLOCKS):
        rows = slice(r * cq, (r + 1) * cq)
        low = 0 if _attn_half(r, 0) == 0 else 1
        tile.append(jnp.where(first, outs[low][rows], outs[1 - low][rows]))
    return (jnp.concatenate(tile, axis=1) * _silu(gate)).astype(BF16)


def _attn_kernel(sinks_ref, q_ref, k_ref, v_ref, ag_ref, qn_ref, o_ref, kp_ref, vp_ref, bias_ref, *, bt, cq, lk, off, t):
    front = WINDOW - off
    lkt = k_ref.shape[0] // bt
    n_chunks = t // cq

    for b in range(bt):
        for src, dst in ((k_ref, kp_ref), (v_ref, vp_ref)):
            x = src[b * lkt:(b + 1) * lkt, :]
            if front:
                dst[:, b, 0:front, :] = jnp.zeros((A_COPIES, front, A_KV_WIDTH), BF16)
            dst[0, b, front:front + lkt, :] = x.astype(BF16)
            dst[1, b, front:front + lkt, :] = pltpu.roll(x, A_HEAD_DIM, 1).astype(BF16)

    first, sks, nvar = _attn_tables(sinks_ref, bias_ref, cq=cq, lk=lk, front=front, fill=pl.program_id(0) == 0)
    qn = qn_ref[...] * (A_HEAD_DIM ** -0.5 * LOG2E)

    def chunk(b, cg):
        k0 = cg * cq if isinstance(cg, int) else pl.multiple_of(cg * cq, cq)
        r0 = b * t + k0
        var = min(cg, nvar - 1) if isinstance(cg, int) else jnp.minimum(cg, nvar - 1)
        kv = [(kp_ref[c, b, pl.ds(k0, lk), :], vp_ref[c, b, pl.ds(k0, lk), :]) for c in range(A_COPIES)]
        o_ref[pl.ds(r0, cq), :] = _attn_chunk(
            q_ref[pl.ds(r0, cq), :], kv, [bias_ref[var, c] for c in range(A_COPIES)], sks, first, qn,
            ag_ref[pl.ds(r0, cq), :], cq)

    for b in range(bt):
        if n_chunks == 1:
            chunk(b, 0)
        else:
            lax.fori_loop(0, n_chunks, lambda cg, carry, b=b: (chunk(b, cg), carry)[1], 0, unroll=min(n_chunks, 8))


def _attention(z, kbuf, vbuf, kv_specs, sinks, qn, *, nb, bt, t, cq, lk, off):
    front = WINDOW - off
    lkt = kbuf.shape[0] // nb
    kern = functools.partial(_attn_kernel, bt=bt, cq=cq, lk=lk, off=off, t=t)
    return pl.pallas_call(
        kern,
        grid=(nb // bt,),
        in_specs=[
            pl.BlockSpec(memory_space=pltpu.SMEM),
            _z_window(bt * t, A_WIDTH, Z_AQ, lambda b: b),
            kv_specs[0],
            kv_specs[1],
            _z_window(bt * t, A_WIDTH, Z_AG, lambda b: b),
            pl.BlockSpec((1, LANES), lambda b: (0, 0)),
        ],
        out_specs=pl.BlockSpec((bt * t, A_WIDTH), lambda b: (b, 0)),
        out_shape=jax.ShapeDtypeStruct((nb * t, A_WIDTH), BF16),
        scratch_shapes=[
            pltpu.VMEM((A_COPIES, bt, front + lkt, A_KV_WIDTH), BF16),
            pltpu.VMEM((A_COPIES, bt, front + lkt, A_KV_WIDTH), BF16),
            pltpu.VMEM((front // cq + 1, A_COPIES, A_QBLOCKS * cq, lk), F32),
        ],
        compiler_params=pltpu.CompilerParams(dimension_semantics=("arbitrary",), vmem_limit_bytes=VMEM_LIMIT),
        name="swa_attention",
    )(sinks, z, kbuf, vbuf, z, qn)


def _bf(x):
    return x.astype(BF16)


def _unit_lower_inverse(a_list, c, ri, lj, expand):
    base = SUBLANES
    same = (ri >> _log2(base)) == (lj >> _log2(base))
    eye = jnp.where(ri == lj, 1.0, 0.0)
    n1 = [jnp.where(same, -a, 0.0) for a in a_list]
    x = [eye + n for n in n1]
    n1b = [_bf(n) for n in n1]
    n2b = [_bf(_dot(nb, expand(nb))) for nb in n1b]
    n2e = [expand(nb) for nb in n2b]
    x = [xi + _dot(_bf(xi), ne) for xi, ne in zip(x, n2e)]
    n4e = [expand(_bf(_dot(nb, ne))) for nb, ne in zip(n2b, n2e)]
    x = [xi + _dot(_bf(xi), ne) for xi, ne in zip(x, n4e)]
    s = base
    while s < c:
        sel = ((ri >> _log2(2 * s)) == (lj >> _log2(2 * s))) & ((ri >> _log2(s)) != (lj >> _log2(s)))
        xb = [_bf(xi) for xi in x]
        xo = [_dot(b, expand(_bf(jnp.where(sel, a, 0.0)))) for b, a in zip(xb, a_list)]
        x = [xi - _dot(_bf(o), expand(b)) for xi, o, b in zip(x, xo, xb)]
        s *= 2
    return x


def _delta_kernel(qkv_ref, ba_ref, bg_ref, hist_ref, s0_ref, convw_ref, gate_ref, onorm_ref,
                  o_ref, sout_ref, ext_ref, st_ref, lhs_ref, add_ref, s_ref, *, bt, c, tc):
    i = pl.program_id(1)
    hd = B_HEAD_DIM
    nh = B_HEADS
    r = nh * c
    nch = tc // c
    pad = SUBLANES
    units = [(b, cc) for b in range(bt) for cc in range(nch)]

    @pl.when(i == 0)
    def _():
        ext_ref[:, 0:pad, :] = hist_ref[...]
        s_ref[...] = s0_ref[...]

    @pl.when(i > 0)
    def _():
        ext_ref[:, 0:pad, :] = ext_ref[:, tc:tc + pad, :]

    for b in range(bt):
        ext_ref[b, pad:pad + tc, :] = qkv_ref[b * tc:(b + 1) * tc, :]

    for u, (b, cc) in enumerate(units):
        e = ext_ref[b, cc * c:cc * c + pad + c, :]
        acc = e * convw_ref[0:1, :]
        for j in range(1, CONV_W):
            acc = pltpu.roll(acc, 1, 0) + e * convw_ref[j:j + 1, :]
        y = _silu(acc[pad:, :])
        for part in range(3):
            for h in range(nh):
                blk = y[:, (part * nh + h) * hd:(part * nh + h + 1) * hd]
                if part < 2:
                    blk = blk * lax.rsqrt(jnp.sum(blk * blk, axis=-1, keepdims=True) + EPS)
                st_ref[part, u, h * c:(h + 1) * c, :] = blk

    ri = lax.broadcasted_iota(jnp.int32, (c, r), 0)
    li = lax.broadcasted_iota(jnp.int32, (c, r), 1)
    lj = li & (c - 1)
    lh = li >> _log2(c)
    lower = ri >= lj
    strict = ri > lj
    head_sel = [jnp.where(lh == h, 1.0, 0.0).astype(BF16) for h in range(nh)]

    def expand(xb):
        return jnp.concatenate([xb * m for m in head_sel], axis=0)

    def lanes(x_st):
        return jnp.concatenate([x_st[h * c:(h + 1) * c] for h in range(nh)], axis=1)

    def head_blocks(x_st):
        zero = jnp.zeros((c, hd), x_st.dtype)
        return jnp.concatenate(
            [jnp.concatenate([x_st[h * c:(h + 1) * c] if h2 == h else zero for h2 in range(nh)], axis=1)
             for h in range(nh)], axis=0)

    ci = lax.broadcasted_iota(jnp.int32, (c, c), 0)
    cj = lax.broadcasted_iota(jnp.int32, (c, c), 1)
    tril = jnp.where(ci >= cj, 1.0, 0.0)
    di = lax.broadcasted_iota(jnp.int32, (hd, hd), 0)
    dj = lax.broadcasted_iota(jnp.int32, (hd, hd), 1)
    eye_hd = di == dj
    neg_rate = -jnp.exp(gate_ref[0:1, :])
    dt_bias = gate_ref[1:2, :]
    onorm = onorm_ref[...]

    def stack_cols(x, lane0):
        return jnp.concatenate([x[:, lane0 + h:lane0 + h + 1] for h in range(nh)], axis=0)

    beta, gc, gcc, gr, glast = [], [], [], [], []
    for b, cc in units:
        ba = ba_ref[b * tc + cc * c:b * tc + (cc + 1) * c, :]
        g_all = neg_rate * _softplus(ba + dt_bias)
        gcum = _dot(tril, g_all, HIGHEST)
        gcum_t = gcum.T
        beta.append(stack_cols(_sigmoid(ba), 0))
        gc.append(stack_cols(gcum, GATE_LANE))
        gcc_h = jnp.zeros((c, r), F32)
        for h in range(nh):
            gcc_h = jnp.where(lh == h, gcum[:, GATE_LANE + h:GATE_LANE + h + 1], gcc_h)
        gcc.append(gcc_h)
        gr.append(jnp.concatenate([gcum_t[GATE_LANE + h:GATE_LANE + h + 1, :] for h in range(nh)], axis=1))
        glast.append(jnp.concatenate(
            [jnp.broadcast_to(gcum[c - 1:c, GATE_LANE + h:GATE_LANE + h + 1], (c, 1)) for h in range(nh)], axis=0))

    def fold_units(us):
        n = range(len(us))
        q = [st_ref[0, u] * (hd ** -0.5) for u in us]
        k = [st_ref[1, u] for u in us]
        v = [st_ref[2, u] for u in us]
        kb = [k[j] * beta[u] for j, u in enumerate(us)]
        big = [_dot_nt(_bf(jnp.concatenate([lanes(kb[j]), lanes(q[j])], axis=0)), head_blocks(_bf(k[j]))) for j in n]
        decay = [jnp.where(lower, jnp.exp(jnp.where(lower, gcc[u] - gr[u], 0.0)), 0.0) for u in us]
        a = [jnp.where(strict, big[j][:c] * decay[j], 0.0) for j in n]
        qkd = [expand(_bf(big[j][c:] * decay[j])) for j in n]
        tinv = _unit_lower_inverse(a, c, ri, lj, expand)
        eg = [jnp.exp(gc[u]) for u in us]
        rhs = [_bf(jnp.concatenate([v[j] * beta[u], kb[j] * eg[j]], axis=1)) for j, u in enumerate(us)]
        sol = [_dot(expand(_bf(tinv[j])), rhs[j]) for j in n]
        solb = [_bf(s) for s in sol]
        fold = [_dot(qkd[j], solb[j]) for j in n]
        k_dec = [_bf(k[j] * jnp.exp(glast[u] - gc[u])) for j, u in enumerate(us)]
        for j, u in enumerate(us):
            qp = q[j] * eg[j] - fold[j][:, hd:]
            for h in range(nh):
                rows = slice(h * c, (h + 1) * c)
                kt = _dot_tn(k_dec[j][rows], solb[j][rows])
                g_tot = jnp.exp(glast[u][h * c:h * c + 1, :])
                lhs_ref[u, h, 0:hd, :] = _bf(jnp.where(eye_hd, g_tot, 0.0) - kt[:, hd:])
                lhs_ref[u, h, hd:hd + c, :] = _bf(qp[rows])
                add_ref[u, h, 0:hd, :] = kt[:, :hd]
                add_ref[u, h, hd:hd + c, :] = fold[j][rows, :hd]

    fold_units(list(range(len(units))))

    for u, (b, cc) in enumerate(units):
        rows = slice(b * tc + cc * c, b * tc + (cc + 1) * c)
        for h in range(nh):
            res = _dot(lhs_ref[u, h], _bf(s_ref[b, h])) + add_ref[u, h]
            s_ref[b, h] = res[:hd]
            o = res[hd:]
            on = o * lax.rsqrt(jnp.mean(o * o, axis=-1, keepdims=True) + EPS) * onorm
            bg = bg_ref[rows, h * hd:(h + 1) * hd]
            o_ref[rows, h * hd:(h + 1) * hd] = (on * _silu(bg)).astype(BF16)
    sout_ref[...] = s_ref[...]


def _delta(z, hist, s0, convw, gate, onorm, *, nb, bt, t, tc, c):
    nt = t // tc
    units = bt * (tc // c)
    state = (bt, B_HEADS, B_HEAD_DIM, B_HEAD_DIM)
    kern = functools.partial(_delta_kernel, bt=bt, c=c, tc=tc)
    return pl.pallas_call(
        kern,
        grid=(nb // bt, nt),
        in_specs=[
            _z_window(bt * tc, 3 * B_WIDTH, Z_BQKV, lambda b, i: b * nt + i),
            _z_window(bt * tc, LANES, Z_BA, lambda b, i: b * nt + i),
            _z_window(bt * tc, B_WIDTH, Z_BG, lambda b, i: b * nt + i),
            pl.BlockSpec((bt, SUBLANES, 3 * B_WIDTH), lambda b, i: (b, 0, 0)),
            pl.BlockSpec(state, lambda b, i: (b, 0, 0, 0)),
            pl.BlockSpec((CONV_W, 3 * B_WIDTH), lambda b, i: (0, 0)),
            pl.BlockSpec((2, LANES), lambda b, i: (0, 0)),
            pl.BlockSpec((1, B_HEAD_DIM), lambda b, i: (0, 0)),
        ],
        out_specs=[
            pl.BlockSpec((bt * tc, B_WIDTH), lambda b, i: (b * nt + i, 0)),
            pl.BlockSpec(state, lambda b, i: (b, 0, 0, 0)),
        ],
        out_shape=[
            jax.ShapeDtypeStruct((nb * t, B_WIDTH), BF16),
            jax.ShapeDtypeStruct((nb, B_HEADS, B_HEAD_DIM, B_HEAD_DIM), F32),
        ],
        scratch_shapes=[
            pltpu.VMEM((bt, SUBLANES + tc, 3 * B_WIDTH), F32),
            pltpu.VMEM((3, units, B_HEADS * c, B_HEAD_DIM), F32),
            pltpu.VMEM((units, B_HEADS, B_HEAD_DIM + c, B_HEAD_DIM), BF16),
            pltpu.VMEM((units, B_HEADS, B_HEAD_DIM + c, B_HEAD_DIM), F32),
            pltpu.VMEM(state, F32),
        ],
        compiler_params=pltpu.CompilerParams(dimension_semantics=("arbitrary", "arbitrary"),
                                             vmem_limit_bytes=VMEM_LIMIT),
        name="gated_delta",
    )(z, z, z, hist, s0, convw, gate, onorm)


def _pool_kernel(h_ref, oa_ref, ob_ref, wa_ref, wb_ref, hist_ref, g_ref, win_ref, wgrp_ref, scale_ref, wout_ref,
                 y_ref, tail_ref, ext_ref, *, bt, tt, pos0):
    i = pl.program_id(1)
    pad = POOL_HIST + 1
    rows = bt * tt

    @pl.when(i == 0)
    def _():
        pos = pos0 - pad + lax.broadcasted_iota(jnp.int32, (1, pad, 1), 1)
        ext_ref[:, 0:pad, :] = jnp.where(pos >= 0, hist_ref[...], 0.0)

    @pl.when(i > 0)
    def _():
        ext_ref[:, 0:pad, :] = ext_ref[:, tt:tt + pad, :]

    x = h_ref[...] + _dot(oa_ref[...], wa_ref[...]) + _dot(ob_ref[...], wb_ref[...])
    xn = x * lax.rsqrt(jnp.mean(x * x, axis=-1, keepdims=True) + EPS) * g_ref[...]
    z = _dot(xn.astype(BF16), win_ref[...])
    u = z[:, :C_WIDTH]
    gate = z[:, C_WIDTH:]
    ext_ref[:, pad:pad + tt, :] = u.reshape(bt, tt, C_WIDTH)
    tail_ref[...] = ext_ref[:, tt:tt + pad, :]

    tpos = pos0 + i * tt + (lax.broadcasted_iota(jnp.int32, (rows, 1), 0) & (tt - 1))
    mixed = []
    for gi, w in enumerate(POOL_SIZES):
        cols = slice(gi * C_GROUP, (gi + 1) * C_GROUP)
        s = ext_ref[:, :, cols].reshape(bt * (pad + tt), C_GROUP)
        sh = 1
        while sh < w:
            s = s + pltpu.roll(s, sh, 0)
            sh *= 2
        s = s.reshape(bt, pad + tt, C_GROUP)[:, pad:, :].reshape(rows, C_GROUP)
        cnt = jnp.minimum(tpos + 1, w).astype(F32)
        pooled = s / cnt - u[:, cols]
        m = _dot(pooled.astype(BF16), wgrp_ref[gi]) * scale_ref[:, cols]
        mixed.append((m * _silu(gate[:, cols])).astype(BF16))
    y_ref[...] = x + _dot(jnp.concatenate(mixed, axis=1), wout_ref[...])


def _pool_layer(h, oa, ob, wa, wb, hist, g, win, wgrp, scale, wout, *, nb, bt, t, tt, pos0):
    nt = t // tt
    pad = POOL_HIST + 1
    rows = bt * tt
    kern = functools.partial(_pool_kernel, bt=bt, tt=tt, pos0=pos0)
    tile = lambda b, i: (b * nt + i, 0)
    const2 = lambda b, i: (0, 0)
    once = pl.Buffered(1)
    return pl.pallas_call(
        kern,
        grid=(nb // bt, nt),
        in_specs=[
            pl.BlockSpec((rows, D_MODEL), tile),
            pl.BlockSpec((rows, A_WIDTH), tile),
            pl.BlockSpec((rows, B_WIDTH), tile),
            pl.BlockSpec((A_WIDTH, D_MODEL), const2, pipeline_mode=once),
            pl.BlockSpec((B_WIDTH, D_MODEL), const2, pipeline_mode=once),
            pl.BlockSpec((bt, pad, C_WIDTH), lambda b, i: (b, 0, 0)),
            pl.BlockSpec((1, D_MODEL), const2),
            pl.BlockSpec((D_MODEL, 2 * C_WIDTH), const2, pipeline_mode=once),
            pl.BlockSpec((len(POOL_SIZES), C_GROUP, C_GROUP), lambda b, i: (0, 0, 0), pipeline_mode=once),
            pl.BlockSpec((1, C_WIDTH), const2),
            pl.BlockSpec((C_WIDTH, D_MODEL), const2, pipeline_mode=once),
        ],
        out_specs=[
            pl.BlockSpec((rows, D_MODEL), tile),
            pl.BlockSpec((bt, pad, C_WIDTH), lambda b, i: (b, 0, 0)),
        ],
        out_shape=[
            jax.ShapeDtypeStruct((nb * t, D_MODEL), F32),
            jax.ShapeDtypeStruct((nb, pad, C_WIDTH), F32),
        ],
        scratch_shapes=[pltpu.VMEM((bt, pad + tt, C_WIDTH), F32)],
        compiler_params=pltpu.CompilerParams(dimension_semantics=("arbitrary", "arbitrary"),
                                             vmem_limit_bytes=VMEM_LIMIT),
        name="out_proj_pool_layer",
    )(h, oa, ob, wa, wb, hist, g, win, wgrp, scale, wout)


def _ab_weights(norm_g, w_in, q_norm, k_norm, sinks, conv_w, a_log, dt_bias, o_norm, w_out):
    w = w_in.astype(BF16)
    lane_pad = (GATE_LANE, LANES - GATE_LANE - B_HEADS)
    gate = jnp.stack([jnp.pad(a_log.astype(F32), lane_pad), jnp.pad(dt_bias.astype(F32), lane_pad)])
    return dict(
        norm_g=norm_g.reshape(1, D_MODEL), w=w, kn=jnp.tile(k_norm, A_KV_HEADS).reshape(1, A_KV_WIDTH),
        qn=jnp.tile(q_norm, A_KV_HEADS).reshape(1, LANES), sinks=sinks.astype(F32), conv_w=conv_w, gate=gate,
        onorm=o_norm.reshape(1, B_HEAD_DIM), wa=w_out[:A_WIDTH].astype(BF16), wb=w_out[A_WIDTH:].astype(BF16))


def _group_tiles(t):
    if t >= CHUNK:
        return dict(proj_rows=1024, seqs_attn=1, seqs_delta=1, delta_tokens=512, chunk=CHUNK, seqs_pool=1, pool_tokens=1024)
    return dict(proj_rows=256, seqs_attn=32, seqs_delta=16, delta_tokens=t, chunk=t, seqs_pool=256 // t, pool_tokens=t)


def _ab_mixers(h, wts, cache_k, cache_v, s0, conv_hist, *, nb, t):
    tl = _group_tiles(t)
    z = _in_proj(h, wts["norm_g"], wts["w"], wts["kn"], min(nb * t, tl["proj_rows"]))
    bt = tl["seqs_attn"]
    if cache_k is None:
        cq, lk, off = CHUNK, WINDOW + CHUNK, 0
        kv_specs = [_z_window(bt * t, A_KV_WIDTH, Z_AK, lambda b: b), _z_window(bt * t, A_KV_WIDTH, Z_AV, lambda b: b)]
        kbuf = vbuf = z
    else:
        cq, lk, off = t, WINDOW + t, WINDOW
        k_new = z[:, Z_AK:Z_AK + A_KV_WIDTH].reshape(nb, t, A_KV_WIDTH)
        v_new = z[:, Z_AV:Z_AV + A_KV_WIDTH].reshape(nb, t, A_KV_WIDTH)
        kbuf = jnp.concatenate([cache_k.reshape(nb, WINDOW, A_KV_WIDTH), k_new], axis=1).reshape(nb * lk, A_KV_WIDTH)
        vbuf = jnp.concatenate([cache_v.reshape(nb, WINDOW, A_KV_WIDTH), v_new], axis=1).reshape(nb * lk, A_KV_WIDTH)
        kv_specs = [pl.BlockSpec((bt * lk, A_KV_WIDTH), lambda b: (b, 0))] * 2
    o_a = _attention(z, kbuf, vbuf, kv_specs, wts["sinks"], wts["qn"], nb=nb, bt=bt, t=t, cq=cq, lk=lk, off=off)
    hist = jnp.pad(conv_hist.astype(F32), ((0, 0), (SUBLANES - (CONV_W - 1), 0), (0, 0)))
    o_b, s_new = _delta(z, hist, s0.astype(F32), wts["conv_w"], wts["gate"], wts["onorm"], nb=nb,
                        bt=tl["seqs_delta"], t=t, tc=tl["delta_tokens"], c=tl["chunk"])
    return o_a, o_b, z, s_new


def _last_rows(z, col, width, nb, t, rows):
    return z.reshape(nb, t, Z_WIDTH)[:, t - rows:, col:col + width]


def _cache_rows(z, col, nb, t, rows):
    return _last_rows(z, col, A_KV_WIDTH, nb, t, rows).reshape(nb, rows, A_KV_HEADS, A_HEAD_DIM)


def kernel(x_prompt, x_sample, cache_a_k, cache_a_v, state_b_s, state_b_conv, state_c_pool,
           norm_ab, w_in_ab, q_norm_a, k_norm_a, sinks_a, conv_b, a_log_b, dt_bias_b, o_norm_b, w_out_ab,
           norm_c, w_in_c, w_grp_c, scale_c, w_out_c):
    bp, tp, _ = x_prompt.shape
    bs, ts, _ = x_sample.shape
    hp = x_prompt.reshape(bp * tp, D_MODEL)
    hs = x_sample.reshape(bs * ts, D_MODEL)

    wts = _ab_weights(norm_ab[0], w_in_ab[0], q_norm_a[0], k_norm_a[0], sinks_a[0], conv_b[0], a_log_b[0],
                      dt_bias_b[0], o_norm_b[0], w_out_ab[0])
    s0 = jnp.zeros((bp, B_HEADS, B_HEAD_DIM, B_HEAD_DIM), F32)
    c0 = jnp.zeros((bp, CONV_W - 1, 3 * B_WIDTH), F32)
    oap, obp, zp, sp = _ab_mixers(hp, wts, None, None, s0, c0, nb=bp, t=tp)
    oas, obs, zs, ss = _ab_mixers(hs, wts, cache_a_k[0], cache_a_v[0], state_b_s[0], state_b_conv[0], nb=bs, t=ts)
    p_a_k = _cache_rows(zp, Z_AK, bp, tp, WINDOW)[None]
    p_a_v = _cache_rows(zp, Z_AV, bp, tp, WINDOW)[None]
    s_a_k = _cache_rows(zs, Z_AK, bs, ts, ts)[None]
    s_a_v = _cache_rows(zs, Z_AV, bs, ts, ts)[None]
    p_b_conv = _last_rows(zp, Z_BQKV, 3 * B_WIDTH, bp, tp, CONV_W - 1)[None]
    s_b_conv = _last_rows(zs, Z_BQKV, 3 * B_WIDTH, bs, ts, CONV_W - 1)[None]

    g_c = norm_c[0].reshape(1, D_MODEL)
    win = w_in_c[0].astype(BF16)
    wgrp = w_grp_c[0].astype(BF16)
    scale = scale_c[0].reshape(1, C_WIDTH)
    wout = w_out_c[0].astype(BF16)
    h0 = jnp.zeros((bp, POOL_HIST + 1, C_WIDTH), F32)
    hs0 = jnp.pad(state_c_pool[0].astype(F32), ((0, 0), (1, 0), (0, 0)))
    tlp, tls = _group_tiles(tp), _group_tiles(ts)
    yp, tail_p = _pool_layer(hp, oap, obp, wts["wa"], wts["wb"], h0, g_c, win, wgrp, scale, wout, nb=bp,
                             bt=tlp["seqs_pool"], t=tp, tt=tlp["pool_tokens"], pos0=0)
    ys, tail_s = _pool_layer(hs, oas, obs, wts["wa"], wts["wb"], hs0, g_c, win, wgrp, scale, wout, nb=bs,
                             bt=tls["seqs_pool"], t=ts, tt=tls["pool_tokens"], pos0=PAST_LEN)

    return (yp.reshape(bp, tp, D_MODEL), ys.reshape(bs, ts, D_MODEL),
            p_a_k, p_a_v, sp[None], p_b_conv, tail_p[:, 1:][None],
            s_a_k, s_a_v, ss[None], s_b_conv, tail_s[:, 1:][None])
```

```python
import functools

import jax
import jax.numpy as jnp
from jax import lax
from jax.experimental import pallas as pl
from jax.experimental.pallas import tpu as pltpu

F32 = jnp.float32
BF16 = jnp.bfloat16
HIGHEST = lax.Precision.HIGHEST

D_MODEL = 1024
CHUNK = 64
PAST_LEN = 2048
EPS = 1e-6
NEG_INF = -1e30
LOG2E = 1.4426950408889634

A_HEADS = 8
A_KV_HEADS = 2
A_HEAD_DIM = 64
A_WIDTH = A_HEADS * A_HEAD_DIM
A_KV_WIDTH = A_KV_HEADS * A_HEAD_DIM
A_REP = A_HEADS // A_KV_HEADS
WINDOW = 128

B_HEADS = 4
B_HEAD_DIM = 128
B_WIDTH = B_HEADS * B_HEAD_DIM
CONV_W = 4

POOL_SIZES = (2, 4, 8, 16)
C_WIDTH = D_MODEL
C_GROUP = C_WIDTH // len(POOL_SIZES)
POOL_HIST = max(POOL_SIZES) - 1

LANES = 128
SUBLANES = 8

Z_AQ = 0
Z_AK = Z_AQ + A_WIDTH
Z_AV = Z_AK + A_KV_WIDTH
Z_AG = Z_AV + A_KV_WIDTH
Z_BQKV = Z_AG + A_WIDTH
Z_BG = Z_BQKV + 3 * B_WIDTH
Z_BA = Z_BG + B_WIDTH
Z_WIDTH = Z_BA + LANES
GATE_LANE = B_HEADS


def _z_window(rows, width, col, row_block):
    return pl.BlockSpec((pl.Element(rows), pl.Element(width)), lambda *idx: (row_block(*idx) * rows, col))

VMEM_LIMIT = 48 * 1024 * 1024


def _sigmoid(x):
    return 1.0 / (1.0 + jnp.exp(-x))


def _silu(x):
    return x * _sigmoid(x)


def _softplus(x):
    return jnp.maximum(x, 0.0) + jnp.log(1.0 + jnp.exp(-jnp.abs(x)))


def _dot(a, b, precision=None):
    return jnp.dot(a, b, preferred_element_type=F32, precision=precision)


def _dot_nt(a, b, precision=None):
    return lax.dot_general(a, b, (((1,), (1,)), ((), ())), preferred_element_type=F32, precision=precision)


def _dot_tn(a, b, precision=None):
    return lax.dot_general(a, b, (((0,), (0,)), ((), ())), preferred_element_type=F32, precision=precision)


def _log2(n):
    assert n & (n - 1) == 0
    return n.bit_length() - 1


def _in_proj_kernel(x_ref, g_ref, w_ref, kn_ref, z_ref):
    x = x_ref[...]
    xn = x * lax.rsqrt(jnp.mean(x * x, axis=-1, keepdims=True) + EPS) * g_ref[...]
    n_in = w_ref.shape[0]
    z_ref[:, Z_BA:Z_WIDTH] = jnp.zeros((x.shape[0], Z_WIDTH - Z_BA), F32)
    z_ref[:, 0:n_in] = _dot_nt(xn.astype(BF16), w_ref[...])
    k = z_ref[:, Z_AK:Z_AK + A_KV_WIDTH]
    sq = k * k
    lane = lax.broadcasted_iota(jnp.int32, k.shape, 1)
    first = lane < A_HEAD_DIM
    s0 = jnp.sum(jnp.where(first, sq, 0.0), axis=-1, keepdims=True)
    s1 = jnp.sum(jnp.where(first, 0.0, sq), axis=-1, keepdims=True)
    ms = jnp.where(first, s0, s1) * (1.0 / A_HEAD_DIM)
    z_ref[:, Z_AK:Z_AK + A_KV_WIDTH] = k * lax.rsqrt(ms + EPS) * kn_ref[...]


def _in_proj(x, g, w, kn, tm):
    n = x.shape[0]
    return pl.pallas_call(
        _in_proj_kernel,
        grid=(n // tm,),
        in_specs=[
            pl.BlockSpec((tm, D_MODEL), lambda i: (i, 0)),
            pl.BlockSpec((1, D_MODEL), lambda i: (0, 0)),
            pl.BlockSpec(w.shape, lambda i: (0, 0), pipeline_mode=pl.Buffered(1)),
            pl.BlockSpec((1, A_KV_WIDTH), lambda i: (0, 0)),
        ],
        out_specs=pl.BlockSpec((tm, Z_WIDTH), lambda i: (i, 0)),
        out_shape=jax.ShapeDtypeStruct((n, Z_WIDTH), F32),
        compiler_params=pltpu.CompilerParams(dimension_semantics=("arbitrary",), vmem_limit_bytes=VMEM_LIMIT),
        name="in_proj",
    )(x, g, w, kn)


A_HALVES = LANES // A_HEAD_DIM
A_QBLOCKS = A_WIDTH // LANES
A_COPIES = A_KV_HEADS


def _attn_half(r, c):
    g = r * A_HALVES // A_REP
    return (g + c) % A_HALVES


def _attn_tables(sinks_ref, bias_ref, *, cq, lk, front, fill):
    rows = A_QBLOCKS * cq
    nvar = front // cq + 1
    row = lax.broadcasted_iota(jnp.int32, (rows, 1), 0)
    blk = row >> _log2(cq)

    def per_row(value):
        out = []
        for c in range(A_COPIES):
            col = jnp.zeros((rows, 1), F32)
            for r in range(A_QBLOCKS):
                col = jnp.where(blk == r, value(r * A_HALVES + _attn_half(r, c)), col)
            out.append(col)
        return out

    @pl.when(fill)
    def _():
        col = lax.broadcasted_iota(jnp.int32, (rows, lk), 1)
        dist = jnp.abs((row & (cq - 1)) + WINDOW - col).astype(F32)
        slopes = per_row(lambda h: 2.0 ** (-8.0 * (h + 1) / A_HEADS))
        for c in range(A_COPIES):
            for var in range(nvar):
                bias_ref[var, c] = jnp.where(col >= front - var * cq, -slopes[c] * dist, NEG_INF) * LOG2E

    first = lax.broadcasted_iota(jnp.int32, (cq, LANES), 1) < A_HEAD_DIM
    sks = per_row(lambda h: sinks_ref[h] * LOG2E)
    return first, sks, nvar


def _attn_chunk(qc, kv, biases, sks, first, qn, gate, cq):
    blocks = []
    for r in range(A_QBLOCKS):
        x = qc[:, r * LANES:(r + 1) * LANES]
        sq = x * x
        s_lo = jnp.sum(jnp.where(first, sq, 0.0), axis=-1, keepdims=True)
        s_hi = jnp.sum(jnp.where(first, 0.0, sq), axis=-1, keepdims=True)
        ms = jnp.where(first, s_lo, s_hi) * (1.0 / A_HEAD_DIM)
        blocks.append(x * lax.rsqrt(ms + EPS) * qn)
    outs = []
    for c in range(A_COPIES):
        lhs = jnp.concatenate(
            [jnp.where(first if _attn_half(r, c) == 0 else jnp.logical_not(first), blocks[r], 0.0)
             for r in range(A_QBLOCKS)], axis=0)
        kk, vv = kv[c]
        sc = _dot_nt(lhs.astype(BF16), kk) + biases[c]
        m = jnp.maximum(jnp.max(sc, axis=-1, keepdims=True), sks[c])
        p = jnp.exp2(sc - m)
        den = jnp.sum(p, axis=-1, keepdims=True) + jnp.exp2(sks[c] - m)
        outs.append(_dot(p.astype(BF16), vv) * (1.0 / den))
    tile = []
    for r in range(A_QBLOCKS):
        rows = slice(r * cq, (r + 1) * cq)
        low = 0 if _attn_half(r, 0) == 0 else 1
        tile.append(jnp.where(first, outs[low][rows], outs[1 - low][rows]))
    return (jnp.concatenate(tile, axis=1) * _silu(gate)).astype(BF16)


def _attn_kernel(sinks_ref, q_ref, k_ref, v_ref, ag_ref, qn_ref, o_ref, kp_ref, vp_ref, bias_ref, *, bt, cq, lk, off, t):
    front = WINDOW - off
    lkt = k_ref.shape[0] // bt
    n_chunks = t // cq

    for b in range(bt):
        for src, dst in ((k_ref, kp_ref), (v_ref, vp_ref)):
            x = src[b * lkt:(b + 1) * lkt, :]
            if front:
                dst[:, b, 0:front, :] = jnp.zeros((A_COPIES, front, A_KV_WIDTH), BF16)
            dst[0, b, front:front + lkt, :] = x.astype(BF16)
            dst[1, b, front:front + lkt, :] = pltpu.roll(x, A_HEAD_DIM, 1).astype(BF16)

    first, sks, nvar = _attn_tables(sinks_ref, bias_ref, cq=cq, lk=lk, front=front, fill=pl.program_id(0) == 0)
    qn = qn_ref[...] * (A_HEAD_DIM ** -0.5 * LOG2E)

    def chunk(b, cg):
        k0 = cg * cq if isinstance(cg, int) else pl.multiple_of(cg * cq, cq)
        r0 = b * t + k0
        var = min(cg, nvar - 1) if isinstance(cg, int) else jnp.minimum(cg, nvar - 1)
        kv = [(kp_ref[c, b, pl.ds(k0, lk), :], vp_ref[c, b, pl.ds(k0, lk), :]) for c in range(A_COPIES)]
        o_ref[pl.ds(r0, cq), :] = _attn_chunk(
            q_ref[pl.ds(r0, cq), :], kv, [bias_ref[var, c] for c in range(A_COPIES)], sks, first, qn,
            ag_ref[pl.ds(r0, cq), :], cq)

    for b in range(bt):
        if n_chunks == 1:
            chunk(b, 0)
        else:
            lax.fori_loop(0, n_chunks, lambda cg, carry, b=b: (chunk(b, cg), carry)[1], 0, unroll=min(n_chunks, 8))


def _attention(z, kbuf, vbuf, kv_specs, sinks, qn, *, nb, bt, t, cq, lk, off):
    front = WINDOW - off
    lkt = kbuf.shape[0] // nb
    kern = functools.partial(_attn_kernel, bt=bt, cq=cq, lk=lk, off=off, t=t)
    return pl.pallas_call(
        kern,
        grid=(nb // bt,),
        in_specs=[
            pl.BlockSpec(memory_space=pltpu.SMEM),
            _z_window(bt * t, A_WIDTH, Z_AQ, lambda b: b),
            kv_specs[0],
            kv_specs[1],
            _z_window(bt * t, A_WIDTH, Z_AG, lambda b: b),
            pl.BlockSpec((1, LANES), lambda b: (0, 0)),
        ],
        out_specs=pl.BlockSpec((bt * t, A_WIDTH), lambda b: (b, 0)),
        out_shape=jax.ShapeDtypeStruct((nb * t, A_WIDTH), BF16),
        scratch_shapes=[
            pltpu.VMEM((A_COPIES, bt, front + lkt, A_KV_WIDTH), BF16),
            pltpu.VMEM((A_COPIES, bt, front + lkt, A_KV_WIDTH), BF16),
            pltpu.VMEM((front // cq + 1, A_COPIES, A_QBLOCKS * cq, lk), F32),
        ],
        compiler_params=pltpu.CompilerParams(dimension_semantics=("arbitrary",), vmem_limit_bytes=VMEM_LIMIT),
        name="swa_attention",
    )(sinks, z, kbuf, vbuf, z, qn)


def _bf(x):
    return x.astype(BF16)


def _unit_lower_inverse(a_list, c, ri, lj, expand):
    base = SUBLANES
    same = (ri >> _log2(base)) == (lj >> _log2(base))
    eye = jnp.where(ri == lj, 1.0, 0.0)
    n1 = [jnp.where(same, -a, 0.0) for a in a_list]
    x = [eye + n for n in n1]
    n1b = [_bf(n) for n in n1]
    n2b = [_bf(_dot(nb, expand(nb))) for nb in n1b]
    n2e = [expand(nb) for nb in n2b]
    x = [xi + _dot(_bf(xi), ne) for xi, ne in zip(x, n2e)]
    n4e = [expand(_bf(_dot(nb, ne))) for nb, ne in zip(n2b, n2e)]
    x = [xi + _dot(_bf(xi), ne) for xi, ne in zip(x, n4e)]
    s = base
    while s < c:
        sel = ((ri >> _log2(2 * s)) == (lj >> _log2(2 * s))) & ((ri >> _log2(s)) != (lj >> _log2(s)))
        xb = [_bf(xi) for xi in x]
        xo = [_dot(b, expand(_bf(jnp.where(sel, a, 0.0)))) for b, a in zip(xb, a_list)]
        x = [xi - _dot(_bf(o), expand(b)) for xi, o, b in zip(x, xo, xb)]
        s *= 2
    return x


def _delta_kernel(qkv_ref, ba_ref, bg_ref, hist_ref, s0_ref, convw_ref, gate_ref, onorm_ref,
                  o_ref, sout_ref, ext_ref, st_ref, lhs_ref, add_ref, s_ref, *, bt, c, tc):
    i = pl.program_id(1)
    hd = B_HEAD_DIM
    nh = B_HEADS
    r = nh * c
    nch = tc // c
    pad = SUBLANES
    units = [(b, cc) for b in range(bt) for cc in range(nch)]

    @pl.when(i == 0)
    def _():
        ext_ref[:, 0:pad, :] = hist_ref[...]
        s_ref[...] = s0_ref[...]

    @pl.when(i > 0)
    def _():
        ext_ref[:, 0:pad, :] = ext_ref[:, tc:tc + pad, :]

    for b in range(bt):
        ext_ref[b, pad:pad + tc, :] = qkv_ref[b * tc:(b + 1) * tc, :]

    for u, (b, cc) in enumerate(units):
        e = ext_ref[b, cc * c:cc * c + pad + c, :]
        acc = e * convw_ref[0:1, :]
        for j in range(1, CONV_W):
            acc = pltpu.roll(acc, 1, 0) + e * convw_ref[j:j + 1, :]
        y = _silu(acc[pad:, :])
        for part in range(3):
            for h in range(nh):
                blk = y[:, (part * nh + h) * hd:(part * nh + h + 1) * hd]
                if part < 2:
                    blk = blk * lax.rsqrt(jnp.sum(blk * blk, axis=-1, keepdims=True) + EPS)
                st_ref[part, u, h * c:(h + 1) * c, :] = blk

    ri = lax.broadcasted_iota(jnp.int32, (c, r), 0)
    li = lax.broadcasted_iota(jnp.int32, (c, r), 1)
    lj = li & (c - 1)
    lh = li >> _log2(c)
    lower = ri >= lj
    strict = ri > lj
    head_sel = [jnp.where(lh == h, 1.0, 0.0).astype(BF16) for h in range(nh)]

    def expand(xb):
        return jnp.concatenate([xb * m for m in head_sel], axis=0)

    def lanes(x_st):
        return jnp.concatenate([x_st[h * c:(h + 1) * c] for h in range(nh)], axis=1)

    def head_blocks(x_st):
        zero = jnp.zeros((c, hd), x_st.dtype)
        return jnp.concatenate(
            [jnp.concatenate([x_st[h * c:(h + 1) * c] if h2 == h else zero for h2 in range(nh)], axis=1)
             for h in range(nh)], axis=0)

    ci = lax.broadcasted_iota(jnp.int32, (c, c), 0)
    cj = lax.broadcasted_iota(jnp.int32, (c, c), 1)
    tril = jnp.where(ci >= cj, 1.0, 0.0)
    di = lax.broadcasted_iota(jnp.int32, (hd, hd), 0)
    dj = lax.broadcasted_iota(jnp.int32, (hd, hd), 1)
    eye_hd = di == dj
    neg_rate = -jnp.exp(gate_ref[0:1, :])
    dt_bias = gate_ref[1:2, :]
    onorm = onorm_ref[...]

    def stack_cols(x, lane0):
        return jnp.concatenate([x[:, lane0 + h:lane0 + h + 1] for h in range(nh)], axis=0)

    beta, gc, gcc, gr, glast = [], [], [], [], []
    for b, cc in units:
        ba = ba_ref[b * tc + cc * c:b * tc + (cc + 1) * c, :]
        g_all = neg_rate * _softplus(ba + dt_bias)
        gcum = _dot(tril, g_all, HIGHEST)
        gcum_t = gcum.T
        beta.append(stack_cols(_sigmoid(ba), 0))
        gc.append(stack_cols(gcum, GATE_LANE))
        gcc_h = jnp.zeros((c, r), F32)
        for h in range(nh):
            gcc_h = jnp.where(lh == h, gcum[:, GATE_LANE + h:GATE_LANE + h + 1], gcc_h)
        gcc.append(gcc_h)
        gr.append(jnp.concatenate([gcum_t[GATE_LANE + h:GATE_LANE + h + 1, :] for h in range(nh)], axis=1))
        glast.append(jnp.concatenate(
            [jnp.broadcast_to(gcum[c - 1:c, GATE_LANE + h:GATE_LANE + h + 1], (c, 1)) for h in range(nh)], axis=0))

    def fold_units(us):
        n = range(len(us))
        q = [st_ref[0, u] * (hd ** -0.5) for u in us]
        k = [st_ref[1, u] for u in us]
        v = [st_ref[2, u] for u in us]
        kb = [k[j] * beta[u] for j, u in enumerate(us)]
        big = [_dot_nt(_bf(jnp.concatenate([lanes(kb[j]), lanes(q[j])], axis=0)), head_blocks(_bf(k[j]))) for j in n]
        decay = [jnp.where(lower, jnp.exp(jnp.where(lower, gcc[u] - gr[u], 0.0)), 0.0) for u in us]
        a = [jnp.where(strict, big[j][:c] * decay[j], 0.0) for j in n]
        qkd = [expand(_bf(big[j][c:] * decay[j])) for j in n]
        tinv = _unit_lower_inverse(a, c, ri, lj, expand)
        eg = [jnp.exp(gc[u]) for u in us]
        rhs = [_bf(jnp.concatenate([v[j] * beta[u], kb[j] * eg[j]], axis=1)) for j, u in enumerate(us)]
        sol = [_dot(expand(_bf(tinv[j])), rhs[j]) for j in n]
        solb = [_bf(s) for s in sol]
        fold = [_dot(qkd[j], solb[j]) for j in n]
        k_dec = [_bf(k[j] * jnp.exp(glast[u] - gc[u])) for j, u in enumerate(us)]
        for j, u in enumerate(us):
            qp = q[j] * eg[j] - fold[j][:, hd:]
            for h in range(nh):
                rows = slice(h * c, (h + 1) * c)
                kt = _dot_tn(k_dec[j][rows], solb[j][rows])
                g_tot = jnp.exp(glast[u][h * c:h * c + 1, :])
                lhs_ref[u, h, 0:hd, :] = _bf(jnp.where(eye_hd, g_tot, 0.0) - kt[:, hd:])
                lhs_ref[u, h, hd:hd + c, :] = _bf(qp[rows])
                add_ref[u, h, 0:hd, :] = kt[:, :hd]
                add_ref[u, h, hd:hd + c, :] = fold[j][rows, :hd]

    fold_units(list(range(len(units))))

    for u, (b, cc) in enumerate(units):
        rows = slice(b * tc + cc * c, b * tc + (cc + 1) * c)
        for h in range(nh):
            res = _dot(lhs_ref[u, h], _bf(s_ref[b, h])) + add_ref[u, h]
            s_ref[b, h] = res[:hd]
            o = res[hd:]
            on = o * lax.rsqrt(jnp.mean(o * o, axis=-1, keepdims=True) + EPS) * onorm
            bg = bg_ref[rows, h * hd:(h + 1) * hd]
            o_ref[rows, h * hd:(h + 1) * hd] = (on * _silu(bg)).astype(BF16)
    sout_ref[...] = s_ref[...]


def _delta(z, hist, s0, convw, gate, onorm, *, nb, bt, t, tc, c):
    nt = t // tc
    units = bt * (tc // c)
    state = (bt, B_HEADS, B_HEAD_DIM, B_HEAD_DIM)
    kern = functools.partial(_delta_kernel, bt=bt, c=c, tc=tc)
    return pl.pallas_call(
        kern,
        grid=(nb // bt, nt),
        in_specs=[
            _z_window(bt * tc, 3 * B_WIDTH, Z_BQKV, lambda b, i: b * nt + i),
            _z_window(bt * tc, LANES, Z_BA, lambda b, i: b * nt + i),
            _z_window(bt * tc, B_WIDTH, Z_BG, lambda b, i: b * nt + i),
            pl.BlockSpec((bt, SUBLANES, 3 * B_WIDTH), lambda b, i: (b, 0, 0)),
            pl.BlockSpec(state, lambda b, i: (b, 0, 0, 0)),
            pl.BlockSpec((CONV_W, 3 * B_WIDTH), lambda b, i: (0, 0)),
            pl.BlockSpec((2, LANES), lambda b, i: (0, 0)),
            pl.BlockSpec((1, B_HEAD_DIM), lambda b, i: (0, 0)),
        ],
        out_specs=[
            pl.BlockSpec((bt * tc, B_WIDTH), lambda b, i: (b * nt + i, 0)),
            pl.BlockSpec(state, lambda b, i: (b, 0, 0, 0)),
        ],
        out_shape=[
            jax.ShapeDtypeStruct((nb * t, B_WIDTH), BF16),
            jax.ShapeDtypeStruct((nb, B_HEADS, B_HEAD_DIM, B_HEAD_DIM), F32),
        ],
        scratch_shapes=[
            pltpu.VMEM((bt, SUBLANES + tc, 3 * B_WIDTH), F32),
            pltpu.VMEM((3, units, B_HEADS * c, B_HEAD_DIM), F32),
            pltpu.VMEM((units, B_HEADS, B_HEAD_DIM + c, B_HEAD_DIM), BF16),
            pltpu.VMEM((units, B_HEADS, B_HEAD_DIM + c, B_HEAD_DIM), F32),
            pltpu.VMEM(state, F32),
        ],
        compiler_params=pltpu.CompilerParams(dimension_semantics=("arbitrary", "arbitrary"),
                                             vmem_limit_bytes=VMEM_LIMIT),
        name="gated_delta",
    )(z, z, z, hist, s0, convw, gate, onorm)


def _pool_kernel(h_ref, oa_ref, ob_ref, wa_ref, wb_ref, hist_ref, g_ref, win_ref, wgrp_ref, scale_ref, wout_ref,
                 y_ref, tail_ref, ext_ref, *, bt, tt, pos0):
    i = pl.program_id(1)
    pad = POOL_HIST + 1
    rows = bt * tt

    @pl.when(i == 0)
    def _():
        pos = pos0 - pad + lax.broadcasted_iota(jnp.int32, (1, pad, 1), 1)
        ext_ref[:, 0:pad, :] = jnp.where(pos >= 0, hist_ref[...], 0.0)

    @pl.when(i > 0)
    def _():
        ext_ref[:, 0:pad, :] = ext_ref[:, tt:tt + pad, :]

    x = h_ref[...] + _dot(oa_ref[...], wa_ref[...]) + _dot(ob_ref[...], wb_ref[...])
    xn = x * lax.rsqrt(jnp.mean(x * x, axis=-1, keepdims=True) + EPS) * g_ref[...]
    z = _dot(xn.astype(BF16), win_ref[...])
    u = z[:, :C_WIDTH]
    gate = z[:, C_WIDTH:]
    ext_ref[:, pad:pad + tt, :] = u.reshape(bt, tt, C_WIDTH)
    tail_ref[...] = ext_ref[:, tt:tt + pad, :]

    tpos = pos0 + i * tt + (lax.broadcasted_iota(jnp.int32, (rows, 1), 0) & (tt - 1))
    mixed = []
    for gi, w in enumerate(POOL_SIZES):
        cols = slice(gi * C_GROUP, (gi + 1) * C_GROUP)
        s = ext_ref[:, :, cols].reshape(bt * (pad + tt), C_GROUP)
        sh = 1
        while sh < w:
            s = s + pltpu.roll(s, sh, 0)
            sh *= 2
        s = s.reshape(bt, pad + tt, C_GROUP)[:, pad:, :].reshape(rows, C_GROUP)
        cnt = jnp.minimum(tpos + 1, w).astype(F32)
        pooled = s / cnt - u[:, cols]
        m = _dot(pooled.astype(BF16), wgrp_ref[gi]) * scale_ref[:, cols]
        mixed.append((m * _silu(gate[:, cols])).astype(BF16))
    y_ref[...] = x + _dot(jnp.concatenate(mixed, axis=1), wout_ref[...])


def _pool_layer(h, oa, ob, wa, wb, hist, g, win, wgrp, scale, wout, *, nb, bt, t, tt, pos0):
    nt = t // tt
    pad = POOL_HIST + 1
    rows = bt * tt
    kern = functools.partial(_pool_kernel, bt=bt, tt=tt, pos0=pos0)
    tile = lambda b, i: (b * nt + i, 0)
    const2 = lambda b, i: (0, 0)
    once = pl.Buffered(1)
    return pl.pallas_call(
        kern,
        grid=(nb // bt, nt),
        in_specs=[
            pl.BlockSpec((rows, D_MODEL), tile),
            pl.BlockSpec((rows, A_WIDTH), tile),
            pl.BlockSpec((rows, B_WIDTH), tile),
            pl.BlockSpec((A_WIDTH, D_MODEL), const2, pipeline_mode=once),
            pl.BlockSpec((B_WIDTH, D_MODEL), const2, pipeline_mode=once),
            pl.BlockSpec((bt, pad, C_WIDTH), lambda b, i: (b, 0, 0)),
            pl.BlockSpec((1, D_MODEL), const2),
            pl.BlockSpec((D_MODEL, 2 * C_WIDTH), const2, pipeline_mode=once),
            pl.BlockSpec((len(POOL_SIZES), C_GROUP, C_GROUP), lambda b, i: (0, 0, 0), pipeline_mode=once),
            pl.BlockSpec((1, C_WIDTH), const2),
            pl.BlockSpec((C_WIDTH, D_MODEL), const2, pipeline_mode=once),
        ],
        out_specs=[
            pl.BlockSpec((rows, D_MODEL), tile),
            pl.BlockSpec((bt, pad, C_WIDTH), lambda b, i: (b, 0, 0)),
        ],
        out_shape=[
            jax.ShapeDtypeStruct((nb * t, D_MODEL), F32),
            jax.ShapeDtypeStruct((nb, pad, C_WIDTH), F32),
        ],
        scratch_shapes=[pltpu.VMEM((bt, pad + tt, C_WIDTH), F32)],
        compiler_params=pltpu.CompilerParams(dimension_semantics=("arbitrary", "arbitrary"),
                                             vmem_limit_bytes=VMEM_LIMIT),
        name="out_proj_pool_layer",
    )(h, oa, ob, wa, wb, hist, g, win, wgrp, scale, wout)


def _ab_weights(norm_g, w_in, q_norm, k_norm, sinks, conv_w, a_log, dt_bias, o_norm, w_out):
    w = jnp.swapaxes(w_in, 0, 1).astype(BF16)
    lane_pad = (GATE_LANE, LANES - GATE_LANE - B_HEADS)
    gate = jnp.stack([jnp.pad(a_log.astype(F32), lane_pad), jnp.pad(dt_bias.astype(F32), lane_pad)])
    return dict(
        norm_g=norm_g.reshape(1, D_MODEL), w=w, kn=jnp.tile(k_norm, A_KV_HEADS).reshape(1, A_KV_WIDTH),
        qn=jnp.tile(q_norm, A_KV_HEADS).reshape(1, LANES), sinks=sinks.astype(F32), conv_w=conv_w, gate=gate,
        onorm=o_norm.reshape(1, B_HEAD_DIM), wa=w_out[:A_WIDTH].astype(BF16), wb=w_out[A_WIDTH:].astype(BF16))


def _group_tiles(t):
    if t >= CHUNK:
        return dict(proj_rows=1024, seqs_attn=1, seqs_delta=1, delta_tokens=512, chunk=CHUNK, seqs_pool=1, pool_tokens=1024)
    return dict(proj_rows=256, seqs_attn=32, seqs_delta=16, delta_tokens=t, chunk=t, seqs_pool=256 // t, pool_tokens=t)


def _ab_mixers(h, wts, cache_k, cache_v, s0, conv_hist, *, nb, t):
    tl = _group_tiles(t)
    z = _in_proj(h, wts["norm_g"], wts["w"], wts["kn"], min(nb * t, tl["proj_rows"]))
    bt = tl["seqs_attn"]
    if cache_k is None:
        cq, lk, off = CHUNK, WINDOW + CHUNK, 0
        kv_specs = [_z_window(bt * t, A_KV_WIDTH, Z_AK, lambda b: b), _z_window(bt * t, A_KV_WIDTH, Z_AV, lambda b: b)]
        kbuf = vbuf = z
    else:
        cq, lk, off = t, WINDOW + t, WINDOW
        k_new = z[:, Z_AK:Z_AK + A_KV_WIDTH].reshape(nb, t, A_KV_WIDTH)
        v_new = z[:, Z_AV:Z_AV + A_KV_WIDTH].reshape(nb, t, A_KV_WIDTH)
        kbuf = jnp.concatenate([cache_k.reshape(nb, WINDOW, A_KV_WIDTH), k_new], axis=1).reshape(nb * lk, A_KV_WIDTH)
        vbuf = jnp.concatenate([cache_v.reshape(nb, WINDOW, A_KV_WIDTH), v_new], axis=1).reshape(nb * lk, A_KV_WIDTH)
        kv_specs = [pl.BlockSpec((bt * lk, A_KV_WIDTH), lambda b: (b, 0))] * 2
    o_a = _attention(z, kbuf, vbuf, kv_specs, wts["sinks"], wts["qn"], nb=nb, bt=bt, t=t, cq=cq, lk=lk, off=off)
    hist = jnp.pad(conv_hist.astype(F32), ((0, 0), (SUBLANES - (CONV_W - 1), 0), (0, 0)))
    o_b, s_new = _delta(z, hist, s0.astype(F32), wts["conv_w"], wts["gate"], wts["onorm"], nb=nb,
                        bt=tl["seqs_delta"], t=t, tc=tl["delta_tokens"], c=tl["chunk"])
    return o_a, o_b, z, s_new


def _last_rows(z, col, width, nb, t, rows):
    return z.reshape(nb, t, Z_WIDTH)[:, t - rows:, col:col + width]


def _cache_rows(z, col, nb, t, rows):
    return _last_rows(z, col, A_KV_WIDTH, nb, t, rows).reshape(nb, rows, A_KV_HEADS, A_HEAD_DIM)


def kernel(x_prompt, x_sample, cache_a_k, cache_a_v, state_b_s, state_b_conv, state_c_pool,
           norm_ab, w_in_ab, q_norm_a, k_norm_a, sinks_a, conv_b, a_log_b, dt_bias_b, o_norm_b, w_out_ab,
           norm_c, w_in_c, w_grp_c, scale_c, w_out_c):
    bp, tp, _ = x_prompt.shape
    bs, ts, _ = x_sample.shape
    hp = x_prompt.reshape(bp * tp, D_MODEL)
    hs = x_sample.reshape(bs * ts, D_MODEL)

    wts = _ab_weights(norm_ab[0], w_in_ab[0], q_norm_a[0], k_norm_a[0], sinks_a[0], conv_b[0], a_log_b[0],
                      dt_bias_b[0], o_norm_b[0], w_out_ab[0])
    s0 = jnp.zeros((bp, B_HEADS, B_HEAD_DIM, B_HEAD_DIM), F32)
    c0 = jnp.zeros((bp, CONV_W - 1, 3 * B_WIDTH), F32)
    oap, obp, zp, sp = _ab_mixers(hp, wts, None, None, s0, c0, nb=bp, t=tp)
    oas, obs, zs, ss = _ab_mixers(hs, wts, cache_a_k[0], cache_a_v[0], state_b_s[0], state_b_conv[0], nb=bs, t=ts)
    p_a_k = _cache_rows(zp, Z_AK, bp, tp, WINDOW)[None]
    p_a_v = _cache_rows(zp, Z_AV, bp, tp, WINDOW)[None]
    s_a_k = _cache_rows(zs, Z_AK, bs, ts, ts)[None]
    s_a_v = _cache_rows(zs, Z_AV, bs, ts, ts)[None]
    p_b_conv = _last_rows(zp, Z_BQKV, 3 * B_WIDTH, bp, tp, CONV_W - 1)[None]
    s_b_conv = _last_rows(zs, Z_BQKV, 3 * B_WIDTH, bs, ts, CONV_W - 1)[None]

    g_c = norm_c[0].reshape(1, D_MODEL)
    win = w_in_c[0].astype(BF16)
    wgrp = w_grp_c[0].astype(BF16)
    scale = scale_c[0].reshape(1, C_WIDTH)
    wout = w_out_c[0].astype(BF16)
    h0 = jnp.zeros((bp, POOL_HIST + 1, C_WIDTH), F32)
    hs0 = jnp.pad(state_c_pool[0].astype(F32), ((0, 0), (1, 0), (0, 0)))
    tlp, tls = _group_tiles(tp), _group_tiles(ts)
    yp, tail_p = _pool_layer(hp, oap, obp, wts["wa"], wts["wb"], h0, g_c, win, wgrp, scale, wout, nb=bp,
                             bt=tlp["seqs_pool"], t=tp, tt=tlp["pool_tokens"], pos0=0)
    ys, tail_s = _pool_layer(hs, oas, obs, wts["wa"], wts["wb"], hs0, g_c, win, wgrp, scale, wout, nb=bs,
                             bt=tls["seqs_pool"], t=ts, tt=tls["pool_tokens"], pos0=PAST_LEN)

    return (yp.reshape(bp, tp, D_MODEL), ys.reshape(bs, ts, D_MODEL),
            p_a_k, p_a_v, sp[None], p_b_conv, tail_p[:, 1:][None],
            s_a_k, s_a_v, ss[None], s_b_conv, tail_s[:, 1:][None])
```

```python
import functools

import jax
import jax.numpy as jnp
from jax import lax
from jax.experimental import pallas as pl
from jax.experimental.pallas import tpu as pltpu

F32 = jnp.float32
BF16 = jnp.bfloat16
HIGHEST = lax.Precision.HIGHEST

D_MODEL = 1024
CHUNK = 64
PAST_LEN = 2048
EPS = 1e-6
NEG_INF = -1e30
LOG2E = 1.4426950408889634

A_HEADS = 8
A_KV_HEADS = 2
A_HEAD_DIM = 64
A_WIDTH = A_HEADS * A_HEAD_DIM
A_KV_WIDTH = A_KV_HEADS * A_HEAD_DIM
A_REP = A_HEADS // A_KV_HEADS
WINDOW = 128

B_HEADS = 4
B_HEAD_DIM = 128
B_WIDTH = B_HEADS * B_HEAD_DIM
CONV_W = 4

POOL_SIZES = (2, 4, 8, 16)
C_WIDTH = D_MODEL
C_GROUP = C_WIDTH // len(POOL_SIZES)
POOL_HIST = max(POOL_SIZES) - 1

LANES = 128
SUBLANES = 8

Z_AQ = 0
Z_AK = Z_AQ + A_WIDTH
Z_AV = Z_AK + A_KV_WIDTH
Z_AG = Z_AV + A_KV_WIDTH
Z_BQKV = Z_AG + A_WIDTH
Z_BG = Z_BQKV + 3 * B_WIDTH
Z_BA = Z_BG + B_WIDTH
Z_WIDTH = Z_BA + LANES
GATE_LANE = B_HEADS


def _z_window(rows, width, col, row_block):
    return pl.BlockSpec((pl.Element(rows), pl.Element(width)), lambda *idx: (row_block(*idx) * rows, col))

VMEM_LIMIT = 48 * 1024 * 1024


def _sigmoid(x):
    return 1.0 / (1.0 + jnp.exp(-x))


def _silu(x):
    return x * _sigmoid(x)


def _softplus(x):
    return jnp.maximum(x, 0.0) + jnp.log(1.0 + jnp.exp(-jnp.abs(x)))


def _dot(a, b, precision=None):
    return jnp.dot(a, b, preferred_element_type=F32, precision=precision)


def _dot_nt(a, b, precision=None):
    return lax.dot_general(a, b, (((1,), (1,)), ((), ())), preferred_element_type=F32, precision=precision)


def _dot_tn(a, b, precision=None):
    return lax.dot_general(a, b, (((0,), (0,)), ((), ())), preferred_element_type=F32, precision=precision)


def _log2(n):
    assert n & (n - 1) == 0
    return n.bit_length() - 1


def _in_proj_kernel(x_ref, g_ref, w_ref, kn_ref, z_ref):
    x = x_ref[...]
    xn = x * lax.rsqrt(jnp.mean(x * x, axis=-1, keepdims=True) + EPS) * g_ref[...]
    n_in = w_ref.shape[0]
    z_ref[:, Z_BA:Z_WIDTH] = jnp.zeros((x.shape[0], Z_WIDTH - Z_BA), F32)
    z_ref[:, 0:n_in] = _dot_nt(xn.astype(BF16), w_ref[...])
    k = z_ref[:, Z_AK:Z_AK + A_KV_WIDTH]
    sq = k * k
    lane = lax.broadcasted_iota(jnp.int32, k.shape, 1)
    first = lane < A_HEAD_DIM
    s0 = jnp.sum(jnp.where(first, sq, 0.0), axis=-1, keepdims=True)
    s1 = jnp.sum(jnp.where(first, 0.0, sq), axis=-1, keepdims=True)
    ms = jnp.where(first, s0, s1) * (1.0 / A_HEAD_DIM)
    z_ref[:, Z_AK:Z_AK + A_KV_WIDTH] = k * lax.rsqrt(ms + EPS) * kn_ref[...]


def _in_proj(x, g, w, kn, tm):
    n = x.shape[0]
    return pl.pallas_call(
        _in_proj_kernel,
        grid=(n // tm,),
        in_specs=[
            pl.BlockSpec((tm, D_MODEL), lambda i: (i, 0)),
            pl.BlockSpec((1, D_MODEL), lambda i: (0, 0)),
            pl.BlockSpec(w.shape, lambda i: (0, 0), pipeline_mode=pl.Buffered(1)),
            pl.BlockSpec((1, A_KV_WIDTH), lambda i: (0, 0)),
        ],
        out_specs=pl.BlockSpec((tm, Z_WIDTH), lambda i: (i, 0)),
        out_shape=jax.ShapeDtypeStruct((n, Z_WIDTH), F32),
        compiler_params=pltpu.CompilerParams(dimension_semantics=("arbitrary",), vmem_limit_bytes=VMEM_LIMIT),
        name="in_proj",
    )(x, g, w, kn)


A_HALVES = LANES // A_HEAD_DIM
A_QBLOCKS = A_WIDTH // LANES
A_COPIES = A_KV_HEADS


def _attn_half(r, c):
    g = r * A_HALVES // A_REP
    return (g + c) % A_HALVES


def _attn_tables(sinks_ref, bias_ref, *, cq, lk, front, fill):
    rows = A_QBLOCKS * cq
    nvar = front // cq + 1
    row = lax.broadcasted_iota(jnp.int32, (rows, 1), 0)
    blk = row >> _log2(cq)

    def per_row(value):
        out = []
        for c in range(A_COPIES):
            col = jnp.zeros((rows, 1), F32)
            for r in range(A_QBLOCKS):
                col = jnp.where(blk == r, value(r * A_HALVES + _attn_half(r, c)), col)
            out.append(col)
        return out

    @pl.when(fill)
    def _():
        col = lax.broadcasted_iota(jnp.int32, (rows, lk), 1)
        dist = jnp.abs((row & (cq - 1)) + WINDOW - col).astype(F32)
        slopes = per_row(lambda h: 2.0 ** (-8.0 * (h + 1) / A_HEADS))
        for c in range(A_COPIES):
            for var in range(nvar):
                bias_ref[var, c] = jnp.where(col >= front - var * cq, -slopes[c] * dist, NEG_INF) * LOG2E

    first = lax.broadcasted_iota(jnp.int32, (cq, LANES), 1) < A_HEAD_DIM
    sks = per_row(lambda h: sinks_ref[h] * LOG2E)
    return first, sks, nvar


def _attn_chunk(qc, kv, biases, sks, first, qn, gate, cq):
    blocks = []
    for r in range(A_QBLOCKS):
        x = qc[:, r * LANES:(r + 1) * LANES]
        sq = x * x
        s_lo = jnp.sum(jnp.where(first, sq, 0.0), axis=-1, keepdims=True)
        s_hi = jnp.sum(jnp.where(first, 0.0, sq), axis=-1, keepdims=True)
        ms = jnp.where(first, s_lo, s_hi) * (1.0 / A_HEAD_DIM)
        blocks.append(x * lax.rsqrt(ms + EPS) * qn)
    outs = []
    for c in range(A_COPIES):
        lhs = jnp.concatenate(
            [jnp.where(first if _attn_half(r, c) == 0 else jnp.logical_not(first), blocks[r], 0.0)
             for r in range(A_QBLOCKS)], axis=0)
        kk, vv = kv[c]
        sc = _dot_nt(lhs.astype(BF16), kk) + biases[c]
        m = jnp.maximum(jnp.max(sc, axis=-1, keepdims=True), sks[c])
        p = jnp.exp2(sc - m)
        den = jnp.sum(p, axis=-1, keepdims=True) + jnp.exp2(sks[c] - m)
        outs.append(_dot(p.astype(BF16), vv) * (1.0 / den))
    tile = []
    for r in range(A_QBLOCKS):
        rows = slice(r * cq, (r + 1) * cq)
        low = 0 if _attn_half(r, 0) == 0 else 1
        tile.append(jnp.where(first, outs[low][rows], outs[1 - low][rows]))
    return (jnp.concatenate(tile, axis=1) * _silu(gate)).astype(BF16)


def _attn_kernel(sinks_ref, q_ref, k_ref, v_ref, ag_ref, qn_ref, o_ref, kp_ref, vp_ref, bias_ref,
                 *, bt, cq, lk, off, t):
    front = WINDOW - off
    lkt = k_ref.shape[0] // bt
    n_chunks = t // cq

    for b in range(bt):
        for src, dst in ((k_ref, kp_ref), (v_ref, vp_ref)):
            x = src[b * lkt:(b + 1) * lkt, :]
            if front:
                dst[:, b, 0:front, :] = jnp.zeros((A_COPIES, front, A_KV_WIDTH), BF16)
            dst[0, b, front:front + lkt, :] = x.astype(BF16)
            dst[1, b, front:front + lkt, :] = pltpu.roll(x, A_HEAD_DIM, 1).astype(BF16)

    first, sks, nvar = _attn_tables(sinks_ref, bias_ref, cq=cq, lk=lk, front=front, fill=pl.program_id(0) == 0)
    qn = qn_ref[...] * (A_HEAD_DIM ** -0.5 * LOG2E)

    def chunk(b, cg):
        k0 = cg * cq if isinstance(cg, int) else pl.multiple_of(cg * cq, cq)
        r0 = b * t + k0
        var = min(cg, nvar - 1) if isinstance(cg, int) else jnp.minimum(cg, nvar - 1)
        kv = [(kp_ref[c, b, pl.ds(k0, lk), :], vp_ref[c, b, pl.ds(k0, lk), :]) for c in range(A_COPIES)]
        o_ref[pl.ds(r0, cq), :] = _attn_chunk(
            q_ref[pl.ds(r0, cq), :], kv, [bias_ref[var, c] for c in range(A_COPIES)], sks, first, qn,
            ag_ref[pl.ds(r0, cq), :], cq)

    for b in range(bt):
        if n_chunks == 1:
            chunk(b, 0)
        else:
            lax.fori_loop(0, n_chunks, lambda cg, carry, b=b: (chunk(b, cg), carry)[1], 0, unroll=min(n_chunks, 8))


def _attention(z, kbuf, vbuf, kv_specs, sinks, qn, *, nb, bt, t, cq, lk, off):
    front = WINDOW - off
    lkt = kbuf.shape[0] // nb
    kern = functools.partial(_attn_kernel, bt=bt, cq=cq, lk=lk, off=off, t=t)
    return pl.pallas_call(
        kern,
        grid=(nb // bt,),
        in_specs=[
            pl.BlockSpec(memory_space=pltpu.SMEM),
            _z_window(bt * t, A_WIDTH, Z_AQ, lambda b: b),
            kv_specs[0],
            kv_specs[1],
            _z_window(bt * t, A_WIDTH, Z_AG, lambda b: b),
            pl.BlockSpec((1, LANES), lambda b: (0, 0)),
        ],
        out_specs=pl.BlockSpec((bt * t, A_WIDTH), lambda b: (b, 0)),
        out_shape=jax.ShapeDtypeStruct((nb * t, A_WIDTH), BF16),
        scratch_shapes=[
            pltpu.VMEM((A_COPIES, bt, front + lkt, A_KV_WIDTH), BF16),
            pltpu.VMEM((A_COPIES, bt, front + lkt, A_KV_WIDTH), BF16),
            pltpu.VMEM((front // cq + 1, A_COPIES, A_QBLOCKS * cq, lk), F32),
        ],
        compiler_params=pltpu.CompilerParams(dimension_semantics=("arbitrary",), vmem_limit_bytes=VMEM_LIMIT),
        name="swa_attention",
    )(sinks, z, kbuf, vbuf, z, qn)


def _bf(x):
    return x.astype(BF16)


def _unit_lower_inverse(a_list, c, ri, lj, expand):
    base = SUBLANES
    same = (ri >> _log2(base)) == (lj >> _log2(base))
    eye = jnp.where(ri == lj, 1.0, 0.0)
    n1 = [jnp.where(same, -a, 0.0) for a in a_list]
    x = [eye + n for n in n1]
    n1b = [_bf(n) for n in n1]
    n2b = [_bf(_dot(nb, expand(nb))) for nb in n1b]
    n2e = [expand(nb) for nb in n2b]
    x = [xi + _dot(_bf(xi), ne) for xi, ne in zip(x, n2e)]
    n4e = [expand(_bf(_dot(nb, ne))) for nb, ne in zip(n2b, n2e)]
    x = [xi + _dot(_bf(xi), ne) for xi, ne in zip(x, n4e)]
    s = base
    while s < c:
        sel = ((ri >> _log2(2 * s)) == (lj >> _log2(2 * s))) & ((ri >> _log2(s)) != (lj >> _log2(s)))
        xb = [_bf(xi) for xi in x]
        xo = [_dot(b, expand(_bf(jnp.where(sel, a, 0.0)))) for b, a in zip(xb, a_list)]
        x = [xi - _dot(_bf(o), expand(b)) for xi, o, b in zip(x, xo, xb)]
        s *= 2
    return x


def _delta_kernel(qkv_ref, ba_ref, bg_ref, hist_ref, s0_ref, convw_ref, gate_ref, onorm_ref,
                  o_ref, sout_ref, ext_ref, st_ref, lhs_ref, add_ref, s_ref, *, bt, c, tc):
    i = pl.program_id(1)
    hd = B_HEAD_DIM
    nh = B_HEADS
    r = nh * c
    nch = tc // c
    pad = SUBLANES
    units = [(b, cc) for b in range(bt) for cc in range(nch)]

    @pl.when(i == 0)
    def _():
        ext_ref[:, 0:pad, :] = hist_ref[...]
        s_ref[...] = s0_ref[...]

    @pl.when(i > 0)
    def _():
        ext_ref[:, 0:pad, :] = ext_ref[:, tc:tc + pad, :]

    for b in range(bt):
        ext_ref[b, pad:pad + tc, :] = qkv_ref[b * tc:(b + 1) * tc, :]

    for u, (b, cc) in enumerate(units):
        e = ext_ref[b, cc * c:cc * c + pad + c, :]
        acc = e * convw_ref[0:1, :]
        for j in range(1, CONV_W):
            acc = pltpu.roll(acc, 1, 0) + e * convw_ref[j:j + 1, :]
        y = _silu(acc[pad:, :])
        for part in range(3):
            for h in range(nh):
                blk = y[:, (part * nh + h) * hd:(part * nh + h + 1) * hd]
                if part < 2:
                    blk = blk * lax.rsqrt(jnp.sum(blk * blk, axis=-1, keepdims=True) + EPS)
                st_ref[part, u, h * c:(h + 1) * c, :] = blk

    ri = lax.broadcasted_iota(jnp.int32, (c, r), 0)
    li = lax.broadcasted_iota(jnp.int32, (c, r), 1)
    lj = li & (c - 1)
    lh = li >> _log2(c)
    lower = ri >= lj
    strict = ri > lj
    head_sel = [jnp.where(lh == h, 1.0, 0.0).astype(BF16) for h in range(nh)]

    def expand(xb):
        return jnp.concatenate([xb * m for m in head_sel], axis=0)

    def lanes(x_st):
        return jnp.concatenate([x_st[h * c:(h + 1) * c] for h in range(nh)], axis=1)

    def head_blocks(x_st):
        zero = jnp.zeros((c, hd), x_st.dtype)
        return jnp.concatenate(
            [jnp.concatenate([x_st[h * c:(h + 1) * c] if h2 == h else zero for h2 in range(nh)], axis=1)
             for h in range(nh)], axis=0)

    ci = lax.broadcasted_iota(jnp.int32, (c, c), 0)
    cj = lax.broadcasted_iota(jnp.int32, (c, c), 1)
    tril = jnp.where(ci >= cj, 1.0, 0.0)
    di = lax.broadcasted_iota(jnp.int32, (hd, hd), 0)
    dj = lax.broadcasted_iota(jnp.int32, (hd, hd), 1)
    eye_hd = di == dj
    neg_rate = -jnp.exp(gate_ref[0:1, :])
    dt_bias = gate_ref[1:2, :]
    onorm = onorm_ref[...]

    def stack_cols(x, lane0):
        return jnp.concatenate([x[:, lane0 + h:lane0 + h + 1] for h in range(nh)], axis=0)

    beta, gc, gcc, gr, glast = [], [], [], [], []
    for b, cc in units:
        ba = ba_ref[b * tc + cc * c:b * tc + (cc + 1) * c, :]
        g_all = neg_rate * _softplus(ba + dt_bias)
        gcum = _dot(tril, g_all, HIGHEST)
        gcum_t = gcum.T
        beta.append(stack_cols(_sigmoid(ba), 0))
        gc.append(stack_cols(gcum, GATE_LANE))
        gcc_h = jnp.zeros((c, r), F32)
        for h in range(nh):
            gcc_h = jnp.where(lh == h, gcum[:, GATE_LANE + h:GATE_LANE + h + 1], gcc_h)
        gcc.append(gcc_h)
        gr.append(jnp.concatenate([gcum_t[GATE_LANE + h:GATE_LANE + h + 1, :] for h in range(nh)], axis=1))
        glast.append(jnp.concatenate(
            [jnp.broadcast_to(gcum[c - 1:c, GATE_LANE + h:GATE_LANE + h + 1], (c, 1)) for h in range(nh)], axis=0))

    def fold_units(us):
        n = range(len(us))
        q = [st_ref[0, u] * (hd ** -0.5) for u in us]
        k = [st_ref[1, u] for u in us]
        v = [st_ref[2, u] for u in us]
        kb = [k[j] * beta[u] for j, u in enumerate(us)]
        big = [_dot_nt(_bf(jnp.concatenate([lanes(kb[j]), lanes(q[j])], axis=0)), head_blocks(_bf(k[j]))) for j in n]
        decay = [jnp.where(lower, jnp.exp(jnp.where(lower, gcc[u] - gr[u], 0.0)), 0.0) for u in us]
        a = [jnp.where(strict, big[j][:c] * decay[j], 0.0) for j in n]
        qkd = [expand(_bf(big[j][c:] * decay[j])) for j in n]
        tinv = _unit_lower_inverse(a, c, ri, lj, expand)
        eg = [jnp.exp(gc[u]) for u in us]
        rhs = [_bf(jnp.concatenate([v[j] * beta[u], kb[j] * eg[j]], axis=1)) for j, u in enumerate(us)]
        sol = [_dot(expand(_bf(tinv[j])), rhs[j]) for j in n]
        solb = [_bf(s) for s in sol]
        fold = [_dot(qkd[j], solb[j]) for j in n]
        k_dec = [_bf(k[j] * jnp.exp(glast[u] - gc[u])) for j, u in enumerate(us)]
        for j, u in enumerate(us):
            qp = q[j] * eg[j] - fold[j][:, hd:]
            for h in range(nh):
                rows = slice(h * c, (h + 1) * c)
                kt = _dot_tn(k_dec[j][rows], solb[j][rows])
                g_tot = jnp.exp(glast[u][h * c:h * c + 1, :])
                lhs_ref[u, h, 0:hd, :] = _bf(jnp.where(eye_hd, g_tot, 0.0) - kt[:, hd:])
                lhs_ref[u, h, hd:hd + c, :] = _bf(qp[rows])
                add_ref[u, h, 0:hd, :] = kt[:, :hd]
                add_ref[u, h, hd:hd + c, :] = fold[j][rows, :hd]

    fold_units(list(range(len(units))))

    for u, (b, cc) in enumerate(units):
        rows = slice(b * tc + cc * c, b * tc + (cc + 1) * c)
        for h in range(nh):
            res = _dot(lhs_ref[u, h], _bf(s_ref[b, h])) + add_ref[u, h]
            s_ref[b, h] = res[:hd]
            o = res[hd:]
            on = o * lax.rsqrt(jnp.mean(o * o, axis=-1, keepdims=True) + EPS) * onorm
            bg = bg_ref[rows, h * hd:(h + 1) * hd]
            o_ref[rows, h * hd:(h + 1) * hd] = (on * _silu(bg)).astype(BF16)
    sout_ref[...] = s_ref[...]


def _delta(z, hist, s0, convw, gate, onorm, *, nb, bt, t, tc, c):
    nt = t // tc
    units = bt * (tc // c)
    state = (bt, B_HEADS, B_HEAD_DIM, B_HEAD_DIM)
    kern = functools.partial(_delta_kernel, bt=bt, c=c, tc=tc)
    return pl.pallas_call(
        kern,
        grid=(nb // bt, nt),
        in_specs=[
            _z_window(bt * tc, 3 * B_WIDTH, Z_BQKV, lambda b, i: b * nt + i),
            _z_window(bt * tc, LANES, Z_BA, lambda b, i: b * nt + i),
            _z_window(bt * tc, B_WIDTH, Z_BG, lambda b, i: b * nt + i),
            pl.BlockSpec((bt, SUBLANES, 3 * B_WIDTH), lambda b, i: (b, 0, 0)),
            pl.BlockSpec(state, lambda b, i: (b, 0, 0, 0)),
            pl.BlockSpec((CONV_W, 3 * B_WIDTH), lambda b, i: (0, 0)),
            pl.BlockSpec((2, LANES), lambda b, i: (0, 0)),
            pl.BlockSpec((1, B_HEAD_DIM), lambda b, i: (0, 0)),
        ],
        out_specs=[
            pl.BlockSpec((bt * tc, B_WIDTH), lambda b, i: (b * nt + i, 0)),
            pl.BlockSpec(state, lambda b, i: (b, 0, 0, 0)),
        ],
        out_shape=[
            jax.ShapeDtypeStruct((nb * t, B_WIDTH), BF16),
            jax.ShapeDtypeStruct((nb, B_HEADS, B_HEAD_DIM, B_HEAD_DIM), F32),
        ],
        scratch_shapes=[
            pltpu.VMEM((bt, SUBLANES + tc, 3 * B_WIDTH), F32),
            pltpu.VMEM((3, units, B_HEADS * c, B_HEAD_DIM), F32),
            pltpu.VMEM((units, B_HEADS, B_HEAD_DIM + c, B_HEAD_DIM), BF16),
            pltpu.VMEM((units, B_HEADS, B_HEAD_DIM + c, B_HEAD_DIM), F32),
            pltpu.VMEM(state, F32),
        ],
        compiler_params=pltpu.CompilerParams(dimension_semantics=("arbitrary", "arbitrary"),
                                             vmem_limit_bytes=VMEM_LIMIT),
        name="gated_delta",
    )(z, z, z, hist, s0, convw, gate, onorm)


def _pool_kernel(h_ref, oa_ref, ob_ref, wa_ref, wb_ref, hist_ref, g_ref, win_ref, wgrp_ref, scale_ref, wout_ref,
                 y_ref, tail_ref, ext_ref, *, bt, tt, pos0):
    i = pl.program_id(1)
    pad = POOL_HIST + 1
    rows = bt * tt

    @pl.when(i == 0)
    def _():
        pos = pos0 - pad + lax.broadcasted_iota(jnp.int32, (1, pad, 1), 1)
        ext_ref[:, 0:pad, :] = jnp.where(pos >= 0, hist_ref[...], 0.0)

    @pl.when(i > 0)
    def _():
        ext_ref[:, 0:pad, :] = ext_ref[:, tt:tt + pad, :]

    x = h_ref[...] + _dot(oa_ref[...], wa_ref[...]) + _dot(ob_ref[...], wb_ref[...])
    xn = x * lax.rsqrt(jnp.mean(x * x, axis=-1, keepdims=True) + EPS) * g_ref[...]
    z = _dot(xn.astype(BF16), win_ref[...])
    u = z[:, :C_WIDTH]
    gate = z[:, C_WIDTH:]
    ext_ref[:, pad:pad + tt, :] = u.reshape(bt, tt, C_WIDTH)
    tail_ref[...] = ext_ref[:, tt:tt + pad, :]

    tpos = pos0 + i * tt + (lax.broadcasted_iota(jnp.int32, (rows, 1), 0) & (tt - 1))
    mixed = []
    for gi, w in enumerate(POOL_SIZES):
        cols = slice(gi * C_GROUP, (gi + 1) * C_GROUP)
        s = ext_ref[:, :, cols].reshape(bt * (pad + tt), C_GROUP)
        sh = 1
        while sh < w:
            s = s + pltpu.roll(s, sh, 0)
            sh *= 2
        s = s.reshape(bt, pad + tt, C_GROUP)[:, pad:, :].reshape(rows, C_GROUP)
        cnt = jnp.minimum(tpos + 1, w).astype(F32)
        pooled = s / cnt - u[:, cols]
        m = _dot(pooled.astype(BF16), wgrp_ref[gi]) * scale_ref[:, cols]
        mixed.append((m * _silu(gate[:, cols])).astype(BF16))
    y_ref[...] = x + _dot(jnp.concatenate(mixed, axis=1), wout_ref[...])


def _pool_layer(h, oa, ob, wa, wb, hist, g, win, wgrp, scale, wout, *, nb, bt, t, tt, pos0):
    nt = t // tt
    pad = POOL_HIST + 1
    rows = bt * tt
    kern = functools.partial(_pool_kernel, bt=bt, tt=tt, pos0=pos0)
    tile = lambda b, i: (b * nt + i, 0)
    const2 = lambda b, i: (0, 0)
    once = pl.Buffered(1)
    return pl.pallas_call(
        kern,
        grid=(nb // bt, nt),
        in_specs=[
            pl.BlockSpec((rows, D_MODEL), tile),
            pl.BlockSpec((rows, A_WIDTH), tile),
            pl.BlockSpec((rows, B_WIDTH), tile),
            pl.BlockSpec((A_WIDTH, D_MODEL), const2, pipeline_mode=once),
            pl.BlockSpec((B_WIDTH, D_MODEL), const2, pipeline_mode=once),
            pl.BlockSpec((bt, pad, C_WIDTH), lambda b, i: (b, 0, 0)),
            pl.BlockSpec((1, D_MODEL), const2),
            pl.BlockSpec((D_MODEL, 2 * C_WIDTH), const2, pipeline_mode=once),
            pl.BlockSpec((len(POOL_SIZES), C_GROUP, C_GROUP), lambda b, i: (0, 0, 0), pipeline_mode=once),
            pl.BlockSpec((1, C_WIDTH), const2),
            pl.BlockSpec((C_WIDTH, D_MODEL), const2, pipeline_mode=once),
        ],
        out_specs=[
            pl.BlockSpec((rows, D_MODEL), tile),
            pl.BlockSpec((bt, pad, C_WIDTH), lambda b, i: (b, 0, 0)),
        ],
        out_shape=[
            jax.ShapeDtypeStruct((nb * t, D_MODEL), F32),
            jax.ShapeDtypeStruct((nb, pad, C_WIDTH), F32),
        ],
        scratch_shapes=[pltpu.VMEM((bt, pad + tt, C_WIDTH), F32)],
        compiler_params=pltpu.CompilerParams(dimension_semantics=("arbitrary", "arbitrary"),
                                             vmem_limit_bytes=VMEM_LIMIT),
        name="out_proj_pool_layer",
    )(h, oa, ob, wa, wb, hist, g, win, wgrp, scale, wout)


def _ab_weights(norm_g, w_in, q_norm, k_norm, sinks, conv_w, a_log, dt_bias, o_norm, w_out):
    w = jnp.swapaxes(w_in, 0, 1).astype(BF16)
    lane_pad = (GATE_LANE, LANES - GATE_LANE - B_HEADS)
    gate = jnp.stack([jnp.pad(a_log.astype(F32), lane_pad), jnp.pad(dt_bias.astype(F32), lane_pad)])
    return dict(
        norm_g=norm_g.reshape(1, D_MODEL), w=w, kn=jnp.tile(k_norm, A_KV_HEADS).reshape(1, A_KV_WIDTH),
        qn=jnp.tile(q_norm, A_KV_HEADS).reshape(1, LANES), sinks=sinks.astype(F32), conv_w=conv_w, gate=gate,
        onorm=o_norm.reshape(1, B_HEAD_DIM), wa=w_out[:A_WIDTH].astype(BF16), wb=w_out[A_WIDTH:].astype(BF16))


def _group_tiles(t):
    if t >= CHUNK:
        return dict(proj_rows=1024, seqs_attn=1, seqs_delta=1, delta_tokens=1024, chunk=CHUNK, seqs_pool=1,
                    pool_tokens=1024)
    return dict(proj_rows=256, seqs_attn=32, seqs_delta=16, delta_tokens=t, chunk=t, seqs_pool=256 // t, pool_tokens=t)


def _ab_mixers(h, wts, cache_k, cache_v, s0, conv_hist, *, nb, t):
    tl = _group_tiles(t)
    z = _in_proj(h, wts["norm_g"], wts["w"], wts["kn"], min(nb * t, tl["proj_rows"]))
    bt = tl["seqs_attn"]
    if cache_k is None:
        cq, lk, off = CHUNK, WINDOW + CHUNK, 0
        kv_specs = [_z_window(bt * t, A_KV_WIDTH, Z_AK, lambda b: b), _z_window(bt * t, A_KV_WIDTH, Z_AV, lambda b: b)]
        kbuf = vbuf = z
    else:
        cq, lk, off = t, WINDOW + t, WINDOW
        k_new = z[:, Z_AK:Z_AK + A_KV_WIDTH].reshape(nb, t, A_KV_WIDTH)
        v_new = z[:, Z_AV:Z_AV + A_KV_WIDTH].reshape(nb, t, A_KV_WIDTH)
        kbuf = jnp.concatenate([cache_k.reshape(nb, WINDOW, A_KV_WIDTH), k_new], axis=1).reshape(nb * lk, A_KV_WIDTH)
        vbuf = jnp.concatenate([cache_v.reshape(nb, WINDOW, A_KV_WIDTH), v_new], axis=1).reshape(nb * lk, A_KV_WIDTH)
        kv_specs = [pl.BlockSpec((bt * lk, A_KV_WIDTH), lambda b: (b, 0))] * 2
    o_a = _attention(z, kbuf, vbuf, kv_specs, wts["sinks"], wts["qn"], nb=nb, bt=bt, t=t, cq=cq, lk=lk, off=off)
    hist = jnp.pad(conv_hist.astype(F32), ((0, 0), (SUBLANES - (CONV_W - 1), 0), (0, 0)))
    o_b, s_new = _delta(z, hist, s0.astype(F32), wts["conv_w"], wts["gate"], wts["onorm"], nb=nb,
                        bt=tl["seqs_delta"], t=t, tc=tl["delta_tokens"], c=tl["chunk"])
    return o_a, o_b, z, s_new


def _last_rows(z, col, width, nb, t, rows):
    return z.reshape(nb, t, Z_WIDTH)[:, t - rows:, col:col + width]


def _cache_rows(z, col, nb, t, rows):
    return _last_rows(z, col, A_KV_WIDTH, nb, t, rows).reshape(nb, rows, A_KV_HEADS, A_HEAD_DIM)


def kernel(x_prompt, x_sample, cache_a_k, cache_a_v, state_b_s, state_b_conv, state_c_pool,
           norm_ab, w_in_ab, q_norm_a, k_norm_a, sinks_a, conv_b, a_log_b, dt_bias_b, o_norm_b, w_out_ab,
           norm_c, w_in_c, w_grp_c, scale_c, w_out_c):
    bp, tp, _ = x_prompt.shape
    bs, ts, _ = x_sample.shape
    hp = x_prompt.reshape(bp * tp, D_MODEL)
    hs = x_sample.reshape(bs * ts, D_MODEL)

    wts = _ab_weights(norm_ab[0], w_in_ab[0], q_norm_a[0], k_norm_a[0], sinks_a[0], conv_b[0], a_log_b[0],
                      dt_bias_b[0], o_norm_b[0], w_out_ab[0])
    s0 = jnp.zeros((bp, B_HEADS, B_HEAD_DIM, B_HEAD_DIM), F32)
    c0 = jnp.zeros((bp, CONV_W - 1, 3 * B_WIDTH), F32)
    oap, obp, zp, sp = _ab_mixers(hp, wts, None, None, s0, c0, nb=bp, t=tp)
    oas, obs, zs, ss = _ab_mixers(hs, wts, cache_a_k[0], cache_a_v[0], state_b_s[0], state_b_conv[0], nb=bs, t=ts)
    p_a_k = _cache_rows(zp, Z_AK, bp, tp, WINDOW)[None]
    p_a_v = _cache_rows(zp, Z_AV, bp, tp, WINDOW)[None]
    s_a_k = _cache_rows(zs, Z_AK, bs, ts, ts)[None]
    s_a_v = _cache_rows(zs, Z_AV, bs, ts, ts)[None]
    p_b_conv = _last_rows(zp, Z_BQKV, 3 * B_WIDTH, bp, tp, CONV_W - 1)[None]
    s_b_conv = _last_rows(zs, Z_BQKV, 3 * B_WIDTH, bs, ts, CONV_W - 1)[None]

    g_c = norm_c[0].reshape(1, D_MODEL)
    win = w_in_c[0].astype(BF16)
    wgrp = w_grp_c[0].astype(BF16)
    scale = scale_c[0].reshape(1, C_WIDTH)
    wout = w_out_c[0].astype(BF16)
    h0 = jnp.zeros((bp, POOL_HIST + 1, C_WIDTH), F32)
    hs0 = jnp.pad(state_c_pool[0].astype(F32), ((0, 0), (1, 0), (0, 0)))
    tlp, tls = _group_tiles(tp), _group_tiles(ts)
    yp, tail_p = _pool_layer(hp, oap, obp, wts["wa"], wts["wb"], h0, g_c, win, wgrp, scale, wout, nb=bp,
                             bt=tlp["seqs_pool"], t=tp, tt=tlp["pool_tokens"], pos0=0)
    ys, tail_s = _pool_layer(hs, oas, obs, wts["wa"], wts["wb"], hs0, g_c, win, wgrp, scale, wout, nb=bs,
                             bt=tls["seqs_pool"], t=ts, tt=tls["pool_tokens"], pos0=PAST_LEN)

    return (yp.reshape(bp, tp, D_MODEL), ys.reshape(bs, ts, D_MODEL),
            p_a_k, p_a_v, sp[None], p_b_conv, tail_p[:, 1:][None],
            s_a_k, s_a_v, ss[None], s_b_conv, tail_s[:, 1:][None])
```

```python
import functools

import jax
import jax.numpy as jnp
from jax import lax
from jax.experimental import pallas as pl
from jax.experimental.pallas import tpu as pltpu

F32 = jnp.float32
BF16 = jnp.bfloat16
HIGHEST = lax.Precision.HIGHEST

D_MODEL = 1024
CHUNK = 64
PAST_LEN = 2048
EPS = 1e-6
NEG_INF = -1e30
LOG2E = 1.4426950408889634

A_HEADS = 8
A_KV_HEADS = 2
A_HEAD_DIM = 64
A_WIDTH = A_HEADS * A_HEAD_DIM
A_KV_WIDTH = A_KV_HEADS * A_HEAD_DIM
A_REP = A_HEADS // A_KV_HEADS
WINDOW = 128

B_HEADS = 4
B_HEAD_DIM = 128
B_WIDTH = B_HEADS * B_HEAD_DIM
CONV_W = 4

POOL_SIZES = (2, 4, 8, 16)
C_WIDTH = D_MODEL
C_GROUP = C_WIDTH // len(POOL_SIZES)
POOL_HIST = max(POOL_SIZES) - 1

LANES = 128
SUBLANES = 8

Z_AQ = 0
Z_AK = Z_AQ + A_WIDTH
Z_AV = Z_AK + A_KV_WIDTH
Z_AG = Z_AV + A_KV_WIDTH
Z_BQKV = Z_AG + A_WIDTH
Z_BG = Z_BQKV + 3 * B_WIDTH
Z_BA = Z_BG + B_WIDTH
Z_WIDTH = Z_BA + LANES
GATE_LANE = B_HEADS


def _z_window(rows, width, col, row_block):
    return pl.BlockSpec((pl.Element(rows), pl.Element(width)), lambda *idx: (row_block(*idx) * rows, col))

VMEM_LIMIT = 48 * 1024 * 1024


def _sigmoid(x):
    return 1.0 / (1.0 + jnp.exp(-x))


def _silu(x):
    return x * _sigmoid(x)


def _softplus(x):
    return jnp.maximum(x, 0.0) + jnp.log(1.0 + jnp.exp(-jnp.abs(x)))


def _dot(a, b, precision=None):
    return jnp.dot(a, b, preferred_element_type=F32, precision=precision)


def _dot_nt(a, b, precision=None):
    return lax.dot_general(a, b, (((1,), (1,)), ((), ())), preferred_element_type=F32, precision=precision)


def _dot_tn(a, b, precision=None):
    return lax.dot_general(a, b, (((0,), (0,)), ((), ())), preferred_element_type=F32, precision=precision)


def _log2(n):
    assert n & (n - 1) == 0
    return n.bit_length() - 1


def _in_proj_kernel(x_ref, g_ref, w_ref, kn_ref, z_ref):
    x = x_ref[...]
    xn = x * lax.rsqrt(jnp.mean(x * x, axis=-1, keepdims=True) + EPS) * g_ref[...]
    n_in = w_ref.shape[0]
    z_ref[:, Z_BA:Z_WIDTH] = jnp.zeros((x.shape[0], Z_WIDTH - Z_BA), F32)
    z_ref[:, 0:n_in] = _dot_nt(xn.astype(BF16), w_ref[...])
    k = z_ref[:, Z_AK:Z_AK + A_KV_WIDTH]
    sq = k * k
    lane = lax.broadcasted_iota(jnp.int32, k.shape, 1)
    first = lane < A_HEAD_DIM
    s0 = jnp.sum(jnp.where(first, sq, 0.0), axis=-1, keepdims=True)
    s1 = jnp.sum(jnp.where(first, 0.0, sq), axis=-1, keepdims=True)
    ms = jnp.where(first, s0, s1) * (1.0 / A_HEAD_DIM)
    z_ref[:, Z_AK:Z_AK + A_KV_WIDTH] = k * lax.rsqrt(ms + EPS) * kn_ref[...]


def _in_proj(x, g, w, kn, tm):
    n = x.shape[0]
    return pl.pallas_call(
        _in_proj_kernel,
        grid=(n // tm,),
        in_specs=[
            pl.BlockSpec((tm, D_MODEL), lambda i: (i, 0)),
            pl.BlockSpec((1, D_MODEL), lambda i: (0, 0)),
            pl.BlockSpec(w.shape, lambda i: (0, 0), pipeline_mode=pl.Buffered(1)),
            pl.BlockSpec((1, A_KV_WIDTH), lambda i: (0, 0)),
        ],
        out_specs=pl.BlockSpec((tm, Z_WIDTH), lambda i: (i, 0)),
        out_shape=jax.ShapeDtypeStruct((n, Z_WIDTH), F32),
        compiler_params=pltpu.CompilerParams(dimension_semantics=("arbitrary",), vmem_limit_bytes=VMEM_LIMIT),
        name="in_proj",
    )(x, g, w, kn)


A_HALVES = LANES // A_HEAD_DIM
A_QBLOCKS = A_WIDTH // LANES
A_COPIES = A_KV_HEADS


def _attn_half(r, c):
    g = r * A_HALVES // A_REP
    return (g + c) % A_HALVES


def _attn_tables(sinks_ref, bias_ref, *, cq, lk, front, slack, fill):
    rows = A_QBLOCKS * cq
    nvar = front // cq + 1
    row = lax.broadcasted_iota(jnp.int32, (rows, 1), 0)
    blk = row >> _log2(cq)

    def per_row(value):
        out = []
        for c in range(A_COPIES):
            col = jnp.zeros((rows, 1), F32)
            for r in range(A_QBLOCKS):
                col = jnp.where(blk == r, value(r * A_HALVES + _attn_half(r, c)), col)
            out.append(col)
        return out

    @pl.when(fill)
    def _():
        col = lax.broadcasted_iota(jnp.int32, (rows, lk + slack), 1)
        dist = jnp.abs((row & (cq - 1)) + WINDOW - col).astype(F32)
        slopes = per_row(lambda h: 2.0 ** (-8.0 * (h + 1) / A_HEADS))
        sinks = per_row(lambda h: sinks_ref[h])
        for c in range(A_COPIES):
            for var in range(nvar):
                keys = jnp.where((col >= front - var * cq) & (col < lk), -slopes[c] * dist, NEG_INF)
                bias_ref[var, c] = jnp.where(col == lk, sinks[c], keys) * LOG2E

    first = lax.broadcasted_iota(jnp.int32, (cq, LANES), 1) < A_HEAD_DIM
    return first, nvar


def _attn_chunk(qc, kv, biases, first, qn, gate, cq):
    blocks = []
    for r in range(A_QBLOCKS):
        x = qc[:, r * LANES:(r + 1) * LANES]
        sq = x * x
        s_lo = jnp.sum(jnp.where(first, sq, 0.0), axis=-1, keepdims=True)
        s_hi = jnp.sum(jnp.where(first, 0.0, sq), axis=-1, keepdims=True)
        ms = jnp.where(first, s_lo, s_hi) * (1.0 / A_HEAD_DIM)
        blocks.append(x * lax.rsqrt(ms + EPS) * qn)
    outs = []
    for c in range(A_COPIES):
        lhs = jnp.concatenate(
            [jnp.where(first if _attn_half(r, c) == 0 else jnp.logical_not(first), blocks[r], 0.0)
             for r in range(A_QBLOCKS)], axis=0)
        zero_keys = jnp.zeros((biases[c].shape[1] - kv[c][0].shape[0], A_KV_WIDTH), BF16)
        kk, vv = (jnp.concatenate([x, zero_keys], axis=0) for x in kv[c])
        sc = _dot_nt(lhs.astype(BF16), kk) + biases[c]
        p = jnp.exp2(sc - jnp.max(sc, axis=-1, keepdims=True))
        outs.append(_dot(p.astype(BF16), vv) * (1.0 / jnp.sum(p, axis=-1, keepdims=True)))
    tile = []
    for r in range(A_QBLOCKS):
        rows = slice(r * cq, (r + 1) * cq)
        low = 0 if _attn_half(r, 0) == 0 else 1
        tile.append(jnp.where(first, outs[low][rows], outs[1 - low][rows]))
    return (jnp.concatenate(tile, axis=1) * _silu(gate)).astype(BF16)


def _attn_kernel(sinks_ref, q_ref, k_ref, v_ref, ag_ref, qn_ref, o_ref, kp_ref, vp_ref, bias_ref,
                 *, bt, cq, lk, off, t):
    front = WINDOW - off
    lkt = k_ref.shape[0] // bt
    n_chunks = t // cq

    for b in range(bt):
        for src, dst in ((k_ref, kp_ref), (v_ref, vp_ref)):
            x = src[b * lkt:(b + 1) * lkt, :]
            if front:
                dst[:, b, 0:front, :] = jnp.zeros((A_COPIES, front, A_KV_WIDTH), BF16)
            dst[0, b, front:front + lkt, :] = x.astype(BF16)
            dst[1, b, front:front + lkt, :] = pltpu.roll(x, A_HEAD_DIM, 1).astype(BF16)

    first, nvar = _attn_tables(sinks_ref, bias_ref, cq=cq, lk=lk, front=front, slack=bias_ref.shape[-1] - lk,
                               fill=pl.program_id(0) == 0)
    qn = qn_ref[...] * (A_HEAD_DIM ** -0.5 * LOG2E)

    def chunk(b, cg):
        k0 = cg * cq if isinstance(cg, int) else pl.multiple_of(cg * cq, cq)
        r0 = b * t + k0
        var = min(cg, nvar - 1) if isinstance(cg, int) else jnp.minimum(cg, nvar - 1)
        kv = [(kp_ref[c, b, pl.ds(k0, lk), :], vp_ref[c, b, pl.ds(k0, lk), :]) for c in range(A_COPIES)]
        o_ref[pl.ds(r0, cq), :] = _attn_chunk(
            q_ref[pl.ds(r0, cq), :], kv, [bias_ref[var, c] for c in range(A_COPIES)], first, qn,
            ag_ref[pl.ds(r0, cq), :], cq)

    for b in range(bt):
        if n_chunks == 1:
            chunk(b, 0)
        else:
            lax.fori_loop(0, n_chunks, lambda cg, carry, b=b: (chunk(b, cg), carry)[1], 0, unroll=min(n_chunks, 8))


def _attention(z, kbuf, vbuf, kv_specs, sinks, qn, *, nb, bt, t, cq, lk, off):
    front = WINDOW - off
    lkt = kbuf.shape[0] // nb
    kern = functools.partial(_attn_kernel, bt=bt, cq=cq, lk=lk, off=off, t=t)
    return pl.pallas_call(
        kern,
        grid=(nb // bt,),
        in_specs=[
            pl.BlockSpec(memory_space=pltpu.SMEM),
            _z_window(bt * t, A_WIDTH, Z_AQ, lambda b: b),
            kv_specs[0],
            kv_specs[1],
            _z_window(bt * t, A_WIDTH, Z_AG, lambda b: b),
            pl.BlockSpec((1, LANES), lambda b: (0, 0)),
        ],
        out_specs=pl.BlockSpec((bt * t, A_WIDTH), lambda b: (b, 0)),
        out_shape=jax.ShapeDtypeStruct((nb * t, A_WIDTH), BF16),
        scratch_shapes=[
            pltpu.VMEM((A_COPIES, bt, front + lkt, A_KV_WIDTH), BF16),
            pltpu.VMEM((A_COPIES, bt, front + lkt, A_KV_WIDTH), BF16),
            pltpu.VMEM((front // cq + 1, A_COPIES, A_QBLOCKS * cq, lk + LANES - lk % LANES), F32),
        ],
        compiler_params=pltpu.CompilerParams(dimension_semantics=("arbitrary",), vmem_limit_bytes=VMEM_LIMIT),
        name="swa_attention",
    )(sinks, z, kbuf, vbuf, z, qn)


def _bf(x):
    return x.astype(BF16)


def _unit_lower_inverse(a_list, c, ri, lj, expand):
    base = SUBLANES
    same = (ri >> _log2(base)) == (lj >> _log2(base))
    eye = jnp.where(ri == lj, 1.0, 0.0)
    n1 = [jnp.where(same, -a, 0.0) for a in a_list]
    x = [eye + n for n in n1]
    n1b = [_bf(n) for n in n1]
    n2b = [_bf(_dot(nb, expand(nb))) for nb in n1b]
    n2e = [expand(nb) for nb in n2b]
    x = [xi + _dot(_bf(xi), ne) for xi, ne in zip(x, n2e)]
    n4e = [expand(_bf(_dot(nb, ne))) for nb, ne in zip(n2b, n2e)]
    x = [xi + _dot(_bf(xi), ne) for xi, ne in zip(x, n4e)]
    s = base
    while s < c:
        sel = ((ri >> _log2(2 * s)) == (lj >> _log2(2 * s))) & ((ri >> _log2(s)) != (lj >> _log2(s)))
        xb = [_bf(xi) for xi in x]
        xo = [_dot(b, expand(_bf(jnp.where(sel, a, 0.0)))) for b, a in zip(xb, a_list)]
        x = [xi - _dot(_bf(o), expand(b)) for xi, o, b in zip(x, xo, xb)]
        s *= 2
    return x


def _delta_kernel(qkv_ref, ba_ref, bg_ref, hist_ref, s0_ref, convw_ref, gate_ref, onorm_ref,
                  o_ref, sout_ref, ext_ref, st_ref, lhs_ref, add_ref, s_ref, *, bt, c, tc):
    i = pl.program_id(1)
    hd = B_HEAD_DIM
    nh = B_HEADS
    r = nh * c
    nch = tc // c
    pad = SUBLANES
    units = [(b, cc) for b in range(bt) for cc in range(nch)]

    @pl.when(i == 0)
    def _():
        ext_ref[:, 0:pad, :] = hist_ref[...]
        s_ref[...] = s0_ref[...]

    @pl.when(i > 0)
    def _():
        ext_ref[:, 0:pad, :] = ext_ref[:, tc:tc + pad, :]

    for b in range(bt):
        ext_ref[b, pad:pad + tc, :] = qkv_ref[b * tc:(b + 1) * tc, :]

    for u, (b, cc) in enumerate(units):
        e = ext_ref[b, cc * c:cc * c + pad + c, :]
        acc = e * convw_ref[0:1, :]
        for j in range(1, CONV_W):
            acc = pltpu.roll(acc, 1, 0) + e * convw_ref[j:j + 1, :]
        y = _silu(acc[pad:, :])
        for part in range(3):
            for h in range(nh):
                blk = y[:, (part * nh + h) * hd:(part * nh + h + 1) * hd]
                if part < 2:
                    blk = blk * lax.rsqrt(jnp.sum(blk * blk, axis=-1, keepdims=True) + EPS)
                st_ref[part, u, h * c:(h + 1) * c, :] = blk

    ri = lax.broadcasted_iota(jnp.int32, (c, r), 0)
    li = lax.broadcasted_iota(jnp.int32, (c, r), 1)
    lj = li & (c - 1)
    lh = li >> _log2(c)
    lower = ri >= lj
    strict = ri > lj
    head_sel = [jnp.where(lh == h, 1.0, 0.0).astype(BF16) for h in range(nh)]

    def expand(xb):
        return jnp.concatenate([xb * m for m in head_sel], axis=0)

    def lanes(x_st):
        return jnp.concatenate([x_st[h * c:(h + 1) * c] for h in range(nh)], axis=1)

    def head_blocks(x_st):
        zero = jnp.zeros((c, hd), x_st.dtype)
        return jnp.concatenate(
            [jnp.concatenate([x_st[h * c:(h + 1) * c] if h2 == h else zero for h2 in range(nh)], axis=1)
             for h in range(nh)], axis=0)

    ci = lax.broadcasted_iota(jnp.int32, (c, c), 0)
    cj = lax.broadcasted_iota(jnp.int32, (c, c), 1)
    tril = jnp.where(ci >= cj, 1.0, 0.0)
    di = lax.broadcasted_iota(jnp.int32, (hd, hd), 0)
    dj = lax.broadcasted_iota(jnp.int32, (hd, hd), 1)
    eye_hd = di == dj
    neg_rate = -jnp.exp(gate_ref[0:1, :])
    dt_bias = gate_ref[1:2, :]
    onorm = onorm_ref[...]

    def stack_cols(x, lane0):
        return jnp.concatenate([x[:, lane0 + h:lane0 + h + 1] for h in range(nh)], axis=0)

    beta, gc, gcc, gr, glast = [], [], [], [], []
    for b, cc in units:
        ba = ba_ref[b * tc + cc * c:b * tc + (cc + 1) * c, :]
        g_all = neg_rate * _softplus(ba + dt_bias)
        gcum = _dot(tril, g_all, HIGHEST)
        gcum_t = gcum.T
        beta.append(stack_cols(_sigmoid(ba), 0))
        gc.append(stack_cols(gcum, GATE_LANE))
        gcc_h = jnp.zeros((c, r), F32)
        for h in range(nh):
            gcc_h = jnp.where(lh == h, gcum[:, GATE_LANE + h:GATE_LANE + h + 1], gcc_h)
        gcc.append(gcc_h)
        gr.append(jnp.concatenate([gcum_t[GATE_LANE + h:GATE_LANE + h + 1, :] for h in range(nh)], axis=1))
        glast.append(jnp.concatenate(
            [jnp.broadcast_to(gcum[c - 1:c, GATE_LANE + h:GATE_LANE + h + 1], (c, 1)) for h in range(nh)], axis=0))

    def fold_units(us):
        n = range(len(us))
        q = [st_ref[0, u] * (hd ** -0.5) for u in us]
        k = [st_ref[1, u] for u in us]
        v = [st_ref[2, u] for u in us]
        kb = [k[j] * beta[u] for j, u in enumerate(us)]
        big = [_dot_nt(_bf(jnp.concatenate([lanes(kb[j]), lanes(q[j])], axis=0)), head_blocks(_bf(k[j]))) for j in n]
        decay = [jnp.where(lower, jnp.exp(jnp.where(lower, gcc[u] - gr[u], 0.0)), 0.0) for u in us]
        a = [jnp.where(strict, big[j][:c] * decay[j], 0.0) for j in n]
        qkd = [expand(_bf(big[j][c:] * decay[j])) for j in n]
        tinv = _unit_lower_inverse(a, c, ri, lj, expand)
        eg = [jnp.exp(gc[u]) for u in us]
        rhs = [_bf(jnp.concatenate([v[j] * beta[u], kb[j] * eg[j]], axis=1)) for j, u in enumerate(us)]
        sol = [_dot(expand(_bf(tinv[j])), rhs[j]) for j in n]
        solb = [_bf(s) for s in sol]
        fold = [_dot(qkd[j], solb[j]) for j in n]
        k_dec = [_bf(k[j] * jnp.exp(glast[u] - gc[u])) for j, u in enumerate(us)]
        for j, u in enumerate(us):
            qp = q[j] * eg[j] - fold[j][:, hd:]
            for h in range(nh):
                rows = slice(h * c, (h + 1) * c)
                kt = _dot_tn(k_dec[j][rows], solb[j][rows])
                g_tot = jnp.exp(glast[u][h * c:h * c + 1, :])
                lhs_ref[u, h, 0:hd, :] = _bf(jnp.where(eye_hd, g_tot, 0.0) - kt[:, hd:])
                lhs_ref[u, h, hd:hd + c, :] = _bf(qp[rows])
                add_ref[u, h, 0:hd, :] = kt[:, :hd]
                add_ref[u, h, hd:hd + c, :] = fold[j][rows, :hd]

    fold_units(list(range(len(units))))

    for u, (b, cc) in enumerate(units):
        rows = slice(b * tc + cc * c, b * tc + (cc + 1) * c)
        for h in range(nh):
            res = _dot(lhs_ref[u, h], _bf(s_ref[b, h])) + add_ref[u, h]
            s_ref[b, h] = res[:hd]
            o = res[hd:]
            on = o * lax.rsqrt(jnp.mean(o * o, axis=-1, keepdims=True) + EPS) * onorm
            bg = bg_ref[rows, h * hd:(h + 1) * hd]
            o_ref[rows, h * hd:(h + 1) * hd] = (on * _silu(bg)).astype(BF16)
    sout_ref[...] = s_ref[...]


def _delta(z, hist, s0, convw, gate, onorm, *, nb, bt, t, tc, c):
    nt = t // tc
    units = bt * (tc // c)
    state = (bt, B_HEADS, B_HEAD_DIM, B_HEAD_DIM)
    kern = functools.partial(_delta_kernel, bt=bt, c=c, tc=tc)
    return pl.pallas_call(
        kern,
        grid=(nb // bt, nt),
        in_specs=[
            _z_window(bt * tc, 3 * B_WIDTH, Z_BQKV, lambda b, i: b * nt + i),
            _z_window(bt * tc, LANES, Z_BA, lambda b, i: b * nt + i),
            _z_window(bt * tc, B_WIDTH, Z_BG, lambda b, i: b * nt + i),
            pl.BlockSpec((bt, SUBLANES, 3 * B_WIDTH), lambda b, i: (b, 0, 0)),
            pl.BlockSpec(state, lambda b, i: (b, 0, 0, 0)),
            pl.BlockSpec((CONV_W, 3 * B_WIDTH), lambda b, i: (0, 0)),
            pl.BlockSpec((2, LANES), lambda b, i: (0, 0)),
            pl.BlockSpec((1, B_HEAD_DIM), lambda b, i: (0, 0)),
        ],
        out_specs=[
            pl.BlockSpec((bt * tc, B_WIDTH), lambda b, i: (b * nt + i, 0)),
            pl.BlockSpec(state, lambda b, i: (b, 0, 0, 0)),
        ],
        out_shape=[
            jax.ShapeDtypeStruct((nb * t, B_WIDTH), BF16),
            jax.ShapeDtypeStruct((nb, B_HEADS, B_HEAD_DIM, B_HEAD_DIM), F32),
        ],
        scratch_shapes=[
            pltpu.VMEM((bt, SUBLANES + tc, 3 * B_WIDTH), F32),
            pltpu.VMEM((3, units, B_HEADS * c, B_HEAD_DIM), F32),
            pltpu.VMEM((units, B_HEADS, B_HEAD_DIM + c, B_HEAD_DIM), BF16),
            pltpu.VMEM((units, B_HEADS, B_HEAD_DIM + c, B_HEAD_DIM), F32),
            pltpu.VMEM(state, F32),
        ],
        compiler_params=pltpu.CompilerParams(dimension_semantics=("arbitrary", "arbitrary"),
                                             vmem_limit_bytes=VMEM_LIMIT),
        name="gated_delta",
    )(z, z, z, hist, s0, convw, gate, onorm)


def _pool_kernel(h_ref, oa_ref, ob_ref, wa_ref, wb_ref, hist_ref, g_ref, win_ref, wgrp_ref, scale_ref, wout_ref,
                 y_ref, tail_ref, ext_ref, *, bt, tt, pos0):
    i = pl.program_id(1)
    pad = POOL_HIST + 1
    rows = bt * tt

    @pl.when(i == 0)
    def _():
        pos = pos0 - pad + lax.broadcasted_iota(jnp.int32, (1, pad, 1), 1)
        ext_ref[:, 0:pad, :] = jnp.where(pos >= 0, hist_ref[...], 0.0)

    @pl.when(i > 0)
    def _():
        ext_ref[:, 0:pad, :] = ext_ref[:, tt:tt + pad, :]

    x = h_ref[...] + _dot(oa_ref[...], wa_ref[...]) + _dot(ob_ref[...], wb_ref[...])
    xn = x * lax.rsqrt(jnp.mean(x * x, axis=-1, keepdims=True) + EPS) * g_ref[...]
    z = _dot(xn.astype(BF16), win_ref[...])
    u = z[:, :C_WIDTH]
    gate = z[:, C_WIDTH:]
    ext_ref[:, pad:pad + tt, :] = u.reshape(bt, tt, C_WIDTH)
    tail_ref[...] = ext_ref[:, tt:tt + pad, :]

    tpos = pos0 + i * tt + (lax.broadcasted_iota(jnp.int32, (rows, 1), 0) & (tt - 1))
    mixed = []
    for gi, w in enumerate(POOL_SIZES):
        cols = slice(gi * C_GROUP, (gi + 1) * C_GROUP)
        s = ext_ref[:, :, cols].reshape(bt * (pad + tt), C_GROUP)
        sh = 1
        while sh < w:
            s = s + pltpu.roll(s, sh, 0)
            sh *= 2
        s = s.reshape(bt, pad + tt, C_GROUP)[:, pad:, :].reshape(rows, C_GROUP)
        cnt = jnp.minimum(tpos + 1, w).astype(F32)
        pooled = s / cnt - u[:, cols]
        m = _dot(pooled.astype(BF16), wgrp_ref[gi]) * scale_ref[:, cols]
        mixed.append((m * _silu(gate[:, cols])).astype(BF16))
    y_ref[...] = x + _dot(jnp.concatenate(mixed, axis=1), wout_ref[...])


def _pool_layer(h, oa, ob, wa, wb, hist, g, win, wgrp, scale, wout, *, nb, bt, t, tt, pos0):
    nt = t // tt
    pad = POOL_HIST + 1
    rows = bt * tt
    kern = functools.partial(_pool_kernel, bt=bt, tt=tt, pos0=pos0)
    tile = lambda b, i: (b * nt + i, 0)
    const2 = lambda b, i: (0, 0)
    once = pl.Buffered(1)
    return pl.pallas_call(
        kern,
        grid=(nb // bt, nt),
        in_specs=[
            pl.BlockSpec((rows, D_MODEL), tile),
            pl.BlockSpec((rows, A_WIDTH), tile),
            pl.BlockSpec((rows, B_WIDTH), tile),
            pl.BlockSpec((A_WIDTH, D_MODEL), const2, pipeline_mode=once),
            pl.BlockSpec((B_WIDTH, D_MODEL), const2, pipeline_mode=once),
            pl.BlockSpec((bt, pad, C_WIDTH), lambda b, i: (b, 0, 0)),
            pl.BlockSpec((1, D_MODEL), const2),
            pl.BlockSpec((D_MODEL, 2 * C_WIDTH), const2, pipeline_mode=once),
            pl.BlockSpec((len(POOL_SIZES), C_GROUP, C_GROUP), lambda b, i: (0, 0, 0), pipeline_mode=once),
            pl.BlockSpec((1, C_WIDTH), const2),
            pl.BlockSpec((C_WIDTH, D_MODEL), const2, pipeline_mode=once),
        ],
        out_specs=[
            pl.BlockSpec((rows, D_MODEL), tile),
            pl.BlockSpec((bt, pad, C_WIDTH), lambda b, i: (b, 0, 0)),
        ],
        out_shape=[
            jax.ShapeDtypeStruct((nb * t, D_MODEL), F32),
            jax.ShapeDtypeStruct((nb, pad, C_WIDTH), F32),
        ],
        scratch_shapes=[pltpu.VMEM((bt, pad + tt, C_WIDTH), F32)],
        compiler_params=pltpu.CompilerParams(dimension_semantics=("arbitrary", "arbitrary"),
                                             vmem_limit_bytes=VMEM_LIMIT),
        name="out_proj_pool_layer",
    )(h, oa, ob, wa, wb, hist, g, win, wgrp, scale, wout)


def _ab_weights(norm_g, w_in, q_norm, k_norm, sinks, conv_w, a_log, dt_bias, o_norm, w_out):
    w = jnp.swapaxes(w_in, 0, 1).astype(BF16)
    lane_pad = (GATE_LANE, LANES - GATE_LANE - B_HEADS)
    gate = jnp.stack([jnp.pad(a_log.astype(F32), lane_pad), jnp.pad(dt_bias.astype(F32), lane_pad)])
    return dict(
        norm_g=norm_g.reshape(1, D_MODEL), w=w, kn=jnp.tile(k_norm, A_KV_HEADS).reshape(1, A_KV_WIDTH),
        qn=jnp.tile(q_norm, A_KV_HEADS).reshape(1, LANES), sinks=sinks.astype(F32), conv_w=conv_w, gate=gate,
        onorm=o_norm.reshape(1, B_HEAD_DIM), wa=w_out[:A_WIDTH].astype(BF16), wb=w_out[A_WIDTH:].astype(BF16))


def _group_tiles(t):
    if t >= CHUNK:
        return dict(proj_rows=1024, seqs_attn=1, seqs_delta=1, delta_tokens=1024, chunk=CHUNK, seqs_pool=1,
                    pool_tokens=1024)
    return dict(proj_rows=256, seqs_attn=32, seqs_delta=16, delta_tokens=t, chunk=t, seqs_pool=256 // t, pool_tokens=t)


def _ab_mixers(h, wts, cache_k, cache_v, s0, conv_hist, *, nb, t):
    tl = _group_tiles(t)
    z = _in_proj(h, wts["norm_g"], wts["w"], wts["kn"], min(nb * t, tl["proj_rows"]))
    bt = tl["seqs_attn"]
    if cache_k is None:
        cq, lk, off = CHUNK, WINDOW + CHUNK, 0
        kv_specs = [_z_window(bt * t, A_KV_WIDTH, Z_AK, lambda b: b), _z_window(bt * t, A_KV_WIDTH, Z_AV, lambda b: b)]
        kbuf = vbuf = z
    else:
        cq, lk, off = t, WINDOW + t, WINDOW
        k_new = z[:, Z_AK:Z_AK + A_KV_WIDTH].reshape(nb, t, A_KV_WIDTH)
        v_new = z[:, Z_AV:Z_AV + A_KV_WIDTH].reshape(nb, t, A_KV_WIDTH)
        kbuf = jnp.concatenate([cache_k.reshape(nb, WINDOW, A_KV_WIDTH), k_new], axis=1).reshape(nb * lk, A_KV_WIDTH)
        vbuf = jnp.concatenate([cache_v.reshape(nb, WINDOW, A_KV_WIDTH), v_new], axis=1).reshape(nb * lk, A_KV_WIDTH)
        kv_specs = [pl.BlockSpec((bt * lk, A_KV_WIDTH), lambda b: (b, 0))] * 2
    o_a = _attention(z, kbuf, vbuf, kv_specs, wts["sinks"], wts["qn"], nb=nb, bt=bt, t=t, cq=cq, lk=lk, off=off)
    hist = jnp.pad(conv_hist.astype(F32), ((0, 0), (SUBLANES - (CONV_W - 1), 0), (0, 0)))
    o_b, s_new = _delta(z, hist, s0.astype(F32), wts["conv_w"], wts["gate"], wts["onorm"], nb=nb,
                        bt=tl["seqs_delta"], t=t, tc=tl["delta_tokens"], c=tl["chunk"])
    return o_a, o_b, z, s_new


def _last_rows(z, col, width, nb, t, rows):
    return z.reshape(nb, t, Z_WIDTH)[:, t - rows:, col:col + width]


def _cache_rows(z, col, nb, t, rows):
    return _last_rows(z, col, A_KV_WIDTH, nb, t, rows).reshape(nb, rows, A_KV_HEADS, A_HEAD_DIM)


def kernel(x_prompt, x_sample, cache_a_k, cache_a_v, state_b_s, state_b_conv, state_c_pool,
           norm_ab, w_in_ab, q_norm_a, k_norm_a, sinks_a, conv_b, a_log_b, dt_bias_b, o_norm_b, w_out_ab,
           norm_c, w_in_c, w_grp_c, scale_c, w_out_c):
    bp, tp, _ = x_prompt.shape
    bs, ts, _ = x_sample.shape
    hp = x_prompt.reshape(bp * tp, D_MODEL)
    hs = x_sample.reshape(bs * ts, D_MODEL)

    wts = _ab_weights(norm_ab[0], w_in_ab[0], q_norm_a[0], k_norm_a[0], sinks_a[0], conv_b[0], a_log_b[0],
                      dt_bias_b[0], o_norm_b[0], w_out_ab[0])
    s0 = jnp.zeros((bp, B_HEADS, B_HEAD_DIM, B_HEAD_DIM), F32)
    c0 = jnp.zeros((bp, CONV_W - 1, 3 * B_WIDTH), F32)
    oap, obp, zp, sp = _ab_mixers(hp, wts, None, None, s0, c0, nb=bp, t=tp)
    oas, obs, zs, ss = _ab_mixers(hs, wts, cache_a_k[0], cache_a_v[0], state_b_s[0], state_b_conv[0], nb=bs, t=ts)
    p_a_k = _cache_rows(zp, Z_AK, bp, tp, WINDOW)[None]
    p_a_v = _cache_rows(zp, Z_AV, bp, tp, WINDOW)[None]
    s_a_k = _cache_rows(zs, Z_AK, bs, ts, ts)[None]
    s_a_v = _cache_rows(zs, Z_AV, bs, ts, ts)[None]
    p_b_conv = _last_rows(zp, Z_BQKV, 3 * B_WIDTH, bp, tp, CONV_W - 1)[None]
    s_b_conv = _last_rows(zs, Z_BQKV, 3 * B_WIDTH, bs, ts, CONV_W - 1)[None]

    g_c = norm_c[0].reshape(1, D_MODEL)
    win = w_in_c[0].astype(BF16)
    wgrp = w_grp_c[0].astype(BF16)
    scale = scale_c[0].reshape(1, C_WIDTH)
    wout = w_out_c[0].astype(BF16)
    h0 = jnp.zeros((bp, POOL_HIST + 1, C_WIDTH), F32)
    hs0 = jnp.pad(state_c_pool[0].astype(F32), ((0, 0), (1, 0), (0, 0)))
    tlp, tls = _group_tiles(tp), _group_tiles(ts)
    yp, tail_p = _pool_layer(hp, oap, obp, wts["wa"], wts["wb"], h0, g_c, win, wgrp, scale, wout, nb=bp,
                             bt=tlp["seqs_pool"], t=tp, tt=tlp["pool_tokens"], pos0=0)
    ys, tail_s = _pool_layer(hs, oas, obs, wts["wa"], wts["wb"], hs0, g_c, win, wgrp, scale, wout, nb=bs,
                             bt=tls["seqs_pool"], t=ts, tt=tls["pool_tokens"], pos0=PAST_LEN)

    return (yp.reshape(bp, tp, D_MODEL), ys.reshape(bs, ts, D_MODEL),
            p_a_k, p_a_v, sp[None], p_b_conv, tail_p[:, 1:][None],
            s_a_k, s_a_v, ss[None], s_b_conv, tail_s[:, 1:][None])
```

```python
import functools

import jax
import jax.numpy as jnp
from jax import lax
from jax.experimental import pallas as pl
from jax.experimental.pallas import tpu as pltpu

F32 = jnp.float32
BF16 = jnp.bfloat16
HIGHEST = lax.Precision.HIGHEST

D_MODEL = 1024
CHUNK = 64
PAST_LEN = 2048
EPS = 1e-6
NEG_INF = -1e30
LOG2E = 1.4426950408889634

A_HEADS = 8
A_KV_HEADS = 2
A_HEAD_DIM = 64
A_WIDTH = A_HEADS * A_HEAD_DIM
A_KV_WIDTH = A_KV_HEADS * A_HEAD_DIM
A_REP = A_HEADS // A_KV_HEADS
WINDOW = 128

B_HEADS = 4
B_HEAD_DIM = 128
B_WIDTH = B_HEADS * B_HEAD_DIM
CONV_W = 4

POOL_SIZES = (2, 4, 8, 16)
C_WIDTH = D_MODEL
C_GROUP = C_WIDTH // len(POOL_SIZES)
POOL_HIST = max(POOL_SIZES) - 1

LANES = 128
SUBLANES = 8

Z_AQ = 0
Z_AK = Z_AQ + A_WIDTH
Z_AV = Z_AK + A_KV_WIDTH
Z_AG = Z_AV + A_KV_WIDTH
Z_BQKV = Z_AG + A_WIDTH
Z_BG = Z_BQKV + 3 * B_WIDTH
Z_BA = Z_BG + B_WIDTH
Z_WIDTH = Z_BA + LANES
GATE_LANE = B_HEADS


def _z_window(rows, width, col, row_block):
    return pl.BlockSpec((pl.Element(rows), pl.Element(width)), lambda *idx: (row_block(*idx) * rows, col))

VMEM_LIMIT = 48 * 1024 * 1024


def _sigmoid(x):
    return 1.0 / (1.0 + jnp.exp(-x))


def _silu(x):
    return x * _sigmoid(x)


def _softplus(x):
    return jnp.maximum(x, 0.0) + jnp.log(1.0 + jnp.exp(-jnp.abs(x)))


def _dot(a, b, precision=None):
    return jnp.dot(a, b, preferred_element_type=F32, precision=precision)


def _dot_nt(a, b, precision=None):
    return lax.dot_general(a, b, (((1,), (1,)), ((), ())), preferred_element_type=F32, precision=precision)


def _dot_tn(a, b, precision=None):
    return lax.dot_general(a, b, (((0,), (0,)), ((), ())), preferred_element_type=F32, precision=precision)


def _log2(n):
    assert n & (n - 1) == 0
    return n.bit_length() - 1


def _in_proj_kernel(x_ref, g_ref, w_ref, kn_ref, qn_ref, z_ref):
    x = x_ref[...]
    xn = x * lax.rsqrt(jnp.mean(x * x, axis=-1, keepdims=True) + EPS) * g_ref[...]
    n_in = w_ref.shape[0]
    z_ref[:, Z_BA:Z_WIDTH] = jnp.zeros((x.shape[0], Z_WIDTH - Z_BA), F32)
    z_ref[:, 0:n_in] = _dot_nt(xn.astype(BF16), w_ref[...])
    first = lax.broadcasted_iota(jnp.int32, (x.shape[0], LANES), 1) < A_HEAD_DIM
    q_gain = qn_ref[...] * (A_HEAD_DIM ** -0.5 * LOG2E)
    for col, gain in [(Z_AQ + r * LANES, q_gain) for r in range(A_WIDTH // LANES)] + [(Z_AK, kn_ref[...])]:
        blk = z_ref[:, col:col + LANES]
        sq = blk * blk
        s0 = jnp.sum(jnp.where(first, sq, 0.0), axis=-1, keepdims=True)
        s1 = jnp.sum(jnp.where(first, 0.0, sq), axis=-1, keepdims=True)
        ms = jnp.where(first, s0, s1) * (1.0 / A_HEAD_DIM)
        z_ref[:, col:col + LANES] = blk * lax.rsqrt(ms + EPS) * gain


def _in_proj(x, g, w, kn, qn, tm):
    n = x.shape[0]
    return pl.pallas_call(
        _in_proj_kernel,
        grid=(n // tm,),
        in_specs=[
            pl.BlockSpec((tm, D_MODEL), lambda i: (i, 0)),
            pl.BlockSpec((1, D_MODEL), lambda i: (0, 0)),
            pl.BlockSpec(w.shape, lambda i: (0, 0), pipeline_mode=pl.Buffered(1)),
            pl.BlockSpec((1, A_KV_WIDTH), lambda i: (0, 0)),
            pl.BlockSpec((1, LANES), lambda i: (0, 0)),
        ],
        out_specs=pl.BlockSpec((tm, Z_WIDTH), lambda i: (i, 0)),
        out_shape=jax.ShapeDtypeStruct((n, Z_WIDTH), F32),
        compiler_params=pltpu.CompilerParams(dimension_semantics=("arbitrary",), vmem_limit_bytes=VMEM_LIMIT),
        name="in_proj",
    )(x, g, w, kn, qn)


A_HALVES = LANES // A_HEAD_DIM
A_QBLOCKS = A_WIDTH // LANES
A_COPIES = A_KV_HEADS


def _attn_half(r, c):
    g = r * A_HALVES // A_REP
    return (g + c) % A_HALVES


def _attn_tables(sinks_ref, bias_ref, *, cq, lk, front, slack, fill):
    rows = A_QBLOCKS * cq
    nvar = front // cq + 1
    row = lax.broadcasted_iota(jnp.int32, (rows, 1), 0)
    blk = row >> _log2(cq)

    def per_row(value):
        out = []
        for c in range(A_COPIES):
            col = jnp.zeros((rows, 1), F32)
            for r in range(A_QBLOCKS):
                col = jnp.where(blk == r, value(r * A_HALVES + _attn_half(r, c)), col)
            out.append(col)
        return out

    @pl.when(fill)
    def _():
        col = lax.broadcasted_iota(jnp.int32, (rows, lk + slack), 1)
        dist = jnp.abs((row & (cq - 1)) + WINDOW - col).astype(F32)
        slopes = per_row(lambda h: 2.0 ** (-8.0 * (h + 1) / A_HEADS))
        sinks = per_row(lambda h: sinks_ref[h])
        for c in range(A_COPIES):
            for var in range(nvar):
                keys = jnp.where((col >= front - var * cq) & (col < lk), -slopes[c] * dist, NEG_INF)
                bias_ref[var, c] = jnp.where(col == lk, sinks[c], keys) * LOG2E

    first = lax.broadcasted_iota(jnp.int32, (cq, LANES), 1) < A_HEAD_DIM
    return first, nvar


def _attn_chunk(qc, kv, biases, first, gate, cq):
    blocks = [qc[:, r * LANES:(r + 1) * LANES] for r in range(A_QBLOCKS)]
    outs = []
    for c in range(A_COPIES):
        lhs = jnp.concatenate(
            [jnp.where(first if _attn_half(r, c) == 0 else jnp.logical_not(first), blocks[r], 0.0)
             for r in range(A_QBLOCKS)], axis=0)
        zero_keys = jnp.zeros((biases[c].shape[1] - kv[c][0].shape[0], A_KV_WIDTH), BF16)
        kk, vv = (jnp.concatenate([x, zero_keys], axis=0) for x in kv[c])
        sc = _dot_nt(lhs.astype(BF16), kk) + biases[c]
        p = jnp.exp2(sc - jnp.max(sc, axis=-1, keepdims=True))
        outs.append(_dot(p.astype(BF16), vv) * (1.0 / jnp.sum(p, axis=-1, keepdims=True)))
    tile = []
    for r in range(A_QBLOCKS):
        rows = slice(r * cq, (r + 1) * cq)
        low = 0 if _attn_half(r, 0) == 0 else 1
        tile.append(jnp.where(first, outs[low][rows], outs[1 - low][rows]))
    return (jnp.concatenate(tile, axis=1) * _silu(gate)).astype(BF16)


def _attn_kernel(sinks_ref, q_ref, k_ref, v_ref, ag_ref, o_ref, kp_ref, vp_ref, bias_ref, *, bt, cq, lk, off, t):
    front = WINDOW - off
    lkt = k_ref.shape[0] // bt
    n_chunks = t // cq

    for b in range(bt):
        for src, dst in ((k_ref, kp_ref), (v_ref, vp_ref)):
            x = src[b * lkt:(b + 1) * lkt, :]
            if front:
                dst[:, b, 0:front, :] = jnp.zeros((A_COPIES, front, A_KV_WIDTH), BF16)
            dst[0, b, front:front + lkt, :] = x.astype(BF16)
            dst[1, b, front:front + lkt, :] = pltpu.roll(x, A_HEAD_DIM, 1).astype(BF16)

    first, nvar = _attn_tables(sinks_ref, bias_ref, cq=cq, lk=lk, front=front, slack=bias_ref.shape[-1] - lk,
                               fill=pl.program_id(0) == 0)

    def chunk(b, cg):
        k0 = cg * cq if isinstance(cg, int) else pl.multiple_of(cg * cq, cq)
        r0 = b * t + k0
        var = min(cg, nvar - 1) if isinstance(cg, int) else jnp.minimum(cg, nvar - 1)
        kv = [(kp_ref[c, b, pl.ds(k0, lk), :], vp_ref[c, b, pl.ds(k0, lk), :]) for c in range(A_COPIES)]
        o_ref[pl.ds(r0, cq), :] = _attn_chunk(
            q_ref[pl.ds(r0, cq), :], kv, [bias_ref[var, c] for c in range(A_COPIES)], first,
            ag_ref[pl.ds(r0, cq), :], cq)

    for b in range(bt):
        if n_chunks == 1:
            chunk(b, 0)
        else:
            lax.fori_loop(0, n_chunks, lambda cg, carry, b=b: (chunk(b, cg), carry)[1], 0, unroll=min(n_chunks, 8))


def _attention(z, kbuf, vbuf, kv_specs, sinks, *, nb, bt, t, cq, lk, off):
    front = WINDOW - off
    lkt = kbuf.shape[0] // nb
    kern = functools.partial(_attn_kernel, bt=bt, cq=cq, lk=lk, off=off, t=t)
    return pl.pallas_call(
        kern,
        grid=(nb // bt,),
        in_specs=[
            pl.BlockSpec(memory_space=pltpu.SMEM),
            _z_window(bt * t, A_WIDTH, Z_AQ, lambda b: b),
            kv_specs[0],
            kv_specs[1],
            _z_window(bt * t, A_WIDTH, Z_AG, lambda b: b),
        ],
        out_specs=pl.BlockSpec((bt * t, A_WIDTH), lambda b: (b, 0)),
        out_shape=jax.ShapeDtypeStruct((nb * t, A_WIDTH), BF16),
        scratch_shapes=[
            pltpu.VMEM((A_COPIES, bt, front + lkt, A_KV_WIDTH), BF16),
            pltpu.VMEM((A_COPIES, bt, front + lkt, A_KV_WIDTH), BF16),
            pltpu.VMEM((front // cq + 1, A_COPIES, A_QBLOCKS * cq, lk + LANES - lk % LANES), F32),
        ],
        compiler_params=pltpu.CompilerParams(dimension_semantics=("arbitrary",), vmem_limit_bytes=VMEM_LIMIT),
        name="swa_attention",
    )(sinks, z, kbuf, vbuf, z)


def _bf(x):
    return x.astype(BF16)


def _unit_lower_inverse(a_list, c, ri, lj, expand):
    base = SUBLANES
    same = (ri >> _log2(base)) == (lj >> _log2(base))
    eye = jnp.where(ri == lj, 1.0, 0.0)
    n1 = [jnp.where(same, -a, 0.0) for a in a_list]
    x = [eye + n for n in n1]
    n1b = [_bf(n) for n in n1]
    n2b = [_bf(_dot(nb, expand(nb))) for nb in n1b]
    n2e = [expand(nb) for nb in n2b]
    x = [xi + _dot(_bf(xi), ne) for xi, ne in zip(x, n2e)]
    n4e = [expand(_bf(_dot(nb, ne))) for nb, ne in zip(n2b, n2e)]
    x = [xi + _dot(_bf(xi), ne) for xi, ne in zip(x, n4e)]
    s = base
    while s < c:
        sel = ((ri >> _log2(2 * s)) == (lj >> _log2(2 * s))) & ((ri >> _log2(s)) != (lj >> _log2(s)))
        xb = [_bf(xi) for xi in x]
        xo = [_dot(b, expand(_bf(jnp.where(sel, a, 0.0)))) for b, a in zip(xb, a_list)]
        x = [xi - _dot(_bf(o), expand(b)) for xi, o, b in zip(x, xo, xb)]
        s *= 2
    return x


def _delta_kernel(qkv_ref, ba_ref, bg_ref, hist_ref, s0_ref, convw_ref, gate_ref, onorm_ref,
                  o_ref, sout_ref, ext_ref, st_ref, lhs_ref, add_ref, s_ref, *, bt, c, tc):
    i = pl.program_id(1)
    hd = B_HEAD_DIM
    nh = B_HEADS
    r = nh * c
    nch = tc // c
    pad = SUBLANES
    units = [(b, cc) for b in range(bt) for cc in range(nch)]

    @pl.when(i == 0)
    def _():
        ext_ref[:, 0:pad, :] = hist_ref[...]
        s_ref[...] = s0_ref[...]

    @pl.when(i > 0)
    def _():
        ext_ref[:, 0:pad, :] = ext_ref[:, tc:tc + pad, :]

    for b in range(bt):
        ext_ref[b, pad:pad + tc, :] = qkv_ref[b * tc:(b + 1) * tc, :]

    for u, (b, cc) in enumerate(units):
        e = ext_ref[b, cc * c:cc * c + pad + c, :]
        acc = e * convw_ref[0:1, :]
        for j in range(1, CONV_W):
            acc = pltpu.roll(acc, 1, 0) + e * convw_ref[j:j + 1, :]
        y = _silu(acc[pad:, :])
        for part in range(3):
            for h in range(nh):
                blk = y[:, (part * nh + h) * hd:(part * nh + h + 1) * hd]
                if part < 2:
                    blk = blk * lax.rsqrt(jnp.sum(blk * blk, axis=-1, keepdims=True) + EPS)
                st_ref[part, u, h * c:(h + 1) * c, :] = blk

    ri = lax.broadcasted_iota(jnp.int32, (c, r), 0)
    li = lax.broadcasted_iota(jnp.int32, (c, r), 1)
    lj = li & (c - 1)
    lh = li >> _log2(c)
    lower = ri >= lj
    strict = ri > lj
    head_sel = [jnp.where(lh == h, 1.0, 0.0).astype(BF16) for h in range(nh)]

    def expand(xb):
        return jnp.concatenate([xb * m for m in head_sel], axis=0)

    def lanes(x_st):
        return jnp.concatenate([x_st[h * c:(h + 1) * c] for h in range(nh)], axis=1)

    def head_blocks(x_st):
        zero = jnp.zeros((c, hd), x_st.dtype)
        return jnp.concatenate(
            [jnp.concatenate([x_st[h * c:(h + 1) * c] if h2 == h else zero for h2 in range(nh)], axis=1)
             for h in range(nh)], axis=0)

    ci = lax.broadcasted_iota(jnp.int32, (c, c), 0)
    cj = lax.broadcasted_iota(jnp.int32, (c, c), 1)
    tril = jnp.where(ci >= cj, 1.0, 0.0)
    di = lax.broadcasted_iota(jnp.int32, (hd, hd), 0)
    dj = lax.broadcasted_iota(jnp.int32, (hd, hd), 1)
    eye_hd = di == dj
    neg_rate = -jnp.exp(gate_ref[0:1, :])
    dt_bias = gate_ref[1:2, :]
    onorm = onorm_ref[...]

    def stack_cols(x, lane0):
        return jnp.concatenate([x[:, lane0 + h:lane0 + h + 1] for h in range(nh)], axis=0)

    beta, gc, gcc, gr, glast = [], [], [], [], []
    for b, cc in units:
        ba = ba_ref[b * tc + cc * c:b * tc + (cc + 1) * c, :]
        g_all = neg_rate * _softplus(ba + dt_bias)
        gcum = _dot(tril, g_all, HIGHEST)
        gcum_t = gcum.T
        beta.append(stack_cols(_sigmoid(ba), 0))
        gc.append(stack_cols(gcum, GATE_LANE))
        gcc_h = jnp.zeros((c, r), F32)
        for h in range(nh):
            gcc_h = jnp.where(lh == h, gcum[:, GATE_LANE + h:GATE_LANE + h + 1], gcc_h)
        gcc.append(gcc_h)
        gr.append(jnp.concatenate([gcum_t[GATE_LANE + h:GATE_LANE + h + 1, :] for h in range(nh)], axis=1))
        glast.append(jnp.concatenate(
            [jnp.broadcast_to(gcum[c - 1:c, GATE_LANE + h:GATE_LANE + h + 1], (c, 1)) for h in range(nh)], axis=0))

    def fold_units(us):
        n = range(len(us))
        q = [st_ref[0, u] * (hd ** -0.5) for u in us]
        k = [st_ref[1, u] for u in us]
        v = [st_ref[2, u] for u in us]
        kb = [k[j] * beta[u] for j, u in enumerate(us)]
        big = [_dot_nt(_bf(jnp.concatenate([lanes(kb[j]), lanes(q[j])], axis=0)), head_blocks(_bf(k[j]))) for j in n]
        decay = [jnp.where(lower, jnp.exp(jnp.where(lower, gcc[u] - gr[u], 0.0)), 0.0) for u in us]
        a = [jnp.where(strict, big[j][:c] * decay[j], 0.0) for j in n]
        qkd = [expand(_bf(big[j][c:] * decay[j])) for j in n]
        tinv = _unit_lower_inverse(a, c, ri, lj, expand)
        eg = [jnp.exp(gc[u]) for u in us]
        rhs = [_bf(jnp.concatenate([v[j] * beta[u], kb[j] * eg[j]], axis=1)) for j, u in enumerate(us)]
        sol = [_dot(expand(_bf(tinv[j])), rhs[j]) for j in n]
        solb = [_bf(s) for s in sol]
        fold = [_dot(qkd[j], solb[j]) for j in n]
        k_dec = [_bf(k[j] * jnp.exp(glast[u] - gc[u])) for j, u in enumerate(us)]
        for j, u in enumerate(us):
            qp = q[j] * eg[j] - fold[j][:, hd:]
            for h in range(nh):
                rows = slice(h * c, (h + 1) * c)
                kt = _dot_tn(k_dec[j][rows], solb[j][rows])
                g_tot = jnp.exp(glast[u][h * c:h * c + 1, :])
                lhs_ref[u, h, 0:hd, :] = _bf(jnp.where(eye_hd, g_tot, 0.0) - kt[:, hd:])
                lhs_ref[u, h, hd:hd + c, :] = _bf(qp[rows])
                add_ref[u, h, 0:hd, :] = kt[:, :hd]
                add_ref[u, h, hd:hd + c, :] = fold[j][rows, :hd]

    fold_units(list(range(len(units))))

    for u, (b, cc) in enumerate(units):
        rows = slice(b * tc + cc * c, b * tc + (cc + 1) * c)
        for h in range(nh):
            res = _dot(lhs_ref[u, h], _bf(s_ref[b, h])) + add_ref[u, h]
            s_ref[b, h] = res[:hd]
            o = res[hd:]
            on = o * lax.rsqrt(jnp.mean(o * o, axis=-1, keepdims=True) + EPS) * onorm
            bg = bg_ref[rows, h * hd:(h + 1) * hd]
            o_ref[rows, h * hd:(h + 1) * hd] = (on * _silu(bg)).astype(BF16)
    sout_ref[...] = s_ref[...]


def _delta(z, hist, s0, convw, gate, onorm, *, nb, bt, t, tc, c):
    nt = t // tc
    units = bt * (tc // c)
    state = (bt, B_HEADS, B_HEAD_DIM, B_HEAD_DIM)
    kern = functools.partial(_delta_kernel, bt=bt, c=c, tc=tc)
    return pl.pallas_call(
        kern,
        grid=(nb // bt, nt),
        in_specs=[
            _z_window(bt * tc, 3 * B_WIDTH, Z_BQKV, lambda b, i: b * nt + i),
            _z_window(bt * tc, LANES, Z_BA, lambda b, i: b * nt + i),
            _z_window(bt * tc, B_WIDTH, Z_BG, lambda b, i: b * nt + i),
            pl.BlockSpec((bt, SUBLANES, 3 * B_WIDTH), lambda b, i: (b, 0, 0)),
            pl.BlockSpec(state, lambda b, i: (b, 0, 0, 0)),
            pl.BlockSpec((CONV_W, 3 * B_WIDTH), lambda b, i: (0, 0)),
            pl.BlockSpec((2, LANES), lambda b, i: (0, 0)),
            pl.BlockSpec((1, B_HEAD_DIM), lambda b, i: (0, 0)),
        ],
        out_specs=[
            pl.BlockSpec((bt * tc, B_WIDTH), lambda b, i: (b * nt + i, 0)),
            pl.BlockSpec(state, lambda b, i: (b, 0, 0, 0)),
        ],
        out_shape=[
            jax.ShapeDtypeStruct((nb * t, B_WIDTH), BF16),
            jax.ShapeDtypeStruct((nb, B_HEADS, B_HEAD_DIM, B_HEAD_DIM), F32),
        ],
        scratch_shapes=[
            pltpu.VMEM((bt, SUBLANES + tc, 3 * B_WIDTH), F32),
            pltpu.VMEM((3, units, B_HEADS * c, B_HEAD_DIM), F32),
            pltpu.VMEM((units, B_HEADS, B_HEAD_DIM + c, B_HEAD_DIM), BF16),
            pltpu.VMEM((units, B_HEADS, B_HEAD_DIM + c, B_HEAD_DIM), F32),
            pltpu.VMEM(state, F32),
        ],
        compiler_params=pltpu.CompilerParams(dimension_semantics=("arbitrary", "arbitrary"),
                                             vmem_limit_bytes=VMEM_LIMIT),
        name="gated_delta",
    )(z, z, z, hist, s0, convw, gate, onorm)


def _pool_kernel(h_ref, oa_ref, ob_ref, wa_ref, wb_ref, hist_ref, g_ref, win_ref, wgrp_ref, scale_ref, wout_ref,
                 y_ref, tail_ref, ext_ref, *, bt, tt, pos0):
    i = pl.program_id(1)
    pad = POOL_HIST + 1
    rows = bt * tt

    @pl.when(i == 0)
    def _():
        pos = pos0 - pad + lax.broadcasted_iota(jnp.int32, (1, pad, 1), 1)
        ext_ref[:, 0:pad, :] = jnp.where(pos >= 0, hist_ref[...], 0.0)

    @pl.when(i > 0)
    def _():
        ext_ref[:, 0:pad, :] = ext_ref[:, tt:tt + pad, :]

    x = h_ref[...] + _dot(oa_ref[...], wa_ref[...]) + _dot(ob_ref[...], wb_ref[...])
    xn = x * lax.rsqrt(jnp.mean(x * x, axis=-1, keepdims=True) + EPS) * g_ref[...]
    z = _dot(xn.astype(BF16), win_ref[...])
    u = z[:, :C_WIDTH]
    gate = z[:, C_WIDTH:]
    ext_ref[:, pad:pad + tt, :] = u.reshape(bt, tt, C_WIDTH)
    tail_ref[...] = ext_ref[:, tt:tt + pad, :]

    tpos = pos0 + i * tt + (lax.broadcasted_iota(jnp.int32, (rows, 1), 0) & (tt - 1))
    mixed = []
    for gi, w in enumerate(POOL_SIZES):
        cols = slice(gi * C_GROUP, (gi + 1) * C_GROUP)
        s = ext_ref[:, :, cols].reshape(bt * (pad + tt), C_GROUP)
        sh = 1
        while sh < w:
            s = s + pltpu.roll(s, sh, 0)
            sh *= 2
        s = s.reshape(bt, pad + tt, C_GROUP)[:, pad:, :].reshape(rows, C_GROUP)
        cnt = jnp.minimum(tpos + 1, w).astype(F32)
        pooled = s / cnt - u[:, cols]
        m = _dot(pooled.astype(BF16), wgrp_ref[gi]) * scale_ref[:, cols]
        mixed.append((m * _silu(gate[:, cols])).astype(BF16))
    y_ref[...] = x + _dot(jnp.concatenate(mixed, axis=1), wout_ref[...])


def _pool_layer(h, oa, ob, wa, wb, hist, g, win, wgrp, scale, wout, *, nb, bt, t, tt, pos0):
    nt = t // tt
    pad = POOL_HIST + 1
    rows = bt * tt
    kern = functools.partial(_pool_kernel, bt=bt, tt=tt, pos0=pos0)
    tile = lambda b, i: (b * nt + i, 0)
    const2 = lambda b, i: (0, 0)
    once = pl.Buffered(1)
    return pl.pallas_call(
        kern,
        grid=(nb // bt, nt),
        in_specs=[
            pl.BlockSpec((rows, D_MODEL), tile),
            pl.BlockSpec((rows, A_WIDTH), tile),
            pl.BlockSpec((rows, B_WIDTH), tile),
            pl.BlockSpec((A_WIDTH, D_MODEL), const2, pipeline_mode=once),
            pl.BlockSpec((B_WIDTH, D_MODEL), const2, pipeline_mode=once),
            pl.BlockSpec((bt, pad, C_WIDTH), lambda b, i: (b, 0, 0)),
            pl.BlockSpec((1, D_MODEL), const2),
            pl.BlockSpec((D_MODEL, 2 * C_WIDTH), const2, pipeline_mode=once),
            pl.BlockSpec((len(POOL_SIZES), C_GROUP, C_GROUP), lambda b, i: (0, 0, 0), pipeline_mode=once),
            pl.BlockSpec((1, C_WIDTH), const2),
            pl.BlockSpec((C_WIDTH, D_MODEL), const2, pipeline_mode=once),
        ],
        out_specs=[
            pl.BlockSpec((rows, D_MODEL), tile),
            pl.BlockSpec((bt, pad, C_WIDTH), lambda b, i: (b, 0, 0)),
        ],
        out_shape=[
            jax.ShapeDtypeStruct((nb * t, D_MODEL), F32),
            jax.ShapeDtypeStruct((nb, pad, C_WIDTH), F32),
        ],
        scratch_shapes=[pltpu.VMEM((bt, pad + tt, C_WIDTH), F32)],
        compiler_params=pltpu.CompilerParams(dimension_semantics=("arbitrary", "arbitrary"),
                                             vmem_limit_bytes=VMEM_LIMIT),
        name="out_proj_pool_layer",
    )(h, oa, ob, wa, wb, hist, g, win, wgrp, scale, wout)


def _ab_weights(norm_g, w_in, q_norm, k_norm, sinks, conv_w, a_log, dt_bias, o_norm, w_out):
    w = jnp.swapaxes(w_in, 0, 1).astype(BF16)
    lane_pad = (GATE_LANE, LANES - GATE_LANE - B_HEADS)
    gate = jnp.stack([jnp.pad(a_log.astype(F32), lane_pad), jnp.pad(dt_bias.astype(F32), lane_pad)])
    return dict(
        norm_g=norm_g.reshape(1, D_MODEL), w=w, kn=jnp.tile(k_norm, A_KV_HEADS).reshape(1, A_KV_WIDTH),
        qn=jnp.tile(q_norm, A_KV_HEADS).reshape(1, LANES), sinks=sinks.astype(F32), conv_w=conv_w, gate=gate,
        onorm=o_norm.reshape(1, B_HEAD_DIM), wa=w_out[:A_WIDTH].astype(BF16), wb=w_out[A_WIDTH:].astype(BF16))


def _group_tiles(t):
    if t >= CHUNK:
        return dict(proj_rows=1024, seqs_attn=1, seqs_delta=1, delta_tokens=1024, chunk=CHUNK, seqs_pool=1,
                    pool_tokens=1024)
    return dict(proj_rows=256, seqs_attn=32, seqs_delta=16, delta_tokens=t, chunk=t, seqs_pool=256 // t, pool_tokens=t)


def _ab_mixers(h, wts, cache_k, cache_v, s0, conv_hist, *, nb, t):
    tl = _group_tiles(t)
    z = _in_proj(h, wts["norm_g"], wts["w"], wts["kn"], wts["qn"], min(nb * t, tl["proj_rows"]))
    bt = tl["seqs_attn"]
    if cache_k is None:
        cq, lk, off = CHUNK, WINDOW + CHUNK, 0
        kv_specs = [_z_window(bt * t, A_KV_WIDTH, Z_AK, lambda b: b), _z_window(bt * t, A_KV_WIDTH, Z_AV, lambda b: b)]
        kbuf = vbuf = z
    else:
        cq, lk, off = t, WINDOW + t, WINDOW
        k_new = z[:, Z_AK:Z_AK + A_KV_WIDTH].reshape(nb, t, A_KV_WIDTH)
        v_new = z[:, Z_AV:Z_AV + A_KV_WIDTH].reshape(nb, t, A_KV_WIDTH)
        kbuf = jnp.concatenate([cache_k.reshape(nb, WINDOW, A_KV_WIDTH), k_new], axis=1).reshape(nb * lk, A_KV_WIDTH)
        vbuf = jnp.concatenate([cache_v.reshape(nb, WINDOW, A_KV_WIDTH), v_new], axis=1).reshape(nb * lk, A_KV_WIDTH)
        kv_specs = [pl.BlockSpec((bt * lk, A_KV_WIDTH), lambda b: (b, 0))] * 2
    o_a = _attention(z, kbuf, vbuf, kv_specs, wts["sinks"], nb=nb, bt=bt, t=t, cq=cq, lk=lk, off=off)
    hist = jnp.pad(conv_hist.astype(F32), ((0, 0), (SUBLANES - (CONV_W - 1), 0), (0, 0)))
    o_b, s_new = _delta(z, hist, s0.astype(F32), wts["conv_w"], wts["gate"], wts["onorm"], nb=nb,
                        bt=tl["seqs_delta"], t=t, tc=tl["delta_tokens"], c=tl["chunk"])
    return o_a, o_b, z, s_new


def _last_rows(z, col, width, nb, t, rows):
    return z.reshape(nb, t, Z_WIDTH)[:, t - rows:, col:col + width]


def _cache_rows(z, col, nb, t, rows):
    return _last_rows(z, col, A_KV_WIDTH, nb, t, rows).reshape(nb, rows, A_KV_HEADS, A_HEAD_DIM)


def kernel(x_prompt, x_sample, cache_a_k, cache_a_v, state_b_s, state_b_conv, state_c_pool,
           norm_ab, w_in_ab, q_norm_a, k_norm_a, sinks_a, conv_b, a_log_b, dt_bias_b, o_norm_b, w_out_ab,
           norm_c, w_in_c, w_grp_c, scale_c, w_out_c):
    bp, tp, _ = x_prompt.shape
    bs, ts, _ = x_sample.shape
    hp = x_prompt.reshape(bp * tp, D_MODEL)
    hs = x_sample.reshape(bs * ts, D_MODEL)

    wts = _ab_weights(norm_ab[0], w_in_ab[0], q_norm_a[0], k_norm_a[0], sinks_a[0], conv_b[0], a_log_b[0],
                      dt_bias_b[0], o_norm_b[0], w_out_ab[0])
    s0 = jnp.zeros((bp, B_HEADS, B_HEAD_DIM, B_HEAD_DIM), F32)
    c0 = jnp.zeros((bp, CONV_W - 1, 3 * B_WIDTH), F32)
    oap, obp, zp, sp = _ab_mixers(hp, wts, None, None, s0, c0, nb=bp, t=tp)
    oas, obs, zs, ss = _ab_mixers(hs, wts, cache_a_k[0], cache_a_v[0], state_b_s[0], state_b_conv[0], nb=bs, t=ts)
    p_a_k = _cache_rows(zp, Z_AK, bp, tp, WINDOW)[None]
    p_a_v = _cache_rows(zp, Z_AV, bp, tp, WINDOW)[None]
    s_a_k = _cache_rows(zs, Z_AK, bs, ts, ts)[None]
    s_a_v = _cache_rows(zs, Z_AV, bs, ts, ts)[None]
    p_b_conv = _last_rows(zp, Z_BQKV, 3 * B_WIDTH, bp, tp, CONV_W - 1)[None]
    s_b_conv = _last_rows(zs, Z_BQKV, 3 * B_WIDTH, bs, ts, CONV_W - 1)[None]

    g_c = norm_c[0].reshape(1, D_MODEL)
    win = w_in_c[0].astype(BF16)
    wgrp = w_grp_c[0].astype(BF16)
    scale = scale_c[0].reshape(1, C_WIDTH)
    wout = w_out_c[0].astype(BF16)
    h0 = jnp.zeros((bp, POOL_HIST + 1, C_WIDTH), F32)
    hs0 = jnp.pad(state_c_pool[0].astype(F32), ((0, 0), (1, 0), (0, 0)))
    tlp, tls = _group_tiles(tp), _group_tiles(ts)
    yp, tail_p = _pool_layer(hp, oap, obp, wts["wa"], wts["wb"], h0, g_c, win, wgrp, scale, wout, nb=bp,
                             bt=tlp["seqs_pool"], t=tp, tt=tlp["pool_tokens"], pos0=0)
    ys, tail_s = _pool_layer(hs, oas, obs, wts["wa"], wts["wb"], hs0, g_c, win, wgrp, scale, wout, nb=bs,
                             bt=tls["seqs_pool"], t=ts, tt=tls["pool_tokens"], pos0=PAST_LEN)

    return (yp.reshape(bp, tp, D_MODEL), ys.reshape(bs, ts, D_MODEL),
            p_a_k, p_a_v, sp[None], p_b_conv, tail_p[:, 1:][None],
            s_a_k, s_a_v, ss[None], s_b_conv, tail_s[:, 1:][None])
```

```python
import functools

import jax
import jax.numpy as jnp
from jax import lax
from jax.experimental import pallas as pl
from jax.experimental.pallas import tpu as pltpu

F32 = jnp.float32
BF16 = jnp.bfloat16
HIGHEST = lax.Precision.HIGHEST

D_MODEL = 1024
CHUNK = 64
PAST_LEN = 2048
EPS = 1e-6
NEG_INF = -1e30
LOG2E = 1.4426950408889634

A_HEADS = 8
A_KV_HEADS = 2
A_HEAD_DIM = 64
A_WIDTH = A_HEADS * A_HEAD_DIM
A_KV_WIDTH = A_KV_HEADS * A_HEAD_DIM
A_REP = A_HEADS // A_KV_HEADS
WINDOW = 128

B_HEADS = 4
B_HEAD_DIM = 128
B_WIDTH = B_HEADS * B_HEAD_DIM
CONV_W = 4

POOL_SIZES = (2, 4, 8, 16)
C_WIDTH = D_MODEL
C_GROUP = C_WIDTH // len(POOL_SIZES)
POOL_HIST = max(POOL_SIZES) - 1

LANES = 128
SUBLANES = 8

Z_AQ = 0
Z_AK = Z_AQ + A_WIDTH
Z_AV = Z_AK + A_KV_WIDTH
Z_AG = Z_AV + A_KV_WIDTH
Z_BQKV = Z_AG + A_WIDTH
Z_BG = Z_BQKV + 3 * B_WIDTH
Z_BA = Z_BG + B_WIDTH
Z_WIDTH = Z_BA + LANES
GATE_LANE = B_HEADS


def _z_window(rows, width, col, row_block):
    return pl.BlockSpec((pl.Element(rows), pl.Element(width)), lambda *idx: (row_block(*idx) * rows, col))

VMEM_LIMIT = 48 * 1024 * 1024


def _sigmoid(x):
    return 1.0 / (1.0 + jnp.exp(-x))


def _silu(x):
    return x * _sigmoid(x)


def _softplus(x):
    return jnp.maximum(x, 0.0) + jnp.log(1.0 + jnp.exp(-jnp.abs(x)))


def _dot(a, b, precision=None):
    return jnp.dot(a, b, preferred_element_type=F32, precision=precision)


def _dot_nt(a, b, precision=None):
    return lax.dot_general(a, b, (((1,), (1,)), ((), ())), preferred_element_type=F32, precision=precision)


def _dot_tn(a, b, precision=None):
    return lax.dot_general(a, b, (((0,), (0,)), ((), ())), preferred_element_type=F32, precision=precision)


def _log2(n):
    assert n & (n - 1) == 0
    return n.bit_length() - 1


def _in_proj_kernel(x_ref, g_ref, w_ref, kn_ref, qn_ref, z_ref):
    x = x_ref[...]
    xn = x * lax.rsqrt(jnp.mean(x * x, axis=-1, keepdims=True) + EPS) * g_ref[...]
    n_in = w_ref.shape[0]
    z_ref[:, Z_BA:Z_WIDTH] = jnp.zeros((x.shape[0], Z_WIDTH - Z_BA), F32)
    z_ref[:, 0:n_in] = _dot_nt(xn.astype(BF16), w_ref[...])
    first = lax.broadcasted_iota(jnp.int32, (x.shape[0], LANES), 1) < A_HEAD_DIM
    q_gain = qn_ref[...] * (A_HEAD_DIM ** -0.5 * LOG2E)
    for col, gain in [(Z_AQ + r * LANES, q_gain) for r in range(A_WIDTH // LANES)] + [(Z_AK, kn_ref[...])]:
        blk = z_ref[:, col:col + LANES]
        sq = blk * blk
        s0 = jnp.sum(jnp.where(first, sq, 0.0), axis=-1, keepdims=True)
        s1 = jnp.sum(jnp.where(first, 0.0, sq), axis=-1, keepdims=True)
        ms = jnp.where(first, s0, s1) * (1.0 / A_HEAD_DIM)
        z_ref[:, col:col + LANES] = blk * lax.rsqrt(ms + EPS) * gain


def _in_proj(x, g, w, kn, qn, tm):
    n = x.shape[0]
    return pl.pallas_call(
        _in_proj_kernel,
        grid=(n // tm,),
        in_specs=[
            pl.BlockSpec((tm, D_MODEL), lambda i: (i, 0)),
            pl.BlockSpec((1, D_MODEL), lambda i: (0, 0)),
            pl.BlockSpec(w.shape, lambda i: (0, 0), pipeline_mode=pl.Buffered(1)),
            pl.BlockSpec((1, A_KV_WIDTH), lambda i: (0, 0)),
            pl.BlockSpec((1, LANES), lambda i: (0, 0)),
        ],
        out_specs=pl.BlockSpec((tm, Z_WIDTH), lambda i: (i, 0)),
        out_shape=jax.ShapeDtypeStruct((n, Z_WIDTH), F32),
        compiler_params=pltpu.CompilerParams(dimension_semantics=("arbitrary",), vmem_limit_bytes=VMEM_LIMIT),
        name="in_proj",
    )(x, g, w, kn, qn)


A_HALVES = LANES // A_HEAD_DIM
A_QBLOCKS = A_WIDTH // LANES
A_COPIES = A_KV_HEADS


def _attn_half(r, c):
    g = r * A_HALVES // A_REP
    return (g + c) % A_HALVES


def _attn_tables(sinks_ref, bias_ref, *, cq, lk, front, slack, fill):
    rows = A_QBLOCKS * cq
    nvar = front // cq + 1
    row = lax.broadcasted_iota(jnp.int32, (rows, 1), 0)
    blk = row >> _log2(cq)

    def per_row(value):
        out = []
        for c in range(A_COPIES):
            col = jnp.zeros((rows, 1), F32)
            for r in range(A_QBLOCKS):
                col = jnp.where(blk == r, value(r * A_HALVES + _attn_half(r, c)), col)
            out.append(col)
        return out

    @pl.when(fill)
    def _():
        col = lax.broadcasted_iota(jnp.int32, (rows, lk + slack), 1)
        dist = jnp.abs((row & (cq - 1)) + WINDOW - col).astype(F32)
        slopes = per_row(lambda h: 2.0 ** (-8.0 * (h + 1) / A_HEADS))
        sinks = per_row(lambda h: sinks_ref[h])
        for c in range(A_COPIES):
            for var in range(nvar):
                keys = jnp.where((col >= front - var * cq) & (col < lk), -slopes[c] * dist, NEG_INF)
                bias_ref[var, c] = jnp.where(col == lk, sinks[c], keys) * LOG2E

    first = lax.broadcasted_iota(jnp.int32, (cq, LANES), 1) < A_HEAD_DIM
    return first, nvar


def _attn_chunk(qc, kv, biases, first, gate, cq):
    blocks = [qc[:, r * LANES:(r + 1) * LANES] for r in range(A_QBLOCKS)]
    outs = []
    for c in range(A_COPIES):
        lhs = jnp.concatenate(
            [jnp.where(first if _attn_half(r, c) == 0 else jnp.logical_not(first), blocks[r], 0.0)
             for r in range(A_QBLOCKS)], axis=0)
        zero_keys = jnp.zeros((biases[c].shape[1] - kv[c][0].shape[0], A_KV_WIDTH), BF16)
        kk, vv = (jnp.concatenate([x, zero_keys], axis=0) for x in kv[c])
        sc = _dot_nt(lhs.astype(BF16), kk) + biases[c]
        p = jnp.exp2(sc - jnp.max(sc, axis=-1, keepdims=True))
        outs.append(_dot(p.astype(BF16), vv) * (1.0 / jnp.sum(p, axis=-1, keepdims=True)))
    tile = []
    for r in range(A_QBLOCKS):
        rows = slice(r * cq, (r + 1) * cq)
        low = 0 if _attn_half(r, 0) == 0 else 1
        tile.append(jnp.where(first, outs[low][rows], outs[1 - low][rows]))
    return (jnp.concatenate(tile, axis=1) * _silu(gate)).astype(BF16)


def _attn_kernel(sinks_ref, q_ref, k_ref, v_ref, ag_ref, o_ref, kp_ref, vp_ref, bias_ref, *, bt, cq, lk, off, t):
    front = WINDOW - off
    lkt = k_ref.shape[0] // bt
    n_chunks = t // cq

    for b in range(bt):
        for src, dst in ((k_ref, kp_ref), (v_ref, vp_ref)):
            x = src[b * lkt:(b + 1) * lkt, :]
            if front:
                dst[:, b, 0:front, :] = jnp.zeros((A_COPIES, front, A_KV_WIDTH), BF16)
            dst[0, b, front:front + lkt, :] = x.astype(BF16)
            dst[1, b, front:front + lkt, :] = pltpu.roll(x, A_HEAD_DIM, 1).astype(BF16)

    first, nvar = _attn_tables(sinks_ref, bias_ref, cq=cq, lk=lk, front=front, slack=bias_ref.shape[-1] - lk,
                               fill=pl.program_id(0) == 0)

    def chunk(b, cg):
        k0 = cg * cq if isinstance(cg, int) else pl.multiple_of(cg * cq, cq)
        r0 = b * t + k0
        var = min(cg, nvar - 1) if isinstance(cg, int) else jnp.minimum(cg, nvar - 1)
        kv = [(kp_ref[c, b, pl.ds(k0, lk), :], vp_ref[c, b, pl.ds(k0, lk), :]) for c in range(A_COPIES)]
        o_ref[pl.ds(r0, cq), :] = _attn_chunk(
            q_ref[pl.ds(r0, cq), :], kv, [bias_ref[var, c] for c in range(A_COPIES)], first,
            ag_ref[pl.ds(r0, cq), :], cq)

    for b in range(bt):
        if n_chunks == 1:
            chunk(b, 0)
        else:
            lax.fori_loop(0, n_chunks, lambda cg, carry, b=b: (chunk(b, cg), carry)[1], 0, unroll=min(n_chunks, 8))


def _attention(z, kbuf, vbuf, kv_specs, sinks, *, nb, bt, t, cq, lk, off):
    front = WINDOW - off
    lkt = kbuf.shape[0] // nb
    kern = functools.partial(_attn_kernel, bt=bt, cq=cq, lk=lk, off=off, t=t)
    return pl.pallas_call(
        kern,
        grid=(nb // bt,),
        in_specs=[
            pl.BlockSpec(memory_space=pltpu.SMEM),
            _z_window(bt * t, A_WIDTH, Z_AQ, lambda b: b),
            kv_specs[0],
            kv_specs[1],
            _z_window(bt * t, A_WIDTH, Z_AG, lambda b: b),
        ],
        out_specs=pl.BlockSpec((bt * t, A_WIDTH), lambda b: (b, 0)),
        out_shape=jax.ShapeDtypeStruct((nb * t, A_WIDTH), BF16),
        scratch_shapes=[
            pltpu.VMEM((A_COPIES, bt, front + lkt, A_KV_WIDTH), BF16),
            pltpu.VMEM((A_COPIES, bt, front + lkt, A_KV_WIDTH), BF16),
            pltpu.VMEM((front // cq + 1, A_COPIES, A_QBLOCKS * cq, lk + LANES - lk % LANES), F32),
        ],
        compiler_params=pltpu.CompilerParams(dimension_semantics=("arbitrary",), vmem_limit_bytes=VMEM_LIMIT),
        name="swa_attention",
    )(sinks, z, kbuf, vbuf, z)


STEP_GROUP = 4


def _attn_step_kernel(sinks_ref, q_ref, kc_ref, kn_ref, vc_ref, vn_ref, ag_ref, o_ref, bias_ref, *, bt, t):
    lk = WINDOW + t
    cols = A_HEADS * t
    key = lax.broadcasted_iota(jnp.int32, (lk, cols), 0)
    col = lax.broadcasted_iota(jnp.int32, (1, cols), 1)
    head = col >> _log2(t)
    slope = jnp.zeros((1, cols), F32)
    sink = jnp.zeros((1, cols), F32)
    for h in range(A_HEADS):
        slope = jnp.where(head == h, 2.0 ** (-8.0 * (h + 1) / A_HEADS), slope)
        sink = jnp.where(head == h, sinks_ref[h], sink)
    sink = sink * LOG2E
    bias_ref[...] = jnp.abs((col & (t - 1)) + WINDOW - key).astype(F32) * (-LOG2E * slope)
    first = lax.broadcasted_iota(jnp.int32, (t, LANES), 1) < A_HEAD_DIM

    def to_half(x, have, want):
        return x if have == want else pltpu.roll(x, A_HEAD_DIM, 1)

    def window(b, cache_ref, new_ref):
        return jnp.concatenate([cache_ref[b * WINDOW:(b + 1) * WINDOW, :].astype(F32), new_ref[b * t:(b + 1) * t, :]],
                               axis=0).astype(BF16)

    def scores(b):
        q = q_ref[b * t:(b + 1) * t, :]
        qs = []
        for h in range(A_HEADS):
            r, j, g = h // A_HALVES, h % A_HALVES, h // A_REP
            blk = to_half(q[:, r * LANES:(r + 1) * LANES], j, g)
            qs.append(jnp.where(first if g == 0 else jnp.logical_not(first), blk, 0.0))
        qs = jnp.concatenate(qs, axis=0).astype(BF16)
        return _dot_nt(window(b, kc_ref, kn_ref), qs) + bias_ref[...]

    def softmax(st):
        m = jnp.maximum(jnp.max(st, axis=0, keepdims=True), sink)
        p = jnp.exp2(st - m)
        return (p * (1.0 / (jnp.sum(p, axis=0, keepdims=True) + jnp.exp2(sink - m)))).astype(BF16)

    def write(b, o):
        tile = []
        for r in range(A_QBLOCKS):
            halves = [to_half(o[h * t:(h + 1) * t, :], h // A_REP, h % A_HALVES)
                      for h in range(r * A_HALVES, (r + 1) * A_HALVES)]
            tile.append(jnp.where(first, halves[0], halves[1]))
        rows = slice(b * t, (b + 1) * t)
        o_ref[rows, :] = (jnp.concatenate(tile, axis=1) * _silu(ag_ref[rows, :])).astype(BF16)

    for b0 in range(0, bt, STEP_GROUP):
        group = range(b0, min(b0 + STEP_GROUP, bt))
        p = [softmax(st) for st in [scores(b) for b in group]]
        o = [_dot_tn(pb, window(b, vc_ref, vn_ref)) for b, pb in zip(group, p)]
        for b, ob in zip(group, o):
            write(b, ob)


def _attention_step(z, cache_k, cache_v, sinks, *, nb, bt, t):
    kern = functools.partial(_attn_step_kernel, bt=bt, t=t)
    cache = pl.BlockSpec((bt * WINDOW, A_KV_WIDTH), lambda b: (b, 0))
    return pl.pallas_call(
        kern,
        grid=(nb // bt,),
        in_specs=[
            pl.BlockSpec(memory_space=pltpu.SMEM),
            _z_window(bt * t, A_WIDTH, Z_AQ, lambda b: b),
            cache,
            _z_window(bt * t, A_KV_WIDTH, Z_AK, lambda b: b),
            cache,
            _z_window(bt * t, A_KV_WIDTH, Z_AV, lambda b: b),
            _z_window(bt * t, A_WIDTH, Z_AG, lambda b: b),
        ],
        out_specs=pl.BlockSpec((bt * t, A_WIDTH), lambda b: (b, 0)),
        out_shape=jax.ShapeDtypeStruct((nb * t, A_WIDTH), BF16),
        scratch_shapes=[pltpu.VMEM((WINDOW + t, A_HEADS * t), F32)],
        compiler_params=pltpu.CompilerParams(dimension_semantics=("arbitrary",), vmem_limit_bytes=VMEM_LIMIT),
        name="swa_attention_step",
    )(sinks, z, cache_k, z, cache_v, z, z)


def _bf(x):
    return x.astype(BF16)


def _unit_lower_inverse(a_list, c, ri, lj, expand):
    base = SUBLANES
    same = (ri >> _log2(base)) == (lj >> _log2(base))
    eye = jnp.where(ri == lj, 1.0, 0.0)
    n1 = [jnp.where(same, -a, 0.0) for a in a_list]
    x = [eye + n for n in n1]
    n1b = [_bf(n) for n in n1]
    n2b = [_bf(_dot(nb, expand(nb))) for nb in n1b]
    n2e = [expand(nb) for nb in n2b]
    x = [xi + _dot(_bf(xi), ne) for xi, ne in zip(x, n2e)]
    n4e = [expand(_bf(_dot(nb, ne))) for nb, ne in zip(n2b, n2e)]
    x = [xi + _dot(_bf(xi), ne) for xi, ne in zip(x, n4e)]
    s = base
    while s < c:
        sel = ((ri >> _log2(2 * s)) == (lj >> _log2(2 * s))) & ((ri >> _log2(s)) != (lj >> _log2(s)))
        xb = [_bf(xi) for xi in x]
        xo = [_dot(b, expand(_bf(jnp.where(sel, a, 0.0)))) for b, a in zip(xb, a_list)]
        x = [xi - _dot(_bf(o), expand(b)) for xi, o, b in zip(x, xo, xb)]
        s *= 2
    return x


def _delta_kernel(qkv_ref, ba_ref, bg_ref, hist_ref, s0_ref, convw_ref, gate_ref, onorm_ref,
                  o_ref, sout_ref, ext_ref, st_ref, lhs_ref, add_ref, s_ref, *, bt, c, tc):
    i = pl.program_id(1)
    hd = B_HEAD_DIM
    nh = B_HEADS
    r = nh * c
    nch = tc // c
    pad = SUBLANES
    units = [(b, cc) for b in range(bt) for cc in range(nch)]

    @pl.when(i == 0)
    def _():
        ext_ref[:, 0:pad, :] = hist_ref[...]
        s_ref[...] = s0_ref[...]

    @pl.when(i > 0)
    def _():
        ext_ref[:, 0:pad, :] = ext_ref[:, tc:tc + pad, :]

    for b in range(bt):
        ext_ref[b, pad:pad + tc, :] = qkv_ref[b * tc:(b + 1) * tc, :]

    for u, (b, cc) in enumerate(units):
        e = ext_ref[b, cc * c:cc * c + pad + c, :]
        acc = e * convw_ref[0:1, :]
        for j in range(1, CONV_W):
            acc = pltpu.roll(acc, 1, 0) + e * convw_ref[j:j + 1, :]
        y = _silu(acc[pad:, :])
        for part in range(3):
            for h in range(nh):
                blk = y[:, (part * nh + h) * hd:(part * nh + h + 1) * hd]
                if part < 2:
                    blk = blk * lax.rsqrt(jnp.sum(blk * blk, axis=-1, keepdims=True) + EPS)
                st_ref[part, u, h * c:(h + 1) * c, :] = blk

    ri = lax.broadcasted_iota(jnp.int32, (c, r), 0)
    li = lax.broadcasted_iota(jnp.int32, (c, r), 1)
    lj = li & (c - 1)
    lh = li >> _log2(c)
    lower = ri >= lj
    strict = ri > lj
    head_sel = [jnp.where(lh == h, 1.0, 0.0).astype(BF16) for h in range(nh)]

    def expand(xb):
        return jnp.concatenate([xb * m for m in head_sel], axis=0)

    def lanes(x_st):
        return jnp.concatenate([x_st[h * c:(h + 1) * c] for h in range(nh)], axis=1)

    def head_blocks(x_st):
        zero = jnp.zeros((c, hd), x_st.dtype)
        return jnp.concatenate(
            [jnp.concatenate([x_st[h * c:(h + 1) * c] if h2 == h else zero for h2 in range(nh)], axis=1)
             for h in range(nh)], axis=0)

    ci = lax.broadcasted_iota(jnp.int32, (c, c), 0)
    cj = lax.broadcasted_iota(jnp.int32, (c, c), 1)
    tril = jnp.where(ci >= cj, 1.0, 0.0)
    di = lax.broadcasted_iota(jnp.int32, (hd, hd), 0)
    dj = lax.broadcasted_iota(jnp.int32, (hd, hd), 1)
    eye_hd = di == dj
    neg_rate = -jnp.exp(gate_ref[0:1, :])
    dt_bias = gate_ref[1:2, :]
    onorm = onorm_ref[...]

    def stack_cols(x, lane0):
        return jnp.concatenate([x[:, lane0 + h:lane0 + h + 1] for h in range(nh)], axis=0)

    beta, gc, gcc, gr, glast = [], [], [], [], []
    for b, cc in units:
        ba = ba_ref[b * tc + cc * c:b * tc + (cc + 1) * c, :]
        g_all = neg_rate * _softplus(ba + dt_bias)
        gcum = _dot(tril, g_all, HIGHEST)
        gcum_t = gcum.T
        beta.append(stack_cols(_sigmoid(ba), 0))
        gc.append(stack_cols(gcum, GATE_LANE))
        gcc_h = jnp.zeros((c, r), F32)
        for h in range(nh):
            gcc_h = jnp.where(lh == h, gcum[:, GATE_LANE + h:GATE_LANE + h + 1], gcc_h)
        gcc.append(gcc_h)
        gr.append(jnp.concatenate([gcum_t[GATE_LANE + h:GATE_LANE + h + 1, :] for h in range(nh)], axis=1))
        glast.append(jnp.concatenate(
            [jnp.broadcast_to(gcum[c - 1:c, GATE_LANE + h:GATE_LANE + h + 1], (c, 1)) for h in range(nh)], axis=0))

    def fold_units(us):
        n = range(len(us))
        q = [st_ref[0, u] * (hd ** -0.5) for u in us]
        k = [st_ref[1, u] for u in us]
        v = [st_ref[2, u] for u in us]
        kb = [k[j] * beta[u] for j, u in enumerate(us)]
        big = [_dot_nt(_bf(jnp.concatenate([lanes(kb[j]), lanes(q[j])], axis=0)), head_blocks(_bf(k[j]))) for j in n]
        decay = [jnp.where(lower, jnp.exp(jnp.where(lower, gcc[u] - gr[u], 0.0)), 0.0) for u in us]
        a = [jnp.where(strict, big[j][:c] * decay[j], 0.0) for j in n]
        qkd = [expand(_bf(big[j][c:] * decay[j])) for j in n]
        tinv = _unit_lower_inverse(a, c, ri, lj, expand)
        eg = [jnp.exp(gc[u]) for u in us]
        rhs = [_bf(jnp.concatenate([v[j] * beta[u], kb[j] * eg[j]], axis=1)) for j, u in enumerate(us)]
        sol = [_dot(expand(_bf(tinv[j])), rhs[j]) for j in n]
        solb = [_bf(s) for s in sol]
        fold = [_dot(qkd[j], solb[j]) for j in n]
        k_dec = [_bf(k[j] * jnp.exp(glast[u] - gc[u])) for j, u in enumerate(us)]
        for j, u in enumerate(us):
            qp = q[j] * eg[j] - fold[j][:, hd:]
            for h in range(nh):
                rows = slice(h * c, (h + 1) * c)
                kt = _dot_tn(k_dec[j][rows], solb[j][rows])
                g_tot = jnp.exp(glast[u][h * c:h * c + 1, :])
                lhs_ref[u, h, 0:hd, :] = _bf(jnp.where(eye_hd, g_tot, 0.0) - kt[:, hd:])
                lhs_ref[u, h, hd:hd + c, :] = _bf(qp[rows])
                add_ref[u, h, 0:hd, :] = kt[:, :hd]
                add_ref[u, h, hd:hd + c, :] = fold[j][rows, :hd]

    fold_units(list(range(len(units))))

    for u, (b, cc) in enumerate(units):
        rows = slice(b * tc + cc * c, b * tc + (cc + 1) * c)
        for h in range(nh):
            res = _dot(lhs_ref[u, h], _bf(s_ref[b, h])) + add_ref[u, h]
            s_ref[b, h] = res[:hd]
            o = res[hd:]
            on = o * lax.rsqrt(jnp.mean(o * o, axis=-1, keepdims=True) + EPS) * onorm
            bg = bg_ref[rows, h * hd:(h + 1) * hd]
            o_ref[rows, h * hd:(h + 1) * hd] = (on * _silu(bg)).astype(BF16)
    sout_ref[...] = s_ref[...]


def _delta(z, hist, s0, convw, gate, onorm, *, nb, bt, t, tc, c):
    nt = t // tc
    units = bt * (tc // c)
    state = (bt, B_HEADS, B_HEAD_DIM, B_HEAD_DIM)
    kern = functools.partial(_delta_kernel, bt=bt, c=c, tc=tc)
    return pl.pallas_call(
        kern,
        grid=(nb // bt, nt),
        in_specs=[
            _z_window(bt * tc, 3 * B_WIDTH, Z_BQKV, lambda b, i: b * nt + i),
            _z_window(bt * tc, LANES, Z_BA, lambda b, i: b * nt + i),
            _z_window(bt * tc, B_WIDTH, Z_BG, lambda b, i: b * nt + i),
            pl.BlockSpec((bt, SUBLANES, 3 * B_WIDTH), lambda b, i: (b, 0, 0)),
            pl.BlockSpec(state, lambda b, i: (b, 0, 0, 0)),
            pl.BlockSpec((CONV_W, 3 * B_WIDTH), lambda b, i: (0, 0)),
            pl.BlockSpec((2, LANES), lambda b, i: (0, 0)),
            pl.BlockSpec((1, B_HEAD_DIM), lambda b, i: (0, 0)),
        ],
        out_specs=[
            pl.BlockSpec((bt * tc, B_WIDTH), lambda b, i: (b * nt + i, 0)),
            pl.BlockSpec(state, lambda b, i: (b, 0, 0, 0)),
        ],
        out_shape=[
            jax.ShapeDtypeStruct((nb * t, B_WIDTH), BF16),
            jax.ShapeDtypeStruct((nb, B_HEADS, B_HEAD_DIM, B_HEAD_DIM), F32),
        ],
        scratch_shapes=[
            pltpu.VMEM((bt, SUBLANES + tc, 3 * B_WIDTH), F32),
            pltpu.VMEM((3, units, B_HEADS * c, B_HEAD_DIM), F32),
            pltpu.VMEM((units, B_HEADS, B_HEAD_DIM + c, B_HEAD_DIM), BF16),
            pltpu.VMEM((units, B_HEADS, B_HEAD_DIM + c, B_HEAD_DIM), F32),
            pltpu.VMEM(state, F32),
        ],
        compiler_params=pltpu.CompilerParams(dimension_semantics=("arbitrary", "arbitrary"),
                                             vmem_limit_bytes=VMEM_LIMIT),
        name="gated_delta",
    )(z, z, z, hist, s0, convw, gate, onorm)


def _pool_kernel(h_ref, oa_ref, ob_ref, wa_ref, wb_ref, hist_ref, g_ref, win_ref, wgrp_ref, scale_ref, wout_ref,
                 y_ref, tail_ref, ext_ref, *, bt, tt, pos0):
    i = pl.program_id(1)
    pad = POOL_HIST + 1
    rows = bt * tt

    @pl.when(i == 0)
    def _():
        pos = pos0 - pad + lax.broadcasted_iota(jnp.int32, (1, pad, 1), 1)
        ext_ref[:, 0:pad, :] = jnp.where(pos >= 0, hist_ref[...], 0.0)

    @pl.when(i > 0)
    def _():
        ext_ref[:, 0:pad, :] = ext_ref[:, tt:tt + pad, :]

    x = h_ref[...] + _dot(oa_ref[...], wa_ref[...]) + _dot(ob_ref[...], wb_ref[...])
    xn = x * lax.rsqrt(jnp.mean(x * x, axis=-1, keepdims=True) + EPS) * g_ref[...]
    z = _dot(xn.astype(BF16), win_ref[...])
    u = z[:, :C_WIDTH]
    gate = z[:, C_WIDTH:]
    ext_ref[:, pad:pad + tt, :] = u.reshape(bt, tt, C_WIDTH)
    tail_ref[...] = ext_ref[:, tt:tt + pad, :]

    tpos = pos0 + i * tt + (lax.broadcasted_iota(jnp.int32, (rows, 1), 0) & (tt - 1))
    mixed = []
    for gi, w in enumerate(POOL_SIZES):
        cols = slice(gi * C_GROUP, (gi + 1) * C_GROUP)
        s = ext_ref[:, :, cols].reshape(bt * (pad + tt), C_GROUP)
        sh = 1
        while sh < w:
            s = s + pltpu.roll(s, sh, 0)
            sh *= 2
        s = s.reshape(bt, pad + tt, C_GROUP)[:, pad:, :].reshape(rows, C_GROUP)
        cnt = jnp.minimum(tpos + 1, w).astype(F32)
        pooled = s / cnt - u[:, cols]
        m = _dot(pooled.astype(BF16), wgrp_ref[gi]) * scale_ref[:, cols]
        mixed.append((m * _silu(gate[:, cols])).astype(BF16))
    y_ref[...] = x + _dot(jnp.concatenate(mixed, axis=1), wout_ref[...])


def _pool_layer(h, oa, ob, wa, wb, hist, g, win, wgrp, scale, wout, *, nb, bt, t, tt, pos0):
    nt = t // tt
    pad = POOL_HIST + 1
    rows = bt * tt
    kern = functools.partial(_pool_kernel, bt=bt, tt=tt, pos0=pos0)
    tile = lambda b, i: (b * nt + i, 0)
    const2 = lambda b, i: (0, 0)
    once = pl.Buffered(1)
    return pl.pallas_call(
        kern,
        grid=(nb // bt, nt),
        in_specs=[
            pl.BlockSpec((rows, D_MODEL), tile),
            pl.BlockSpec((rows, A_WIDTH), tile),
            pl.BlockSpec((rows, B_WIDTH), tile),
            pl.BlockSpec((A_WIDTH, D_MODEL), const2, pipeline_mode=once),
            pl.BlockSpec((B_WIDTH, D_MODEL), const2, pipeline_mode=once),
            pl.BlockSpec((bt, pad, C_WIDTH), lambda b, i: (b, 0, 0)),
            pl.BlockSpec((1, D_MODEL), const2),
            pl.BlockSpec((D_MODEL, 2 * C_WIDTH), const2, pipeline_mode=once),
            pl.BlockSpec((len(POOL_SIZES), C_GROUP, C_GROUP), lambda b, i: (0, 0, 0), pipeline_mode=once),
            pl.BlockSpec((1, C_WIDTH), const2),
            pl.BlockSpec((C_WIDTH, D_MODEL), const2, pipeline_mode=once),
        ],
        out_specs=[
            pl.BlockSpec((rows, D_MODEL), tile),
            pl.BlockSpec((bt, pad, C_WIDTH), lambda b, i: (b, 0, 0)),
        ],
        out_shape=[
            jax.ShapeDtypeStruct((nb * t, D_MODEL), F32),
            jax.ShapeDtypeStruct((nb, pad, C_WIDTH), F32),
        ],
        scratch_shapes=[pltpu.VMEM((bt, pad + tt, C_WIDTH), F32)],
        compiler_params=pltpu.CompilerParams(dimension_semantics=("arbitrary", "arbitrary"),
                                             vmem_limit_bytes=VMEM_LIMIT),
        name="out_proj_pool_layer",
    )(h, oa, ob, wa, wb, hist, g, win, wgrp, scale, wout)


def _ab_weights(norm_g, w_in, q_norm, k_norm, sinks, conv_w, a_log, dt_bias, o_norm, w_out):
    w = jnp.swapaxes(w_in, 0, 1).astype(BF16)
    lane_pad = (GATE_LANE, LANES - GATE_LANE - B_HEADS)
    gate = jnp.stack([jnp.pad(a_log.astype(F32), lane_pad), jnp.pad(dt_bias.astype(F32), lane_pad)])
    return dict(
        norm_g=norm_g.reshape(1, D_MODEL), w=w, kn=jnp.tile(k_norm, A_KV_HEADS).reshape(1, A_KV_WIDTH),
        qn=jnp.tile(q_norm, A_KV_HEADS).reshape(1, LANES), sinks=sinks.astype(F32), conv_w=conv_w, gate=gate,
        onorm=o_norm.reshape(1, B_HEAD_DIM), wa=w_out[:A_WIDTH].astype(BF16), wb=w_out[A_WIDTH:].astype(BF16))


def _group_tiles(t):
    if t >= CHUNK:
        return dict(proj_rows=1024, seqs_attn=1, seqs_delta=1, delta_tokens=1024, chunk=CHUNK, seqs_pool=1,
                    pool_tokens=1024)
    return dict(proj_rows=256, seqs_attn=32, seqs_delta=16, delta_tokens=t, chunk=t, seqs_pool=256 // t, pool_tokens=t)


def _ab_mixers(h, wts, cache_k, cache_v, s0, conv_hist, *, nb, t):
    tl = _group_tiles(t)
    z = _in_proj(h, wts["norm_g"], wts["w"], wts["kn"], wts["qn"], min(nb * t, tl["proj_rows"]))
    bt = tl["seqs_attn"]
    if cache_k is None:
        kv_specs = [_z_window(bt * t, A_KV_WIDTH, Z_AK, lambda b: b), _z_window(bt * t, A_KV_WIDTH, Z_AV, lambda b: b)]
        o_a = _attention(z, z, z, kv_specs, wts["sinks"], nb=nb, bt=bt, t=t, cq=CHUNK, lk=WINDOW + CHUNK, off=0)
    else:
        o_a = _attention_step(z, cache_k.reshape(nb * WINDOW, A_KV_WIDTH), cache_v.reshape(nb * WINDOW, A_KV_WIDTH),
                              wts["sinks"], nb=nb, bt=bt, t=t)
    hist = jnp.pad(conv_hist.astype(F32), ((0, 0), (SUBLANES - (CONV_W - 1), 0), (0, 0)))
    o_b, s_new = _delta(z, hist, s0.astype(F32), wts["conv_w"], wts["gate"], wts["onorm"], nb=nb,
                        bt=tl["seqs_delta"], t=t, tc=tl["delta_tokens"], c=tl["chunk"])
    return o_a, o_b, z, s_new


def _last_rows(z, col, width, nb, t, rows):
    return z.reshape(nb, t, Z_WIDTH)[:, t - rows:, col:col + width]


def _cache_rows(z, col, nb, t, rows):
    return _last_rows(z, col, A_KV_WIDTH, nb, t, rows).reshape(nb, rows, A_KV_HEADS, A_HEAD_DIM)


def kernel(x_prompt, x_sample, cache_a_k, cache_a_v, state_b_s, state_b_conv, state_c_pool,
           norm_ab, w_in_ab, q_norm_a, k_norm_a, sinks_a, conv_b, a_log_b, dt_bias_b, o_norm_b, w_out_ab,
           norm_c, w_in_c, w_grp_c, scale_c, w_out_c):
    bp, tp, _ = x_prompt.shape
    bs, ts, _ = x_sample.shape
    hp = x_prompt.reshape(bp * tp, D_MODEL)
    hs = x_sample.reshape(bs * ts, D_MODEL)

    wts = _ab_weights(norm_ab[0], w_in_ab[0], q_norm_a[0], k_norm_a[0], sinks_a[0], conv_b[0], a_log_b[0],
                      dt_bias_b[0], o_norm_b[0], w_out_ab[0])
    s0 = jnp.zeros((bp, B_HEADS, B_HEAD_DIM, B_HEAD_DIM), F32)
    c0 = jnp.zeros((bp, CONV_W - 1, 3 * B_WIDTH), F32)
    oap, obp, zp, sp = _ab_mixers(hp, wts, None, None, s0, c0, nb=bp, t=tp)
    oas, obs, zs, ss = _ab_mixers(hs, wts, cache_a_k[0], cache_a_v[0], state_b_s[0], state_b_conv[0], nb=bs, t=ts)
    p_a_k = _cache_rows(zp, Z_AK, bp, tp, WINDOW)[None]
    p_a_v = _cache_rows(zp, Z_AV, bp, tp, WINDOW)[None]
    s_a_k = _cache_rows(zs, Z_AK, bs, ts, ts)[None]
    s_a_v = _cache_rows(zs, Z_AV, bs, ts, ts)[None]
    p_b_conv = _last_rows(zp, Z_BQKV, 3 * B_WIDTH, bp, tp, CONV_W - 1)[None]
    s_b_conv = _last_rows(zs, Z_BQKV, 3 * B_WIDTH, bs, ts, CONV_W - 1)[None]

    g_c = norm_c[0].reshape(1, D_MODEL)
    win = w_in_c[0].astype(BF16)
    wgrp = w_grp_c[0].astype(BF16)
    scale = scale_c[0].reshape(1, C_WIDTH)
    wout = w_out_c[0].astype(BF16)
    h0 = jnp.zeros((bp, POOL_HIST + 1, C_WIDTH), F32)
    hs0 = jnp.pad(state_c_pool[0].astype(F32), ((0, 0), (1, 0), (0, 0)))
    tlp, tls = _group_tiles(tp), _group_tiles(ts)
    yp, tail_p = _pool_layer(hp, oap, obp, wts["wa"], wts["wb"], h0, g_c, win, wgrp, scale, wout, nb=bp,
                             bt=tlp["seqs_pool"], t=tp, tt=tlp["pool_tokens"], pos0=0)
    ys, tail_s = _pool_layer(hs, oas, obs, wts["wa"], wts["wb"], hs0, g_c, win, wgrp, scale, wout, nb=bs,
                             bt=tls["seqs_pool"], t=ts, tt=tls["pool_tokens"], pos0=PAST_LEN)

    return (yp.reshape(bp, tp, D_MODEL), ys.reshape(bs, ts, D_MODEL),
            p_a_k, p_a_v, sp[None], p_b_conv, tail_p[:, 1:][None],
            s_a_k, s_a_v, ss[None], s_b_conv, tail_s[:, 1:][None])
```

```python
import functools

import jax
import jax.numpy as jnp
from jax import lax
from jax.experimental import pallas as pl
from jax.experimental.pallas import tpu as pltpu

F32 = jnp.float32
BF16 = jnp.bfloat16
HIGHEST = lax.Precision.HIGHEST

D_MODEL = 1024
CHUNK = 64
PAST_LEN = 2048
EPS = 1e-6
NEG_INF = -1e30
LOG2E = 1.4426950408889634

A_HEADS = 8
A_KV_HEADS = 2
A_HEAD_DIM = 64
A_WIDTH = A_HEADS * A_HEAD_DIM
A_KV_WIDTH = A_KV_HEADS * A_HEAD_DIM
A_REP = A_HEADS // A_KV_HEADS
WINDOW = 128

B_HEADS = 4
B_HEAD_DIM = 128
B_WIDTH = B_HEADS * B_HEAD_DIM
CONV_W = 4

POOL_SIZES = (2, 4, 8, 16)
C_WIDTH = D_MODEL
C_GROUP = C_WIDTH // len(POOL_SIZES)
POOL_HIST = max(POOL_SIZES) - 1

LANES = 128
SUBLANES = 8

Z_AQ = 0
Z_AK = Z_AQ + A_WIDTH
Z_AV = Z_AK + A_KV_WIDTH
Z_AG = Z_AV + A_KV_WIDTH
Z_BQKV = Z_AG + A_WIDTH
Z_BG = Z_BQKV + 3 * B_WIDTH
Z_BA = Z_BG + B_WIDTH
Z_WIDTH = Z_BA + LANES
GATE_LANE = B_HEADS


def _z_window(rows, width, col, row_block):
    return pl.BlockSpec((pl.Element(rows), pl.Element(width)), lambda *idx: (row_block(*idx) * rows, col))

VMEM_LIMIT = 48 * 1024 * 1024


def _sigmoid(x):
    return 1.0 / (1.0 + jnp.exp(-x))


def _silu(x):
    return x * _sigmoid(x)


def _softplus(x):
    return jnp.maximum(x, 0.0) + jnp.log(1.0 + jnp.exp(-jnp.abs(x)))


def _dot(a, b, precision=None):
    return jnp.dot(a, b, preferred_element_type=F32, precision=precision)


def _dot_nt(a, b, precision=None):
    return lax.dot_general(a, b, (((1,), (1,)), ((), ())), preferred_element_type=F32, precision=precision)


def _dot_tn(a, b, precision=None):
    return lax.dot_general(a, b, (((0,), (0,)), ((), ())), preferred_element_type=F32, precision=precision)


def _log2(n):
    assert n & (n - 1) == 0
    return n.bit_length() - 1


def _in_proj_kernel(x_ref, g_ref, w_ref, kn_ref, qn_ref, z_ref):
    x = x_ref[...]
    xn = x * lax.rsqrt(jnp.mean(x * x, axis=-1, keepdims=True) + EPS) * g_ref[...]
    n_in = w_ref.shape[0]
    z_ref[:, Z_BA:Z_WIDTH] = jnp.zeros((x.shape[0], Z_WIDTH - Z_BA), F32)
    z_ref[:, 0:n_in] = _dot_nt(xn.astype(BF16), w_ref[...])
    first = lax.broadcasted_iota(jnp.int32, (x.shape[0], LANES), 1) < A_HEAD_DIM
    q_gain = qn_ref[...] * (A_HEAD_DIM ** -0.5 * LOG2E)
    for col, gain in [(Z_AQ + r * LANES, q_gain) for r in range(A_WIDTH // LANES)] + [(Z_AK, kn_ref[...])]:
        blk = z_ref[:, col:col + LANES]
        sq = blk * blk
        s0 = jnp.sum(jnp.where(first, sq, 0.0), axis=-1, keepdims=True)
        s1 = jnp.sum(jnp.where(first, 0.0, sq), axis=-1, keepdims=True)
        ms = jnp.where(first, s0, s1) * (1.0 / A_HEAD_DIM)
        z_ref[:, col:col + LANES] = blk * lax.rsqrt(ms + EPS) * gain


def _in_proj(x, g, w, kn, qn, tm):
    n = x.shape[0]
    return pl.pallas_call(
        _in_proj_kernel,
        grid=(n // tm,),
        in_specs=[
            pl.BlockSpec((tm, D_MODEL), lambda i: (i, 0)),
            pl.BlockSpec((1, D_MODEL), lambda i: (0, 0)),
            pl.BlockSpec(w.shape, lambda i: (0, 0), pipeline_mode=pl.Buffered(1)),
            pl.BlockSpec((1, A_KV_WIDTH), lambda i: (0, 0)),
            pl.BlockSpec((1, LANES), lambda i: (0, 0)),
        ],
        out_specs=pl.BlockSpec((tm, Z_WIDTH), lambda i: (i, 0)),
        out_shape=jax.ShapeDtypeStruct((n, Z_WIDTH), F32),
        compiler_params=pltpu.CompilerParams(dimension_semantics=("arbitrary",), vmem_limit_bytes=VMEM_LIMIT),
        name="in_proj",
    )(x, g, w, kn, qn)


A_HALVES = LANES // A_HEAD_DIM
A_QBLOCKS = A_WIDTH // LANES
A_COPIES = A_KV_HEADS


def _attn_half(r, c):
    g = r * A_HALVES // A_REP
    return (g + c) % A_HALVES


def _attn_tables(sinks_ref, bias_ref, *, cq, lk, front, slack, fill):
    rows = A_QBLOCKS * cq
    nvar = front // cq + 1
    row = lax.broadcasted_iota(jnp.int32, (rows, 1), 0)
    blk = row >> _log2(cq)

    def per_row(value):
        out = []
        for c in range(A_COPIES):
            col = jnp.zeros((rows, 1), F32)
            for r in range(A_QBLOCKS):
                col = jnp.where(blk == r, value(r * A_HALVES + _attn_half(r, c)), col)
            out.append(col)
        return out

    @pl.when(fill)
    def _():
        col = lax.broadcasted_iota(jnp.int32, (rows, lk + slack), 1)
        dist = jnp.abs((row & (cq - 1)) + WINDOW - col).astype(F32)
        slopes = per_row(lambda h: 2.0 ** (-8.0 * (h + 1) / A_HEADS))
        sinks = per_row(lambda h: sinks_ref[h])
        for c in range(A_COPIES):
            for var in range(nvar):
                keys = jnp.where((col >= front - var * cq) & (col < lk), -slopes[c] * dist, NEG_INF)
                bias_ref[var, c] = jnp.where(col == lk, sinks[c], keys) * LOG2E

    first = lax.broadcasted_iota(jnp.int32, (cq, LANES), 1) < A_HEAD_DIM
    return first, nvar


def _attn_chunk(qc, kv, biases, first, gate, cq):
    blocks = [qc[:, r * LANES:(r + 1) * LANES] for r in range(A_QBLOCKS)]
    outs = []
    for c in range(A_COPIES):
        lhs = jnp.concatenate(
            [jnp.where(first if _attn_half(r, c) == 0 else jnp.logical_not(first), blocks[r], 0.0)
             for r in range(A_QBLOCKS)], axis=0)
        zero_keys = jnp.zeros((biases[c].shape[1] - kv[c][0].shape[0], A_KV_WIDTH), BF16)
        kk, vv = (jnp.concatenate([x, zero_keys], axis=0) for x in kv[c])
        sc = _dot_nt(lhs.astype(BF16), kk) + biases[c]
        p = jnp.exp2(sc - jnp.max(sc, axis=-1, keepdims=True))
        outs.append(_dot(p.astype(BF16), vv) * (1.0 / jnp.sum(p, axis=-1, keepdims=True)))
    tile = []
    for r in range(A_QBLOCKS):
        rows = slice(r * cq, (r + 1) * cq)
        low = 0 if _attn_half(r, 0) == 0 else 1
        tile.append(jnp.where(first, outs[low][rows], outs[1 - low][rows]))
    return (jnp.concatenate(tile, axis=1) * _silu(gate)).astype(BF16)


def _attn_kernel(sinks_ref, q_ref, k_ref, v_ref, ag_ref, o_ref, kp_ref, vp_ref, bias_ref, *, bt, cq, lk, off, t):
    front = WINDOW - off
    lkt = k_ref.shape[0] // bt
    n_chunks = t // cq

    for b in range(bt):
        for src, dst in ((k_ref, kp_ref), (v_ref, vp_ref)):
            x = src[b * lkt:(b + 1) * lkt, :]
            if front:
                dst[:, b, 0:front, :] = jnp.zeros((A_COPIES, front, A_KV_WIDTH), BF16)
            dst[0, b, front:front + lkt, :] = x.astype(BF16)
            dst[1, b, front:front + lkt, :] = pltpu.roll(x, A_HEAD_DIM, 1).astype(BF16)

    first, nvar = _attn_tables(sinks_ref, bias_ref, cq=cq, lk=lk, front=front, slack=bias_ref.shape[-1] - lk,
                               fill=pl.program_id(0) == 0)

    def chunk(b, cg):
        k0 = cg * cq if isinstance(cg, int) else pl.multiple_of(cg * cq, cq)
        r0 = b * t + k0
        var = min(cg, nvar - 1) if isinstance(cg, int) else jnp.minimum(cg, nvar - 1)
        kv = [(kp_ref[c, b, pl.ds(k0, lk), :], vp_ref[c, b, pl.ds(k0, lk), :]) for c in range(A_COPIES)]
        o_ref[pl.ds(r0, cq), :] = _attn_chunk(
            q_ref[pl.ds(r0, cq), :], kv, [bias_ref[var, c] for c in range(A_COPIES)], first,
            ag_ref[pl.ds(r0, cq), :], cq)

    for b in range(bt):
        if n_chunks == 1:
            chunk(b, 0)
        else:
            lax.fori_loop(0, n_chunks, lambda cg, carry, b=b: (chunk(b, cg), carry)[1], 0, unroll=min(n_chunks, 8))


def _attention(z, kbuf, vbuf, kv_specs, sinks, *, nb, bt, t, cq, lk, off):
    front = WINDOW - off
    lkt = kbuf.shape[0] // nb
    kern = functools.partial(_attn_kernel, bt=bt, cq=cq, lk=lk, off=off, t=t)
    return pl.pallas_call(
        kern,
        grid=(nb // bt,),
        in_specs=[
            pl.BlockSpec(memory_space=pltpu.SMEM),
            _z_window(bt * t, A_WIDTH, Z_AQ, lambda b: b),
            kv_specs[0],
            kv_specs[1],
            _z_window(bt * t, A_WIDTH, Z_AG, lambda b: b),
        ],
        out_specs=pl.BlockSpec((bt * t, A_WIDTH), lambda b: (b, 0)),
        out_shape=jax.ShapeDtypeStruct((nb * t, A_WIDTH), BF16),
        scratch_shapes=[
            pltpu.VMEM((A_COPIES, bt, front + lkt, A_KV_WIDTH), BF16),
            pltpu.VMEM((A_COPIES, bt, front + lkt, A_KV_WIDTH), BF16),
            pltpu.VMEM((front // cq + 1, A_COPIES, A_QBLOCKS * cq, lk + LANES - lk % LANES), F32),
        ],
        compiler_params=pltpu.CompilerParams(dimension_semantics=("arbitrary",), vmem_limit_bytes=VMEM_LIMIT),
        name="swa_attention",
    )(sinks, z, kbuf, vbuf, z)


STEP_GROUP = 4


def _attn_step_kernel(sinks_ref, q_ref, k_ref, v_ref, ag_ref, o_ref, bias_ref, *, bt, t):
    lk = WINDOW + t
    cols = A_HEADS * t
    key = lax.broadcasted_iota(jnp.int32, (lk, cols), 0)
    col = lax.broadcasted_iota(jnp.int32, (1, cols), 1)
    head = col >> _log2(t)
    slope = jnp.zeros((1, cols), F32)
    sink = jnp.zeros((1, cols), F32)
    for h in range(A_HEADS):
        slope = jnp.where(head == h, 2.0 ** (-8.0 * (h + 1) / A_HEADS), slope)
        sink = jnp.where(head == h, sinks_ref[h], sink)
    sink = sink * LOG2E
    bias_ref[...] = jnp.abs((col & (t - 1)) + WINDOW - key).astype(F32) * (-LOG2E * slope)
    first = lax.broadcasted_iota(jnp.int32, (t, LANES), 1) < A_HEAD_DIM

    def to_half(x, have, want):
        return x if have == want else pltpu.roll(x, A_HEAD_DIM, 1)

    def scores(b):
        q = q_ref[b * t:(b + 1) * t, :]
        qs = []
        for h in range(A_HEADS):
            r, j, g = h // A_HALVES, h % A_HALVES, h // A_REP
            blk = to_half(q[:, r * LANES:(r + 1) * LANES], j, g)
            qs.append(jnp.where(first if g == 0 else jnp.logical_not(first), blk, 0.0))
        qs = jnp.concatenate(qs, axis=0).astype(BF16)
        return _dot_nt(k_ref[b * lk:(b + 1) * lk, :].astype(BF16), qs) + bias_ref[...]

    def softmax(st):
        m = jnp.maximum(jnp.max(st, axis=0, keepdims=True), sink)
        p = jnp.exp2(st - m)
        return (p * (1.0 / (jnp.sum(p, axis=0, keepdims=True) + jnp.exp2(sink - m)))).astype(BF16)

    def write(b, o):
        tile = []
        for r in range(A_QBLOCKS):
            halves = [to_half(o[h * t:(h + 1) * t, :], h // A_REP, h % A_HALVES)
                      for h in range(r * A_HALVES, (r + 1) * A_HALVES)]
            tile.append(jnp.where(first, halves[0], halves[1]))
        rows = slice(b * t, (b + 1) * t)
        o_ref[rows, :] = (jnp.concatenate(tile, axis=1) * _silu(ag_ref[rows, :])).astype(BF16)

    for b0 in range(0, bt, STEP_GROUP):
        group = range(b0, min(b0 + STEP_GROUP, bt))
        p = [softmax(st) for st in [scores(b) for b in group]]
        o = [_dot_tn(pb, v_ref[b * lk:(b + 1) * lk, :].astype(BF16)) for b, pb in zip(group, p)]
        for b, ob in zip(group, o):
            write(b, ob)


def _attention_step(z, kbuf, vbuf, sinks, *, nb, bt, t):
    kern = functools.partial(_attn_step_kernel, bt=bt, t=t)
    keys = pl.BlockSpec((bt * (WINDOW + t), A_KV_WIDTH), lambda b: (b, 0))
    return pl.pallas_call(
        kern,
        grid=(nb // bt,),
        in_specs=[
            pl.BlockSpec(memory_space=pltpu.SMEM),
            _z_window(bt * t, A_WIDTH, Z_AQ, lambda b: b),
            keys,
            keys,
            _z_window(bt * t, A_WIDTH, Z_AG, lambda b: b),
        ],
        out_specs=pl.BlockSpec((bt * t, A_WIDTH), lambda b: (b, 0)),
        out_shape=jax.ShapeDtypeStruct((nb * t, A_WIDTH), BF16),
        scratch_shapes=[pltpu.VMEM((WINDOW + t, A_HEADS * t), F32)],
        compiler_params=pltpu.CompilerParams(dimension_semantics=("arbitrary",), vmem_limit_bytes=VMEM_LIMIT),
        name="swa_attention_step",
    )(sinks, z, kbuf, vbuf, z)


def _bf(x):
    return x.astype(BF16)


def _unit_lower_inverse(a_list, c, ri, lj, expand):
    base = SUBLANES
    same = (ri >> _log2(base)) == (lj >> _log2(base))
    eye = jnp.where(ri == lj, 1.0, 0.0)
    n1 = [jnp.where(same, -a, 0.0) for a in a_list]
    x = [eye + n for n in n1]
    n1b = [_bf(n) for n in n1]
    n2b = [_bf(_dot(nb, expand(nb))) for nb in n1b]
    n2e = [expand(nb) for nb in n2b]
    x = [xi + _dot(_bf(xi), ne) for xi, ne in zip(x, n2e)]
    n4e = [expand(_bf(_dot(nb, ne))) for nb, ne in zip(n2b, n2e)]
    x = [xi + _dot(_bf(xi), ne) for xi, ne in zip(x, n4e)]
    s = base
    while s < c:
        sel = ((ri >> _log2(2 * s)) == (lj >> _log2(2 * s))) & ((ri >> _log2(s)) != (lj >> _log2(s)))
        xb = [_bf(xi) for xi in x]
        xo = [_dot(b, expand(_bf(jnp.where(sel, a, 0.0)))) for b, a in zip(xb, a_list)]
        x = [xi - _dot(_bf(o), expand(b)) for xi, o, b in zip(x, xo, xb)]
        s *= 2
    return x


def _delta_kernel(qkv_ref, ba_ref, bg_ref, hist_ref, s0_ref, convw_ref, gate_ref, onorm_ref,
                  o_ref, sout_ref, ext_ref, st_ref, lhs_ref, add_ref, s_ref, *, bt, c, tc):
    i = pl.program_id(1)
    hd = B_HEAD_DIM
    nh = B_HEADS
    r = nh * c
    nch = tc // c
    pad = SUBLANES
    units = [(b, cc) for b in range(bt) for cc in range(nch)]

    @pl.when(i == 0)
    def _():
        ext_ref[:, 0:pad, :] = hist_ref[...]
        s_ref[...] = s0_ref[...]

    @pl.when(i > 0)
    def _():
        ext_ref[:, 0:pad, :] = ext_ref[:, tc:tc + pad, :]

    for b in range(bt):
        ext_ref[b, pad:pad + tc, :] = qkv_ref[b * tc:(b + 1) * tc, :]

    for u, (b, cc) in enumerate(units):
        e = ext_ref[b, cc * c:cc * c + pad + c, :]
        acc = e * convw_ref[0:1, :]
        for j in range(1, CONV_W):
            acc = pltpu.roll(acc, 1, 0) + e * convw_ref[j:j + 1, :]
        y = _silu(acc[pad:, :])
        for part in range(3):
            for h in range(nh):
                blk = y[:, (part * nh + h) * hd:(part * nh + h + 1) * hd]
                if part < 2:
                    blk = blk * lax.rsqrt(jnp.sum(blk * blk, axis=-1, keepdims=True) + EPS)
                st_ref[part, u, h * c:(h + 1) * c, :] = blk

    ri = lax.broadcasted_iota(jnp.int32, (c, r), 0)
    li = lax.broadcasted_iota(jnp.int32, (c, r), 1)
    lj = li & (c - 1)
    lh = li >> _log2(c)
    lower = ri >= lj
    strict = ri > lj
    head_sel = [jnp.where(lh == h, 1.0, 0.0).astype(BF16) for h in range(nh)]

    def expand(xb):
        return jnp.concatenate([xb * m for m in head_sel], axis=0)

    def lanes(x_st):
        return jnp.concatenate([x_st[h * c:(h + 1) * c] for h in range(nh)], axis=1)

    def head_blocks(x_st):
        zero = jnp.zeros((c, hd), x_st.dtype)
        return jnp.concatenate(
            [jnp.concatenate([x_st[h * c:(h + 1) * c] if h2 == h else zero for h2 in range(nh)], axis=1)
             for h in range(nh)], axis=0)

    ci = lax.broadcasted_iota(jnp.int32, (c, c), 0)
    cj = lax.broadcasted_iota(jnp.int32, (c, c), 1)
    tril = jnp.where(ci >= cj, 1.0, 0.0)
    di = lax.broadcasted_iota(jnp.int32, (hd, hd), 0)
    dj = lax.broadcasted_iota(jnp.int32, (hd, hd), 1)
    eye_hd = di == dj
    neg_rate = -jnp.exp(gate_ref[0:1, :])
    dt_bias = gate_ref[1:2, :]
    onorm = onorm_ref[...]

    def stack_cols(x, lane0):
        return jnp.concatenate([x[:, lane0 + h:lane0 + h + 1] for h in range(nh)], axis=0)

    beta, gc, gcc, gr, glast = [], [], [], [], []
    for b, cc in units:
        ba = ba_ref[b * tc + cc * c:b * tc + (cc + 1) * c, :]
        g_all = neg_rate * _softplus(ba + dt_bias)
        gcum = _dot(tril, g_all, HIGHEST)
        gcum_t = gcum.T
        beta.append(stack_cols(_sigmoid(ba), 0))
        gc.append(stack_cols(gcum, GATE_LANE))
        gcc_h = jnp.zeros((c, r), F32)
        for h in range(nh):
            gcc_h = jnp.where(lh == h, gcum[:, GATE_LANE + h:GATE_LANE + h + 1], gcc_h)
        gcc.append(gcc_h)
        gr.append(jnp.concatenate([gcum_t[GATE_LANE + h:GATE_LANE + h + 1, :] for h in range(nh)], axis=1))
        glast.append(jnp.concatenate(
            [jnp.broadcast_to(gcum[c - 1:c, GATE_LANE + h:GATE_LANE + h + 1], (c, 1)) for h in range(nh)], axis=0))

    def fold_units(us):
        n = range(len(us))
        q = [st_ref[0, u] * (hd ** -0.5) for u in us]
        k = [st_ref[1, u] for u in us]
        v = [st_ref[2, u] for u in us]
        kb = [k[j] * beta[u] for j, u in enumerate(us)]
        big = [_dot_nt(_bf(jnp.concatenate([lanes(kb[j]), lanes(q[j])], axis=0)), head_blocks(_bf(k[j]))) for j in n]
        decay = [jnp.where(lower, jnp.exp(jnp.where(lower, gcc[u] - gr[u], 0.0)), 0.0) for u in us]
        a = [jnp.where(strict, big[j][:c] * decay[j], 0.0) for j in n]
        qkd = [expand(_bf(big[j][c:] * decay[j])) for j in n]
        tinv = _unit_lower_inverse(a, c, ri, lj, expand)
        eg = [jnp.exp(gc[u]) for u in us]
        rhs = [_bf(jnp.concatenate([v[j] * beta[u], kb[j] * eg[j]], axis=1)) for j, u in enumerate(us)]
        sol = [_dot(expand(_bf(tinv[j])), rhs[j]) for j in n]
        solb = [_bf(s) for s in sol]
        fold = [_dot(qkd[j], solb[j]) for j in n]
        k_dec = [_bf(k[j] * jnp.exp(glast[u] - gc[u])) for j, u in enumerate(us)]
        for j, u in enumerate(us):
            qp = q[j] * eg[j] - fold[j][:, hd:]
            for h in range(nh):
                rows = slice(h * c, (h + 1) * c)
                kt = _dot_tn(k_dec[j][rows], solb[j][rows])
                g_tot = jnp.exp(glast[u][h * c:h * c + 1, :])
                lhs_ref[u, h, 0:hd, :] = _bf(jnp.where(eye_hd, g_tot, 0.0) - kt[:, hd:])
                lhs_ref[u, h, hd:hd + c, :] = _bf(qp[rows])
                add_ref[u, h, 0:hd, :] = kt[:, :hd]
                add_ref[u, h, hd:hd + c, :] = fold[j][rows, :hd]

    fold_units(list(range(len(units))))

    for u, (b, cc) in enumerate(units):
        rows = slice(b * tc + cc * c, b * tc + (cc + 1) * c)
        for h in range(nh):
            res = _dot(lhs_ref[u, h], _bf(s_ref[b, h])) + add_ref[u, h]
            s_ref[b, h] = res[:hd]
            o = res[hd:]
            on = o * lax.rsqrt(jnp.mean(o * o, axis=-1, keepdims=True) + EPS) * onorm
            bg = bg_ref[rows, h * hd:(h + 1) * hd]
            o_ref[rows, h * hd:(h + 1) * hd] = (on * _silu(bg)).astype(BF16)
    sout_ref[...] = s_ref[...]


def _delta(z, hist, s0, convw, gate, onorm, *, nb, bt, t, tc, c):
    nt = t // tc
    units = bt * (tc // c)
    state = (bt, B_HEADS, B_HEAD_DIM, B_HEAD_DIM)
    kern = functools.partial(_delta_kernel, bt=bt, c=c, tc=tc)
    return pl.pallas_call(
        kern,
        grid=(nb // bt, nt),
        in_specs=[
            _z_window(bt * tc, 3 * B_WIDTH, Z_BQKV, lambda b, i: b * nt + i),
            _z_window(bt * tc, LANES, Z_BA, lambda b, i: b * nt + i),
            _z_window(bt * tc, B_WIDTH, Z_BG, lambda b, i: b * nt + i),
            pl.BlockSpec((bt, SUBLANES, 3 * B_WIDTH), lambda b, i: (b, 0, 0)),
            pl.BlockSpec(state, lambda b, i: (b, 0, 0, 0)),
            pl.BlockSpec((CONV_W, 3 * B_WIDTH), lambda b, i: (0, 0)),
            pl.BlockSpec((2, LANES), lambda b, i: (0, 0)),
            pl.BlockSpec((1, B_HEAD_DIM), lambda b, i: (0, 0)),
        ],
        out_specs=[
            pl.BlockSpec((bt * tc, B_WIDTH), lambda b, i: (b * nt + i, 0)),
            pl.BlockSpec(state, lambda b, i: (b, 0, 0, 0)),
        ],
        out_shape=[
            jax.ShapeDtypeStruct((nb * t, B_WIDTH), BF16),
            jax.ShapeDtypeStruct((nb, B_HEADS, B_HEAD_DIM, B_HEAD_DIM), F32),
        ],
        scratch_shapes=[
            pltpu.VMEM((bt, SUBLANES + tc, 3 * B_WIDTH), F32),
            pltpu.VMEM((3, units, B_HEADS * c, B_HEAD_DIM), F32),
            pltpu.VMEM((units, B_HEADS, B_HEAD_DIM + c, B_HEAD_DIM), BF16),
            pltpu.VMEM((units, B_HEADS, B_HEAD_DIM + c, B_HEAD_DIM), F32),
            pltpu.VMEM(state, F32),
        ],
        compiler_params=pltpu.CompilerParams(dimension_semantics=("arbitrary", "arbitrary"),
                                             vmem_limit_bytes=VMEM_LIMIT),
        name="gated_delta",
    )(z, z, z, hist, s0, convw, gate, onorm)


def _pool_kernel(h_ref, oa_ref, ob_ref, wa_ref, wb_ref, hist_ref, g_ref, win_ref, wgrp_ref, scale_ref, wout_ref,
                 y_ref, tail_ref, ext_ref, *, bt, tt, pos0):
    i = pl.program_id(1)
    pad = POOL_HIST + 1
    rows = bt * tt

    @pl.when(i == 0)
    def _():
        pos = pos0 - pad + lax.broadcasted_iota(jnp.int32, (1, pad, 1), 1)
        ext_ref[:, 0:pad, :] = jnp.where(pos >= 0, hist_ref[...], 0.0)

    @pl.when(i > 0)
    def _():
        ext_ref[:, 0:pad, :] = ext_ref[:, tt:tt + pad, :]

    x = h_ref[...] + _dot(oa_ref[...], wa_ref[...]) + _dot(ob_ref[...], wb_ref[...])
    xn = x * lax.rsqrt(jnp.mean(x * x, axis=-1, keepdims=True) + EPS) * g_ref[...]
    z = _dot(xn.astype(BF16), win_ref[...])
    u = z[:, :C_WIDTH]
    gate = z[:, C_WIDTH:]
    ext_ref[:, pad:pad + tt, :] = u.reshape(bt, tt, C_WIDTH)
    tail_ref[...] = ext_ref[:, tt:tt + pad, :]

    tpos = pos0 + i * tt + (lax.broadcasted_iota(jnp.int32, (rows, 1), 0) & (tt - 1))
    mixed = []
    for gi, w in enumerate(POOL_SIZES):
        cols = slice(gi * C_GROUP, (gi + 1) * C_GROUP)
        s = ext_ref[:, :, cols].reshape(bt * (pad + tt), C_GROUP)
        sh = 1
        while sh < w:
            s = s + pltpu.roll(s, sh, 0)
            sh *= 2
        s = s.reshape(bt, pad + tt, C_GROUP)[:, pad:, :].reshape(rows, C_GROUP)
        cnt = jnp.minimum(tpos + 1, w).astype(F32)
        pooled = s / cnt - u[:, cols]
        m = _dot(pooled.astype(BF16), wgrp_ref[gi]) * scale_ref[:, cols]
        mixed.append((m * _silu(gate[:, cols])).astype(BF16))
    y_ref[...] = x + _dot(jnp.concatenate(mixed, axis=1), wout_ref[...])


def _pool_layer(h, oa, ob, wa, wb, hist, g, win, wgrp, scale, wout, *, nb, bt, t, tt, pos0):
    nt = t // tt
    pad = POOL_HIST + 1
    rows = bt * tt
    kern = functools.partial(_pool_kernel, bt=bt, tt=tt, pos0=pos0)
    tile = lambda b, i: (b * nt + i, 0)
    const2 = lambda b, i: (0, 0)
    once = pl.Buffered(1)
    return pl.pallas_call(
        kern,
        grid=(nb // bt, nt),
        in_specs=[
            pl.BlockSpec((rows, D_MODEL), tile),
            pl.BlockSpec((rows, A_WIDTH), tile),
            pl.BlockSpec((rows, B_WIDTH), tile),
            pl.BlockSpec((A_WIDTH, D_MODEL), const2, pipeline_mode=once),
            pl.BlockSpec((B_WIDTH, D_MODEL), const2, pipeline_mode=once),
            pl.BlockSpec((bt, pad, C_WIDTH), lambda b, i: (b, 0, 0)),
            pl.BlockSpec((1, D_MODEL), const2),
            pl.BlockSpec((D_MODEL, 2 * C_WIDTH), const2, pipeline_mode=once),
            pl.BlockSpec((len(POOL_SIZES), C_GROUP, C_GROUP), lambda b, i: (0, 0, 0), pipeline_mode=once),
            pl.BlockSpec((1, C_WIDTH), const2),
            pl.BlockSpec((C_WIDTH, D_MODEL), const2, pipeline_mode=once),
        ],
        out_specs=[
            pl.BlockSpec((rows, D_MODEL), tile),
            pl.BlockSpec((bt, pad, C_WIDTH), lambda b, i: (b, 0, 0)),
        ],
        out_shape=[
            jax.ShapeDtypeStruct((nb * t, D_MODEL), F32),
            jax.ShapeDtypeStruct((nb, pad, C_WIDTH), F32),
        ],
        scratch_shapes=[pltpu.VMEM((bt, pad + tt, C_WIDTH), F32)],
        compiler_params=pltpu.CompilerParams(dimension_semantics=("arbitrary", "arbitrary"),
                                             vmem_limit_bytes=VMEM_LIMIT),
        name="out_proj_pool_layer",
    )(h, oa, ob, wa, wb, hist, g, win, wgrp, scale, wout)


def _ab_weights(norm_g, w_in, q_norm, k_norm, sinks, conv_w, a_log, dt_bias, o_norm, w_out):
    w = jnp.swapaxes(w_in, 0, 1).astype(BF16)
    lane_pad = (GATE_LANE, LANES - GATE_LANE - B_HEADS)
    gate = jnp.stack([jnp.pad(a_log.astype(F32), lane_pad), jnp.pad(dt_bias.astype(F32), lane_pad)])
    return dict(
        norm_g=norm_g.reshape(1, D_MODEL), w=w, kn=jnp.tile(k_norm, A_KV_HEADS).reshape(1, A_KV_WIDTH),
        qn=jnp.tile(q_norm, A_KV_HEADS).reshape(1, LANES), sinks=sinks.astype(F32), conv_w=conv_w, gate=gate,
        onorm=o_norm.reshape(1, B_HEAD_DIM), wa=w_out[:A_WIDTH].astype(BF16), wb=w_out[A_WIDTH:].astype(BF16))


def _group_tiles(t):
    if t >= CHUNK:
        return dict(proj_rows=1024, seqs_attn=1, seqs_delta=1, delta_tokens=1024, chunk=CHUNK, seqs_pool=1,
                    pool_tokens=1024)
    return dict(proj_rows=256, seqs_attn=32, seqs_delta=16, delta_tokens=t, chunk=t, seqs_pool=256 // t, pool_tokens=t)


def _ab_mixers(h, wts, cache_k, cache_v, s0, conv_hist, *, nb, t):
    tl = _group_tiles(t)
    z = _in_proj(h, wts["norm_g"], wts["w"], wts["kn"], wts["qn"], min(nb * t, tl["proj_rows"]))
    bt = tl["seqs_attn"]
    if cache_k is None:
        kv_specs = [_z_window(bt * t, A_KV_WIDTH, Z_AK, lambda b: b), _z_window(bt * t, A_KV_WIDTH, Z_AV, lambda b: b)]
        o_a = _attention(z, z, z, kv_specs, wts["sinks"], nb=nb, bt=bt, t=t, cq=CHUNK, lk=WINDOW + CHUNK, off=0)
    else:
        k_new = z[:, Z_AK:Z_AK + A_KV_WIDTH].reshape(nb, t, A_KV_WIDTH)
        v_new = z[:, Z_AV:Z_AV + A_KV_WIDTH].reshape(nb, t, A_KV_WIDTH)
        kbuf = jnp.concatenate([cache_k.reshape(nb, WINDOW, A_KV_WIDTH), k_new], axis=1)
        vbuf = jnp.concatenate([cache_v.reshape(nb, WINDOW, A_KV_WIDTH), v_new], axis=1)
        o_a = _attention_step(z, kbuf.reshape(nb * (WINDOW + t), A_KV_WIDTH), vbuf.reshape(nb * (WINDOW + t), A_KV_WIDTH),
                              wts["sinks"], nb=nb, bt=bt, t=t)
    hist = jnp.pad(conv_hist.astype(F32), ((0, 0), (SUBLANES - (CONV_W - 1), 0), (0, 0)))
    o_b, s_new = _delta(z, hist, s0.astype(F32), wts["conv_w"], wts["gate"], wts["onorm"], nb=nb,
                        bt=tl["seqs_delta"], t=t, tc=tl["delta_tokens"], c=tl["chunk"])
    return o_a, o_b, z, s_new


def _last_rows(z, col, width, nb, t, rows):
    return z.reshape(nb, t, Z_WIDTH)[:, t - rows:, col:col + width]


def _cache_rows(z, col, nb, t, rows):
    return _last_rows(z, col, A_KV_WIDTH, nb, t, rows).reshape(nb, rows, A_KV_HEADS, A_HEAD_DIM)


def kernel(x_prompt, x_sample, cache_a_k, cache_a_v, state_b_s, state_b_conv, state_c_pool,
           norm_ab, w_in_ab, q_norm_a, k_norm_a, sinks_a, conv_b, a_log_b, dt_bias_b, o_norm_b, w_out_ab,
           norm_c, w_in_c, w_grp_c, scale_c, w_out_c):
    bp, tp, _ = x_prompt.shape
    bs, ts, _ = x_sample.shape
    hp = x_prompt.reshape(bp * tp, D_MODEL)
    hs = x_sample.reshape(bs * ts, D_MODEL)

    wts = _ab_weights(norm_ab[0], w_in_ab[0], q_norm_a[0], k_norm_a[0], sinks_a[0], conv_b[0], a_log_b[0],
                      dt_bias_b[0], o_norm_b[0], w_out_ab[0])
    s0 = jnp.zeros((bp, B_HEADS, B_HEAD_DIM, B_HEAD_DIM), F32)
    c0 = jnp.zeros((bp, CONV_W - 1, 3 * B_WIDTH), F32)
    oap, obp, zp, sp = _ab_mixers(hp, wts, None, None, s0, c0, nb=bp, t=tp)
    oas, obs, zs, ss = _ab_mixers(hs, wts, cache_a_k[0], cache_a_v[0], state_b_s[0], state_b_conv[0], nb=bs, t=ts)
    p_a_k = _cache_rows(zp, Z_AK, bp, tp, WINDOW)[None]
    p_a_v = _cache_rows(zp, Z_AV, bp, tp, WINDOW)[None]
    s_a_k = _cache_rows(zs, Z_AK, bs, ts, ts)[None]
    s_a_v = _cache_rows(zs, Z_AV, bs, ts, ts)[None]
    p_b_conv = _last_rows(zp, Z_BQKV, 3 * B_WIDTH, bp, tp, CONV_W - 1)[None]
    s_b_conv = _last_rows(zs, Z_BQKV, 3 * B_WIDTH, bs, ts, CONV_W - 1)[None]

    g_c = norm_c[0].reshape(1, D_MODEL)
    win = w_in_c[0].astype(BF16)
    wgrp = w_grp_c[0].astype(BF16)
    scale = scale_c[0].reshape(1, C_WIDTH)
    wout = w_out_c[0].astype(BF16)
    h0 = jnp.zeros((bp, POOL_HIST + 1, C_WIDTH), F32)
    hs0 = jnp.pad(state_c_pool[0].astype(F32), ((0, 0), (1, 0), (0, 0)))
    tlp, tls = _group_tiles(tp), _group_tiles(ts)
    yp, tail_p = _pool_layer(hp, oap, obp, wts["wa"], wts["wb"], h0, g_c, win, wgrp, scale, wout, nb=bp,
                             bt=tlp["seqs_pool"], t=tp, tt=tlp["pool_tokens"], pos0=0)
    ys, tail_s = _pool_layer(hs, oas, obs, wts["wa"], wts["wb"], hs0, g_c, win, wgrp, scale, wout, nb=bs,
                             bt=tls["seqs_pool"], t=ts, tt=tls["pool_tokens"], pos0=PAST_LEN)

    return (yp.reshape(bp, tp, D_MODEL), ys.reshape(bs, ts, D_MODEL),
            p_a_k, p_a_v, sp[None], p_b_conv, tail_p[:, 1:][None],
            s_a_k, s_a_v, ss[None], s_b_conv, tail_s[:, 1:][None])
```

```python
import functools

import jax
import jax.numpy as jnp
from jax import lax
from jax.experimental import pallas as pl
from jax.experimental.pallas import tpu as pltpu

F32 = jnp.float32
BF16 = jnp.bfloat16
HIGHEST = lax.Precision.HIGHEST

D_MODEL = 1024
CHUNK = 64
PAST_LEN = 2048
EPS = 1e-6
NEG_INF = -1e30
LOG2E = 1.4426950408889634

A_HEADS = 8
A_KV_HEADS = 2
A_HEAD_DIM = 64
A_WIDTH = A_HEADS * A_HEAD_DIM
A_KV_WIDTH = A_KV_HEADS * A_HEAD_DIM
A_REP = A_HEADS // A_KV_HEADS
WINDOW = 128

B_HEADS = 4
B_HEAD_DIM = 128
B_WIDTH = B_HEADS * B_HEAD_DIM
CONV_W = 4

POOL_SIZES = (2, 4, 8, 16)
C_WIDTH = D_MODEL
C_GROUP = C_WIDTH // len(POOL_SIZES)
POOL_HIST = max(POOL_SIZES) - 1

LANES = 128
SUBLANES = 8

Z_AQ = 0
Z_AK = Z_AQ + A_WIDTH
Z_AV = Z_AK + A_KV_WIDTH
Z_AG = Z_AV + A_KV_WIDTH
Z_BQKV = Z_AG + A_WIDTH
Z_BG = Z_BQKV + 3 * B_WIDTH
Z_BA = Z_BG + B_WIDTH
Z_WIDTH = Z_BA + LANES
GATE_LANE = B_HEADS


def _z_window(rows, width, col, row_block):
    return pl.BlockSpec((pl.Element(rows), pl.Element(width)), lambda *idx: (row_block(*idx) * rows, col))

VMEM_LIMIT = 48 * 1024 * 1024


def _sigmoid(x):
    return 1.0 / (1.0 + jnp.exp(-x))


def _silu(x):
    return x * _sigmoid(x)


def _softplus(x):
    return jnp.maximum(x, 0.0) + jnp.log(1.0 + jnp.exp(-jnp.abs(x)))


def _dot(a, b, precision=None):
    return jnp.dot(a, b, preferred_element_type=F32, precision=precision)


def _dot_nt(a, b, precision=None):
    return lax.dot_general(a, b, (((1,), (1,)), ((), ())), preferred_element_type=F32, precision=precision)


def _dot_tn(a, b, precision=None):
    return lax.dot_general(a, b, (((0,), (0,)), ((), ())), preferred_element_type=F32, precision=precision)


def _log2(n):
    assert n & (n - 1) == 0
    return n.bit_length() - 1


def _in_proj_kernel(x_ref, g_ref, w_ref, kn_ref, qn_ref, z_ref):
    x = x_ref[...]
    xn = x * lax.rsqrt(jnp.mean(x * x, axis=-1, keepdims=True) + EPS) * g_ref[...]
    n_in = w_ref.shape[0]
    z_ref[:, Z_BA:Z_WIDTH] = jnp.zeros((x.shape[0], Z_WIDTH - Z_BA), F32)
    z_ref[:, 0:n_in] = _dot_nt(xn.astype(BF16), w_ref[...])
    first = lax.broadcasted_iota(jnp.int32, (x.shape[0], LANES), 1) < A_HEAD_DIM
    q_gain = qn_ref[...] * (A_HEAD_DIM ** -0.5 * LOG2E)
    for col, gain in [(Z_AQ + r * LANES, q_gain) for r in range(A_WIDTH // LANES)] + [(Z_AK, kn_ref[...])]:
        blk = z_ref[:, col:col + LANES]
        sq = blk * blk
        s0 = jnp.sum(jnp.where(first, sq, 0.0), axis=-1, keepdims=True)
        s1 = jnp.sum(jnp.where(first, 0.0, sq), axis=-1, keepdims=True)
        ms = jnp.where(first, s0, s1) * (1.0 / A_HEAD_DIM)
        z_ref[:, col:col + LANES] = blk * lax.rsqrt(ms + EPS) * gain


def _in_proj(x, g, w, kn, qn, tm):
    n = x.shape[0]
    return pl.pallas_call(
        _in_proj_kernel,
        grid=(n // tm,),
        in_specs=[
            pl.BlockSpec((tm, D_MODEL), lambda i: (i, 0)),
            pl.BlockSpec((1, D_MODEL), lambda i: (0, 0)),
            pl.BlockSpec(w.shape, lambda i: (0, 0), pipeline_mode=pl.Buffered(1)),
            pl.BlockSpec((1, A_KV_WIDTH), lambda i: (0, 0)),
            pl.BlockSpec((1, LANES), lambda i: (0, 0)),
        ],
        out_specs=pl.BlockSpec((tm, Z_WIDTH), lambda i: (i, 0)),
        out_shape=jax.ShapeDtypeStruct((n, Z_WIDTH), F32),
        compiler_params=pltpu.CompilerParams(dimension_semantics=("arbitrary",), vmem_limit_bytes=VMEM_LIMIT),
        name="in_proj",
    )(x, g, w, kn, qn)


A_HALVES = LANES // A_HEAD_DIM
A_QBLOCKS = A_WIDTH // LANES
A_COPIES = A_KV_HEADS


def _attn_half(r, c):
    g = r * A_HALVES // A_REP
    return (g + c) % A_HALVES


def _attn_tables(sinks_ref, bias_ref, *, cq, lk, front, slack, fill):
    rows = A_QBLOCKS * cq
    nvar = front // cq + 1
    row = lax.broadcasted_iota(jnp.int32, (rows, 1), 0)
    blk = row >> _log2(cq)

    def per_row(value):
        out = []
        for c in range(A_COPIES):
            col = jnp.zeros((rows, 1), F32)
            for r in range(A_QBLOCKS):
                col = jnp.where(blk == r, value(r * A_HALVES + _attn_half(r, c)), col)
            out.append(col)
        return out

    @pl.when(fill)
    def _():
        col = lax.broadcasted_iota(jnp.int32, (rows, lk + slack), 1)
        dist = jnp.abs((row & (cq - 1)) + WINDOW - col).astype(F32)
        slopes = per_row(lambda h: 2.0 ** (-8.0 * (h + 1) / A_HEADS))
        sinks = per_row(lambda h: sinks_ref[h])
        for c in range(A_COPIES):
            for var in range(nvar):
                keys = jnp.where((col >= front - var * cq) & (col < lk), -slopes[c] * dist, NEG_INF)
                bias_ref[var, c] = jnp.where(col == lk, sinks[c], keys) * LOG2E

    first = lax.broadcasted_iota(jnp.int32, (cq, LANES), 1) < A_HEAD_DIM
    return first, nvar


def _attn_chunk(qc, kv, biases, first, gate, cq):
    blocks = [qc[:, r * LANES:(r + 1) * LANES] for r in range(A_QBLOCKS)]
    outs = []
    for c in range(A_COPIES):
        lhs = jnp.concatenate(
            [jnp.where(first if _attn_half(r, c) == 0 else jnp.logical_not(first), blocks[r], 0.0)
             for r in range(A_QBLOCKS)], axis=0)
        zero_keys = jnp.zeros((biases[c].shape[1] - kv[c][0].shape[0], A_KV_WIDTH), BF16)
        kk, vv = (jnp.concatenate([x, zero_keys], axis=0) for x in kv[c])
        sc = _dot_nt(lhs.astype(BF16), kk) + biases[c]
        p = jnp.exp2(sc - jnp.max(sc, axis=-1, keepdims=True))
        outs.append(_dot(p.astype(BF16), vv) * (1.0 / jnp.sum(p, axis=-1, keepdims=True)))
    tile = []
    for r in range(A_QBLOCKS):
        rows = slice(r * cq, (r + 1) * cq)
        low = 0 if _attn_half(r, 0) == 0 else 1
        tile.append(jnp.where(first, outs[low][rows], outs[1 - low][rows]))
    return (jnp.concatenate(tile, axis=1) * _silu(gate)).astype(BF16)


def _attn_kernel(sinks_ref, q_ref, k_ref, v_ref, ag_ref, o_ref, kp_ref, vp_ref, bias_ref, *, bt, t):
    cq, lk = CHUNK, WINDOW + CHUNK

    for b in range(bt):
        for src, dst in ((k_ref, kp_ref), (v_ref, vp_ref)):
            x = src[b * t:(b + 1) * t, :]
            dst[:, b, 0:WINDOW, :] = jnp.zeros((A_COPIES, WINDOW, A_KV_WIDTH), BF16)
            dst[0, b, WINDOW:WINDOW + t, :] = x.astype(BF16)
            dst[1, b, WINDOW:WINDOW + t, :] = pltpu.roll(x, A_HEAD_DIM, 1).astype(BF16)

    first, nvar = _attn_tables(sinks_ref, bias_ref, cq=cq, lk=lk, front=WINDOW, slack=bias_ref.shape[-1] - lk,
                               fill=pl.program_id(0) == 0)

    def chunk(b, cg):
        k0 = pl.multiple_of(cg * cq, cq)
        r0 = b * t + k0
        var = jnp.minimum(cg, nvar - 1)
        kv = [(kp_ref[c, b, pl.ds(k0, lk), :], vp_ref[c, b, pl.ds(k0, lk), :]) for c in range(A_COPIES)]
        o_ref[pl.ds(r0, cq), :] = _attn_chunk(
            q_ref[pl.ds(r0, cq), :], kv, [bias_ref[var, c] for c in range(A_COPIES)], first,
            ag_ref[pl.ds(r0, cq), :], cq)

    for b in range(bt):
        lax.fori_loop(0, t // cq, lambda cg, carry, b=b: (chunk(b, cg), carry)[1], 0, unroll=min(t // cq, 8))


def _attention(z, sinks, *, nb, bt, t):
    cq, lk = CHUNK, WINDOW + CHUNK
    kern = functools.partial(_attn_kernel, bt=bt, t=t)
    return pl.pallas_call(
        kern,
        grid=(nb // bt,),
        in_specs=[
            pl.BlockSpec(memory_space=pltpu.SMEM),
            _z_window(bt * t, A_WIDTH, Z_AQ, lambda b: b),
            _z_window(bt * t, A_KV_WIDTH, Z_AK, lambda b: b),
            _z_window(bt * t, A_KV_WIDTH, Z_AV, lambda b: b),
            _z_window(bt * t, A_WIDTH, Z_AG, lambda b: b),
        ],
        out_specs=pl.BlockSpec((bt * t, A_WIDTH), lambda b: (b, 0)),
        out_shape=jax.ShapeDtypeStruct((nb * t, A_WIDTH), BF16),
        scratch_shapes=[
            pltpu.VMEM((A_COPIES, bt, WINDOW + t, A_KV_WIDTH), BF16),
            pltpu.VMEM((A_COPIES, bt, WINDOW + t, A_KV_WIDTH), BF16),
            pltpu.VMEM((WINDOW // cq + 1, A_COPIES, A_QBLOCKS * cq, lk + LANES - lk % LANES), F32),
        ],
        compiler_params=pltpu.CompilerParams(dimension_semantics=("arbitrary",), vmem_limit_bytes=VMEM_LIMIT),
        name="swa_attention",
    )(sinks, z, z, z, z)


STEP_GROUP = 4


def _attn_step_kernel(sinks_ref, q_ref, k_ref, v_ref, ag_ref, o_ref, bias_ref, *, bt, t):
    lk = WINDOW + t
    cols = A_HEADS * t
    key = lax.broadcasted_iota(jnp.int32, (lk, cols), 0)
    col = lax.broadcasted_iota(jnp.int32, (1, cols), 1)
    head = col >> _log2(t)
    slope = jnp.zeros((1, cols), F32)
    sink = jnp.zeros((1, cols), F32)
    for h in range(A_HEADS):
        slope = jnp.where(head == h, 2.0 ** (-8.0 * (h + 1) / A_HEADS), slope)
        sink = jnp.where(head == h, sinks_ref[h], sink)
    sink = sink * LOG2E
    bias_ref[...] = jnp.abs((col & (t - 1)) + WINDOW - key).astype(F32) * (-LOG2E * slope)
    first = lax.broadcasted_iota(jnp.int32, (t, LANES), 1) < A_HEAD_DIM

    def to_half(x, have, want):
        return x if have == want else pltpu.roll(x, A_HEAD_DIM, 1)

    def scores(b):
        q = q_ref[b * t:(b + 1) * t, :]
        qs = []
        for h in range(A_HEADS):
            r, j, g = h // A_HALVES, h % A_HALVES, h // A_REP
            blk = to_half(q[:, r * LANES:(r + 1) * LANES], j, g)
            qs.append(jnp.where(first if g == 0 else jnp.logical_not(first), blk, 0.0))
        qs = jnp.concatenate(qs, axis=0).astype(BF16)
        return _dot_nt(k_ref[b * lk:(b + 1) * lk, :].astype(BF16), qs) + bias_ref[...]

    def softmax(st):
        m = jnp.maximum(jnp.max(st, axis=0, keepdims=True), sink)
        p = jnp.exp2(st - m)
        return (p * (1.0 / (jnp.sum(p, axis=0, keepdims=True) + jnp.exp2(sink - m)))).astype(BF16)

    def write(b, o):
        tile = []
        for r in range(A_QBLOCKS):
            halves = [to_half(o[h * t:(h + 1) * t, :], h // A_REP, h % A_HALVES)
                      for h in range(r * A_HALVES, (r + 1) * A_HALVES)]
            tile.append(jnp.where(first, halves[0], halves[1]))
        rows = slice(b * t, (b + 1) * t)
        o_ref[rows, :] = (jnp.concatenate(tile, axis=1) * _silu(ag_ref[rows, :])).astype(BF16)

    for b0 in range(0, bt, STEP_GROUP):
        group = range(b0, min(b0 + STEP_GROUP, bt))
        p = [softmax(st) for st in [scores(b) for b in group]]
        o = [_dot_tn(pb, v_ref[b * lk:(b + 1) * lk, :].astype(BF16)) for b, pb in zip(group, p)]
        for b, ob in zip(group, o):
            write(b, ob)


def _attention_step(z, kbuf, vbuf, sinks, *, nb, bt, t):
    kern = functools.partial(_attn_step_kernel, bt=bt, t=t)
    keys = pl.BlockSpec((bt * (WINDOW + t), A_KV_WIDTH), lambda b: (b, 0))
    return pl.pallas_call(
        kern,
        grid=(nb // bt,),
        in_specs=[
            pl.BlockSpec(memory_space=pltpu.SMEM),
            _z_window(bt * t, A_WIDTH, Z_AQ, lambda b: b),
            keys,
            keys,
            _z_window(bt * t, A_WIDTH, Z_AG, lambda b: b),
        ],
        out_specs=pl.BlockSpec((bt * t, A_WIDTH), lambda b: (b, 0)),
        out_shape=jax.ShapeDtypeStruct((nb * t, A_WIDTH), BF16),
        scratch_shapes=[pltpu.VMEM((WINDOW + t, A_HEADS * t), F32)],
        compiler_params=pltpu.CompilerParams(dimension_semantics=("arbitrary",), vmem_limit_bytes=VMEM_LIMIT),
        name="swa_attention_step",
    )(sinks, z, kbuf, vbuf, z)


def _bf(x):
    return x.astype(BF16)


def _unit_lower_inverse(a_list, c, ri, lj, expand):
    base = SUBLANES
    same = (ri >> _log2(base)) == (lj >> _log2(base))
    eye = jnp.where(ri == lj, 1.0, 0.0)
    n1 = [jnp.where(same, -a, 0.0) for a in a_list]
    x = [eye + n for n in n1]
    n1b = [_bf(n) for n in n1]
    n2b = [_bf(_dot(nb, expand(nb))) for nb in n1b]
    n2e = [expand(nb) for nb in n2b]
    x = [xi + _dot(_bf(xi), ne) for xi, ne in zip(x, n2e)]
    n4e = [expand(_bf(_dot(nb, ne))) for nb, ne in zip(n2b, n2e)]
    x = [xi + _dot(_bf(xi), ne) for xi, ne in zip(x, n4e)]
    s = base
    while s < c:
        sel = ((ri >> _log2(2 * s)) == (lj >> _log2(2 * s))) & ((ri >> _log2(s)) != (lj >> _log2(s)))
        xb = [_bf(xi) for xi in x]
        xo = [_dot(b, expand(_bf(jnp.where(sel, a, 0.0)))) for b, a in zip(xb, a_list)]
        x = [xi - _dot(_bf(o), expand(b)) for xi, o, b in zip(x, xo, xb)]
        s *= 2
    return x


def _delta_kernel(qkv_ref, ba_ref, bg_ref, hist_ref, s0_ref, convw_ref, gate_ref, onorm_ref,
                  o_ref, sout_ref, ext_ref, st_ref, lhs_ref, add_ref, s_ref, *, bt, c, tc):
    i = pl.program_id(1)
    hd = B_HEAD_DIM
    nh = B_HEADS
    r = nh * c
    nch = tc // c
    pad = SUBLANES
    units = [(b, cc) for b in range(bt) for cc in range(nch)]

    @pl.when(i == 0)
    def _():
        ext_ref[:, 0:pad, :] = hist_ref[...]
        s_ref[...] = s0_ref[...]

    @pl.when(i > 0)
    def _():
        ext_ref[:, 0:pad, :] = ext_ref[:, tc:tc + pad, :]

    for b in range(bt):
        ext_ref[b, pad:pad + tc, :] = qkv_ref[b * tc:(b + 1) * tc, :]

    for u, (b, cc) in enumerate(units):
        e = ext_ref[b, cc * c:cc * c + pad + c, :]
        acc = e * convw_ref[0:1, :]
        for j in range(1, CONV_W):
            acc = pltpu.roll(acc, 1, 0) + e * convw_ref[j:j + 1, :]
        y = _silu(acc[pad:, :])
        for part in range(3):
            for h in range(nh):
                blk = y[:, (part * nh + h) * hd:(part * nh + h + 1) * hd]
                if part < 2:
                    blk = blk * lax.rsqrt(jnp.sum(blk * blk, axis=-1, keepdims=True) + EPS)
                st_ref[part, u, h * c:(h + 1) * c, :] = blk

    ri = lax.broadcasted_iota(jnp.int32, (c, r), 0)
    li = lax.broadcasted_iota(jnp.int32, (c, r), 1)
    lj = li & (c - 1)
    lh = li >> _log2(c)
    lower = ri >= lj
    strict = ri > lj
    head_sel = [jnp.where(lh == h, 1.0, 0.0).astype(BF16) for h in range(nh)]

    def expand(xb):
        return jnp.concatenate([xb * m for m in head_sel], axis=0)

    def lanes(x_st):
        return jnp.concatenate([x_st[h * c:(h + 1) * c] for h in range(nh)], axis=1)

    def head_blocks(x_st):
        zero = jnp.zeros((c, hd), x_st.dtype)
        return jnp.concatenate(
            [jnp.concatenate([x_st[h * c:(h + 1) * c] if h2 == h else zero for h2 in range(nh)], axis=1)
             for h in range(nh)], axis=0)

    ci = lax.broadcasted_iota(jnp.int32, (c, c), 0)
    cj = lax.broadcasted_iota(jnp.int32, (c, c), 1)
    tril = jnp.where(ci >= cj, 1.0, 0.0)
    di = lax.broadcasted_iota(jnp.int32, (hd, hd), 0)
    dj = lax.broadcasted_iota(jnp.int32, (hd, hd), 1)
    eye_hd = di == dj
    neg_rate = -jnp.exp(gate_ref[0:1, :])
    dt_bias = gate_ref[1:2, :]
    onorm = onorm_ref[...]

    def stack_cols(x, lane0):
        return jnp.concatenate([x[:, lane0 + h:lane0 + h + 1] for h in range(nh)], axis=0)

    beta, gc, gcc, gr, glast = [], [], [], [], []
    for b, cc in units:
        ba = ba_ref[b * tc + cc * c:b * tc + (cc + 1) * c, :]
        g_all = neg_rate * _softplus(ba + dt_bias)
        gcum = _dot(tril, g_all, HIGHEST)
        gcum_t = gcum.T
        beta.append(stack_cols(_sigmoid(ba), 0))
        gc.append(stack_cols(gcum, GATE_LANE))
        gcc_h = jnp.zeros((c, r), F32)
        for h in range(nh):
            gcc_h = jnp.where(lh == h, gcum[:, GATE_LANE + h:GATE_LANE + h + 1], gcc_h)
        gcc.append(gcc_h)
        gr.append(jnp.concatenate([gcum_t[GATE_LANE + h:GATE_LANE + h + 1, :] for h in range(nh)], axis=1))
        glast.append(jnp.concatenate(
            [jnp.broadcast_to(gcum[c - 1:c, GATE_LANE + h:GATE_LANE + h + 1], (c, 1)) for h in range(nh)], axis=0))

    def fold_units(us):
        n = range(len(us))
        q = [st_ref[0, u] * (hd ** -0.5) for u in us]
        k = [st_ref[1, u] for u in us]
        v = [st_ref[2, u] for u in us]
        kb = [k[j] * beta[u] for j, u in enumerate(us)]
        big = [_dot_nt(_bf(jnp.concatenate([lanes(kb[j]), lanes(q[j])], axis=0)), head_blocks(_bf(k[j]))) for j in n]
        decay = [jnp.where(lower, jnp.exp(jnp.where(lower, gcc[u] - gr[u], 0.0)), 0.0) for u in us]
        a = [jnp.where(strict, big[j][:c] * decay[j], 0.0) for j in n]
        qkd = [expand(_bf(big[j][c:] * decay[j])) for j in n]
        tinv = _unit_lower_inverse(a, c, ri, lj, expand)
        eg = [jnp.exp(gc[u]) for u in us]
        rhs = [_bf(jnp.concatenate([v[j] * beta[u], kb[j] * eg[j]], axis=1)) for j, u in enumerate(us)]
        sol = [_dot(expand(_bf(tinv[j])), rhs[j]) for j in n]
        solb = [_bf(s) for s in sol]
        fold = [_dot(qkd[j], solb[j]) for j in n]
        k_dec = [_bf(k[j] * jnp.exp(glast[u] - gc[u])) for j, u in enumerate(us)]
        for j, u in enumerate(us):
            qp = q[j] * eg[j] - fold[j][:, hd:]
            for h in range(nh):
                rows = slice(h * c, (h + 1) * c)
                kt = _dot_tn(k_dec[j][rows], solb[j][rows])
                g_tot = jnp.exp(glast[u][h * c:h * c + 1, :])
                lhs_ref[u, h, 0:hd, :] = _bf(jnp.where(eye_hd, g_tot, 0.0) - kt[:, hd:])
                lhs_ref[u, h, hd:hd + c, :] = _bf(qp[rows])
                add_ref[u, h, 0:hd, :] = kt[:, :hd]
                add_ref[u, h, hd:hd + c, :] = fold[j][rows, :hd]

    fold_units(list(range(len(units))))

    for u, (b, cc) in enumerate(units):
        rows = slice(b * tc + cc * c, b * tc + (cc + 1) * c)
        for h in range(nh):
            res = _dot(lhs_ref[u, h], _bf(s_ref[b, h])) + add_ref[u, h]
            s_ref[b, h] = res[:hd]
            o = res[hd:]
            on = o * lax.rsqrt(jnp.mean(o * o, axis=-1, keepdims=True) + EPS) * onorm
            bg = bg_ref[rows, h * hd:(h + 1) * hd]
            o_ref[rows, h * hd:(h + 1) * hd] = (on * _silu(bg)).astype(BF16)
    sout_ref[...] = s_ref[...]


def _delta(z, hist, s0, convw, gate, onorm, *, nb, bt, t, tc, c):
    nt = t // tc
    units = bt * (tc // c)
    state = (bt, B_HEADS, B_HEAD_DIM, B_HEAD_DIM)
    kern = functools.partial(_delta_kernel, bt=bt, c=c, tc=tc)
    return pl.pallas_call(
        kern,
        grid=(nb // bt, nt),
        in_specs=[
            _z_window(bt * tc, 3 * B_WIDTH, Z_BQKV, lambda b, i: b * nt + i),
            _z_window(bt * tc, LANES, Z_BA, lambda b, i: b * nt + i),
            _z_window(bt * tc, B_WIDTH, Z_BG, lambda b, i: b * nt + i),
            pl.BlockSpec((bt, SUBLANES, 3 * B_WIDTH), lambda b, i: (b, 0, 0)),
            pl.BlockSpec(state, lambda b, i: (b, 0, 0, 0)),
            pl.BlockSpec((CONV_W, 3 * B_WIDTH), lambda b, i: (0, 0)),
            pl.BlockSpec((2, LANES), lambda b, i: (0, 0)),
            pl.BlockSpec((1, B_HEAD_DIM), lambda b, i: (0, 0)),
        ],
        out_specs=[
            pl.BlockSpec((bt * tc, B_WIDTH), lambda b, i: (b * nt + i, 0)),
            pl.BlockSpec(state, lambda b, i: (b, 0, 0, 0)),
        ],
        out_shape=[
            jax.ShapeDtypeStruct((nb * t, B_WIDTH), BF16),
            jax.ShapeDtypeStruct((nb, B_HEADS, B_HEAD_DIM, B_HEAD_DIM), F32),
        ],
        scratch_shapes=[
            pltpu.VMEM((bt, SUBLANES + tc, 3 * B_WIDTH), F32),
            pltpu.VMEM((3, units, B_HEADS * c, B_HEAD_DIM), F32),
            pltpu.VMEM((units, B_HEADS, B_HEAD_DIM + c, B_HEAD_DIM), BF16),
            pltpu.VMEM((units, B_HEADS, B_HEAD_DIM + c, B_HEAD_DIM), F32),
            pltpu.VMEM(state, F32),
        ],
        compiler_params=pltpu.CompilerParams(dimension_semantics=("arbitrary", "arbitrary"),
                                             vmem_limit_bytes=VMEM_LIMIT),
        name="gated_delta",
    )(z, z, z, hist, s0, convw, gate, onorm)


def _pool_kernel(h_ref, oa_ref, ob_ref, wa_ref, wb_ref, hist_ref, g_ref, win_ref, wgrp_ref, scale_ref, wout_ref,
                 y_ref, tail_ref, ext_ref, *, bt, tt, pos0):
    i = pl.program_id(1)
    pad = POOL_HIST + 1
    rows = bt * tt

    @pl.when(i == 0)
    def _():
        pos = pos0 - pad + lax.broadcasted_iota(jnp.int32, (1, pad, 1), 1)
        ext_ref[:, 0:pad, :] = jnp.where(pos >= 0, hist_ref[...], 0.0)

    @pl.when(i > 0)
    def _():
        ext_ref[:, 0:pad, :] = ext_ref[:, tt:tt + pad, :]

    x = h_ref[...] + _dot(oa_ref[...], wa_ref[...]) + _dot(ob_ref[...], wb_ref[...])
    xn = x * lax.rsqrt(jnp.mean(x * x, axis=-1, keepdims=True) + EPS) * g_ref[...]
    z = _dot(xn.astype(BF16), win_ref[...])
    u = z[:, :C_WIDTH]
    gate = z[:, C_WIDTH:]
    ext_ref[:, pad:pad + tt, :] = u.reshape(bt, tt, C_WIDTH)
    tail_ref[...] = ext_ref[:, tt:tt + pad, :]

    tpos = pos0 + i * tt + (lax.broadcasted_iota(jnp.int32, (rows, 1), 0) & (tt - 1))
    mixed = []
    for gi, w in enumerate(POOL_SIZES):
        cols = slice(gi * C_GROUP, (gi + 1) * C_GROUP)
        s = ext_ref[:, :, cols].reshape(bt * (pad + tt), C_GROUP)
        sh = 1
        while sh < w:
            s = s + pltpu.roll(s, sh, 0)
            sh *= 2
        s = s.reshape(bt, pad + tt, C_GROUP)[:, pad:, :].reshape(rows, C_GROUP)
        cnt = jnp.minimum(tpos + 1, w).astype(F32)
        pooled = s / cnt - u[:, cols]
        m = _dot(pooled.astype(BF16), wgrp_ref[gi]) * scale_ref[:, cols]
        mixed.append((m * _silu(gate[:, cols])).astype(BF16))
    y_ref[...] = x + _dot(jnp.concatenate(mixed, axis=1), wout_ref[...])


def _pool_layer(h, oa, ob, wa, wb, hist, g, win, wgrp, scale, wout, *, nb, bt, t, tt, pos0):
    nt = t // tt
    pad = POOL_HIST + 1
    rows = bt * tt
    kern = functools.partial(_pool_kernel, bt=bt, tt=tt, pos0=pos0)
    tile = lambda b, i: (b * nt + i, 0)
    const2 = lambda b, i: (0, 0)
    once = pl.Buffered(1)
    return pl.pallas_call(
        kern,
        grid=(nb // bt, nt),
        in_specs=[
            pl.BlockSpec((rows, D_MODEL), tile),
            pl.BlockSpec((rows, A_WIDTH), tile),
            pl.BlockSpec((rows, B_WIDTH), tile),
            pl.BlockSpec((A_WIDTH, D_MODEL), const2, pipeline_mode=once),
            pl.BlockSpec((B_WIDTH, D_MODEL), const2, pipeline_mode=once),
            pl.BlockSpec((bt, pad, C_WIDTH), lambda b, i: (b, 0, 0)),
            pl.BlockSpec((1, D_MODEL), const2),
            pl.BlockSpec((D_MODEL, 2 * C_WIDTH), const2, pipeline_mode=once),
            pl.BlockSpec((len(POOL_SIZES), C_GROUP, C_GROUP), lambda b, i: (0, 0, 0), pipeline_mode=once),
            pl.BlockSpec((1, C_WIDTH), const2),
            pl.BlockSpec((C_WIDTH, D_MODEL), const2, pipeline_mode=once),
        ],
        out_specs=[
            pl.BlockSpec((rows, D_MODEL), tile),
            pl.BlockSpec((bt, pad, C_WIDTH), lambda b, i: (b, 0, 0)),
        ],
        out_shape=[
            jax.ShapeDtypeStruct((nb * t, D_MODEL), F32),
            jax.ShapeDtypeStruct((nb, pad, C_WIDTH), F32),
        ],
        scratch_shapes=[pltpu.VMEM((bt, pad + tt, C_WIDTH), F32)],
        compiler_params=pltpu.CompilerParams(dimension_semantics=("arbitrary", "arbitrary"),
                                             vmem_limit_bytes=VMEM_LIMIT),
        name="out_proj_pool_layer",
    )(h, oa, ob, wa, wb, hist, g, win, wgrp, scale, wout)


def _ab_weights(norm_g, w_in, q_norm, k_norm, sinks, conv_w, a_log, dt_bias, o_norm, w_out):
    w = jnp.swapaxes(w_in, 0, 1).astype(BF16)
    lane_pad = (GATE_LANE, LANES - GATE_LANE - B_HEADS)
    gate = jnp.stack([jnp.pad(a_log.astype(F32), lane_pad), jnp.pad(dt_bias.astype(F32), lane_pad)])
    return dict(
        norm_g=norm_g.reshape(1, D_MODEL), w=w, kn=jnp.tile(k_norm, A_KV_HEADS).reshape(1, A_KV_WIDTH),
        qn=jnp.tile(q_norm, A_KV_HEADS).reshape(1, LANES), sinks=sinks.astype(F32), conv_w=conv_w, gate=gate,
        onorm=o_norm.reshape(1, B_HEAD_DIM), wa=w_out[:A_WIDTH].astype(BF16), wb=w_out[A_WIDTH:].astype(BF16))


def _group_tiles(t):
    if t >= CHUNK:
        return dict(proj_rows=1024, seqs_attn=1, seqs_delta=1, delta_tokens=1024, chunk=CHUNK, seqs_pool=1,
                    pool_tokens=1024)
    return dict(proj_rows=256, seqs_attn=32, seqs_delta=16, delta_tokens=t, chunk=t, seqs_pool=256 // t, pool_tokens=t)


def _ab_mixers(h, wts, cache_k, cache_v, s0, conv_hist, *, nb, t):
    tl = _group_tiles(t)
    z = _in_proj(h, wts["norm_g"], wts["w"], wts["kn"], wts["qn"], min(nb * t, tl["proj_rows"]))
    bt = tl["seqs_attn"]
    if cache_k is None:
        o_a = _attention(z, wts["sinks"], nb=nb, bt=bt, t=t)
    else:
        k_new = z[:, Z_AK:Z_AK + A_KV_WIDTH].reshape(nb, t, A_KV_WIDTH)
        v_new = z[:, Z_AV:Z_AV + A_KV_WIDTH].reshape(nb, t, A_KV_WIDTH)
        kbuf = jnp.concatenate([cache_k.reshape(nb, WINDOW, A_KV_WIDTH), k_new], axis=1)
        vbuf = jnp.concatenate([cache_v.reshape(nb, WINDOW, A_KV_WIDTH), v_new], axis=1)
        o_a = _attention_step(z, kbuf.reshape(nb * (WINDOW + t), A_KV_WIDTH), vbuf.reshape(nb * (WINDOW + t), A_KV_WIDTH),
                              wts["sinks"], nb=nb, bt=bt, t=t)
    hist = jnp.pad(conv_hist.astype(F32), ((0, 0), (SUBLANES - (CONV_W - 1), 0), (0, 0)))
    o_b, s_new = _delta(z, hist, s0.astype(F32), wts["conv_w"], wts["gate"], wts["onorm"], nb=nb,
                        bt=tl["seqs_delta"], t=t, tc=tl["delta_tokens"], c=tl["chunk"])
    return o_a, o_b, z, s_new


def _last_rows(z, col, width, nb, t, rows):
    return z.reshape(nb, t, Z_WIDTH)[:, t - rows:, col:col + width]


def _cache_rows(z, col, nb, t, rows):
    return _last_rows(z, col, A_KV_WIDTH, nb, t, rows).reshape(nb, rows, A_KV_HEADS, A_HEAD_DIM)


def kernel(x_prompt, x_sample, cache_a_k, cache_a_v, state_b_s, state_b_conv, state_c_pool,
           norm_ab, w_in_ab, q_norm_a, k_norm_a, sinks_a, conv_b, a_log_b, dt_bias_b, o_norm_b, w_out_ab,
           norm_c, w_in_c, w_grp_c, scale_c, w_out_c):
    bp, tp, _ = x_prompt.shape
    bs, ts, _ = x_sample.shape
    hp = x_prompt.reshape(bp * tp, D_MODEL)
    hs = x_sample.reshape(bs * ts, D_MODEL)

    wts = _ab_weights(norm_ab[0], w_in_ab[0], q_norm_a[0], k_norm_a[0], sinks_a[0], conv_b[0], a_log_b[0],
                      dt_bias_b[0], o_norm_b[0], w_out_ab[0])
    s0 = jnp.zeros((bp, B_HEADS, B_HEAD_DIM, B_HEAD_DIM), F32)
    c0 = jnp.zeros((bp, CONV_W - 1, 3 * B_WIDTH), F32)
    oap, obp, zp, sp = _ab_mixers(hp, wts, None, None, s0, c0, nb=bp, t=tp)
    oas, obs, zs, ss = _ab_mixers(hs, wts, cache_a_k[0], cache_a_v[0], state_b_s[0], state_b_conv[0], nb=bs, t=ts)
    p_a_k = _cache_rows(zp, Z_AK, bp, tp, WINDOW)[None]
    p_a_v = _cache_rows(zp, Z_AV, bp, tp, WINDOW)[None]
    s_a_k = _cache_rows(zs, Z_AK, bs, ts, ts)[None]
    s_a_v = _cache_rows(zs, Z_AV, bs, ts, ts)[None]
    p_b_conv = _last_rows(zp, Z_BQKV, 3 * B_WIDTH, bp, tp, CONV_W - 1)[None]
    s_b_conv = _last_rows(zs, Z_BQKV, 3 * B_WIDTH, bs, ts, CONV_W - 1)[None]

    g_c = norm_c[0].reshape(1, D_MODEL)
    win = w_in_c[0].astype(BF16)
    wgrp = w_grp_c[0].astype(BF16)
    scale = scale_c[0].reshape(1, C_WIDTH)
    wout = w_out_c[0].astype(BF16)
    h0 = jnp.zeros((bp, POOL_HIST + 1, C_WIDTH), F32)
    hs0 = jnp.pad(state_c_pool[0].astype(F32), ((0, 0), (1, 0), (0, 0)))
    tlp, tls = _group_tiles(tp), _group_tiles(ts)
    yp, tail_p = _pool_layer(hp, oap, obp, wts["wa"], wts["wb"], h0, g_c, win, wgrp, scale, wout, nb=bp,
                             bt=tlp["seqs_pool"], t=tp, tt=tlp["pool_tokens"], pos0=0)
    ys, tail_s = _pool_layer(hs, oas, obs, wts["wa"], wts["wb"], hs0, g_c, win, wgrp, scale, wout, nb=bs,
                             bt=tls["seqs_pool"], t=ts, tt=tls["pool_tokens"], pos0=PAST_LEN)

    return (yp.reshape(bp, tp, D_MODEL), ys.reshape(bs, ts, D_MODEL),
            p_a_k, p_a_v, sp[None], p_b_conv, tail_p[:, 1:][None],
            s_a_k, s_a_v, ss[None], s_b_conv, tail_s[:, 1:][None])
```

```python
import functools

import jax
import jax.numpy as jnp
from jax import lax
from jax.experimental import pallas as pl
from jax.experimental.pallas import tpu as pltpu

F32 = jnp.float32
BF16 = jnp.bfloat16
HIGHEST = lax.Precision.HIGHEST

D_MODEL = 1024
CHUNK = 64
PAST_LEN = 2048
EPS = 1e-6
NEG_INF = -1e30
LOG2E = 1.4426950408889634

A_HEADS = 8
A_KV_HEADS = 2
A_HEAD_DIM = 64
A_WIDTH = A_HEADS * A_HEAD_DIM
A_KV_WIDTH = A_KV_HEADS * A_HEAD_DIM
A_REP = A_HEADS // A_KV_HEADS
WINDOW = 128

B_HEADS = 4
B_HEAD_DIM = 128
B_WIDTH = B_HEADS * B_HEAD_DIM
CONV_W = 4

POOL_SIZES = (2, 4, 8, 16)
C_WIDTH = D_MODEL
C_GROUP = C_WIDTH // len(POOL_SIZES)
POOL_HIST = max(POOL_SIZES) - 1

LANES = 128
SUBLANES = 8

Z_AQ = 0
Z_AK = Z_AQ + A_WIDTH
Z_AV = Z_AK + A_KV_WIDTH
Z_AG = Z_AV + A_KV_WIDTH
Z_BQKV = Z_AG + A_WIDTH
Z_BG = Z_BQKV + 3 * B_WIDTH
Z_BA = Z_BG + B_WIDTH
Z_WIDTH = Z_BA + LANES
GATE_LANE = B_HEADS


def _z_window(rows, width, col, row_block):
    return pl.BlockSpec((pl.Element(rows), pl.Element(width)), lambda *idx: (row_block(*idx) * rows, col))

VMEM_LIMIT = 48 * 1024 * 1024


def _sigmoid(x):
    return 1.0 / (1.0 + jnp.exp(-x))


def _silu(x):
    return x * _sigmoid(x)


def _softplus(x):
    return jnp.maximum(x, 0.0) + jnp.log(1.0 + jnp.exp(-jnp.abs(x)))


def _dot(a, b, precision=None):
    return jnp.dot(a, b, preferred_element_type=F32, precision=precision)


def _dot_nt(a, b, precision=None):
    return lax.dot_general(a, b, (((1,), (1,)), ((), ())), preferred_element_type=F32, precision=precision)


def _dot_tn(a, b, precision=None):
    return lax.dot_general(a, b, (((0,), (0,)), ((), ())), preferred_element_type=F32, precision=precision)


def _log2(n):
    assert n & (n - 1) == 0
    return n.bit_length() - 1


def _in_proj_kernel(x_ref, g_ref, w_ref, kn_ref, qn_ref, z_ref):
    x = x_ref[...]
    xn = x * lax.rsqrt(jnp.mean(x * x, axis=-1, keepdims=True) + EPS) * g_ref[...]
    n_in = w_ref.shape[0]
    z_ref[:, Z_BA:Z_WIDTH] = jnp.zeros((x.shape[0], Z_WIDTH - Z_BA), F32)
    z_ref[:, 0:n_in] = _dot_nt(xn.astype(BF16), w_ref[...])
    first = lax.broadcasted_iota(jnp.int32, (x.shape[0], LANES), 1) < A_HEAD_DIM
    q_gain = qn_ref[...] * (A_HEAD_DIM ** -0.5 * LOG2E)
    for col, gain in [(Z_AQ + r * LANES, q_gain) for r in range(A_WIDTH // LANES)] + [(Z_AK, kn_ref[...])]:
        blk = z_ref[:, col:col + LANES]
        sq = blk * blk
        s0 = jnp.sum(jnp.where(first, sq, 0.0), axis=-1, keepdims=True)
        s1 = jnp.sum(jnp.where(first, 0.0, sq), axis=-1, keepdims=True)
        ms = jnp.where(first, s0, s1) * (1.0 / A_HEAD_DIM)
        z_ref[:, col:col + LANES] = blk * lax.rsqrt(ms + EPS) * gain


def _in_proj(x, g, w, kn, qn, tm):
    n = x.shape[0]
    return pl.pallas_call(
        _in_proj_kernel,
        grid=(n // tm,),
        in_specs=[
            pl.BlockSpec((tm, D_MODEL), lambda i: (i, 0)),
            pl.BlockSpec((1, D_MODEL), lambda i: (0, 0)),
            pl.BlockSpec(w.shape, lambda i: (0, 0), pipeline_mode=pl.Buffered(1)),
            pl.BlockSpec((1, A_KV_WIDTH), lambda i: (0, 0)),
            pl.BlockSpec((1, LANES), lambda i: (0, 0)),
        ],
        out_specs=pl.BlockSpec((tm, Z_WIDTH), lambda i: (i, 0)),
        out_shape=jax.ShapeDtypeStruct((n, Z_WIDTH), F32),
        compiler_params=pltpu.CompilerParams(dimension_semantics=("arbitrary",), vmem_limit_bytes=VMEM_LIMIT),
        name="in_proj",
    )(x, g, w, kn, qn)


A_HALVES = LANES // A_HEAD_DIM
A_QBLOCKS = A_WIDTH // LANES
A_COPIES = A_KV_HEADS


def _attn_half(r, c):
    g = r * A_HALVES // A_REP
    return (g + c) % A_HALVES


def _attn_tables(sinks_ref, bias_ref, *, cq, lk, front, slack, fill):
    rows = A_QBLOCKS * cq
    nvar = front // cq + 1
    row = lax.broadcasted_iota(jnp.int32, (rows, 1), 0)
    blk = row >> _log2(cq)

    def per_row(value):
        out = []
        for c in range(A_COPIES):
            col = jnp.zeros((rows, 1), F32)
            for r in range(A_QBLOCKS):
                col = jnp.where(blk == r, value(r * A_HALVES + _attn_half(r, c)), col)
            out.append(col)
        return out

    @pl.when(fill)
    def _():
        col = lax.broadcasted_iota(jnp.int32, (rows, lk + slack), 1)
        dist = jnp.abs((row & (cq - 1)) + WINDOW - col).astype(F32)
        slopes = per_row(lambda h: 2.0 ** (-8.0 * (h + 1) / A_HEADS))
        sinks = per_row(lambda h: sinks_ref[h])
        for c in range(A_COPIES):
            for var in range(nvar):
                keys = jnp.where((col >= front - var * cq) & (col < lk), -slopes[c] * dist, NEG_INF)
                bias_ref[var, c] = jnp.where(col == lk, sinks[c], keys) * LOG2E

    first = lax.broadcasted_iota(jnp.int32, (cq, LANES), 1) < A_HEAD_DIM
    return first, nvar


def _attn_chunk(qc, kv, biases, first, gate, cq):
    blocks = [qc[:, r * LANES:(r + 1) * LANES] for r in range(A_QBLOCKS)]
    outs = []
    for c in range(A_COPIES):
        lhs = jnp.concatenate(
            [jnp.where(first if _attn_half(r, c) == 0 else jnp.logical_not(first), blocks[r], 0.0)
             for r in range(A_QBLOCKS)], axis=0)
        zero_keys = jnp.zeros((biases[c].shape[1] - kv[c][0].shape[0], A_KV_WIDTH), BF16)
        kk, vv = (jnp.concatenate([x, zero_keys], axis=0) for x in kv[c])
        sc = _dot_nt(lhs.astype(BF16), kk) + biases[c]
        p = jnp.exp2(sc - jnp.max(sc, axis=-1, keepdims=True))
        outs.append(_dot(p.astype(BF16), vv) * (1.0 / jnp.sum(p, axis=-1, keepdims=True)))
    tile = []
    for r in range(A_QBLOCKS):
        rows = slice(r * cq, (r + 1) * cq)
        low = 0 if _attn_half(r, 0) == 0 else 1
        tile.append(jnp.where(first, outs[low][rows], outs[1 - low][rows]))
    return (jnp.concatenate(tile, axis=1) * _silu(gate)).astype(BF16)


def _attn_kernel(sinks_ref, q_ref, k_ref, v_ref, ag_ref, o_ref, kp_ref, vp_ref, bias_ref, *, bt, t):
    cq, lk = CHUNK, WINDOW + CHUNK

    for b in range(bt):
        for src, dst in ((k_ref, kp_ref), (v_ref, vp_ref)):
            x = src[b * t:(b + 1) * t, :]
            dst[:, b, 0:WINDOW, :] = jnp.zeros((A_COPIES, WINDOW, A_KV_WIDTH), BF16)
            dst[0, b, WINDOW:WINDOW + t, :] = x.astype(BF16)
            dst[1, b, WINDOW:WINDOW + t, :] = pltpu.roll(x, A_HEAD_DIM, 1).astype(BF16)

    first, nvar = _attn_tables(sinks_ref, bias_ref, cq=cq, lk=lk, front=WINDOW, slack=bias_ref.shape[-1] - lk,
                               fill=pl.program_id(0) == 0)

    def chunk(b, cg):
        k0 = pl.multiple_of(cg * cq, cq)
        r0 = b * t + k0
        var = jnp.minimum(cg, nvar - 1)
        kv = [(kp_ref[c, b, pl.ds(k0, lk), :], vp_ref[c, b, pl.ds(k0, lk), :]) for c in range(A_COPIES)]
        o_ref[pl.ds(r0, cq), :] = _attn_chunk(
            q_ref[pl.ds(r0, cq), :], kv, [bias_ref[var, c] for c in range(A_COPIES)], first,
            ag_ref[pl.ds(r0, cq), :], cq)

    for b in range(bt):
        lax.fori_loop(0, t // cq, lambda cg, carry, b=b: (chunk(b, cg), carry)[1], 0, unroll=min(t // cq, 8))


def _attention(z, sinks, *, nb, bt, t):
    cq, lk = CHUNK, WINDOW + CHUNK
    kern = functools.partial(_attn_kernel, bt=bt, t=t)
    return pl.pallas_call(
        kern,
        grid=(nb // bt,),
        in_specs=[
            pl.BlockSpec(memory_space=pltpu.SMEM),
            _z_window(bt * t, A_WIDTH, Z_AQ, lambda b: b),
            _z_window(bt * t, A_KV_WIDTH, Z_AK, lambda b: b),
            _z_window(bt * t, A_KV_WIDTH, Z_AV, lambda b: b),
            _z_window(bt * t, A_WIDTH, Z_AG, lambda b: b),
        ],
        out_specs=pl.BlockSpec((bt * t, A_WIDTH), lambda b: (b, 0)),
        out_shape=jax.ShapeDtypeStruct((nb * t, A_WIDTH), BF16),
        scratch_shapes=[
            pltpu.VMEM((A_COPIES, bt, WINDOW + t, A_KV_WIDTH), BF16),
            pltpu.VMEM((A_COPIES, bt, WINDOW + t, A_KV_WIDTH), BF16),
            pltpu.VMEM((WINDOW // cq + 1, A_COPIES, A_QBLOCKS * cq, lk + LANES - lk % LANES), F32),
        ],
        compiler_params=pltpu.CompilerParams(dimension_semantics=("arbitrary",), vmem_limit_bytes=VMEM_LIMIT),
        name="swa_attention",
    )(sinks, z, z, z, z)


STEP_GROUP = 4


def _attn_step_kernel(sinks_ref, q_ref, k_ref, v_ref, ag_ref, o_ref, bias_ref, *, bt, t):
    lk = WINDOW + t
    cols = A_HEADS * t
    key = lax.broadcasted_iota(jnp.int32, (lk, cols), 0)
    col = lax.broadcasted_iota(jnp.int32, (1, cols), 1)
    head = col >> _log2(t)
    slope = jnp.zeros((1, cols), F32)
    sink = jnp.zeros((1, cols), F32)
    for h in range(A_HEADS):
        slope = jnp.where(head == h, 2.0 ** (-8.0 * (h + 1) / A_HEADS), slope)
        sink = jnp.where(head == h, sinks_ref[h], sink)
    sink = sink * LOG2E
    bias_ref[...] = jnp.abs((col & (t - 1)) + WINDOW - key).astype(F32) * (-LOG2E * slope)
    first = lax.broadcasted_iota(jnp.int32, (t, LANES), 1) < A_HEAD_DIM

    def to_half(x, have, want):
        return x if have == want else pltpu.roll(x, A_HEAD_DIM, 1)

    def scores(b):
        q = q_ref[b * t:(b + 1) * t, :]
        qs = []
        for h in range(A_HEADS):
            r, j, g = h // A_HALVES, h % A_HALVES, h // A_REP
            blk = to_half(q[:, r * LANES:(r + 1) * LANES], j, g)
            qs.append(jnp.where(first if g == 0 else jnp.logical_not(first), blk, 0.0))
        qs = jnp.concatenate(qs, axis=0).astype(BF16)
        return _dot_nt(k_ref[b * lk:(b + 1) * lk, :], qs) + bias_ref[...]

    def softmax(st):
        m = jnp.maximum(jnp.max(st, axis=0, keepdims=True), sink)
        p = jnp.exp2(st - m)
        return (p * (1.0 / (jnp.sum(p, axis=0, keepdims=True) + jnp.exp2(sink - m)))).astype(BF16)

    def write(b, o):
        tile = []
        for r in range(A_QBLOCKS):
            halves = [to_half(o[h * t:(h + 1) * t, :], h // A_REP, h % A_HALVES)
                      for h in range(r * A_HALVES, (r + 1) * A_HALVES)]
            tile.append(jnp.where(first, halves[0], halves[1]))
        rows = slice(b * t, (b + 1) * t)
        o_ref[rows, :] = (jnp.concatenate(tile, axis=1) * _silu(ag_ref[rows, :])).astype(BF16)

    for b0 in range(0, bt, STEP_GROUP):
        group = range(b0, min(b0 + STEP_GROUP, bt))
        p = [softmax(st) for st in [scores(b) for b in group]]
        o = [_dot_tn(pb, v_ref[b * lk:(b + 1) * lk, :]) for b, pb in zip(group, p)]
        for b, ob in zip(group, o):
            write(b, ob)


def _attention_step(z, kbuf, vbuf, sinks, *, nb, bt, t):
    kern = functools.partial(_attn_step_kernel, bt=bt, t=t)
    keys = pl.BlockSpec((bt * (WINDOW + t), A_KV_WIDTH), lambda b: (b, 0))
    return pl.pallas_call(
        kern,
        grid=(nb // bt,),
        in_specs=[
            pl.BlockSpec(memory_space=pltpu.SMEM),
            _z_window(bt * t, A_WIDTH, Z_AQ, lambda b: b),
            keys,
            keys,
            _z_window(bt * t, A_WIDTH, Z_AG, lambda b: b),
        ],
        out_specs=pl.BlockSpec((bt * t, A_WIDTH), lambda b: (b, 0)),
        out_shape=jax.ShapeDtypeStruct((nb * t, A_WIDTH), BF16),
        scratch_shapes=[pltpu.VMEM((WINDOW + t, A_HEADS * t), F32)],
        compiler_params=pltpu.CompilerParams(dimension_semantics=("arbitrary",), vmem_limit_bytes=VMEM_LIMIT),
        name="swa_attention_step",
    )(sinks, z, kbuf, vbuf, z)


def _bf(x):
    return x.astype(BF16)


def _unit_lower_inverse(a_list, c, ri, lj, expand):
    base = SUBLANES
    same = (ri >> _log2(base)) == (lj >> _log2(base))
    eye = jnp.where(ri == lj, 1.0, 0.0)
    n1 = [jnp.where(same, -a, 0.0) for a in a_list]
    x = [eye + n for n in n1]
    n1b = [_bf(n) for n in n1]
    n2b = [_bf(_dot(nb, expand(nb))) for nb in n1b]
    n2e = [expand(nb) for nb in n2b]
    x = [xi + _dot(_bf(xi), ne) for xi, ne in zip(x, n2e)]
    n4e = [expand(_bf(_dot(nb, ne))) for nb, ne in zip(n2b, n2e)]
    x = [xi + _dot(_bf(xi), ne) for xi, ne in zip(x, n4e)]
    s = base
    while s < c:
        sel = ((ri >> _log2(2 * s)) == (lj >> _log2(2 * s))) & ((ri >> _log2(s)) != (lj >> _log2(s)))
        xb = [_bf(xi) for xi in x]
        xo = [_dot(b, expand(_bf(jnp.where(sel, a, 0.0)))) for b, a in zip(xb, a_list)]
        x = [xi - _dot(_bf(o), expand(b)) for xi, o, b in zip(x, xo, xb)]
        s *= 2
    return x


def _delta_kernel(qkv_ref, ba_ref, bg_ref, hist_ref, s0_ref, convw_ref, gate_ref, onorm_ref,
                  o_ref, sout_ref, ext_ref, st_ref, lhs_ref, add_ref, s_ref, *, bt, c, tc):
    i = pl.program_id(1)
    hd = B_HEAD_DIM
    nh = B_HEADS
    r = nh * c
    nch = tc // c
    pad = SUBLANES
    units = [(b, cc) for b in range(bt) for cc in range(nch)]

    @pl.when(i == 0)
    def _():
        ext_ref[:, 0:pad, :] = hist_ref[...]
        s_ref[...] = s0_ref[...]

    @pl.when(i > 0)
    def _():
        ext_ref[:, 0:pad, :] = ext_ref[:, tc:tc + pad, :]

    for b in range(bt):
        ext_ref[b, pad:pad + tc, :] = qkv_ref[b * tc:(b + 1) * tc, :]

    for u, (b, cc) in enumerate(units):
        e = ext_ref[b, cc * c:cc * c + pad + c, :]
        acc = e * convw_ref[0:1, :]
        for j in range(1, CONV_W):
            acc = pltpu.roll(acc, 1, 0) + e * convw_ref[j:j + 1, :]
        y = _silu(acc[pad:, :])
        for part in range(3):
            for h in range(nh):
                blk = y[:, (part * nh + h) * hd:(part * nh + h + 1) * hd]
                if part < 2:
                    blk = blk * lax.rsqrt(jnp.sum(blk * blk, axis=-1, keepdims=True) + EPS)
                st_ref[part, u, h * c:(h + 1) * c, :] = blk

    ri = lax.broadcasted_iota(jnp.int32, (c, r), 0)
    li = lax.broadcasted_iota(jnp.int32, (c, r), 1)
    lj = li & (c - 1)
    lh = li >> _log2(c)
    lower = ri >= lj
    strict = ri > lj
    head_sel = [jnp.where(lh == h, 1.0, 0.0).astype(BF16) for h in range(nh)]

    def expand(xb):
        return jnp.concatenate([xb * m for m in head_sel], axis=0)

    def lanes(x_st):
        return jnp.concatenate([x_st[h * c:(h + 1) * c] for h in range(nh)], axis=1)

    def head_blocks(x_st):
        zero = jnp.zeros((c, hd), x_st.dtype)
        return jnp.concatenate(
            [jnp.concatenate([x_st[h * c:(h + 1) * c] if h2 == h else zero for h2 in range(nh)], axis=1)
             for h in range(nh)], axis=0)

    ci = lax.broadcasted_iota(jnp.int32, (c, c), 0)
    cj = lax.broadcasted_iota(jnp.int32, (c, c), 1)
    tril = jnp.where(ci >= cj, 1.0, 0.0)
    di = lax.broadcasted_iota(jnp.int32, (hd, hd), 0)
    dj = lax.broadcasted_iota(jnp.int32, (hd, hd), 1)
    eye_hd = di == dj
    neg_rate = -jnp.exp(gate_ref[0:1, :])
    dt_bias = gate_ref[1:2, :]
    onorm = onorm_ref[...]

    def stack_cols(x, lane0):
        return jnp.concatenate([x[:, lane0 + h:lane0 + h + 1] for h in range(nh)], axis=0)

    beta, gc, gcc, gr, glast = [], [], [], [], []
    for b, cc in units:
        ba = ba_ref[b * tc + cc * c:b * tc + (cc + 1) * c, :]
        g_all = neg_rate * _softplus(ba + dt_bias)
        gcum = _dot(tril, g_all, HIGHEST)
        gcum_t = gcum.T
        beta.append(stack_cols(_sigmoid(ba), 0))
        gc.append(stack_cols(gcum, GATE_LANE))
        gcc_h = jnp.zeros((c, r), F32)
        for h in range(nh):
            gcc_h = jnp.where(lh == h, gcum[:, GATE_LANE + h:GATE_LANE + h + 1], gcc_h)
        gcc.append(gcc_h)
        gr.append(jnp.concatenate([gcum_t[GATE_LANE + h:GATE_LANE + h + 1, :] for h in range(nh)], axis=1))
        glast.append(jnp.concatenate(
            [jnp.broadcast_to(gcum[c - 1:c, GATE_LANE + h:GATE_LANE + h + 1], (c, 1)) for h in range(nh)], axis=0))

    def fold_units(us):
        n = range(len(us))
        q = [st_ref[0, u] * (hd ** -0.5) for u in us]
        k = [st_ref[1, u] for u in us]
        v = [st_ref[2, u] for u in us]
        kb = [k[j] * beta[u] for j, u in enumerate(us)]
        big = [_dot_nt(_bf(jnp.concatenate([lanes(kb[j]), lanes(q[j])], axis=0)), head_blocks(_bf(k[j]))) for j in n]
        decay = [jnp.where(lower, jnp.exp(jnp.where(lower, gcc[u] - gr[u], 0.0)), 0.0) for u in us]
        a = [jnp.where(strict, big[j][:c] * decay[j], 0.0) for j in n]
        qkd = [expand(_bf(big[j][c:] * decay[j])) for j in n]
        tinv = _unit_lower_inverse(a, c, ri, lj, expand)
        eg = [jnp.exp(gc[u]) for u in us]
        rhs = [_bf(jnp.concatenate([v[j] * beta[u], kb[j] * eg[j]], axis=1)) for j, u in enumerate(us)]
        sol = [_dot(expand(_bf(tinv[j])), rhs[j]) for j in n]
        solb = [_bf(s) for s in sol]
        fold = [_dot(qkd[j], solb[j]) for j in n]
        k_dec = [_bf(k[j] * jnp.exp(glast[u] - gc[u])) for j, u in enumerate(us)]
        for j, u in enumerate(us):
            qp = q[j] * eg[j] - fold[j][:, hd:]
            for h in range(nh):
                rows = slice(h * c, (h + 1) * c)
                kt = _dot_tn(k_dec[j][rows], solb[j][rows])
                g_tot = jnp.exp(glast[u][h * c:h * c + 1, :])
                lhs_ref[u, h, 0:hd, :] = _bf(jnp.where(eye_hd, g_tot, 0.0) - kt[:, hd:])
                lhs_ref[u, h, hd:hd + c, :] = _bf(qp[rows])
                add_ref[u, h, 0:hd, :] = kt[:, :hd]
                add_ref[u, h, hd:hd + c, :] = fold[j][rows, :hd]

    fold_units(list(range(len(units))))

    for u, (b, cc) in enumerate(units):
        rows = slice(b * tc + cc * c, b * tc + (cc + 1) * c)
        for h in range(nh):
            res = _dot(lhs_ref[u, h], _bf(s_ref[b, h])) + add_ref[u, h]
            s_ref[b, h] = res[:hd]
            o = res[hd:]
            on = o * lax.rsqrt(jnp.mean(o * o, axis=-1, keepdims=True) + EPS) * onorm
            bg = bg_ref[rows, h * hd:(h + 1) * hd]
            o_ref[rows, h * hd:(h + 1) * hd] = (on * _silu(bg)).astype(BF16)
    sout_ref[...] = s_ref[...]


def _delta(z, hist, s0, convw, gate, onorm, *, nb, bt, t, tc, c):
    nt = t // tc
    units = bt * (tc // c)
    state = (bt, B_HEADS, B_HEAD_DIM, B_HEAD_DIM)
    kern = functools.partial(_delta_kernel, bt=bt, c=c, tc=tc)
    return pl.pallas_call(
        kern,
        grid=(nb // bt, nt),
        in_specs=[
            _z_window(bt * tc, 3 * B_WIDTH, Z_BQKV, lambda b, i: b * nt + i),
            _z_window(bt * tc, LANES, Z_BA, lambda b, i: b * nt + i),
            _z_window(bt * tc, B_WIDTH, Z_BG, lambda b, i: b * nt + i),
            pl.BlockSpec((bt, SUBLANES, 3 * B_WIDTH), lambda b, i: (b, 0, 0)),
            pl.BlockSpec(state, lambda b, i: (b, 0, 0, 0)),
            pl.BlockSpec((CONV_W, 3 * B_WIDTH), lambda b, i: (0, 0)),
            pl.BlockSpec((2, LANES), lambda b, i: (0, 0)),
            pl.BlockSpec((1, B_HEAD_DIM), lambda b, i: (0, 0)),
        ],
        out_specs=[
            pl.BlockSpec((bt * tc, B_WIDTH), lambda b, i: (b * nt + i, 0)),
            pl.BlockSpec(state, lambda b, i: (b, 0, 0, 0)),
        ],
        out_shape=[
            jax.ShapeDtypeStruct((nb * t, B_WIDTH), BF16),
            jax.ShapeDtypeStruct((nb, B_HEADS, B_HEAD_DIM, B_HEAD_DIM), F32),
        ],
        scratch_shapes=[
            pltpu.VMEM((bt, SUBLANES + tc, 3 * B_WIDTH), F32),
            pltpu.VMEM((3, units, B_HEADS * c, B_HEAD_DIM), F32),
            pltpu.VMEM((units, B_HEADS, B_HEAD_DIM + c, B_HEAD_DIM), BF16),
            pltpu.VMEM((units, B_HEADS, B_HEAD_DIM + c, B_HEAD_DIM), F32),
            pltpu.VMEM(state, F32),
        ],
        compiler_params=pltpu.CompilerParams(dimension_semantics=("arbitrary", "arbitrary"),
                                             vmem_limit_bytes=VMEM_LIMIT),
        name="gated_delta",
    )(z, z, z, hist, s0, convw, gate, onorm)


def _pool_kernel(h_ref, oa_ref, ob_ref, wa_ref, wb_ref, hist_ref, g_ref, win_ref, wgrp_ref, scale_ref, wout_ref,
                 y_ref, tail_ref, ext_ref, *, bt, tt, pos0):
    i = pl.program_id(1)
    pad = POOL_HIST + 1
    rows = bt * tt

    @pl.when(i == 0)
    def _():
        pos = pos0 - pad + lax.broadcasted_iota(jnp.int32, (1, pad, 1), 1)
        ext_ref[:, 0:pad, :] = jnp.where(pos >= 0, hist_ref[...], 0.0)

    @pl.when(i > 0)
    def _():
        ext_ref[:, 0:pad, :] = ext_ref[:, tt:tt + pad, :]

    x = h_ref[...] + _dot(oa_ref[...], wa_ref[...]) + _dot(ob_ref[...], wb_ref[...])
    xn = x * lax.rsqrt(jnp.mean(x * x, axis=-1, keepdims=True) + EPS) * g_ref[...]
    z = _dot(xn.astype(BF16), win_ref[...])
    u = z[:, :C_WIDTH]
    gate = z[:, C_WIDTH:]
    ext_ref[:, pad:pad + tt, :] = u.reshape(bt, tt, C_WIDTH)
    tail_ref[...] = ext_ref[:, tt:tt + pad, :]

    tpos = pos0 + i * tt + (lax.broadcasted_iota(jnp.int32, (rows, 1), 0) & (tt - 1))
    mixed = []
    for gi, w in enumerate(POOL_SIZES):
        cols = slice(gi * C_GROUP, (gi + 1) * C_GROUP)
        s = ext_ref[:, :, cols].reshape(bt * (pad + tt), C_GROUP)
        sh = 1
        while sh < w:
            s = s + pltpu.roll(s, sh, 0)
            sh *= 2
        s = s.reshape(bt, pad + tt, C_GROUP)[:, pad:, :].reshape(rows, C_GROUP)
        cnt = jnp.minimum(tpos + 1, w).astype(F32)
        pooled = s / cnt - u[:, cols]
        m = _dot(pooled.astype(BF16), wgrp_ref[gi]) * scale_ref[:, cols]
        mixed.append((m * _silu(gate[:, cols])).astype(BF16))
    y_ref[...] = x + _dot(jnp.concatenate(mixed, axis=1), wout_ref[...])


def _pool_layer(h, oa, ob, wa, wb, hist, g, win, wgrp, scale, wout, *, nb, bt, t, tt, pos0):
    nt = t // tt
    pad = POOL_HIST + 1
    rows = bt * tt
    kern = functools.partial(_pool_kernel, bt=bt, tt=tt, pos0=pos0)
    tile = lambda b, i: (b * nt + i, 0)
    const2 = lambda b, i: (0, 0)
    once = pl.Buffered(1)
    return pl.pallas_call(
        kern,
        grid=(nb // bt, nt),
        in_specs=[
            pl.BlockSpec((rows, D_MODEL), tile),
            pl.BlockSpec((rows, A_WIDTH), tile),
            pl.BlockSpec((rows, B_WIDTH), tile),
            pl.BlockSpec((A_WIDTH, D_MODEL), const2, pipeline_mode=once),
            pl.BlockSpec((B_WIDTH, D_MODEL), const2, pipeline_mode=once),
            pl.BlockSpec((bt, pad, C_WIDTH), lambda b, i: (b, 0, 0)),
            pl.BlockSpec((1, D_MODEL), const2),
            pl.BlockSpec((D_MODEL, 2 * C_WIDTH), const2, pipeline_mode=once),
            pl.BlockSpec((len(POOL_SIZES), C_GROUP, C_GROUP), lambda b, i: (0, 0, 0), pipeline_mode=once),
            pl.BlockSpec((1, C_WIDTH), const2),
            pl.BlockSpec((C_WIDTH, D_MODEL), const2, pipeline_mode=once),
        ],
        out_specs=[
            pl.BlockSpec((rows, D_MODEL), tile),
            pl.BlockSpec((bt, pad, C_WIDTH), lambda b, i: (b, 0, 0)),
        ],
        out_shape=[
            jax.ShapeDtypeStruct((nb * t, D_MODEL), F32),
            jax.ShapeDtypeStruct((nb, pad, C_WIDTH), F32),
        ],
        scratch_shapes=[pltpu.VMEM((bt, pad + tt, C_WIDTH), F32)],
        compiler_params=pltpu.CompilerParams(dimension_semantics=("arbitrary", "arbitrary"),
                                             vmem_limit_bytes=VMEM_LIMIT),
        name="out_proj_pool_layer",
    )(h, oa, ob, wa, wb, hist, g, win, wgrp, scale, wout)


def _ab_weights(norm_g, w_in, q_norm, k_norm, sinks, conv_w, a_log, dt_bias, o_norm, w_out):
    w = jnp.swapaxes(w_in, 0, 1).astype(BF16)
    lane_pad = (GATE_LANE, LANES - GATE_LANE - B_HEADS)
    gate = jnp.stack([jnp.pad(a_log.astype(F32), lane_pad), jnp.pad(dt_bias.astype(F32), lane_pad)])
    return dict(
        norm_g=norm_g.reshape(1, D_MODEL), w=w, kn=jnp.tile(k_norm, A_KV_HEADS).reshape(1, A_KV_WIDTH),
        qn=jnp.tile(q_norm, A_KV_HEADS).reshape(1, LANES), sinks=sinks.astype(F32), conv_w=conv_w, gate=gate,
        onorm=o_norm.reshape(1, B_HEAD_DIM), wa=w_out[:A_WIDTH].astype(BF16), wb=w_out[A_WIDTH:].astype(BF16))


def _group_tiles(t):
    if t >= CHUNK:
        return dict(proj_rows=1024, seqs_attn=1, seqs_delta=1, delta_tokens=1024, chunk=CHUNK, seqs_pool=1,
                    pool_tokens=1024)
    return dict(proj_rows=256, seqs_attn=32, seqs_delta=16, delta_tokens=t, chunk=t, seqs_pool=256 // t, pool_tokens=t)


def _ab_mixers(h, wts, cache_k, cache_v, s0, conv_hist, *, nb, t):
    tl = _group_tiles(t)
    z = _in_proj(h, wts["norm_g"], wts["w"], wts["kn"], wts["qn"], min(nb * t, tl["proj_rows"]))
    bt = tl["seqs_attn"]
    if cache_k is None:
        o_a = _attention(z, wts["sinks"], nb=nb, bt=bt, t=t)
    else:
        k_new = z[:, Z_AK:Z_AK + A_KV_WIDTH].reshape(nb, t, A_KV_WIDTH)
        v_new = z[:, Z_AV:Z_AV + A_KV_WIDTH].reshape(nb, t, A_KV_WIDTH)
        kbuf = jnp.concatenate([cache_k.reshape(nb, WINDOW, A_KV_WIDTH), k_new], axis=1).astype(BF16)
        vbuf = jnp.concatenate([cache_v.reshape(nb, WINDOW, A_KV_WIDTH), v_new], axis=1).astype(BF16)
        o_a = _attention_step(z, kbuf.reshape(nb * (WINDOW + t), A_KV_WIDTH), vbuf.reshape(nb * (WINDOW + t), A_KV_WIDTH),
                              wts["sinks"], nb=nb, bt=bt, t=t)
    hist = jnp.pad(conv_hist.astype(F32), ((0, 0), (SUBLANES - (CONV_W - 1), 0), (0, 0)))
    o_b, s_new = _delta(z, hist, s0.astype(F32), wts["conv_w"], wts["gate"], wts["onorm"], nb=nb,
                        bt=tl["seqs_delta"], t=t, tc=tl["delta_tokens"], c=tl["chunk"])
    return o_a, o_b, z, s_new


def _last_rows(z, col, width, nb, t, rows):
    return z.reshape(nb, t, Z_WIDTH)[:, t - rows:, col:col + width]


def _cache_rows(z, col, nb, t, rows):
    return _last_rows(z, col, A_KV_WIDTH, nb, t, rows).reshape(nb, rows, A_KV_HEADS, A_HEAD_DIM)


def kernel(x_prompt, x_sample, cache_a_k, cache_a_v, state_b_s, state_b_conv, state_c_pool,
           norm_ab, w_in_ab, q_norm_a, k_norm_a, sinks_a, conv_b, a_log_b, dt_bias_b, o_norm_b, w_out_ab,
           norm_c, w_in_c, w_grp_c, scale_c, w_out_c):
    bp, tp, _ = x_prompt.shape
    bs, ts, _ = x_sample.shape
    hp = x_prompt.reshape(bp * tp, D_MODEL)
    hs = x_sample.reshape(bs * ts, D_MODEL)

    wts = _ab_weights(norm_ab[0], w_in_ab[0], q_norm_a[0], k_norm_a[0], sinks_a[0], conv_b[0], a_log_b[0],
                      dt_bias_b[0], o_norm_b[0], w_out_ab[0])
    s0 = jnp.zeros((bp, B_HEADS, B_HEAD_DIM, B_HEAD_DIM), F32)
    c0 = jnp.zeros((bp, CONV_W - 1, 3 * B_WIDTH), F32)
    oap, obp, zp, sp = _ab_mixers(hp, wts, None, None, s0, c0, nb=bp, t=tp)
    oas, obs, zs, ss = _ab_mixers(hs, wts, cache_a_k[0], cache_a_v[0], state_b_s[0], state_b_conv[0], nb=bs, t=ts)
    p_a_k = _cache_rows(zp, Z_AK, bp, tp, WINDOW)[None]
    p_a_v = _cache_rows(zp, Z_AV, bp, tp, WINDOW)[None]
    s_a_k = _cache_rows(zs, Z_AK, bs, ts, ts)[None]
    s_a_v = _cache_rows(zs, Z_AV, bs, ts, ts)[None]
    p_b_conv = _last_rows(zp, Z_BQKV, 3 * B_WIDTH, bp, tp, CONV_W - 1)[None]
    s_b_conv = _last_rows(zs, Z_BQKV, 3 * B_WIDTH, bs, ts, CONV_W - 1)[None]

    g_c = norm_c[0].reshape(1, D_MODEL)
    win = w_in_c[0].astype(BF16)
    wgrp = w_grp_c[0].astype(BF16)
    scale = scale_c[0].reshape(1, C_WIDTH)
    wout = w_out_c[0].astype(BF16)
    h0 = jnp.zeros((bp, POOL_HIST + 1, C_WIDTH), F32)
    hs0 = jnp.pad(state_c_pool[0].astype(F32), ((0, 0), (1, 0), (0, 0)))
    tlp, tls = _group_tiles(tp), _group_tiles(ts)
    yp, tail_p = _pool_layer(hp, oap, obp, wts["wa"], wts["wb"], h0, g_c, win, wgrp, scale, wout, nb=bp,
                             bt=tlp["seqs_pool"], t=tp, tt=tlp["pool_tokens"], pos0=0)
    ys, tail_s = _pool_layer(hs, oas, obs, wts["wa"], wts["wb"], hs0, g_c, win, wgrp, scale, wout, nb=bs,
                             bt=tls["seqs_pool"], t=ts, tt=tls["pool_tokens"], pos0=PAST_LEN)

    return (yp.reshape(bp, tp, D_MODEL), ys.reshape(bs, ts, D_MODEL),
            p_a_k, p_a_v, sp[None], p_b_conv, tail_p[:, 1:][None],
            s_a_k, s_a_v, ss[None], s_b_conv, tail_s[:, 1:][None])
```

```python
import functools

import jax
import jax.numpy as jnp
from jax import lax
from jax.experimental import pallas as pl
from jax.experimental.pallas import tpu as pltpu

F32 = jnp.float32
BF16 = jnp.bfloat16
HIGHEST = lax.Precision.HIGHEST

D_MODEL = 1024
CHUNK = 64
PAST_LEN = 2048
EPS = 1e-6
NEG_INF = -1e30
LOG2E = 1.4426950408889634

A_HEADS = 8
A_KV_HEADS = 2
A_HEAD_DIM = 64
A_WIDTH = A_HEADS * A_HEAD_DIM
A_KV_WIDTH = A_KV_HEADS * A_HEAD_DIM
A_REP = A_HEADS // A_KV_HEADS
WINDOW = 128

B_HEADS = 4
B_HEAD_DIM = 128
B_WIDTH = B_HEADS * B_HEAD_DIM
CONV_W = 4

POOL_SIZES = (2, 4, 8, 16)
C_WIDTH = D_MODEL
C_GROUP = C_WIDTH // len(POOL_SIZES)
POOL_HIST = max(POOL_SIZES) - 1

LANES = 128
SUBLANES = 8

Z_AQ = 0
Z_AK = Z_AQ + A_WIDTH
Z_AV = Z_AK + A_KV_WIDTH
Z_AG = Z_AV + A_KV_WIDTH
Z_BQKV = Z_AG + A_WIDTH
Z_BG = Z_BQKV + 3 * B_WIDTH
Z_BA = Z_BG + B_WIDTH
Z_WIDTH = Z_BA + LANES
GATE_LANE = B_HEADS


def _z_window(rows, width, col, row_block):
    return pl.BlockSpec((pl.Element(rows), pl.Element(width)), lambda *idx: (row_block(*idx) * rows, col))

VMEM_LIMIT = 48 * 1024 * 1024


def _sigmoid(x):
    return 1.0 / (1.0 + jnp.exp(-x))


def _silu(x):
    return x * _sigmoid(x)


def _softplus(x):
    return jnp.maximum(x, 0.0) + jnp.log(1.0 + jnp.exp(-jnp.abs(x)))


def _dot(a, b, precision=None):
    return jnp.dot(a, b, preferred_element_type=F32, precision=precision)


def _dot_nt(a, b, precision=None):
    return lax.dot_general(a, b, (((1,), (1,)), ((), ())), preferred_element_type=F32, precision=precision)


def _dot_tn(a, b, precision=None):
    return lax.dot_general(a, b, (((0,), (0,)), ((), ())), preferred_element_type=F32, precision=precision)


def _log2(n):
    assert n & (n - 1) == 0
    return n.bit_length() - 1


def _in_proj_kernel(x_ref, g_ref, w_ref, kn_ref, qn_ref, z_ref):
    x = x_ref[...]
    xn = x * lax.rsqrt(jnp.mean(x * x, axis=-1, keepdims=True) + EPS) * g_ref[...]
    n_in = w_ref.shape[0]
    z_ref[:, Z_BA:Z_WIDTH] = jnp.zeros((x.shape[0], Z_WIDTH - Z_BA), F32)
    z_ref[:, 0:n_in] = _dot_nt(xn.astype(BF16), w_ref[...])
    first = lax.broadcasted_iota(jnp.int32, (x.shape[0], LANES), 1) < A_HEAD_DIM
    q_gain = qn_ref[...] * (A_HEAD_DIM ** -0.5 * LOG2E)
    for col, gain in [(Z_AQ + r * LANES, q_gain) for r in range(A_WIDTH // LANES)] + [(Z_AK, kn_ref[...])]:
        blk = z_ref[:, col:col + LANES]
        sq = blk * blk
        s0 = jnp.sum(jnp.where(first, sq, 0.0), axis=-1, keepdims=True)
        s1 = jnp.sum(jnp.where(first, 0.0, sq), axis=-1, keepdims=True)
        ms = jnp.where(first, s0, s1) * (1.0 / A_HEAD_DIM)
        z_ref[:, col:col + LANES] = blk * lax.rsqrt(ms + EPS) * gain


def _in_proj(x, g, w, kn, qn, tm):
    n = x.shape[0]
    return pl.pallas_call(
        _in_proj_kernel,
        grid=(n // tm,),
        in_specs=[
            pl.BlockSpec((tm, D_MODEL), lambda i: (i, 0)),
            pl.BlockSpec((1, D_MODEL), lambda i: (0, 0)),
            pl.BlockSpec(w.shape, lambda i: (0, 0), pipeline_mode=pl.Buffered(1)),
            pl.BlockSpec((1, A_KV_WIDTH), lambda i: (0, 0)),
            pl.BlockSpec((1, LANES), lambda i: (0, 0)),
        ],
        out_specs=pl.BlockSpec((tm, Z_WIDTH), lambda i: (i, 0)),
        out_shape=jax.ShapeDtypeStruct((n, Z_WIDTH), F32),
        compiler_params=pltpu.CompilerParams(dimension_semantics=("arbitrary",), vmem_limit_bytes=VMEM_LIMIT),
        name="in_proj",
    )(x, g, w, kn, qn)


A_HALVES = LANES // A_HEAD_DIM
A_QBLOCKS = A_WIDTH // LANES
A_COPIES = A_KV_HEADS


def _attn_half(r, c):
    g = r * A_HALVES // A_REP
    return (g + c) % A_HALVES


def _attn_tables(sinks_ref, bias_ref, *, cq, lk, front, slack, fill):
    rows = A_QBLOCKS * cq
    nvar = front // cq + 1
    row = lax.broadcasted_iota(jnp.int32, (rows, 1), 0)
    blk = row >> _log2(cq)

    def per_row(value):
        out = []
        for c in range(A_COPIES):
            col = jnp.zeros((rows, 1), F32)
            for r in range(A_QBLOCKS):
                col = jnp.where(blk == r, value(r * A_HALVES + _attn_half(r, c)), col)
            out.append(col)
        return out

    @pl.when(fill)
    def _():
        col = lax.broadcasted_iota(jnp.int32, (rows, lk + slack), 1)
        dist = jnp.abs((row & (cq - 1)) + WINDOW - col).astype(F32)
        slopes = per_row(lambda h: 2.0 ** (-8.0 * (h + 1) / A_HEADS))
        sinks = per_row(lambda h: sinks_ref[h])
        for c in range(A_COPIES):
            for var in range(nvar):
                keys = jnp.where((col >= front - var * cq) & (col < lk), -slopes[c] * dist, NEG_INF)
                bias_ref[var, c] = jnp.where(col == lk, sinks[c], keys) * LOG2E

    first = lax.broadcasted_iota(jnp.int32, (cq, LANES), 1) < A_HEAD_DIM
    return first, nvar


def _attn_chunk(qc, kv, biases, first, gate, cq):
    blocks = [qc[:, r * LANES:(r + 1) * LANES] for r in range(A_QBLOCKS)]
    outs = []
    for c in range(A_COPIES):
        lhs = jnp.concatenate(
            [jnp.where(first if _attn_half(r, c) == 0 else jnp.logical_not(first), blocks[r], 0.0)
             for r in range(A_QBLOCKS)], axis=0)
        zero_keys = jnp.zeros((biases[c].shape[1] - kv[c][0].shape[0], A_KV_WIDTH), BF16)
        kk, vv = (jnp.concatenate([x, zero_keys], axis=0) for x in kv[c])
        sc = _dot_nt(lhs.astype(BF16), kk) + biases[c]
        p = jnp.exp2(sc - jnp.max(sc, axis=-1, keepdims=True))
        outs.append(_dot(p.astype(BF16), vv) * (1.0 / jnp.sum(p, axis=-1, keepdims=True)))
    tile = []
    for r in range(A_QBLOCKS):
        rows = slice(r * cq, (r + 1) * cq)
        low = 0 if _attn_half(r, 0) == 0 else 1
        tile.append(jnp.where(first, outs[low][rows], outs[1 - low][rows]))
    return (jnp.concatenate(tile, axis=1) * _silu(gate)).astype(BF16)


ATTN_STAGE_CHUNKS = 8


def _attn_kernel(sinks_ref, q_ref, k_ref, v_ref, ag_ref, o_ref, kp_ref, vp_ref, bias_ref, *, bt, t):
    cq, lk = CHUNK, WINDOW + CHUNK
    n_chunks = t // cq
    ahead = min(n_chunks, ATTN_STAGE_CHUNKS)

    first, nvar = _attn_tables(sinks_ref, bias_ref, cq=cq, lk=lk, front=WINDOW, slack=bias_ref.shape[-1] - lk,
                               fill=pl.program_id(0) == 0)

    def stage(b, c0, n):
        r0, r1 = c0 * cq, min(c0 + n, n_chunks) * cq
        for src, dst in ((k_ref, kp_ref), (v_ref, vp_ref)):
            x = src[b * t + r0:b * t + r1, :]
            dst[0, b, WINDOW + r0:WINDOW + r1, :] = x.astype(BF16)
            dst[1, b, WINDOW + r0:WINDOW + r1, :] = pltpu.roll(x, A_HEAD_DIM, 1).astype(BF16)

    def chunk(b, cg):
        k0 = cg * cq
        r0 = b * t + k0
        var = min(cg, nvar - 1)
        kv = [(kp_ref[c, b, k0:k0 + lk, :], vp_ref[c, b, k0:k0 + lk, :]) for c in range(A_COPIES)]
        o_ref[r0:r0 + cq, :] = _attn_chunk(
            q_ref[r0:r0 + cq, :], kv, [bias_ref[var, c] for c in range(A_COPIES)], first, ag_ref[r0:r0 + cq, :], cq)

    for b in range(bt):
        for dst in (kp_ref, vp_ref):
            dst[:, b, 0:WINDOW, :] = jnp.zeros((A_COPIES, WINDOW, A_KV_WIDTH), BF16)
        stage(b, 0, ahead)
        for c0 in range(0, n_chunks, ahead):
            if c0 + ahead < n_chunks:
                stage(b, c0 + ahead, ahead)
            for cg in range(c0, min(c0 + ahead, n_chunks)):
                chunk(b, cg)


def _attention(z, sinks, *, nb, bt, t):
    cq, lk = CHUNK, WINDOW + CHUNK
    kern = functools.partial(_attn_kernel, bt=bt, t=t)
    return pl.pallas_call(
        kern,
        grid=(nb // bt,),
        in_specs=[
            pl.BlockSpec(memory_space=pltpu.SMEM),
            _z_window(bt * t, A_WIDTH, Z_AQ, lambda b: b),
            _z_window(bt * t, A_KV_WIDTH, Z_AK, lambda b: b),
            _z_window(bt * t, A_KV_WIDTH, Z_AV, lambda b: b),
            _z_window(bt * t, A_WIDTH, Z_AG, lambda b: b),
        ],
        out_specs=pl.BlockSpec((bt * t, A_WIDTH), lambda b: (b, 0)),
        out_shape=jax.ShapeDtypeStruct((nb * t, A_WIDTH), BF16),
        scratch_shapes=[
            pltpu.VMEM((A_COPIES, bt, WINDOW + t, A_KV_WIDTH), BF16),
            pltpu.VMEM((A_COPIES, bt, WINDOW + t, A_KV_WIDTH), BF16),
            pltpu.VMEM((WINDOW // cq + 1, A_COPIES, A_QBLOCKS * cq, lk + LANES - lk % LANES), F32),
        ],
        compiler_params=pltpu.CompilerParams(dimension_semantics=("arbitrary",), vmem_limit_bytes=VMEM_LIMIT),
        name="swa_attention",
    )(sinks, z, z, z, z)


STEP_GROUP = 4


def _attn_step_kernel(sinks_ref, q_ref, k_ref, v_ref, ag_ref, o_ref, bias_ref, *, bt, t):
    lk = WINDOW + t
    cols = A_HEADS * t
    key = lax.broadcasted_iota(jnp.int32, (lk, cols), 0)
    col = lax.broadcasted_iota(jnp.int32, (1, cols), 1)
    head = col >> _log2(t)
    slope = jnp.zeros((1, cols), F32)
    sink = jnp.zeros((1, cols), F32)
    for h in range(A_HEADS):
        slope = jnp.where(head == h, 2.0 ** (-8.0 * (h + 1) / A_HEADS), slope)
        sink = jnp.where(head == h, sinks_ref[h], sink)
    sink = sink * LOG2E
    bias_ref[...] = jnp.abs((col & (t - 1)) + WINDOW - key).astype(F32) * (-LOG2E * slope)
    first = lax.broadcasted_iota(jnp.int32, (t, LANES), 1) < A_HEAD_DIM

    def to_half(x, have, want):
        return x if have == want else pltpu.roll(x, A_HEAD_DIM, 1)

    def scores(b):
        q = q_ref[b * t:(b + 1) * t, :]
        qs = []
        for h in range(A_HEADS):
            r, j, g = h // A_HALVES, h % A_HALVES, h // A_REP
            blk = to_half(q[:, r * LANES:(r + 1) * LANES], j, g)
            qs.append(jnp.where(first if g == 0 else jnp.logical_not(first), blk, 0.0))
        qs = jnp.concatenate(qs, axis=0).astype(BF16)
        return _dot_nt(k_ref[b * lk:(b + 1) * lk, :].astype(BF16), qs) + bias_ref[...]

    def softmax(st):
        m = jnp.maximum(jnp.max(st, axis=0, keepdims=True), sink)
        p = jnp.exp2(st - m)
        return (p * (1.0 / (jnp.sum(p, axis=0, keepdims=True) + jnp.exp2(sink - m)))).astype(BF16)

    def write(b, o):
        tile = []
        for r in range(A_QBLOCKS):
            halves = [to_half(o[h * t:(h + 1) * t, :], h // A_REP, h % A_HALVES)
                      for h in range(r * A_HALVES, (r + 1) * A_HALVES)]
            tile.append(jnp.where(first, halves[0], halves[1]))
        rows = slice(b * t, (b + 1) * t)
        o_ref[rows, :] = (jnp.concatenate(tile, axis=1) * _silu(ag_ref[rows, :])).astype(BF16)

    for b0 in range(0, bt, STEP_GROUP):
        group = range(b0, min(b0 + STEP_GROUP, bt))
        p = [softmax(st) for st in [scores(b) for b in group]]
        o = [_dot_tn(pb, v_ref[b * lk:(b + 1) * lk, :].astype(BF16)) for b, pb in zip(group, p)]
        for b, ob in zip(group, o):
            write(b, ob)


def _attention_step(z, kbuf, vbuf, sinks, *, nb, bt, t):
    kern = functools.partial(_attn_step_kernel, bt=bt, t=t)
    keys = pl.BlockSpec((bt * (WINDOW + t), A_KV_WIDTH), lambda b: (b, 0))
    return pl.pallas_call(
        kern,
        grid=(nb // bt,),
        in_specs=[
            pl.BlockSpec(memory_space=pltpu.SMEM),
            _z_window(bt * t, A_WIDTH, Z_AQ, lambda b: b),
            keys,
            keys,
            _z_window(bt * t, A_WIDTH, Z_AG, lambda b: b),
        ],
        out_specs=pl.BlockSpec((bt * t, A_WIDTH), lambda b: (b, 0)),
        out_shape=jax.ShapeDtypeStruct((nb * t, A_WIDTH), BF16),
        scratch_shapes=[pltpu.VMEM((WINDOW + t, A_HEADS * t), F32)],
        compiler_params=pltpu.CompilerParams(dimension_semantics=("arbitrary",), vmem_limit_bytes=VMEM_LIMIT),
        name="swa_attention_step",
    )(sinks, z, kbuf, vbuf, z)


def _bf(x):
    return x.astype(BF16)


def _unit_lower_inverse(a_list, c, ri, lj, expand):
    base = SUBLANES
    same = (ri >> _log2(base)) == (lj >> _log2(base))
    eye = jnp.where(ri == lj, 1.0, 0.0)
    n1 = [jnp.where(same, -a, 0.0) for a in a_list]
    x = [eye + n for n in n1]
    n1b = [_bf(n) for n in n1]
    n2b = [_bf(_dot(nb, expand(nb))) for nb in n1b]
    n2e = [expand(nb) for nb in n2b]
    x = [xi + _dot(_bf(xi), ne) for xi, ne in zip(x, n2e)]
    n4e = [expand(_bf(_dot(nb, ne))) for nb, ne in zip(n2b, n2e)]
    x = [xi + _dot(_bf(xi), ne) for xi, ne in zip(x, n4e)]
    s = base
    while s < c:
        sel = ((ri >> _log2(2 * s)) == (lj >> _log2(2 * s))) & ((ri >> _log2(s)) != (lj >> _log2(s)))
        xb = [_bf(xi) for xi in x]
        xo = [_dot(b, expand(_bf(jnp.where(sel, a, 0.0)))) for b, a in zip(xb, a_list)]
        x = [xi - _dot(_bf(o), expand(b)) for xi, o, b in zip(x, xo, xb)]
        s *= 2
    return x


def _delta_kernel(qkv_ref, ba_ref, bg_ref, hist_ref, s0_ref, convw_ref, gate_ref, onorm_ref,
                  o_ref, sout_ref, ext_ref, st_ref, lhs_ref, add_ref, s_ref, *, bt, c, tc):
    i = pl.program_id(1)
    hd = B_HEAD_DIM
    nh = B_HEADS
    r = nh * c
    nch = tc // c
    pad = SUBLANES
    units = [(b, cc) for b in range(bt) for cc in range(nch)]

    @pl.when(i == 0)
    def _():
        ext_ref[:, 0:pad, :] = hist_ref[...]
        s_ref[...] = s0_ref[...]

    @pl.when(i > 0)
    def _():
        ext_ref[:, 0:pad, :] = ext_ref[:, tc:tc + pad, :]

    for b in range(bt):
        ext_ref[b, pad:pad + tc, :] = qkv_ref[b * tc:(b + 1) * tc, :]

    for u, (b, cc) in enumerate(units):
        e = ext_ref[b, cc * c:cc * c + pad + c, :]
        acc = e * convw_ref[0:1, :]
        for j in range(1, CONV_W):
            acc = pltpu.roll(acc, 1, 0) + e * convw_ref[j:j + 1, :]
        y = _silu(acc[pad:, :])
        for part in range(3):
            for h in range(nh):
                blk = y[:, (part * nh + h) * hd:(part * nh + h + 1) * hd]
                if part < 2:
                    blk = blk * lax.rsqrt(jnp.sum(blk * blk, axis=-1, keepdims=True) + EPS)
                st_ref[part, u, h * c:(h + 1) * c, :] = blk

    ri = lax.broadcasted_iota(jnp.int32, (c, r), 0)
    li = lax.broadcasted_iota(jnp.int32, (c, r), 1)
    lj = li & (c - 1)
    lh = li >> _log2(c)
    lower = ri >= lj
    strict = ri > lj
    head_sel = [jnp.where(lh == h, 1.0, 0.0).astype(BF16) for h in range(nh)]

    def expand(xb):
        return jnp.concatenate([xb * m for m in head_sel], axis=0)

    def lanes(x_st):
        return jnp.concatenate([x_st[h * c:(h + 1) * c] for h in range(nh)], axis=1)

    def head_blocks(x_st):
        zero = jnp.zeros((c, hd), x_st.dtype)
        return jnp.concatenate(
            [jnp.concatenate([x_st[h * c:(h + 1) * c] if h2 == h else zero for h2 in range(nh)], axis=1)
             for h in range(nh)], axis=0)

    ci = lax.broadcasted_iota(jnp.int32, (c, c), 0)
    cj = lax.broadcasted_iota(jnp.int32, (c, c), 1)
    tril = jnp.where(ci >= cj, 1.0, 0.0)
    di = lax.broadcasted_iota(jnp.int32, (hd, hd), 0)
    dj = lax.broadcasted_iota(jnp.int32, (hd, hd), 1)
    eye_hd = di == dj
    neg_rate = -jnp.exp(gate_ref[0:1, :])
    dt_bias = gate_ref[1:2, :]
    onorm = onorm_ref[...]

    def stack_cols(x, lane0):
        return jnp.concatenate([x[:, lane0 + h:lane0 + h + 1] for h in range(nh)], axis=0)

    beta, gc, gcc, gr, glast = [], [], [], [], []
    for b, cc in units:
        ba = ba_ref[b * tc + cc * c:b * tc + (cc + 1) * c, :]
        g_all = neg_rate * _softplus(ba + dt_bias)
        gcum = _dot(tril, g_all, HIGHEST)
        gcum_t = gcum.T
        beta.append(stack_cols(_sigmoid(ba), 0))
        gc.append(stack_cols(gcum, GATE_LANE))
        gcc_h = jnp.zeros((c, r), F32)
        for h in range(nh):
            gcc_h = jnp.where(lh == h, gcum[:, GATE_LANE + h:GATE_LANE + h + 1], gcc_h)
        gcc.append(gcc_h)
        gr.append(jnp.concatenate([gcum_t[GATE_LANE + h:GATE_LANE + h + 1, :] for h in range(nh)], axis=1))
        glast.append(jnp.concatenate(
            [jnp.broadcast_to(gcum[c - 1:c, GATE_LANE + h:GATE_LANE + h + 1], (c, 1)) for h in range(nh)], axis=0))

    def fold_units(us):
        n = range(len(us))
        q = [st_ref[0, u] * (hd ** -0.5) for u in us]
        k = [st_ref[1, u] for u in us]
        v = [st_ref[2, u] for u in us]
        kb = [k[j] * beta[u] for j, u in enumerate(us)]
        big = [_dot_nt(_bf(jnp.concatenate([lanes(kb[j]), lanes(q[j])], axis=0)), head_blocks(_bf(k[j]))) for j in n]
        decay = [jnp.where(lower, jnp.exp(jnp.where(lower, gcc[u] - gr[u], 0.0)), 0.0) for u in us]
        a = [jnp.where(strict, big[j][:c] * decay[j], 0.0) for j in n]
        qkd = [expand(_bf(big[j][c:] * decay[j])) for j in n]
        tinv = _unit_lower_inverse(a, c, ri, lj, expand)
        eg = [jnp.exp(gc[u]) for u in us]
        rhs = [_bf(jnp.concatenate([v[j] * beta[u], kb[j] * eg[j]], axis=1)) for j, u in enumerate(us)]
        sol = [_dot(expand(_bf(tinv[j])), rhs[j]) for j in n]
        solb = [_bf(s) for s in sol]
        fold = [_dot(qkd[j], solb[j]) for j in n]
        k_dec = [_bf(k[j] * jnp.exp(glast[u] - gc[u])) for j, u in enumerate(us)]
        for j, u in enumerate(us):
            qp = q[j] * eg[j] - fold[j][:, hd:]
            for h in range(nh):
                rows = slice(h * c, (h + 1) * c)
                kt = _dot_tn(k_dec[j][rows], solb[j][rows])
                g_tot = jnp.exp(glast[u][h * c:h * c + 1, :])
                lhs_ref[u, h, 0:hd, :] = _bf(jnp.where(eye_hd, g_tot, 0.0) - kt[:, hd:])
                lhs_ref[u, h, hd:hd + c, :] = _bf(qp[rows])
                add_ref[u, h, 0:hd, :] = kt[:, :hd]
                add_ref[u, h, hd:hd + c, :] = fold[j][rows, :hd]

    fold_units(list(range(len(units))))

    for u, (b, cc) in enumerate(units):
        rows = slice(b * tc + cc * c, b * tc + (cc + 1) * c)
        for h in range(nh):
            res = _dot(lhs_ref[u, h], _bf(s_ref[b, h])) + add_ref[u, h]
            s_ref[b, h] = res[:hd]
            o = res[hd:]
            on = o * lax.rsqrt(jnp.mean(o * o, axis=-1, keepdims=True) + EPS) * onorm
            bg = bg_ref[rows, h * hd:(h + 1) * hd]
            o_ref[rows, h * hd:(h + 1) * hd] = (on * _silu(bg)).astype(BF16)
    sout_ref[...] = s_ref[...]


def _delta(z, hist, s0, convw, gate, onorm, *, nb, bt, t, tc, c):
    nt = t // tc
    units = bt * (tc // c)
    state = (bt, B_HEADS, B_HEAD_DIM, B_HEAD_DIM)
    kern = functools.partial(_delta_kernel, bt=bt, c=c, tc=tc)
    return pl.pallas_call(
        kern,
        grid=(nb // bt, nt),
        in_specs=[
            _z_window(bt * tc, 3 * B_WIDTH, Z_BQKV, lambda b, i: b * nt + i),
            _z_window(bt * tc, LANES, Z_BA, lambda b, i: b * nt + i),
            _z_window(bt * tc, B_WIDTH, Z_BG, lambda b, i: b * nt + i),
            pl.BlockSpec((bt, SUBLANES, 3 * B_WIDTH), lambda b, i: (b, 0, 0)),
            pl.BlockSpec(state, lambda b, i: (b, 0, 0, 0)),
            pl.BlockSpec((CONV_W, 3 * B_WIDTH), lambda b, i: (0, 0)),
            pl.BlockSpec((2, LANES), lambda b, i: (0, 0)),
            pl.BlockSpec((1, B_HEAD_DIM), lambda b, i: (0, 0)),
        ],
        out_specs=[
            pl.BlockSpec((bt * tc, B_WIDTH), lambda b, i: (b * nt + i, 0)),
            pl.BlockSpec(state, lambda b, i: (b, 0, 0, 0)),
        ],
        out_shape=[
            jax.ShapeDtypeStruct((nb * t, B_WIDTH), BF16),
            jax.ShapeDtypeStruct((nb, B_HEADS, B_HEAD_DIM, B_HEAD_DIM), F32),
        ],
        scratch_shapes=[
            pltpu.VMEM((bt, SUBLANES + tc, 3 * B_WIDTH), F32),
            pltpu.VMEM((3, units, B_HEADS * c, B_HEAD_DIM), F32),
            pltpu.VMEM((units, B_HEADS, B_HEAD_DIM + c, B_HEAD_DIM), BF16),
            pltpu.VMEM((units, B_HEADS, B_HEAD_DIM + c, B_HEAD_DIM), F32),
            pltpu.VMEM(state, F32),
        ],
        compiler_params=pltpu.CompilerParams(dimension_semantics=("arbitrary", "arbitrary"),
                                             vmem_limit_bytes=VMEM_LIMIT),
        name="gated_delta",
    )(z, z, z, hist, s0, convw, gate, onorm)


def _pool_kernel(h_ref, oa_ref, ob_ref, wa_ref, wb_ref, hist_ref, g_ref, win_ref, wgrp_ref, scale_ref, wout_ref,
                 y_ref, tail_ref, ext_ref, *, bt, tt, pos0):
    i = pl.program_id(1)
    pad = POOL_HIST + 1
    rows = bt * tt

    @pl.when(i == 0)
    def _():
        pos = pos0 - pad + lax.broadcasted_iota(jnp.int32, (1, pad, 1), 1)
        ext_ref[:, 0:pad, :] = jnp.where(pos >= 0, hist_ref[...], 0.0)

    @pl.when(i > 0)
    def _():
        ext_ref[:, 0:pad, :] = ext_ref[:, tt:tt + pad, :]

    x = h_ref[...] + _dot(oa_ref[...], wa_ref[...]) + _dot(ob_ref[...], wb_ref[...])
    xn = x * lax.rsqrt(jnp.mean(x * x, axis=-1, keepdims=True) + EPS) * g_ref[...]
    z = _dot(xn.astype(BF16), win_ref[...])
    u = z[:, :C_WIDTH]
    gate = z[:, C_WIDTH:]
    ext_ref[:, pad:pad + tt, :] = u.reshape(bt, tt, C_WIDTH)
    tail_ref[...] = ext_ref[:, tt:tt + pad, :]

    tpos = pos0 + i * tt + (lax.broadcasted_iota(jnp.int32, (rows, 1), 0) & (tt - 1))
    mixed = []
    for gi, w in enumerate(POOL_SIZES):
        cols = slice(gi * C_GROUP, (gi + 1) * C_GROUP)
        s = ext_ref[:, :, cols].reshape(bt * (pad + tt), C_GROUP)
        sh = 1
        while sh < w:
            s = s + pltpu.roll(s, sh, 0)
            sh *= 2
        s = s.reshape(bt, pad + tt, C_GROUP)[:, pad:, :].reshape(rows, C_GROUP)
        cnt = jnp.minimum(tpos + 1, w).astype(F32)
        pooled = s / cnt - u[:, cols]
        m = _dot(pooled.astype(BF16), wgrp_ref[gi]) * scale_ref[:, cols]
        mixed.append((m * _silu(gate[:, cols])).astype(BF16))
    y_ref[...] = x + _dot(jnp.concatenate(mixed, axis=1), wout_ref[...])


def _pool_layer(h, oa, ob, wa, wb, hist, g, win, wgrp, scale, wout, *, nb, bt, t, tt, pos0):
    nt = t // tt
    pad = POOL_HIST + 1
    rows = bt * tt
    kern = functools.partial(_pool_kernel, bt=bt, tt=tt, pos0=pos0)
    tile = lambda b, i: (b * nt + i, 0)
    const2 = lambda b, i: (0, 0)
    once = pl.Buffered(1)
    return pl.pallas_call(
        kern,
        grid=(nb // bt, nt),
        in_specs=[
            pl.BlockSpec((rows, D_MODEL), tile),
            pl.BlockSpec((rows, A_WIDTH), tile),
            pl.BlockSpec((rows, B_WIDTH), tile),
            pl.BlockSpec((A_WIDTH, D_MODEL), const2, pipeline_mode=once),
            pl.BlockSpec((B_WIDTH, D_MODEL), const2, pipeline_mode=once),
            pl.BlockSpec((bt, pad, C_WIDTH), lambda b, i: (b, 0, 0)),
            pl.BlockSpec((1, D_MODEL), const2),
            pl.BlockSpec((D_MODEL, 2 * C_WIDTH), const2, pipeline_mode=once),
            pl.BlockSpec((len(POOL_SIZES), C_GROUP, C_GROUP), lambda b, i: (0, 0, 0), pipeline_mode=once),
            pl.BlockSpec((1, C_WIDTH), const2),
            pl.BlockSpec((C_WIDTH, D_MODEL), const2, pipeline_mode=once),
        ],
        out_specs=[
            pl.BlockSpec((rows, D_MODEL), tile),
            pl.BlockSpec((bt, pad, C_WIDTH), lambda b, i: (b, 0, 0)),
        ],
        out_shape=[
            jax.ShapeDtypeStruct((nb * t, D_MODEL), F32),
            jax.ShapeDtypeStruct((nb, pad, C_WIDTH), F32),
        ],
        scratch_shapes=[pltpu.VMEM((bt, pad + tt, C_WIDTH), F32)],
        compiler_params=pltpu.CompilerParams(dimension_semantics=("arbitrary", "arbitrary"),
                                             vmem_limit_bytes=VMEM_LIMIT),
        name="out_proj_pool_layer",
    )(h, oa, ob, wa, wb, hist, g, win, wgrp, scale, wout)


def _ab_weights(norm_g, w_in, q_norm, k_norm, sinks, conv_w, a_log, dt_bias, o_norm, w_out):
    w = jnp.swapaxes(w_in, 0, 1).astype(BF16)
    lane_pad = (GATE_LANE, LANES - GATE_LANE - B_HEADS)
    gate = jnp.stack([jnp.pad(a_log.astype(F32), lane_pad), jnp.pad(dt_bias.astype(F32), lane_pad)])
    return dict(
        norm_g=norm_g.reshape(1, D_MODEL), w=w, kn=jnp.tile(k_norm, A_KV_HEADS).reshape(1, A_KV_WIDTH),
        qn=jnp.tile(q_norm, A_KV_HEADS).reshape(1, LANES), sinks=sinks.astype(F32), conv_w=conv_w, gate=gate,
        onorm=o_norm.reshape(1, B_HEAD_DIM), wa=w_out[:A_WIDTH].astype(BF16), wb=w_out[A_WIDTH:].astype(BF16))


def _group_tiles(t):
    if t >= CHUNK:
        return dict(proj_rows=1024, seqs_attn=1, seqs_delta=1, delta_tokens=1024, chunk=CHUNK, seqs_pool=1,
                    pool_tokens=1024)
    return dict(proj_rows=256, seqs_attn=32, seqs_delta=16, delta_tokens=t, chunk=t, seqs_pool=256 // t, pool_tokens=t)


def _ab_mixers(h, wts, cache_k, cache_v, s0, conv_hist, *, nb, t):
    tl = _group_tiles(t)
    z = _in_proj(h, wts["norm_g"], wts["w"], wts["kn"], wts["qn"], min(nb * t, tl["proj_rows"]))
    bt = tl["seqs_attn"]
    if cache_k is None:
        o_a = _attention(z, wts["sinks"], nb=nb, bt=bt, t=t)
    else:
        k_new = z[:, Z_AK:Z_AK + A_KV_WIDTH].reshape(nb, t, A_KV_WIDTH)
        v_new = z[:, Z_AV:Z_AV + A_KV_WIDTH].reshape(nb, t, A_KV_WIDTH)
        kbuf = jnp.concatenate([cache_k.reshape(nb, WINDOW, A_KV_WIDTH), k_new], axis=1)
        vbuf = jnp.concatenate([cache_v.reshape(nb, WINDOW, A_KV_WIDTH), v_new], axis=1)
        o_a = _attention_step(z, kbuf.reshape(nb * (WINDOW + t), A_KV_WIDTH), vbuf.reshape(nb * (WINDOW + t), A_KV_WIDTH),
                              wts["sinks"], nb=nb, bt=bt, t=t)
    hist = jnp.pad(conv_hist.astype(F32), ((0, 0), (SUBLANES - (CONV_W - 1), 0), (0, 0)))
    o_b, s_new = _delta(z, hist, s0.astype(F32), wts["conv_w"], wts["gate"], wts["onorm"], nb=nb,
                        bt=tl["seqs_delta"], t=t, tc=tl["delta_tokens"], c=tl["chunk"])
    return o_a, o_b, z, s_new


def _last_rows(z, col, width, nb, t, rows):
    return z.reshape(nb, t, Z_WIDTH)[:, t - rows:, col:col + width]


def _cache_rows(z, col, nb, t, rows):
    return _last_rows(z, col, A_KV_WIDTH, nb, t, rows).reshape(nb, rows, A_KV_HEADS, A_HEAD_DIM)


def kernel(x_prompt, x_sample, cache_a_k, cache_a_v, state_b_s, state_b_conv, state_c_pool,
           norm_ab, w_in_ab, q_norm_a, k_norm_a, sinks_a, conv_b, a_log_b, dt_bias_b, o_norm_b, w_out_ab,
           norm_c, w_in_c, w_grp_c, scale_c, w_out_c):
    bp, tp, _ = x_prompt.shape
    bs, ts, _ = x_sample.shape
    hp = x_prompt.reshape(bp * tp, D_MODEL)
    hs = x_sample.reshape(bs * ts, D_MODEL)

    wts = _ab_weights(norm_ab[0], w_in_ab[0], q_norm_a[0], k_norm_a[0], sinks_a[0], conv_b[0], a_log_b[0],
                      dt_bias_b[0], o_norm_b[0], w_out_ab[0])
    s0 = jnp.zeros((bp, B_HEADS, B_HEAD_DIM, B_HEAD_DIM), F32)
    c0 = jnp.zeros((bp, CONV_W - 1, 3 * B_WIDTH), F32)
    oap, obp, zp, sp = _ab_mixers(hp, wts, None, None, s0, c0, nb=bp, t=tp)
    oas, obs, zs, ss = _ab_mixers(hs, wts, cache_a_k[0], cache_a_v[0], state_b_s[0], state_b_conv[0], nb=bs, t=ts)
    p_a_k = _cache_rows(zp, Z_AK, bp, tp, WINDOW)[None]
    p_a_v = _cache_rows(zp, Z_AV, bp, tp, WINDOW)[None]
    s_a_k = _cache_rows(zs, Z_AK, bs, ts, ts)[None]
    s_a_v = _cache_rows(zs, Z_AV, bs, ts, ts)[None]
    p_b_conv = _last_rows(zp, Z_BQKV, 3 * B_WIDTH, bp, tp, CONV_W - 1)[None]
    s_b_conv = _last_rows(zs, Z_BQKV, 3 * B_WIDTH, bs, ts, CONV_W - 1)[None]

    g_c = norm_c[0].reshape(1, D_MODEL)
    win = w_in_c[0].astype(BF16)
    wgrp = w_grp_c[0].astype(BF16)
    scale = scale_c[0].reshape(1, C_WIDTH)
    wout = w_out_c[0].astype(BF16)
    h0 = jnp.zeros((bp, POOL_HIST + 1, C_WIDTH), F32)
    hs0 = jnp.pad(state_c_pool[0].astype(F32), ((0, 0), (1, 0), (0, 0)))
    tlp, tls = _group_tiles(tp), _group_tiles(ts)
    yp, tail_p = _pool_layer(hp, oap, obp, wts["wa"], wts["wb"], h0, g_c, win, wgrp, scale, wout, nb=bp,
                             bt=tlp["seqs_pool"], t=tp, tt=tlp["pool_tokens"], pos0=0)
    ys, tail_s = _pool_layer(hs, oas, obs, wts["wa"], wts["wb"], hs0, g_c, win, wgrp, scale, wout, nb=bs,
                             bt=tls["seqs_pool"], t=ts, tt=tls["pool_tokens"], pos0=PAST_LEN)

    return (yp.reshape(bp, tp, D_MODEL), ys.reshape(bs, ts, D_MODEL),
            p_a_k, p_a_v, sp[None], p_b_conv, tail_p[:, 1:][None],
            s_a_k, s_a_v, ss[None], s_b_conv, tail_s[:, 1:][None])
```
